```python
import math
import jax, jax.numpy as jnp
from jax import lax
import numpy as np

D_MODEL = 1024
BATCH = 32
SEQ = 2048
DEPTH = 1

C_CONV = 1024
CONV_WIDTH = 31
HEAD_DIM = 64
N_Q_HEADS = 16
N_KV_HEADS = 2
GROUP = N_Q_HEADS // N_KV_HEADS
WINDOW = 128
BLOCK = 128
D_FF = int(math.ceil((8 * D_MODEL / 3) / 256) * 256)
EPS = 1e-6
NEG = -1e30

Q_W = N_Q_HEADS * HEAD_DIM
KV_W = N_KV_HEADS * HEAD_DIM
IN_COLS = 2 * C_CONV + Q_W + 2 * KV_W + 2 * D_MODEL

kernel_name = "hybrid_conformer_conv_swa_sink_alibi_block"


def rms_norm(x, g):
    xf = x.astype(jnp.float32)
    y = xf * lax.rsqrt(jnp.mean(xf * xf, axis=-1, keepdims=True) + EPS)
    return (y * g.astype(jnp.float32)).astype(x.dtype)


def layer_norm(x, g, b):
    xf = x.astype(jnp.float32)
    mu = jnp.mean(xf, axis=-1, keepdims=True)
    var = jnp.mean(jnp.square(xf - mu), axis=-1, keepdims=True)
    y = (xf - mu) * lax.rsqrt(var + EPS)
    return (y * g.astype(jnp.float32) + b.astype(jnp.float32)).astype(x.dtype)


def conv_module(u, dw_w, dw_b, ln_g, ln_b, w_out):
    a, gate = jnp.split(u, 2, axis=-1)
    h = a * jax.nn.sigmoid(gate)
    h = lax.conv_general_dilated(
        h, dw_w, window_strides=(1,), padding=[(CONV_WIDTH - 1, 0)],
        dimension_numbers=('NWC', 'WIO', 'NWC'),
        feature_group_count=C_CONV) + dw_b
    h = jax.nn.silu(layer_norm(h, ln_g, ln_b))
    return h @ w_out


def alibi_slopes(n_heads):
    h = jnp.arange(1, n_heads + 1, dtype=jnp.float32)
    return jnp.exp2(-8.0 * h / n_heads)


def swa_attention(q, k, v, q_g, k_g, sinks):
    B, S = q.shape[0], q.shape[1]
    nb = S // BLOCK
    q = rms_norm(q, q_g)
    k = rms_norm(k, k_g)
    qb = q.reshape(B, nb, BLOCK, N_KV_HEADS, GROUP, HEAD_DIM)
    pad = ((0, 0), (BLOCK, 0), (0, 0), (0, 0))
    kb = jnp.pad(k, pad).reshape(B, nb + 1, BLOCK, N_KV_HEADS, HEAD_DIM)
    vb = jnp.pad(v, pad).reshape(B, nb + 1, BLOCK, N_KV_HEADS, HEAD_DIM)
    kwin = jnp.concatenate([kb[:, :-1], kb[:, 1:]], axis=2)
    vwin = jnp.concatenate([vb[:, :-1], vb[:, 1:]], axis=2)

    scale = 1.0 / math.sqrt(HEAD_DIM)
    s = jnp.einsum('bnqkgd,bnskd->bnkgqs', qb, kwin).astype(jnp.float32) * scale

    qi = jnp.arange(BLOCK)[:, None]
    sj = jnp.arange(2 * BLOCK)[None, :]
    dist = qi + BLOCK - sj
    s_glob = jnp.arange(nb)[:, None, None] * BLOCK - BLOCK + sj
    valid = (dist >= 0) & (dist < WINDOW) & (s_glob >= 0)
    slopes = alibi_slopes(N_Q_HEADS).reshape(N_KV_HEADS, GROUP, 1, 1)
    bias = -slopes * dist.astype(jnp.float32)
    s = jnp.where(valid[None, :, None, None], s + bias, NEG)

    sink = sinks.astype(jnp.float32).reshape(N_KV_HEADS, GROUP, 1, 1)
    m = jnp.maximum(jnp.max(s, axis=-1, keepdims=True), sink)
    p = jnp.exp(s - m)
    p = p / (jnp.sum(p, axis=-1, keepdims=True) + jnp.exp(sink - m))
    o = jnp.einsum('bnkgqs,bnskd->bnqkgd', p.astype(v.dtype), vwin)
    return o.reshape(B, S, N_Q_HEADS * HEAD_DIM)


def _fwd_setup_inputs(seed: int = 0) -> dict:
    key = jax.random.key(seed)
    ks = jax.random.split(key, 17)
    f32 = jnp.float32
    nrm = lambda k, shape, s: jax.random.normal(k, shape, f32) * s
    return {
        "x": nrm(ks[0], (BATCH, SEQ, D_MODEL), 1.0),
        "norm_mix_g": 1.0 + nrm(ks[1], (D_MODEL,), 0.02),
        "w_in": nrm(ks[2], (D_MODEL, IN_COLS), D_MODEL ** -0.5),
        "conv_dw_w": nrm(ks[3], (CONV_WIDTH, 1, C_CONV), CONV_WIDTH ** -0.5),
        "conv_dw_b": nrm(ks[4], (C_CONV,), 0.02),
        "conv_ln_g": 1.0 + nrm(ks[5], (C_CONV,), 0.02),
        "conv_ln_b": nrm(ks[6], (C_CONV,), 0.02),
        "w_conv_out": nrm(ks[7], (C_CONV, D_MODEL), C_CONV ** -0.5),
        "q_norm_g": 1.0 + nrm(ks[8], (HEAD_DIM,), 0.02),
        "k_norm_g": 1.0 + nrm(ks[9], (HEAD_DIM,), 0.02),
        "sinks": nrm(ks[10], (N_Q_HEADS,), 0.5),
        "w_attn_out": nrm(ks[11], (Q_W, D_MODEL), Q_W ** -0.5),
        "w_merge_out": nrm(ks[12], (D_MODEL, D_MODEL), D_MODEL ** -0.5),
        "norm_ffn_g": 1.0 + nrm(ks[13], (D_MODEL,), 0.02),
        "w_ffn_in": nrm(ks[14], (D_MODEL, 2 * D_FF), D_MODEL ** -0.5),
        "w_ffn_down": nrm(ks[15], (D_FF, D_MODEL), D_FF ** -0.5),
    }


def _fwd_reference(x, norm_mix_g, w_in, conv_dw_w, conv_dw_b, conv_ln_g, conv_ln_b,
              w_conv_out, q_norm_g, k_norm_g, sinks, w_attn_out, w_merge_out,
              norm_ffn_g, w_ffn_in, w_ffn_down):
    B, S, _ = x.shape
    h = x
    for _layer in range(DEPTH):
        xn = rms_norm(h, norm_mix_g)
        u = xn @ w_in
        o1 = 2 * C_CONV
        o2 = o1 + Q_W
        o3 = o2 + KV_W
        o4 = o3 + KV_W
        o5 = o4 + D_MODEL
        conv_in = u[..., :o1]
        q = u[..., o1:o2].reshape(B, S, N_Q_HEADS, HEAD_DIM)
        k = u[..., o2:o3].reshape(B, S, N_KV_HEADS, HEAD_DIM)
        v = u[..., o3:o4].reshape(B, S, N_KV_HEADS, HEAD_DIM)
        g_conv = jax.nn.sigmoid(u[..., o4:o5])
        g_attn = jax.nn.sigmoid(u[..., o5:])

        y_conv = conv_module(conv_in, conv_dw_w, conv_dw_b, conv_ln_g, conv_ln_b, w_conv_out)
        y_attn = swa_attention(q, k, v, q_norm_g, k_norm_g, sinks) @ w_attn_out
        h = h + (g_conv * y_conv + g_attn * y_attn) @ w_merge_out

        hn = rms_norm(h, norm_ffn_g)
        gate, up = jnp.split(hn @ w_ffn_in, 2, axis=-1)
        h = h + (jax.nn.silu(gate) * up) @ w_ffn_down
    return h


import jax as _jax
import jax.numpy as _jnp

TWIN_FORMAT = 'train_step'
FWD_PARAMS = ['x', 'norm_mix_g', 'w_in', 'conv_dw_w', 'conv_dw_b', 'conv_ln_g', 'conv_ln_b', 'w_conv_out', 'q_norm_g', 'k_norm_g', 'sinks', 'w_attn_out', 'w_merge_out', 'norm_ffn_g', 'w_ffn_in', 'w_ffn_down']
TWIN_WEIGHTS = ['norm_mix_g', 'w_in', 'conv_dw_w', 'conv_dw_b', 'conv_ln_g', 'conv_ln_b', 'w_conv_out', 'q_norm_g', 'k_norm_g', 'sinks', 'w_attn_out', 'w_merge_out', 'norm_ffn_g', 'w_ffn_in', 'w_ffn_down']
TWIN_DIFF_INPUT = 'x'
TWIN_INPUTS = ['x', 'norm_mix_g', 'w_in', 'conv_dw_w', 'conv_dw_b', 'conv_ln_g', 'conv_ln_b', 'w_conv_out', 'q_norm_g', 'k_norm_g', 'sinks', 'w_attn_out', 'w_merge_out', 'norm_ffn_g', 'w_ffn_in', 'w_ffn_down', 'loss_target', 'm_norm_mix_g', 'm_w_in', 'm_conv_dw_w', 'm_conv_dw_b', 'm_conv_ln_g', 'm_conv_ln_b', 'm_w_conv_out', 'm_q_norm_g', 'm_k_norm_g', 'm_sinks', 'm_w_attn_out', 'm_w_merge_out', 'm_norm_ffn_g', 'm_w_ffn_in', 'm_w_ffn_down', 'v_norm_mix_g', 'v_w_in', 'v_conv_dw_w', 'v_conv_dw_b', 'v_conv_ln_g', 'v_conv_ln_b', 'v_w_conv_out', 'v_q_norm_g', 'v_k_norm_g', 'v_sinks', 'v_w_attn_out', 'v_w_merge_out', 'v_norm_ffn_g', 'v_w_ffn_in', 'v_w_ffn_down']
TWIN_OUTPUTS = ['loss', 'grad_x', 'grad_norm_mix_g', 'grad_w_in', 'grad_conv_dw_w', 'grad_conv_dw_b', 'grad_conv_ln_g', 'grad_conv_ln_b', 'grad_w_conv_out', 'grad_q_norm_g', 'grad_k_norm_g', 'grad_sinks', 'grad_w_attn_out', 'grad_w_merge_out', 'grad_norm_ffn_g', 'grad_w_ffn_in', 'grad_w_ffn_down', 'delta_norm_mix_g', 'delta_w_in', 'delta_conv_dw_w', 'delta_conv_dw_b', 'delta_conv_ln_g', 'delta_conv_ln_b', 'delta_w_conv_out', 'delta_q_norm_g', 'delta_k_norm_g', 'delta_sinks', 'delta_w_attn_out', 'delta_w_merge_out', 'delta_norm_ffn_g', 'delta_w_ffn_in', 'delta_w_ffn_down', 'new_m_norm_mix_g', 'new_m_w_in', 'new_m_conv_dw_w', 'new_m_conv_dw_b', 'new_m_conv_ln_g', 'new_m_conv_ln_b', 'new_m_w_conv_out', 'new_m_q_norm_g', 'new_m_k_norm_g', 'new_m_sinks', 'new_m_w_attn_out', 'new_m_w_merge_out', 'new_m_norm_ffn_g', 'new_m_w_ffn_in', 'new_m_w_ffn_down', 'new_v_norm_mix_g', 'new_v_w_in', 'new_v_conv_dw_w', 'new_v_conv_dw_b', 'new_v_conv_ln_g', 'new_v_conv_ln_b', 'new_v_w_conv_out', 'new_v_q_norm_g', 'new_v_k_norm_g', 'new_v_sinks', 'new_v_w_attn_out', 'new_v_w_merge_out', 'new_v_norm_ffn_g', 'new_v_w_ffn_in', 'new_v_w_ffn_down']
TWIN_LEAF_KINDS = {'loss': 'loss', 'grad_x': 'grad_x', 'grad_norm_mix_g': 'grad_w', 'grad_w_in': 'grad_w', 'grad_conv_dw_w': 'grad_w', 'grad_conv_dw_b': 'grad_w', 'grad_conv_ln_g': 'grad_w', 'grad_conv_ln_b': 'grad_w', 'grad_w_conv_out': 'grad_w', 'grad_q_norm_g': 'grad_w', 'grad_k_norm_g': 'grad_w', 'grad_sinks': 'grad_w', 'grad_w_attn_out': 'grad_w', 'grad_w_merge_out': 'grad_w', 'grad_norm_ffn_g': 'grad_w', 'grad_w_ffn_in': 'grad_w', 'grad_w_ffn_down': 'grad_w', 'delta_norm_mix_g': 'delta_w', 'delta_w_in': 'delta_w', 'delta_conv_dw_w': 'delta_w', 'delta_conv_dw_b': 'delta_w', 'delta_conv_ln_g': 'delta_w', 'delta_conv_ln_b': 'delta_w', 'delta_w_conv_out': 'delta_w', 'delta_q_norm_g': 'delta_w', 'delta_k_norm_g': 'delta_w', 'delta_sinks': 'delta_w', 'delta_w_attn_out': 'delta_w', 'delta_w_merge_out': 'delta_w', 'delta_norm_ffn_g': 'delta_w', 'delta_w_ffn_in': 'delta_w', 'delta_w_ffn_down': 'delta_w', 'new_m_norm_mix_g': 'new_m', 'new_m_w_in': 'new_m', 'new_m_conv_dw_w': 'new_m', 'new_m_conv_dw_b': 'new_m', 'new_m_conv_ln_g': 'new_m', 'new_m_conv_ln_b': 'new_m', 'new_m_w_conv_out': 'new_m', 'new_m_q_norm_g': 'new_m', 'new_m_k_norm_g': 'new_m', 'new_m_sinks': 'new_m', 'new_m_w_attn_out': 'new_m', 'new_m_w_merge_out': 'new_m', 'new_m_norm_ffn_g': 'new_m', 'new_m_w_ffn_in': 'new_m', 'new_m_w_ffn_down': 'new_m', 'new_v_norm_mix_g': 'new_v', 'new_v_w_in': 'new_v', 'new_v_conv_dw_w': 'new_v', 'new_v_conv_dw_b': 'new_v', 'new_v_conv_ln_g': 'new_v', 'new_v_conv_ln_b': 'new_v', 'new_v_w_conv_out': 'new_v', 'new_v_q_norm_g': 'new_v', 'new_v_k_norm_g': 'new_v', 'new_v_sinks': 'new_v', 'new_v_w_attn_out': 'new_v', 'new_v_w_merge_out': 'new_v', 'new_v_norm_ffn_g': 'new_v', 'new_v_w_ffn_in': 'new_v', 'new_v_w_ffn_down': 'new_v'}


def _forward(args):
    return _fwd_reference(*[args[k] for k in FWD_PARAMS])


def _output_shape():
    out = _jax.eval_shape(lambda: _forward(_fwd_setup_inputs(0)))
    return out.shape, out.dtype

N_MICROBATCH = 1
ADAM_LR = 0.001
ADAM_B1 = 0.9
ADAM_B2 = 0.999
ADAM_EPS = 1e-08
ADAM_WD = 0.01
ADAM_STEP = 10
PER_EXAMPLE_BATCH_AXIS = {'x': 0, 'loss_target': 0}
SHARED_INPUTS = []
_WEIGHT_DTYPES = {'norm_mix_g': _jnp.float32, 'w_in': _jnp.float32, 'conv_dw_w': _jnp.float32, 'conv_dw_b': _jnp.float32, 'conv_ln_g': _jnp.float32, 'conv_ln_b': _jnp.float32, 'w_conv_out': _jnp.float32, 'q_norm_g': _jnp.float32, 'k_norm_g': _jnp.float32, 'sinks': _jnp.float32, 'w_attn_out': _jnp.float32, 'w_merge_out': _jnp.float32, 'norm_ffn_g': _jnp.float32, 'w_ffn_in': _jnp.float32, 'w_ffn_down': _jnp.float32}
MOMENT_SCALE = {'norm_mix_g': 1.814407e+00, 'w_in': 1.221506e-01, 'conv_dw_w': 2.922741e-01, 'conv_dw_b': 5.700703e+00, 'conv_ln_g': 8.356487e+00, 'conv_ln_b': 6.248615e+00, 'w_conv_out': 1.183696e+00, 'q_norm_g': 1.034311e+01, 'k_norm_g': 1.037011e+01, 'sinks': 2.288760e+01, 'w_attn_out': 1.197912e-01, 'w_merge_out': 9.302136e-01, 'norm_ffn_g': 4.919325e+01, 'w_ffn_in': 3.322081e-01, 'w_ffn_down': 3.874518e-01}


def _to_microbatches(a, axis):
    t = _jnp.moveaxis(a, axis, 0)
    t = t.reshape((N_MICROBATCH, t.shape[0] // N_MICROBATCH) + t.shape[1:])
    return _jnp.moveaxis(t, 1, axis + 1)


def setup_inputs(seed: int = 0) -> dict:
    inp = _fwd_setup_inputs(seed)
    key = _jax.random.fold_in(_jax.random.key(seed), 7919)
    shape, _ = _output_shape()
    out = dict(inp)
    out["loss_target"] = _jax.random.normal(_jax.random.fold_in(key, 0), shape, _jnp.float32)
    for i, name in enumerate(TWIN_WEIGHTS):
        w = inp[name].astype(_jnp.float32)
        if MOMENT_SCALE is None:
            s = _jnp.sqrt(_jnp.mean(_jnp.square(w)) + 1e-30)
        else:
            s = MOMENT_SCALE[name]
        km, kv = _jax.random.split(_jax.random.fold_in(key, i + 1))
        out[name] = w
        out["m_" + name] = s * _jax.random.normal(km, w.shape, _jnp.float32)
        out["v_" + name] = (s * s) * _jax.random.uniform(kv, w.shape, _jnp.float32, 0.5, 1.5)
    if N_MICROBATCH > 1:
        for name, axis in PER_EXAMPLE_BATCH_AXIS.items():
            out[name] = _to_microbatches(out[name], axis)
    return {'x': out['x'], 'norm_mix_g': out['norm_mix_g'], 'w_in': out['w_in'], 'conv_dw_w': out['conv_dw_w'], 'conv_dw_b': out['conv_dw_b'], 'conv_ln_g': out['conv_ln_g'], 'conv_ln_b': out['conv_ln_b'], 'w_conv_out': out['w_conv_out'], 'q_norm_g': out['q_norm_g'], 'k_norm_g': out['k_norm_g'], 'sinks': out['sinks'], 'w_attn_out': out['w_attn_out'], 'w_merge_out': out['w_merge_out'], 'norm_ffn_g': out['norm_ffn_g'], 'w_ffn_in': out['w_ffn_in'], 'w_ffn_down': out['w_ffn_down'], 'loss_target': out['loss_target'], 'm_norm_mix_g': out['m_norm_mix_g'], 'm_w_in': out['m_w_in'], 'm_conv_dw_w': out['m_conv_dw_w'], 'm_conv_dw_b': out['m_conv_dw_b'], 'm_conv_ln_g': out['m_conv_ln_g'], 'm_conv_ln_b': out['m_conv_ln_b'], 'm_w_conv_out': out['m_w_conv_out'], 'm_q_norm_g': out['m_q_norm_g'], 'm_k_norm_g': out['m_k_norm_g'], 'm_sinks': out['m_sinks'], 'm_w_attn_out': out['m_w_attn_out'], 'm_w_merge_out': out['m_w_merge_out'], 'm_norm_ffn_g': out['m_norm_ffn_g'], 'm_w_ffn_in': out['m_w_ffn_in'], 'm_w_ffn_down': out['m_w_ffn_down'], 'v_norm_mix_g': out['v_norm_mix_g'], 'v_w_in': out['v_w_in'], 'v_conv_dw_w': out['v_conv_dw_w'], 'v_conv_dw_b': out['v_conv_dw_b'], 'v_conv_ln_g': out['v_conv_ln_g'], 'v_conv_ln_b': out['v_conv_ln_b'], 'v_w_conv_out': out['v_w_conv_out'], 'v_q_norm_g': out['v_q_norm_g'], 'v_k_norm_g': out['v_k_norm_g'], 'v_sinks': out['v_sinks'], 'v_w_attn_out': out['v_w_attn_out'], 'v_w_merge_out': out['v_w_merge_out'], 'v_norm_ffn_g': out['v_norm_ffn_g'], 'v_w_ffn_in': out['v_w_ffn_in'], 'v_w_ffn_down': out['v_w_ffn_down']}


def _loss(weights, diff, rest, loss_target):
    with _jax.named_scope("forward"):
        args = {**rest, TWIN_DIFF_INPUT: diff, **{k: w.astype(_WEIGHT_DTYPES[k]) for k, w in weights.items()}}
        y = _forward(args)
    with _jax.named_scope("loss_head"):
        err = _jnp.square(y.astype(_jnp.float32) - loss_target)
        return 0.5 * _jnp.sum(_jnp.mean(err, axis=-1)) if err.ndim else 0.5 * err


def _adamw(w, g, m, v):
    m = ADAM_B1 * m + (1.0 - ADAM_B1) * g
    v = ADAM_B2 * v + (1.0 - ADAM_B2) * _jnp.square(g)
    m_hat = m / (1.0 - ADAM_B1 ** ADAM_STEP)
    v_hat = v / (1.0 - ADAM_B2 ** ADAM_STEP)
    delta = -ADAM_LR * (m_hat / (_jnp.sqrt(v_hat) + ADAM_EPS) + ADAM_WD * w)
    return delta, m, v


def reference(x, norm_mix_g, w_in, conv_dw_w, conv_dw_b, conv_ln_g, conv_ln_b, w_conv_out, q_norm_g, k_norm_g, sinks, w_attn_out, w_merge_out, norm_ffn_g, w_ffn_in, w_ffn_down, loss_target, m_norm_mix_g, m_w_in, m_conv_dw_w, m_conv_dw_b, m_conv_ln_g, m_conv_ln_b, m_w_conv_out, m_q_norm_g, m_k_norm_g, m_sinks, m_w_attn_out, m_w_merge_out, m_norm_ffn_g, m_w_ffn_in, m_w_ffn_down, v_norm_mix_g, v_w_in, v_conv_dw_w, v_conv_dw_b, v_conv_ln_g, v_conv_ln_b, v_w_conv_out, v_q_norm_g, v_k_norm_g, v_sinks, v_w_attn_out, v_w_merge_out, v_norm_ffn_g, v_w_ffn_in, v_w_ffn_down):
    given = dict(x=x, norm_mix_g=norm_mix_g, w_in=w_in, conv_dw_w=conv_dw_w, conv_dw_b=conv_dw_b, conv_ln_g=conv_ln_g, conv_ln_b=conv_ln_b, w_conv_out=w_conv_out, q_norm_g=q_norm_g, k_norm_g=k_norm_g, sinks=sinks, w_attn_out=w_attn_out, w_merge_out=w_merge_out, norm_ffn_g=norm_ffn_g, w_ffn_in=w_ffn_in, w_ffn_down=w_ffn_down, loss_target=loss_target, m_norm_mix_g=m_norm_mix_g, m_w_in=m_w_in, m_conv_dw_w=m_conv_dw_w, m_conv_dw_b=m_conv_dw_b, m_conv_ln_g=m_conv_ln_g, m_conv_ln_b=m_conv_ln_b, m_w_conv_out=m_w_conv_out, m_q_norm_g=m_q_norm_g, m_k_norm_g=m_k_norm_g, m_sinks=m_sinks, m_w_attn_out=m_w_attn_out, m_w_merge_out=m_w_merge_out, m_norm_ffn_g=m_norm_ffn_g, m_w_ffn_in=m_w_ffn_in, m_w_ffn_down=m_w_ffn_down, v_norm_mix_g=v_norm_mix_g, v_w_in=v_w_in, v_conv_dw_w=v_conv_dw_w, v_conv_dw_b=v_conv_dw_b, v_conv_ln_g=v_conv_ln_g, v_conv_ln_b=v_conv_ln_b, v_w_conv_out=v_w_conv_out, v_q_norm_g=v_q_norm_g, v_k_norm_g=v_k_norm_g, v_sinks=v_sinks, v_w_attn_out=v_w_attn_out, v_w_merge_out=v_w_merge_out, v_norm_ffn_g=v_norm_ffn_g, v_w_ffn_in=v_w_ffn_in, v_w_ffn_down=v_w_ffn_down)
    weights = {n: given[n] for n in TWIN_WEIGHTS}
    shared = {n: given[n] for n in SHARED_INPUTS}
    per_example = {n: given[n] for n in ['x']}
    grad_fn = _jax.value_and_grad(_loss, argnums=(0, 1))

    def one_microbatch(ex, loss_target):
        ex = dict(ex)
        diff = ex.pop(TWIN_DIFF_INPUT)
        return grad_fn(weights, diff, {**shared, **ex}, loss_target)

    if N_MICROBATCH == 1:
        loss, (grad_w, grad_x) = one_microbatch(per_example, given["loss_target"])
    else:
        def body(carry, xs):
            loss_sum, grad_sum = carry
            l_k, (gw_k, gx_k) = one_microbatch(xs[0], xs[1])
            with _jax.named_scope("update"):
                return (loss_sum + l_k, _jax.tree.map(_jnp.add, grad_sum, gw_k)), gx_k

        init = (_jnp.zeros((), _jnp.float32), _jax.tree.map(_jnp.zeros_like, weights))
        (loss, grad_w), grad_x = _jax.lax.scan(body, init, (per_example, given["loss_target"]))
    with _jax.named_scope("update"):
        delta_w, new_m, new_v = {}, {}, {}
        for n in TWIN_WEIGHTS:
            delta_w[n], new_m[n], new_v[n] = _adamw(weights[n], grad_w[n], given["m_" + n], given["v_" + n])
    return (loss, grad_x, *[grad_w[n] for n in TWIN_WEIGHTS], *[delta_w[n] for n in TWIN_WEIGHTS],
            *[new_m[n] for n in TWIN_WEIGHTS], *[new_v[n] for n in TWIN_WEIGHTS])
```

```python
import functools
import math

import jax
import jax.numpy as jnp
from jax import lax
from jax.experimental import pallas as pl
from jax.experimental.pallas import tpu as pltpu

F32 = jnp.float32
BF16 = jnp.bfloat16

D = 1024
CW = 31
HD = 64
NQ = 16
NKV = 2
GROUP = NQ // NKV
BLK = 128
DFF = 2816
EPS = 1e-6
NEG = -1e30
IN_COLS = 5376
SCALE = 1.0 / math.sqrt(HD)

LR, B1, B2, AEPS, WD, STEP = 0.001, 0.9, 0.999, 1e-08, 0.01, 10

MIB = 1024 * 1024
MESH = pl.DeviceIdType.MESH

ROWS_W_IN = 1344
ROWS_SQ = 256
ROWS_FFN_IN = 1408
ROWS_DOWN = 704
ROWS_MAT = ROWS_W_IN + 3 * ROWS_SQ + ROWS_FFN_IN + ROWS_DOWN
ROWS_DW = 32
ROWS_PACK = ROWS_MAT + ROWS_DW
VEC_ROWS = 40


def _sig(x):
    return 1.0 / (1.0 + jnp.exp(-x))


def _dot(a, b):
    return jnp.dot(a, b, preferred_element_type=F32)


def _dot_nt(a, b):
    return lax.dot_general(a, b, (((1,), (1,)), ((), ())), preferred_element_type=F32)


def _dot_tn(a, b):
    return lax.dot_general(a, b, (((0,), (0,)), ((), ())), preferred_element_type=F32)


def _params(sem, vmem_mib=48):
    return pltpu.CompilerParams(dimension_semantics=sem, vmem_limit_bytes=vmem_mib * MIB)


def _resident(shape):
    return pl.BlockSpec(shape, lambda *_: (0,) * len(shape), pipeline_mode=pl.Buffered(1))


def _whole(shape):
    return pl.BlockSpec(shape, lambda *_: (0,) * len(shape))


def _rows(tm, cols, col_block=0):
    return pl.BlockSpec((tm, cols), lambda i: (i, col_block))


def _in_proj(x2, g1, w_in):
    T = x2.shape[0]
    tm = min(256, T)

    def body(x_ref, g_ref, w_ref, xn_ref, ag_ref, q_ref, kv_ref, gg_ref):
        x = x_ref[...]
        rstd = lax.rsqrt(jnp.mean(x * x, axis=-1, keepdims=True) + EPS)
        xn = (x * rstd * g_ref[...]).astype(BF16)
        xn_ref[...] = xn
        ag_ref[...] = _dot(xn, w_ref[:, 0:2048])
        q_ref[...] = _dot(xn, w_ref[:, 2048:3072])
        kv_ref[...] = _dot(xn, w_ref[:, 3072:3328])
        gg_ref[...] = _dot(xn, w_ref[:, 3328:5376])

    return pl.pallas_call(
        body, name="in_proj", grid=(T // tm,),
        in_specs=[_rows(tm, D), _resident((1, D)), _resident((D, IN_COLS))],
        out_specs=[_rows(tm, D), _rows(tm, 2048), _rows(tm, D), _rows(tm, 256), _rows(tm, 2048)],
        out_shape=[jax.ShapeDtypeStruct((T, D), BF16), jax.ShapeDtypeStruct((T, 2048), F32),
                   jax.ShapeDtypeStruct((T, D), F32), jax.ShapeDtypeStruct((T, 256), F32),
                   jax.ShapeDtypeStruct((T, 2048), F32)],
        compiler_params=_params(("parallel",)),
    )(x2, g1, w_in)


def _conv_fwd(ag, dw, dwb, lng, lnb, wc, n_seq, S):
    T = n_seq * S
    tc = min(256, S)
    nt = S // tc

    def body(a_ref, gt_ref, dw_ref, dwb_ref, lng_ref, lnb_ref, wc_ref, h1_ref, h3_ref, yc_ref, ext):
        i = pl.program_id(1)

        @pl.when(i == 0)
        def _():
            ext[0:32, :] = jnp.zeros((32, D), F32)

        ext[32:32 + tc, :] = a_ref[...] * _sig(gt_ref[...])
        for rb in range(tc // 128):
            for cb in range(D // 128):
                cs = slice(cb * 128, (cb + 1) * 128)
                acc = jnp.broadcast_to(dwb_ref[:, cs], (128, 128))
                for j in range(CW):
                    r0 = rb * 128 + 2 + j
                    acc = acc + dw_ref[j:j + 1, cs] * ext[r0:r0 + 128, cs]
                h1_ref[rb * 128:(rb + 1) * 128, cs] = acc
        ext[0:32, :] = ext[tc:tc + 32, :]
        h1 = h1_ref[...]
        mu = jnp.mean(h1, axis=-1, keepdims=True)
        cen = h1 - mu
        var = jnp.mean(cen * cen, axis=-1, keepdims=True)
        h2 = cen * lax.rsqrt(var + EPS) * lng_ref[...] + lnb_ref[...]
        h3 = (h2 * _sig(h2)).astype(BF16)
        h3_ref[...] = h3
        yc_ref[...] = _dot(h3, wc_ref[...])

    tile = lambda col: pl.BlockSpec((tc, D), lambda b, i: (b * nt + i, col))
    return pl.pallas_call(
        body, name="conv_fwd", grid=(n_seq, nt),
        in_specs=[tile(0), tile(1), _resident((32, D)), _resident((1, D)), _resident((1, D)), _resident((1, D)),
                  _resident((D, D))],
        out_specs=[tile(0), tile(0), tile(0)],
        out_shape=[jax.ShapeDtypeStruct((T, D), F32), jax.ShapeDtypeStruct((T, D), BF16),
                   jax.ShapeDtypeStruct((T, D), F32)],
        scratch_shapes=[pltpu.VMEM((32 + tc, D), F32)],
        compiler_params=_params(("parallel", "arbitrary")),
    )(ag, ag, dw, dwb, lng, lnb, wc)


def _attn_masks(n):
    qi = lax.broadcasted_iota(jnp.int32, (GROUP * BLK, 2 * BLK), 0) % BLK
    sj = lax.broadcasted_iota(jnp.int32, (GROUP * BLK, 2 * BLK), 1)
    dist = qi + BLK - sj
    first_key = jnp.where(n > 0, 0, BLK)
    valid = (dist >= 0) & (dist < BLK) & (sj >= first_key)
    return valid, dist.astype(F32)


def _head_columns(kh, sink_ref):
    head = lax.broadcasted_iota(jnp.int32, (GROUP * BLK, 1), 0) // BLK
    slope = jnp.exp((head + (kh * GROUP + 1)).astype(F32) * (-8.0 / NQ * math.log(2.0)))
    sink = jnp.zeros((GROUP * BLK, 1), F32)
    for g in range(GROUP):
        sink = jnp.where(head == g, sink_ref[kh * GROUP + g], sink)
    return slope, sink


def _stack_heads(ref, kh):
    return jnp.concatenate([ref[:, (kh * GROUP + g) * HD:(kh * GROUP + g + 1) * HD] for g in range(GROUP)], axis=0)


def _rms64(t):
    return lax.rsqrt(jnp.mean(t * t, axis=-1, keepdims=True) + EPS)


def _attn_probs(kh, n, sink_ref, q_ref, kvc_ref, kvp_ref, qg_ref, kg_ref):
    valid, distf = _attn_masks(n)
    slope, sink = _head_columns(kh, sink_ref)
    ks = slice(kh * HD, (kh + 1) * HD)
    vs = slice(2 * HD + kh * HD, 2 * HD + (kh + 1) * HD)
    kcat = jnp.concatenate([kvp_ref[:, ks], kvc_ref[:, ks]], axis=0)
    vcat = jnp.concatenate([kvp_ref[:, vs], kvc_ref[:, vs]], axis=0)
    ky = kcat * _rms64(kcat)
    khb = (ky * kg_ref[...]).astype(BF16)
    qs = _stack_heads(q_ref, kh)
    rq = _rms64(qs)
    qy = qs * rq
    qhb = (qy * qg_ref[...]).astype(BF16)
    s = _dot_nt(qhb, khb) * SCALE
    s = jnp.where(valid, s - slope * distf, NEG)
    m = jnp.maximum(jnp.max(s, axis=-1, keepdims=True), sink)
    e = jnp.exp(s - m)
    es = jnp.exp(sink - m)
    z = jnp.sum(e, axis=-1, keepdims=True) + es
    p = e / z
    return p, es / z, qhb, khb, vcat.astype(BF16), qy, rq, ky, kcat


def _attn_specs(n_seq, S):
    nb = S // BLK
    cur = lambda cols: pl.BlockSpec((BLK, cols), lambda b, n: (b * nb + n, 0))
    prev = lambda cols: pl.BlockSpec((BLK, cols), lambda b, n: (b * nb + jnp.maximum(n - 1, 0), 0))
    return nb, cur, prev


def _attn_fwd(q, kv, qg, kg, sinks, n_seq, S):
    T = n_seq * S
    nb, cur, prev = _attn_specs(n_seq, S)

    def body(sink_ref, q_ref, kvc_ref, kvp_ref, qg_ref, kg_ref, o_ref):
        n = pl.program_id(1)
        for kh in range(NKV):
            p, _, _, _, vb, _, _, _, _ = _attn_probs(kh, n, sink_ref, q_ref, kvc_ref, kvp_ref, qg_ref, kg_ref)
            o = _dot(p.astype(BF16), vb).astype(BF16)
            for g in range(GROUP):
                h = kh * GROUP + g
                o_ref[:, h * HD:(h + 1) * HD] = o[g * BLK:(g + 1) * BLK]

    return pl.pallas_call(
        body, name="attn_fwd", grid=(n_seq, nb),
        in_specs=[pl.BlockSpec(memory_space=pltpu.SMEM), cur(D), cur(256), prev(256), _resident((1, HD)),
                  _resident((1, HD))],
        out_specs=cur(D),
        out_shape=jax.ShapeDtypeStruct((T, D), BF16),
        compiler_params=_params(("parallel", "parallel")),
    )(sinks, q, kv, kv, qg, kg)


def _merge_fwd(x2, gg, yc, o, wa, wm):
    T = x2.shape[0]
    tm = min(256, T)

    def body(x_ref, gg_ref, yc_ref, o_ref, wa_ref, wm_ref, ya_ref, mix_ref, h_ref):
        ya = _dot(o_ref[...], wa_ref[...])
        mix = (_sig(gg_ref[:, 0:D]) * yc_ref[...] + _sig(gg_ref[:, D:2 * D]) * ya).astype(BF16)
        ya_ref[...] = ya
        mix_ref[...] = mix
        h_ref[...] = x_ref[...] + _dot(mix, wm_ref[...])

    return pl.pallas_call(
        body, name="merge_fwd", grid=(T // tm,),
        in_specs=[_rows(tm, D), _rows(tm, 2 * D), _rows(tm, D), _rows(tm, D), _resident((D, D)), _resident((D, D))],
        out_specs=[_rows(tm, D), _rows(tm, D), _rows(tm, D)],
        out_shape=[jax.ShapeDtypeStruct((T, D), F32), jax.ShapeDtypeStruct((T, D), BF16),
                   jax.ShapeDtypeStruct((T, D), F32)],
        compiler_params=_params(("parallel",)),
    )(x2, gg, yc, o, wa, wm)


FF_CHUNK = DFF // 2


def _ffn(h, tgt, g2, wf, wd):
    T = h.shape[0]
    tm = min(256, T)

    def body(h_ref, t_ref, g_ref, wf_ref, wd_ref, dh_ref, dhb_ref, hn_ref, act_ref, dout_ref, dgu_ref, st_ref,
             gsc, usc):
        @pl.when(pl.program_id(0) == 0)
        def _():
            st_ref[...] = jnp.zeros((8, D), F32)

        hh = h_ref[...]
        rstd = lax.rsqrt(jnp.mean(hh * hh, axis=-1, keepdims=True) + EPS)
        hhat = hh * rstd
        hn = (hhat * g_ref[...]).astype(BF16)
        hn_ref[...] = hn
        out = hh
        for c in range(DFF // FF_CHUNK):
            cs = slice(c * FF_CHUNK, (c + 1) * FF_CHUNK)
            us = slice(DFF + c * FF_CHUNK, DFF + (c + 1) * FF_CHUNK)
            g = _dot(hn, wf_ref[:, cs])
            u = _dot(hn, wf_ref[:, us])
            gsc[:, cs] = g
            usc[:, cs] = u
            act = (g * _sig(g) * u).astype(BF16)
            act_ref[:, cs] = act
            out = out + _dot(act, wd_ref[cs, :])
        err = out - t_ref[...]
        dout = err * (1.0 / D)
        doutb = dout.astype(BF16)
        dout_ref[...] = doutb
        dhn = jnp.zeros((tm, D), F32)
        for c in range(DFF // FF_CHUNK):
            cs = slice(c * FF_CHUNK, (c + 1) * FF_CHUNK)
            us = slice(DFF + c * FF_CHUNK, DFF + (c + 1) * FF_CHUNK)
            g = gsc[:, cs]
            u = usc[:, cs]
            dact = _dot_nt(doutb, wd_ref[cs, :])
            sg = _sig(g)
            dg = (dact * u * (sg * (1.0 + g * (1.0 - sg)))).astype(BF16)
            du = (dact * (g * sg)).astype(BF16)
            dgu_ref[:, cs] = dg
            dgu_ref[:, us] = du
            dhn = dhn + _dot_nt(dg, wf_ref[:, cs]) + _dot_nt(du, wf_ref[:, us])
        st_ref[0:1, :] += jnp.sum(dhn * hhat, axis=0, keepdims=True)
        st_ref[1:2, :] += jnp.sum(err * err, axis=0, keepdims=True)
        dhh = dhn * g_ref[...]
        dh = dout + rstd * (dhh - hhat * jnp.mean(dhh * hhat, axis=-1, keepdims=True))
        dh_ref[...] = dh
        dhb_ref[...] = dh.astype(BF16)

    return pl.pallas_call(
        body, name="ffn_fwd_bwd", grid=(T // tm,),
        in_specs=[_rows(tm, D), _rows(tm, D), _resident((1, D)), _resident((D, 2 * DFF)), _resident((DFF, D))],
        out_specs=[_rows(tm, D), _rows(tm, D), _rows(tm, D), _rows(tm, DFF), _rows(tm, D), _rows(tm, 2 * DFF),
                   _whole((8, D))],
        out_shape=[jax.ShapeDtypeStruct((T, D), F32), jax.ShapeDtypeStruct((T, D), BF16),
                   jax.ShapeDtypeStruct((T, D), BF16), jax.ShapeDtypeStruct((T, DFF), BF16),
                   jax.ShapeDtypeStruct((T, D), BF16), jax.ShapeDtypeStruct((T, 2 * DFF), BF16),
                   jax.ShapeDtypeStruct((8, D), F32)],
        scratch_shapes=[pltpu.VMEM((tm, DFF), F32), pltpu.VMEM((tm, DFF), F32)],
        compiler_params=_params(("arbitrary",), 56),
    )(h, tgt, g2, wf, wd)


def _merge_bwd(dhb, gg, yc, ya, wm, wa, wc):
    T = dhb.shape[0]
    tm = min(256, T)

    def body(dh_ref, gg_ref, yc_ref, ya_ref, wm_ref, wa_ref, wc_ref, dgg_ref, dyc_ref, dya_ref, do_ref, dh3_ref):
        dmix = _dot_nt(dh_ref[...], wm_ref[...])
        gc = _sig(gg_ref[:, 0:D])
        ga = _sig(gg_ref[:, D:2 * D])
        yc = yc_ref[...]
        ya = ya_ref[...]
        dgg_ref[:, 0:D] = (dmix * yc * gc * (1.0 - gc)).astype(BF16)
        dgg_ref[:, D:2 * D] = (dmix * ya * ga * (1.0 - ga)).astype(BF16)
        dyc = (dmix * gc).astype(BF16)
        dya = (dmix * ga).astype(BF16)
        dyc_ref[...] = dyc
        dya_ref[...] = dya
        do_ref[...] = _dot_nt(dya, wa_ref[...])
        dh3_ref[...] = _dot_nt(dyc, wc_ref[...])

    return pl.pallas_call(
        body, name="merge_bwd", grid=(T // tm,),
        in_specs=[_rows(tm, D), _rows(tm, 2 * D), _rows(tm, D), _rows(tm, D), _resident((D, D)), _resident((D, D)),
                  _resident((D, D))],
        out_specs=[_rows(tm, 2 * D), _rows(tm, D), _rows(tm, D), _rows(tm, D), _rows(tm, D)],
        out_shape=[jax.ShapeDtypeStruct((T, 2 * D), BF16), jax.ShapeDtypeStruct((T, D), BF16),
                   jax.ShapeDtypeStruct((T, D), BF16), jax.ShapeDtypeStruct((T, D), F32),
                   jax.ShapeDtypeStruct((T, D), F32)],
        compiler_params=_params(("parallel",)),
    )(dhb, gg, yc, ya, wm, wa, wc)


def _attn_bwd(q, kv, do, qg, kg, sinks, n_seq, S):
    T = n_seq * S
    nb, cur, prev = _attn_specs(n_seq, S)

    def body(sink_ref, q_ref, kvc_ref, kvp_ref, do_ref, qg_ref, kg_ref, dq_ref, dkc_ref, dkp_ref, dqg_ref, dsk_ref):
        n = pl.program_id(1)

        @pl.when((pl.program_id(0) == 0) & (n == 0))
        def _():
            dqg_ref[...] = jnp.zeros((1, HD), F32)
            dsk_ref[...] = jnp.zeros((8, 128), F32)

        lane = lax.broadcasted_iota(jnp.int32, (1, 128), 1)
        for kh in range(NKV):
            p, ps, qhb, khb, vb, qy, rq, _, _ = _attn_probs(kh, n, sink_ref, q_ref, kvc_ref, kvp_ref, qg_ref, kg_ref)
            dob = _stack_heads(do_ref, kh).astype(BF16)
            dp = _dot_nt(dob, vb)
            delta = jnp.sum(p * dp, axis=-1, keepdims=True)
            dsc = (p * (dp - delta) * SCALE).astype(BF16)
            dsink = -ps * delta
            dv = _dot_tn(p.astype(BF16), dob)
            dkh = _dot_tn(dsc, qhb)
            dqh = _dot(dsc, khb)
            dqg_ref[...] += jnp.sum(dqh * qy, axis=0, keepdims=True)
            dy = dqh * qg_ref[...]
            dq = (rq * (dy - qy * jnp.mean(dy * qy, axis=-1, keepdims=True))).astype(BF16)
            row = jnp.zeros((1, 128), F32)
            for g in range(GROUP):
                h = kh * GROUP + g
                dq_ref[:, h * HD:(h + 1) * HD] = dq[g * BLK:(g + 1) * BLK]
                row = row + jnp.where(lane == h, jnp.sum(dsink[g * BLK:(g + 1) * BLK], axis=0, keepdims=True), 0.0)
            dsk_ref[0:1, :] += row
            ks = slice(kh * HD, (kh + 1) * HD)
            vs = slice(2 * HD + kh * HD, 2 * HD + (kh + 1) * HD)
            dkp_ref[:, ks] = dkh[0:BLK]
            dkc_ref[:, ks] = dkh[BLK:2 * BLK]
            dkp_ref[:, vs] = dv[0:BLK]
            dkc_ref[:, vs] = dv[BLK:2 * BLK]

    return pl.pallas_call(
        body, name="attn_bwd", grid=(n_seq, nb),
        in_specs=[pl.BlockSpec(memory_space=pltpu.SMEM), cur(D), cur(256), prev(256), cur(D), _resident((1, HD)),
                  _resident((1, HD))],
        out_specs=[cur(D), cur(256), cur(256), _whole((1, HD)), _whole((8, 128))],
        out_shape=[jax.ShapeDtypeStruct((T, D), BF16), jax.ShapeDtypeStruct((T, 256), F32),
                   jax.ShapeDtypeStruct((T, 256), F32), jax.ShapeDtypeStruct((1, HD), F32),
                   jax.ShapeDtypeStruct((8, 128), F32)],
        compiler_params=_params(("arbitrary", "arbitrary")),
    )(sinks, q, kv, kv, do, qg, kg)


def _kv_bwd(kv, dkc, dkp, kg, n_seq, S):
    T = n_seq * S
    nb, cur, _ = _attn_specs(n_seq, S)
    nxt = pl.BlockSpec((BLK, 256), lambda b, n: (b * nb + jnp.minimum(n + 1, nb - 1), 0))

    def body(kv_ref, dkc_ref, dkp_ref, kg_ref, dkv_ref, dkg_ref):
        n = pl.program_id(1)

        @pl.when((pl.program_id(0) == 0) & (n == 0))
        def _():
            dkg_ref[...] = jnp.zeros((1, HD), F32)

        has_next = jnp.where(n < nb - 1, 1.0, 0.0)
        d = dkc_ref[...] + has_next * dkp_ref[...]
        for kh in range(NKV):
            ks = slice(kh * HD, (kh + 1) * HD)
            k = kv_ref[:, ks]
            r = _rms64(k)
            y = k * r
            dkh = d[:, ks]
            dkg_ref[...] += jnp.sum(dkh * y, axis=0, keepdims=True)
            dy = dkh * kg_ref[...]
            dkv_ref[:, ks] = (r * (dy - y * jnp.mean(dy * y, axis=-1, keepdims=True))).astype(BF16)
        dkv_ref[:, 2 * HD:4 * HD] = d[:, 2 * HD:4 * HD].astype(BF16)

    return pl.pallas_call(
        body, name="kv_bwd", grid=(n_seq, nb),
        in_specs=[cur(256), cur(256), nxt, _resident((1, HD))],
        out_specs=[cur(256), _whole((1, HD))],
        out_shape=[jax.ShapeDtypeStruct((T, 256), BF16), jax.ShapeDtypeStruct((1, HD), F32)],
        compiler_params=_params(("arbitrary", "arbitrary")),
    )(kv, dkc, dkp, kg)


def _conv_bwd(h1, dh3, ag, dw, lng, lnb, n_seq, S):
    T = n_seq * S
    tc = min(256, S)
    nt = S // tc
    halo_blocks = tc // 32

    def body(h1_ref, dh3_ref, a_ref, gt_ref, ah_ref, gh_ref, dw_ref, lng_ref, lnb_ref, dag_ref, ddw_ref, st_ref,
             extd, ext0):
        i = pl.program_id(1)
        ti = nt - 1 - i

        @pl.when((pl.program_id(0) == 0) & (i == 0))
        def _():
            ddw_ref[...] = jnp.zeros((32, D), F32)
            st_ref[...] = jnp.zeros((8, D), F32)

        @pl.when(i == 0)
        def _():
            extd[tc:tc + 32, :] = jnp.zeros((32, D), F32)

        h1 = h1_ref[...]
        mu = jnp.mean(h1, axis=-1, keepdims=True)
        cen = h1 - mu
        rstd = lax.rsqrt(jnp.mean(cen * cen, axis=-1, keepdims=True) + EPS)
        xh = cen * rstd
        h2 = xh * lng_ref[...] + lnb_ref[...]
        sg = _sig(h2)
        dh2 = dh3_ref[...] * (sg * (1.0 + h2 * (1.0 - sg)))
        st_ref[1:2, :] += jnp.sum(dh2 * xh, axis=0, keepdims=True)
        st_ref[2:3, :] += jnp.sum(dh2, axis=0, keepdims=True)
        dxh = dh2 * lng_ref[...]
        dh1 = rstd * (dxh - jnp.mean(dxh, axis=-1, keepdims=True)
                      - xh * jnp.mean(dxh * xh, axis=-1, keepdims=True))
        st_ref[0:1, :] += jnp.sum(dh1, axis=0, keepdims=True)
        extd[0:tc, :] = dh1
        has_past = jnp.where(ti > 0, 1.0, 0.0)
        ext0[0:32, :] = has_past * (ah_ref[...] * _sig(gh_ref[...]))
        ext0[32:32 + tc, :] = a_ref[...] * _sig(gt_ref[...])
        for cb in range(D // 128):
            cs = slice(cb * 128, (cb + 1) * 128)
            taps = [jnp.zeros((1, 128), F32) for _ in range(CW)]
            for rb in range(tc // 128):
                rs = slice(rb * 128, (rb + 1) * 128)
                d = extd[rs, cs]
                acc = jnp.zeros((128, 128), F32)
                for j in range(CW):
                    rd = rb * 128 + (CW - 1) - j
                    acc = acc + dw_ref[j:j + 1, cs] * extd[rd:rd + 128, cs]
                    r0 = rb * 128 + 2 + j
                    taps[j] = taps[j] + jnp.sum(d * ext0[r0:r0 + 128, cs], axis=0, keepdims=True)
                sgt = _sig(gt_ref[rs, cs])
                dag_ref[rs, cs] = (acc * sgt).astype(BF16)
                dag_ref[rs, cb * 128 + D:(cb + 1) * 128 + D] = (acc * a_ref[rs, cs] * sgt * (1.0 - sgt)).astype(BF16)
            ddw_ref[:, cs] += jnp.concatenate(taps + [jnp.zeros((1, 128), F32)], axis=0)
        extd[tc:tc + 32, :] = extd[0:32, :]

    tile = lambda col: pl.BlockSpec((tc, D), lambda b, i: (b * nt + (nt - 1 - i), col))
    halo = lambda col: pl.BlockSpec(
        (32, D), lambda b, i: (jnp.maximum((b * nt + (nt - 1 - i)) * halo_blocks - 1, 0), col))
    return pl.pallas_call(
        body, name="conv_bwd", grid=(n_seq, nt),
        in_specs=[tile(0), tile(0), tile(0), tile(1), halo(0), halo(1), _resident((32, D)), _resident((1, D)),
                  _resident((1, D))],
        out_specs=[pl.BlockSpec((tc, 2 * D), lambda b, i: (b * nt + (nt - 1 - i), 0)), _whole((32, D)),
                   _whole((8, D))],
        out_shape=[jax.ShapeDtypeStruct((T, 2 * D), BF16), jax.ShapeDtypeStruct((32, D), F32),
                   jax.ShapeDtypeStruct((8, D), F32)],
        scratch_shapes=[pltpu.VMEM((tc + 32, D), F32), pltpu.VMEM((32 + tc, D), F32)],
        compiler_params=_params(("arbitrary", "arbitrary")),
    )(h1, dh3, ag, ag, ag, ag, dw, lng, lnb)


def _in_proj_bwd(dag, dq, dkv, dgg, dh, x2, g1, w_in):
    T = x2.shape[0]
    tm = min(256, T)

    def body(dag_ref, dq_ref, dkv_ref, dgg_ref, dh_ref, x_ref, g_ref, w_ref, dx_ref, dg_ref):
        @pl.when(pl.program_id(0) == 0)
        def _():
            dg_ref[...] = jnp.zeros((1, D), F32)

        dxn = (_dot_nt(dag_ref[...], w_ref[:, 0:2048]) + _dot_nt(dq_ref[...], w_ref[:, 2048:3072])
               + _dot_nt(dkv_ref[...], w_ref[:, 3072:3328]) + _dot_nt(dgg_ref[...], w_ref[:, 3328:5376]))
        x = x_ref[...]
        rstd = lax.rsqrt(jnp.mean(x * x, axis=-1, keepdims=True) + EPS)
        xh = x * rstd
        dg_ref[...] += jnp.sum(dxn * xh, axis=0, keepdims=True)
        dxh = dxn * g_ref[...]
        dx_ref[...] = dh_ref[...] + rstd * (dxh - xh * jnp.mean(dxh * xh, axis=-1, keepdims=True))

    return pl.pallas_call(
        body, name="in_proj_bwd", grid=(T // tm,),
        in_specs=[_rows(tm, 2 * D), _rows(tm, D), _rows(tm, 256), _rows(tm, 2 * D), _rows(tm, D), _rows(tm, D),
                  _resident((1, D)), _resident((D, IN_COLS))],
        out_specs=[_rows(tm, D), _whole((1, D))],
        out_shape=[jax.ShapeDtypeStruct((T, D), F32), jax.ShapeDtypeStruct((1, D), F32)],
        compiler_params=_params(("arbitrary",)),
    )(dag, dq, dkv, dgg, dh, x2, g1, w_in)


def _tn_matmul(a, b, name):
    T, K = a.shape
    N = b.shape[1]
    tk = K if K <= 1024 else K // 2
    tn = N if N <= 1024 else (1024 if N % 1024 == 0 else N // 4)
    tt = min(512, T)
    assert K % tk == 0 and N % tn == 0 and T % tt == 0 and tk % 128 == 0 and tn % 128 == 0

    def body(a_ref, b_ref, o_ref):
        @pl.when(pl.program_id(2) == 0)
        def _():
            o_ref[...] = jnp.zeros((tk, tn), F32)

        o_ref[...] += _dot_tn(a_ref[...], b_ref[...])

    return pl.pallas_call(
        body, name=name, grid=(K // tk, N // tn, T // tt),
        in_specs=[pl.BlockSpec((tt, tk), lambda i, j, t: (t, i)), pl.BlockSpec((tt, tn), lambda i, j, t: (t, j))],
        out_specs=pl.BlockSpec((tk, tn), lambda i, j, t: (i, j)),
        out_shape=jax.ShapeDtypeStruct((K, N), F32),
        compiler_params=_params(("parallel", "parallel", "arbitrary")),
    )(a, b)


def _place():
    x, y, c = lax.axis_index("x"), lax.axis_index("y"), lax.axis_index("c")
    chips = [(1 - x, y), (x, 1 - y), (1 - x, 1 - y)]
    return x, y, c, chips


def _gather_weights(pack):
    rows = pack.shape[0]
    half = rows // 2

    def body(src, dst, local_sem, send_sems, recv_sems):
        x, y, c, chips = _place()
        mine = pltpu.make_async_copy(src, dst.at[2 * x + y], local_sem)
        mine.start()

        def piece(px, py, pc):
            return dst.at[2 * px + py, pl.ds(pc * half, half), :]

        def copy(k, block, to, from_src=False):
            return pltpu.make_async_remote_copy(
                src_ref=src.at[pl.ds(c * half, half), :] if from_src else piece(*block), dst_ref=piece(*block),
                send_sem=send_sems.at[k], recv_sem=recv_sems.at[k], device_id=to, device_id_type=MESH)

        first = [copy(k, (x, y, c), (*chip, c), from_src=True) for k, chip in enumerate(chips)]
        for cp in first:
            cp.start()
        passed = [copy(3 + k, (*chip, c), (x, y, 1 - c)) for k, chip in enumerate(chips)]
        for k, chip in enumerate(chips):
            copy(k, (*chip, c), (x, y, c)).wait_recv()
            passed[k].start()
        for k, chip in enumerate(chips):
            copy(3 + k, (*chip, 1 - c), (x, y, c)).wait_recv()
        for cp in first + passed:
            cp.wait_send()
        mine.wait()

    return pl.pallas_call(
        body, name="gather_weights",
        in_specs=[pl.BlockSpec(memory_space=pl.ANY)], out_specs=pl.BlockSpec(memory_space=pl.ANY),
        out_shape=jax.ShapeDtypeStruct((4, rows, D), pack.dtype),
        scratch_shapes=[pltpu.SemaphoreType.DMA, pltpu.SemaphoreType.DMA((6,)), pltpu.SemaphoreType.DMA((6,))],
        compiler_params=pltpu.CompilerParams(has_side_effects=True),
    )(pack)


def _swap_halves(g):
    rows = g.shape[1]
    half = rows // 2

    def body(src, dst, send_sem, recv_sem):
        x, y, c, _ = _place()
        cp = pltpu.make_async_remote_copy(
            src_ref=src.at[:, pl.ds((1 - c) * half, half), :], dst_ref=dst, send_sem=send_sem, recv_sem=recv_sem,
            device_id=(x, y, 1 - c), device_id_type=MESH)
        cp.start()
        cp.wait()

    return pl.pallas_call(
        body, name="grad_swap_halves",
        in_specs=[pl.BlockSpec(memory_space=pl.ANY)], out_specs=pl.BlockSpec(memory_space=pl.ANY),
        out_shape=jax.ShapeDtypeStruct((4, half, D), g.dtype),
        scratch_shapes=[pltpu.SemaphoreType.DMA, pltpu.SemaphoreType.DMA],
        compiler_params=pltpu.CompilerParams(has_side_effects=True),
    )(g)


def _add_halves(g, got, c_idx):
    rows = g.shape[1]
    half = rows // 2
    tr = half // 4
    assert half % 4 == 0 and tr % 16 == 0

    def body(c_ref, g_ref, r_ref, o_ref):
        o_ref[...] = (g_ref[...] + r_ref[...]).astype(BF16)

    return pl.pallas_call(
        body, name="grad_add_halves",
        grid_spec=pltpu.PrefetchScalarGridSpec(
            num_scalar_prefetch=1, grid=(4, 4),
            in_specs=[pl.BlockSpec((1, tr, D), lambda q, i, c_ref: (q, c_ref[0] * 4 + i, 0)),
                      pl.BlockSpec((1, tr, D), lambda q, i, c_ref: (q, i, 0))],
            out_specs=pl.BlockSpec((1, tr, D), lambda q, i, c_ref: (q, i, 0))),
        out_shape=jax.ShapeDtypeStruct((4, half, D), BF16),
        compiler_params=_params(("parallel", "parallel")),
    )(c_idx, g, got)


def _scatter_chips(p):
    half = p.shape[1]

    def body(src, dst, local_sem, send_sems, recv_sems):
        x, y, c, chips = _place()
        me = 2 * x + y
        mine = pltpu.make_async_copy(src.at[me], dst.at[me], local_sem)
        mine.start()
        sends = [pltpu.make_async_remote_copy(
            src_ref=src.at[2 * cx + cy], dst_ref=dst.at[me], send_sem=send_sems.at[k], recv_sem=recv_sems.at[k],
            device_id=(cx, cy, c), device_id_type=MESH) for k, (cx, cy) in enumerate(chips)]
        for cp in sends:
            cp.start()
        for k, (cx, cy) in enumerate(chips):
            pltpu.make_async_remote_copy(
                src_ref=src.at[me], dst_ref=dst.at[2 * cx + cy], send_sem=send_sems.at[k], recv_sem=recv_sems.at[k],
                device_id=(cx, cy, c), device_id_type=MESH).wait_recv()
        for cp in sends:
            cp.wait_send()
        mine.wait()

    return pl.pallas_call(
        body, name="grad_scatter_chips",
        in_specs=[pl.BlockSpec(memory_space=pl.ANY)], out_specs=pl.BlockSpec(memory_space=pl.ANY),
        out_shape=jax.ShapeDtypeStruct((4, half, D), p.dtype),
        scratch_shapes=[pltpu.SemaphoreType.DMA, pltpu.SemaphoreType.DMA((3,)), pltpu.SemaphoreType.DMA((3,))],
        compiler_params=pltpu.CompilerParams(has_side_effects=True),
    )(p)


def _sum_chips(r):
    half = r.shape[1]
    tr = half // 4

    def body(r_ref, o_ref):
        acc = r_ref[0].astype(F32)
        for q in range(1, 4):
            acc = acc + r_ref[q].astype(F32)
        o_ref[...] = acc

    return pl.pallas_call(
        body, name="grad_sum_chips", grid=(4,),
        in_specs=[pl.BlockSpec((4, tr, D), lambda i: (0, i, 0))],
        out_specs=pl.BlockSpec((tr, D), lambda i: (i, 0)),
        out_shape=jax.ShapeDtypeStruct((half, D), F32),
        compiler_params=_params(("parallel",)),
    )(r)


def _join_halves(f):
    half = f.shape[0]

    def body(src, dst, local_sem, send_sem, recv_sem):
        x, y, c, _ = _place()
        mine = pltpu.make_async_copy(src, dst.at[pl.ds(c * half, half), :], local_sem)
        mine.start()
        cp = pltpu.make_async_remote_copy(
            src_ref=src, dst_ref=dst.at[pl.ds(c * half, half), :], send_sem=send_sem, recv_sem=recv_sem,
            device_id=(x, y, 1 - c), device_id_type=MESH)
        cp.start()
        pltpu.make_async_remote_copy(
            src_ref=src, dst_ref=dst.at[pl.ds((1 - c) * half, half), :], send_sem=send_sem, recv_sem=recv_sem,
            device_id=(x, y, 1 - c), device_id_type=MESH).wait_recv()
        cp.wait_send()
        mine.wait()

    return pl.pallas_call(
        body, name="grad_join_halves",
        in_specs=[pl.BlockSpec(memory_space=pl.ANY)], out_specs=pl.BlockSpec(memory_space=pl.ANY),
        out_shape=jax.ShapeDtypeStruct((2 * half, D), f.dtype),
        scratch_shapes=[pltpu.SemaphoreType.DMA, pltpu.SemaphoreType.DMA, pltpu.SemaphoreType.DMA],
        compiler_params=pltpu.CompilerParams(has_side_effects=True),
    )(f)


def _allreduce_small(vec):
    def body(v_ref, o_ref, gath, send_sems, recv_sems):
        x, y, c, _ = _place()
        me = 4 * x + 2 * y + c
        gath[me] = v_ref[...]
        sends = []
        for k in range(1, 8):
            peer = (x ^ (k >> 2), y ^ ((k >> 1) & 1), c ^ (k & 1))
            sends.append(pltpu.make_async_remote_copy(
                src_ref=v_ref, dst_ref=gath.at[me], send_sem=send_sems.at[k - 1], recv_sem=recv_sems.at[k - 1],
                device_id=peer, device_id_type=MESH))
        for cp in sends:
            cp.start()
        for k in range(1, 8):
            peer = (x ^ (k >> 2), y ^ ((k >> 1) & 1), c ^ (k & 1))
            pltpu.make_async_remote_copy(
                src_ref=v_ref, dst_ref=gath.at[4 * peer[0] + 2 * peer[1] + peer[2]], send_sem=send_sems.at[k - 1],
                recv_sem=recv_sems.at[k - 1], device_id=peer, device_id_type=MESH).wait_recv()
        for cp in sends:
            cp.wait_send()
        acc = gath[0]
        for d in range(1, 8):
            acc = acc + gath[d]
        o_ref[...] = acc

    return pl.pallas_call(
        body, name="allreduce_small",
        in_specs=[pl.BlockSpec(memory_space=pltpu.VMEM)], out_specs=pl.BlockSpec(memory_space=pltpu.VMEM),
        out_shape=jax.ShapeDtypeStruct(vec.shape, F32),
        scratch_shapes=[pltpu.VMEM((8,) + vec.shape, F32), pltpu.SemaphoreType.DMA((7,)),
                        pltpu.SemaphoreType.DMA((7,))],
    )(vec)


def _adamw(w, g, m, v, name):
    shape = w.shape
    if w.ndim == 1 or w.size <= 128 * 128:
        two_d = (1, w.size) if w.size % 128 else (w.size // 128, 128)
    else:
        two_d = (w.shape[0], w.size // w.shape[0])
    rows, cols = two_d
    tr = next((t for t in (256, 176, 128) if rows % t == 0), rows)

    def body(w_ref, g_ref, m_ref, v_ref, d_ref, nm_ref, nv_ref):
        gr = g_ref[...]
        nm = B1 * m_ref[...] + (1.0 - B1) * gr
        nv = B2 * v_ref[...] + (1.0 - B2) * (gr * gr)
        m_hat = nm / (1.0 - B1 ** STEP)
        v_hat = nv / (1.0 - B2 ** STEP)
        d_ref[...] = -LR * (m_hat / (jnp.sqrt(v_hat) + AEPS) + WD * w_ref[...])
        nm_ref[...] = nm
        nv_ref[...] = nv

    spec = pl.BlockSpec((tr, cols), lambda i: (i, 0))
    outs = pl.pallas_call(
        body, name=name, grid=(rows // tr,),
        in_specs=[spec] * 4, out_specs=[spec] * 3,
        out_shape=[jax.ShapeDtypeStruct(two_d, F32)] * 3,
        compiler_params=_params(("parallel",)),
    )(*[t.reshape(two_d) for t in (w, g, m, v)])
    return [o.reshape(shape) for o in outs]


def _pack_weights(w_in, w_conv_out, w_attn_out, w_merge_out, w_ffn_in, w_ffn_down, conv_dw_w):
    dw = jnp.pad(conv_dw_w.reshape(CW, 256), ((0, 1), (0, 0)))
    dw_bits = lax.bitcast_convert_type(dw, BF16).reshape(16, D)
    mats = [w_in.reshape(ROWS_W_IN, D), w_conv_out, w_attn_out, w_merge_out, w_ffn_in.reshape(ROWS_FFN_IN, D),
            w_ffn_down]
    return jnp.concatenate([t.astype(BF16) for t in mats] + [dw_bits, jnp.zeros((16, D), BF16)], axis=0)


def _unpack_weights(g):
    def cols(lo, n_rows, width):
        return jnp.concatenate([g[q, lo:lo + n_rows].reshape(D, width) for q in range(4)], axis=1)

    def rows(lo, n_rows):
        return g[:, lo:lo + n_rows].reshape(4 * n_rows, D)

    o = 0
    w_in = cols(o, ROWS_W_IN, ROWS_W_IN); o += ROWS_W_IN
    wc = rows(o, ROWS_SQ); o += ROWS_SQ
    wa = rows(o, ROWS_SQ); o += ROWS_SQ
    wm = rows(o, ROWS_SQ); o += ROWS_SQ
    wf = cols(o, ROWS_FFN_IN, ROWS_FFN_IN); o += ROWS_FFN_IN
    wd = rows(o, ROWS_DOWN); o += ROWS_DOWN
    dw = lax.bitcast_convert_type(g[:, o:o + 16].reshape(4, 32, 256, 2), F32)
    dw = jnp.transpose(dw, (1, 0, 2)).reshape(32, D)
    return w_in, wc, wa, wm, wf, wd, dw


def _pack_grads(d_w_in, d_wc, d_wa, d_wm, d_wf, d_wd):
    def cols(t, width):
        return jnp.transpose(t.reshape(D, 4, width), (1, 0, 2)).reshape(4, width, D)

    def rows(t):
        return t.reshape(4, t.shape[0] // 4, D)

    return jnp.concatenate([cols(d_w_in, ROWS_W_IN), rows(d_wc), rows(d_wa), rows(d_wm), cols(d_wf, ROWS_FFN_IN),
                            rows(d_wd)], axis=1)


def _unpack_shard(s):
    o = 0
    out = []
    for n_rows, shape in ((ROWS_W_IN, (D, ROWS_W_IN)), (ROWS_SQ, (ROWS_SQ, D)), (ROWS_SQ, (ROWS_SQ, D)),
                          (ROWS_SQ, (ROWS_SQ, D)), (ROWS_FFN_IN, (D, ROWS_FFN_IN)), (ROWS_DOWN, (ROWS_DOWN, D))):
        out.append(s[o:o + n_rows].reshape(shape))
        o += n_rows
    return out


def _local_grads(x, loss_target, norm_mix_g, conv_dw_b, conv_ln_g, conv_ln_b, q_norm_g, k_norm_g, sinks, norm_ffn_g,
                 w_in, wc, wa, wm, wf, wd, dw):
    n_seq, S, _ = x.shape
    T = n_seq * S
    x2 = x.reshape(T, D)
    tgt = loss_target.reshape(T, D)
    row = lambda t: t.reshape(1, -1)
    g1, g2 = row(norm_mix_g), row(norm_ffn_g)
    qg, kg = row(q_norm_g), row(k_norm_g)
    lng, lnb, dwb = row(conv_ln_g), row(conv_ln_b), row(conv_dw_b)

    xn, ag, q, kv, gg = _in_proj(x2, g1, w_in)
    h1, h3, yc = _conv_fwd(ag, dw, dwb, lng, lnb, wc, n_seq, S)
    o = _attn_fwd(q, kv, qg, kg, sinks, n_seq, S)
    ya, mix, h = _merge_fwd(x2, gg, yc, o, wa, wm)
    dh, dhb, hn, act, dout, dgu, ffn_stats = _ffn(h, tgt, g2, wf, wd)
    dgg, dyc, dya, do, dh3 = _merge_bwd(dhb, gg, yc, ya, wm, wa, wc)
    dq, dkc, dkp, dqg, dsk = _attn_bwd(q, kv, do, qg, kg, sinks, n_seq, S)
    dkv, dkg = _kv_bwd(kv, dkc, dkp, kg, n_seq, S)
    dag, ddw, conv_stats = _conv_bwd(h1, dh3, ag, dw, lng, lnb, n_seq, S)
    dx, dg1 = _in_proj_bwd(dag, dq, dkv, dgg, dh, x2, g1, w_in)

    d_w_in = jnp.concatenate([_tn_matmul(xn, dag, "dw_in_conv"), _tn_matmul(xn, dq, "dw_in_q"),
                              _tn_matmul(xn, dkv, "dw_in_kv"), _tn_matmul(xn, dgg, "dw_in_gates")], axis=1)
    d_wc = _tn_matmul(h3, dyc, "dw_conv_out")
    d_wa = _tn_matmul(o, dya, "dw_attn_out")
    d_wm = _tn_matmul(mix, dhb, "dw_merge")
    d_wf = _tn_matmul(hn, dgu, "dw_ffn_in")
    d_wd = _tn_matmul(act, dout, "dw_ffn_down")

    heads = jnp.concatenate([dqg[0], dkg[0], dsk[0, :NQ], jnp.zeros((D - 2 * HD - NQ,), F32)])
    vec = jnp.concatenate([dg1, conv_stats[0:3], ffn_stats[0:1], heads[None], jnp.zeros((2, D), F32), ddw], axis=0)
    return ffn_stats[1], dx.reshape(x.shape), (d_w_in, d_wc, d_wa, d_wm, d_wf, d_wd), vec


def kernel(x, norm_mix_g, w_in, conv_dw_w, conv_dw_b, conv_ln_g, conv_ln_b, w_conv_out, q_norm_g, k_norm_g, sinks, w_attn_out, w_merge_out, norm_ffn_g, w_ffn_in, w_ffn_down, loss_target, m_norm_mix_g, m_w_in, m_conv_dw_w, m_conv_dw_b, m_conv_ln_g, m_conv_ln_b, m_w_conv_out, m_q_norm_g, m_k_norm_g, m_sinks, m_w_attn_out, m_w_merge_out, m_norm_ffn_g, m_w_ffn_in, m_w_ffn_down, v_norm_mix_g, v_w_in, v_conv_dw_w, v_conv_dw_b, v_conv_ln_g, v_conv_ln_b, v_w_conv_out, v_q_norm_g, v_k_norm_g, v_sinks, v_w_attn_out, v_w_merge_out, v_norm_ffn_g, v_w_ffn_in, v_w_ffn_down):
    chip = 2 * lax.axis_index("x") + lax.axis_index("y")
    c_idx = lax.axis_index("c").astype(jnp.int32).reshape(1)

    gathered = _gather_weights(
        _pack_weights(w_in, w_conv_out, w_attn_out, w_merge_out, w_ffn_in, w_ffn_down, conv_dw_w))
    full = _unpack_weights(gathered)

    sq_cols, grad_x, mats, vec = _local_grads(x, loss_target, norm_mix_g, conv_dw_b, conv_ln_g, conv_ln_b, q_norm_g,
                                              k_norm_g, sinks, norm_ffn_g, *full)
    loss = lax.psum(0.5 / D * jnp.sum(sq_cols), ("x", "y", "c"))

    g = _pack_grads(*mats)
    p = _add_halves(g, _swap_halves(g), c_idx)
    shard = _join_halves(_sum_chips(_scatter_chips(p)))
    g_w_in, g_wc, g_wa, g_wm, g_wf, g_wd = _unpack_shard(shard)

    small = _allreduce_small(vec)
    g_dw = lax.dynamic_slice_in_dim(small[8:8 + CW], chip * 256, 256, axis=1).reshape(CW, 1, 256)
    grads = {
        "norm_mix_g": small[0], "w_in": g_w_in, "conv_dw_w": g_dw, "conv_dw_b": small[1], "conv_ln_g": small[2],
        "conv_ln_b": small[3], "w_conv_out": g_wc, "q_norm_g": small[5, 0:HD], "k_norm_g": small[5, HD:2 * HD],
        "sinks": small[5, 2 * HD:2 * HD + NQ], "w_attn_out": g_wa, "w_merge_out": g_wm, "norm_ffn_g": small[4],
        "w_ffn_in": g_wf, "w_ffn_down": g_wd,
    }
    weights = dict(norm_mix_g=norm_mix_g, w_in=w_in, conv_dw_w=conv_dw_w, conv_dw_b=conv_dw_b, conv_ln_g=conv_ln_g,
                   conv_ln_b=conv_ln_b, w_conv_out=w_conv_out, q_norm_g=q_norm_g, k_norm_g=k_norm_g, sinks=sinks,
                   w_attn_out=w_attn_out, w_merge_out=w_merge_out, norm_ffn_g=norm_ffn_g, w_ffn_in=w_ffn_in,
                   w_ffn_down=w_ffn_down)
    m_in = dict(norm_mix_g=m_norm_mix_g, w_in=m_w_in, conv_dw_w=m_conv_dw_w, conv_dw_b=m_conv_dw_b,
                conv_ln_g=m_conv_ln_g, conv_ln_b=m_conv_ln_b, w_conv_out=m_w_conv_out, q_norm_g=m_q_norm_g,
                k_norm_g=m_k_norm_g, sinks=m_sinks, w_attn_out=m_w_attn_out, w_merge_out=m_w_merge_out,
                norm_ffn_g=m_norm_ffn_g, w_ffn_in=m_w_ffn_in, w_ffn_down=m_w_ffn_down)
    v_in = dict(norm_mix_g=v_norm_mix_g, w_in=v_w_in, conv_dw_w=v_conv_dw_w, conv_dw_b=v_conv_dw_b,
                conv_ln_g=v_conv_ln_g, conv_ln_b=v_conv_ln_b, w_conv_out=v_w_conv_out, q_norm_g=v_q_norm_g,
                k_norm_g=v_k_norm_g, sinks=v_sinks, w_attn_out=v_w_attn_out, w_merge_out=v_w_merge_out,
                norm_ffn_g=v_norm_ffn_g, w_ffn_in=v_w_ffn_in, w_ffn_down=v_w_ffn_down)
    names = list(weights)
    deltas, new_m, new_v = [], [], []
    for n in names:
        d, nm, nv = _adamw(weights[n], grads[n], m_in[n], v_in[n], "adamw_" + n)
        deltas.append(d)
        new_m.append(nm)
        new_v.append(nv)
    return (loss, grad_x, *[grads[n] for n in names], *deltas, *new_m, *new_v)
```

```python
import functools
import math

import jax
import jax.numpy as jnp
import numpy as np
from jax import lax
from jax.experimental import pallas as pl
from jax.experimental.pallas import tpu as pltpu

F32 = jnp.float32
BF16 = jnp.bfloat16

D = 1024
CW = 31
HD = 64
NQ = 16
NKV = 2
GROUP = NQ // NKV
BLK = 128
DFF = 2816
EPS = 1e-6
NEG = -1e30
IN_COLS = 5376
SCALE = 1.0 / math.sqrt(HD)

LR, B1, B2, AEPS, WD, STEP = 0.001, 0.9, 0.999, 1e-08, 0.01, 10

MIB = 1024 * 1024
MESH = pl.DeviceIdType.MESH

ROWS_W_IN = 1344
ROWS_SQ = 256
ROWS_FFN_IN = 1408
ROWS_DOWN = 704
ROWS_MAT = ROWS_W_IN + 3 * ROWS_SQ + ROWS_FFN_IN + ROWS_DOWN
ROWS_DW = 32
ROWS_PACK = ROWS_MAT + ROWS_DW
VEC_ROWS = 40


def _sig(x):
    return 1.0 / (1.0 + jnp.exp(-x))


def _dot(a, b):
    return jnp.dot(a, b, preferred_element_type=F32)


def _dot_nt(a, b):
    return lax.dot_general(a, b, (((1,), (1,)), ((), ())), preferred_element_type=F32)


def _dot_tn(a, b):
    return lax.dot_general(a, b, (((0,), (0,)), ((), ())), preferred_element_type=F32)


def _params(sem, vmem_mib=48):
    return pltpu.CompilerParams(dimension_semantics=sem, vmem_limit_bytes=vmem_mib * MIB)


def _resident(shape):
    return pl.BlockSpec(shape, lambda *_: (0,) * len(shape), pipeline_mode=pl.Buffered(1))


def _whole(shape):
    return pl.BlockSpec(shape, lambda *_: (0,) * len(shape))


def _rows(tm, cols, col_block=0):
    return pl.BlockSpec((tm, cols), lambda i: (i, col_block))


def _in_proj(x2, g1, w_in):
    T = x2.shape[0]
    tm = min(256, T)

    def body(x_ref, g_ref, w_ref, xn_ref, ag_ref, q_ref, kv_ref, gg_ref):
        x = x_ref[...]
        rstd = lax.rsqrt(jnp.mean(x * x, axis=-1, keepdims=True) + EPS)
        xn = (x * rstd * g_ref[...]).astype(BF16)
        xn_ref[...] = xn
        ag_ref[...] = _dot(xn, w_ref[:, 0:2048])
        q_ref[...] = _dot(xn, w_ref[:, 2048:3072])
        kv_ref[...] = _dot(xn, w_ref[:, 3072:3328])
        gg_ref[...] = _dot(xn, w_ref[:, 3328:5376])

    return pl.pallas_call(
        body, name="in_proj", grid=(T // tm,),
        in_specs=[_rows(tm, D), _resident((1, D)), _resident((D, IN_COLS))],
        out_specs=[_rows(tm, D), _rows(tm, 2048), _rows(tm, D), _rows(tm, 256), _rows(tm, 2048)],
        out_shape=[jax.ShapeDtypeStruct((T, D), BF16), jax.ShapeDtypeStruct((T, 2048), F32),
                   jax.ShapeDtypeStruct((T, D), F32), jax.ShapeDtypeStruct((T, 256), F32),
                   jax.ShapeDtypeStruct((T, 2048), F32)],
        compiler_params=_params(("parallel",)),
    )(x2, g1, w_in)


def _tap_phases():
    return [(phase, list(range(phase, CW, 8))) for phase in range(8)]


def _shift_copies(dst, src, base):
    for phase, taps in _tap_phases():
        n = dst.shape[1] - 8 * (4 - len(taps))
        dst[phase, 0:n, :] = src[base + phase:base + phase + n, :]


def _conv_fwd(ag, dw, dwb, lng, lnb, wc, n_seq, S):
    T = n_seq * S
    tc = min(256, S)
    nt = S // tc

    def body(a_ref, gt_ref, dw_ref, dwb_ref, lng_ref, lnb_ref, wc_ref, h1_ref, h3_ref, yc_ref, ext, sh):
        i = pl.program_id(1)

        @pl.when(i == 0)
        def _():
            ext[0:32, :] = jnp.zeros((32, D), F32)

        ext[32:32 + tc, :] = a_ref[...] * _sig(gt_ref[...])
        _shift_copies(sh, ext, 2)
        for cb in range(D // 128):
            cs = slice(cb * 128, (cb + 1) * 128)
            acc = jnp.broadcast_to(dwb_ref[:, cs], (tc, 128))
            for phase, taps in _tap_phases():
                for m, j in enumerate(taps):
                    acc = acc + dw_ref[j:j + 1, cs] * sh[phase, 8 * m:8 * m + tc, cs]
            h1_ref[:, cs] = acc
        ext[0:32, :] = ext[tc:tc + 32, :]
        h1 = h1_ref[...]
        mu = jnp.mean(h1, axis=-1, keepdims=True)
        cen = h1 - mu
        var = jnp.mean(cen * cen, axis=-1, keepdims=True)
        h2 = cen * lax.rsqrt(var + EPS) * lng_ref[...] + lnb_ref[...]
        h3 = (h2 * _sig(h2)).astype(BF16)
        h3_ref[...] = h3
        yc_ref[...] = _dot(h3, wc_ref[...])

    tile = lambda col: pl.BlockSpec((tc, D), lambda b, i: (b * nt + i, col))
    return pl.pallas_call(
        body, name="conv_fwd", grid=(n_seq, nt),
        in_specs=[tile(0), tile(1), _resident((32, D)), _resident((1, D)), _resident((1, D)), _resident((1, D)),
                  _resident((D, D))],
        out_specs=[tile(0), tile(0), tile(0)],
        out_shape=[jax.ShapeDtypeStruct((T, D), F32), jax.ShapeDtypeStruct((T, D), BF16),
                   jax.ShapeDtypeStruct((T, D), F32)],
        scratch_shapes=[pltpu.VMEM((32 + tc, D), F32), pltpu.VMEM((8, tc + 24, D), F32)],
        compiler_params=_params(("parallel", "arbitrary")),
    )(ag, ag, dw, dwb, lng, lnb, wc)


def _attn_consts():
    k = np.arange(BLK)[:, None]
    i = np.arange(GROUP * BLK)[None, :] % BLK
    from_prev = k > i
    dist = np.where(from_prev, i + BLK - k, i - k).astype(np.float32)
    head = np.arange(GROUP * BLK)[None, :] // BLK
    bias = []
    for kh in range(NKV):
        slope = np.exp2(-8.0 * (kh * GROUP + head + 1) / NQ).astype(np.float32)
        bias.append(-slope * dist)
    return jnp.asarray(from_prev.astype(np.float32)), jnp.asarray(np.stack(bias))


def _stack_heads(ref, kh):
    return jnp.concatenate([ref[:, (kh * GROUP + g) * HD:(kh * GROUP + g + 1) * HD] for g in range(GROUP)], axis=0)


def _rms64(t):
    return lax.rsqrt(jnp.mean(t * t, axis=-1, keepdims=True) + EPS)


def _attn_probs(kh, n, q_ref, kvc_ref, kvp_ref, qg_ref, kg_ref, tri_ref, bias_ref, sink_ref):
    ks = slice(kh * HD, (kh + 1) * HD)
    vs = slice(2 * HD + kh * HD, 2 * HD + (kh + 1) * HD)
    kp, kc = kvp_ref[:, ks], kvc_ref[:, ks]
    kpb = (kp * _rms64(kp) * kg_ref[...]).astype(BF16)
    kcb = (kc * _rms64(kc) * kg_ref[...]).astype(BF16)
    qs = _stack_heads(q_ref, kh)
    rq = _rms64(qs)
    qy = qs * rq
    qhb = (qy * (qg_ref[...] * SCALE)).astype(BF16)
    from_prev = tri_ref[...] > 0.5
    no_prev = jnp.where(n > 0, 0.0, NEG)
    s = jnp.where(from_prev, _dot_nt(kpb, qhb) + no_prev, _dot_nt(kcb, qhb)) + bias_ref[kh]
    sink = sink_ref[kh:kh + 1, :]
    m = jnp.maximum(jnp.max(s, axis=0, keepdims=True), sink)
    e = jnp.exp(s - m)
    es = jnp.exp(sink - m)
    rz = 1.0 / (jnp.sum(e, axis=0, keepdims=True) + es)
    return e * rz, es * rz, from_prev, qhb, kpb, kcb, kvp_ref[:, vs].astype(BF16), kvc_ref[:, vs].astype(BF16), qy, rq


def _unfold(t, from_prev):
    zero = jnp.zeros_like(t)
    return jnp.where(from_prev, t, zero), jnp.where(from_prev, zero, t)


def _attn_specs(n_seq, S):
    nb = S // BLK
    cur = lambda cols: pl.BlockSpec((BLK, cols), lambda b, n: (b * nb + n, 0))
    prev = lambda cols: pl.BlockSpec((BLK, cols), lambda b, n: (b * nb + jnp.maximum(n - 1, 0), 0))
    consts = [_resident((1, HD)), _resident((1, HD)), _resident((BLK, GROUP * BLK)),
              _resident((NKV, BLK, GROUP * BLK)), _resident((NKV, GROUP * BLK))]
    return nb, cur, prev, consts


def _attn_fwd(q, kv, qg, kg, sink_rows, n_seq, S):
    T = n_seq * S
    nb, cur, prev, consts = _attn_specs(n_seq, S)
    tri, bias = _attn_consts()

    def body(q_ref, kvc_ref, kvp_ref, qg_ref, kg_ref, tri_ref, bias_ref, sink_ref, o_ref):
        n = pl.program_id(1)
        for kh in range(NKV):
            p, _, from_prev, _, _, _, vpb, vcb, _, _ = _attn_probs(kh, n, q_ref, kvc_ref, kvp_ref, qg_ref, kg_ref,
                                                                   tri_ref, bias_ref, sink_ref)
            pp, pc = _unfold(p.astype(BF16), from_prev)
            o = (_dot_tn(pp, vpb) + _dot_tn(pc, vcb)).astype(BF16)
            for g in range(GROUP):
                h = kh * GROUP + g
                o_ref[:, h * HD:(h + 1) * HD] = o[g * BLK:(g + 1) * BLK]

    return pl.pallas_call(
        body, name="attn_fwd", grid=(n_seq, nb),
        in_specs=[cur(D), cur(256), prev(256)] + consts,
        out_specs=cur(D),
        out_shape=jax.ShapeDtypeStruct((T, D), BF16),
        compiler_params=_params(("parallel", "parallel")),
    )(q, kv, kv, qg, kg, tri, bias, sink_rows)


def _merge_fwd(x2, gg, yc, o, wa, wm):
    T = x2.shape[0]
    tm = min(256, T)

    def body(x_ref, gg_ref, yc_ref, o_ref, wa_ref, wm_ref, ya_ref, mix_ref, h_ref):
        ya = _dot(o_ref[...], wa_ref[...])
        mix = (_sig(gg_ref[:, 0:D]) * yc_ref[...] + _sig(gg_ref[:, D:2 * D]) * ya).astype(BF16)
        ya_ref[...] = ya
        mix_ref[...] = mix
        h_ref[...] = x_ref[...] + _dot(mix, wm_ref[...])

    return pl.pallas_call(
        body, name="merge_fwd", grid=(T // tm,),
        in_specs=[_rows(tm, D), _rows(tm, 2 * D), _rows(tm, D), _rows(tm, D), _resident((D, D)), _resident((D, D))],
        out_specs=[_rows(tm, D), _rows(tm, D), _rows(tm, D)],
        out_shape=[jax.ShapeDtypeStruct((T, D), F32), jax.ShapeDtypeStruct((T, D), BF16),
                   jax.ShapeDtypeStruct((T, D), F32)],
        compiler_params=_params(("parallel",)),
    )(x2, gg, yc, o, wa, wm)


FF_CHUNK = DFF // 2


def _ffn(h, tgt, g2, wf, wd):
    T = h.shape[0]
    tm = min(256, T)

    def body(h_ref, t_ref, g_ref, wf_ref, wd_ref, dh_ref, dhb_ref, hn_ref, act_ref, dout_ref, dgu_ref, st_ref,
             gsc, usc):
        @pl.when(pl.program_id(0) == 0)
        def _():
            st_ref[...] = jnp.zeros((8, D), F32)

        hh = h_ref[...]
        rstd = lax.rsqrt(jnp.mean(hh * hh, axis=-1, keepdims=True) + EPS)
        hhat = hh * rstd
        hn = (hhat * g_ref[...]).astype(BF16)
        hn_ref[...] = hn
        out = hh
        for c in range(DFF // FF_CHUNK):
            cs = slice(c * FF_CHUNK, (c + 1) * FF_CHUNK)
            us = slice(DFF + c * FF_CHUNK, DFF + (c + 1) * FF_CHUNK)
            g = _dot(hn, wf_ref[:, cs])
            u = _dot(hn, wf_ref[:, us])
            gsc[:, cs] = g
            usc[:, cs] = u
            act = (g * _sig(g) * u).astype(BF16)
            act_ref[:, cs] = act
            out = out + _dot(act, wd_ref[cs, :])
        err = out - t_ref[...]
        dout = err * (1.0 / D)
        doutb = dout.astype(BF16)
        dout_ref[...] = doutb
        dhn = jnp.zeros((tm, D), F32)
        for c in range(DFF // FF_CHUNK):
            cs = slice(c * FF_CHUNK, (c + 1) * FF_CHUNK)
            us = slice(DFF + c * FF_CHUNK, DFF + (c + 1) * FF_CHUNK)
            g = gsc[:, cs]
            u = usc[:, cs]
            dact = _dot_nt(doutb, wd_ref[cs, :])
            sg = _sig(g)
            dg = (dact * u * (sg * (1.0 + g * (1.0 - sg)))).astype(BF16)
            du = (dact * (g * sg)).astype(BF16)
            dgu_ref[:, cs] = dg
            dgu_ref[:, us] = du
            dhn = dhn + _dot_nt(dg, wf_ref[:, cs]) + _dot_nt(du, wf_ref[:, us])
        st_ref[0:1, :] += jnp.sum(dhn * hhat, axis=0, keepdims=True)
        st_ref[1:2, :] += jnp.sum(err * err, axis=0, keepdims=True)
        dhh = dhn * g_ref[...]
        dh = dout + rstd * (dhh - hhat * jnp.mean(dhh * hhat, axis=-1, keepdims=True))
        dh_ref[...] = dh
        dhb_ref[...] = dh.astype(BF16)

    return pl.pallas_call(
        body, name="ffn_fwd_bwd", grid=(T // tm,),
        in_specs=[_rows(tm, D), _rows(tm, D), _resident((1, D)), _resident((D, 2 * DFF)), _resident((DFF, D))],
        out_specs=[_rows(tm, D), _rows(tm, D), _rows(tm, D), _rows(tm, DFF), _rows(tm, D), _rows(tm, 2 * DFF),
                   _whole((8, D))],
        out_shape=[jax.ShapeDtypeStruct((T, D), F32), jax.ShapeDtypeStruct((T, D), BF16),
                   jax.ShapeDtypeStruct((T, D), BF16), jax.ShapeDtypeStruct((T, DFF), BF16),
                   jax.ShapeDtypeStruct((T, D), BF16), jax.ShapeDtypeStruct((T, 2 * DFF), BF16),
                   jax.ShapeDtypeStruct((8, D), F32)],
        scratch_shapes=[pltpu.VMEM((tm, DFF), F32), pltpu.VMEM((tm, DFF), F32)],
        compiler_params=_params(("arbitrary",), 56),
    )(h, tgt, g2, wf, wd)


def _merge_bwd(dhb, gg, yc, ya, wm, wa, wc):
    T = dhb.shape[0]
    tm = min(256, T)

    def body(dh_ref, gg_ref, yc_ref, ya_ref, wm_ref, wa_ref, wc_ref, dgg_ref, dyc_ref, dya_ref, do_ref, dh3_ref):
        dmix = _dot_nt(dh_ref[...], wm_ref[...])
        gc = _sig(gg_ref[:, 0:D])
        ga = _sig(gg_ref[:, D:2 * D])
        yc = yc_ref[...]
        ya = ya_ref[...]
        dgg_ref[:, 0:D] = (dmix * yc * gc * (1.0 - gc)).astype(BF16)
        dgg_ref[:, D:2 * D] = (dmix * ya * ga * (1.0 - ga)).astype(BF16)
        dyc = (dmix * gc).astype(BF16)
        dya = (dmix * ga).astype(BF16)
        dyc_ref[...] = dyc
        dya_ref[...] = dya
        do_ref[...] = _dot_nt(dya, wa_ref[...])
        dh3_ref[...] = _dot_nt(dyc, wc_ref[...])

    return pl.pallas_call(
        body, name="merge_bwd", grid=(T // tm,),
        in_specs=[_rows(tm, D), _rows(tm, 2 * D), _rows(tm, D), _rows(tm, D), _resident((D, D)), _resident((D, D)),
                  _resident((D, D))],
        out_specs=[_rows(tm, 2 * D), _rows(tm, D), _rows(tm, D), _rows(tm, D), _rows(tm, D)],
        out_shape=[jax.ShapeDtypeStruct((T, 2 * D), BF16), jax.ShapeDtypeStruct((T, D), BF16),
                   jax.ShapeDtypeStruct((T, D), BF16), jax.ShapeDtypeStruct((T, D), F32),
                   jax.ShapeDtypeStruct((T, D), F32)],
        compiler_params=_params(("parallel",)),
    )(dhb, gg, yc, ya, wm, wa, wc)


def _attn_bwd(q, kv, do, qg, kg, sink_rows, n_seq, S):
    T = n_seq * S
    nb, cur, prev, consts = _attn_specs(n_seq, S)
    tri, bias = _attn_consts()

    def body(q_ref, kvc_ref, kvp_ref, do_ref, qg_ref, kg_ref, tri_ref, bias_ref, sink_ref, dq_ref, dkc_ref, dkp_ref,
             dqg_ref, dsk_ref):
        n = pl.program_id(1)

        @pl.when((pl.program_id(0) == 0) & (n == 0))
        def _():
            dqg_ref[...] = jnp.zeros((1, HD), F32)
            dsk_ref[...] = jnp.zeros((8, 128), F32)

        lane = lax.broadcasted_iota(jnp.int32, (1, 128), 1)
        for kh in range(NKV):
            p, ps, from_prev, qhb, kpb, kcb, vpb, vcb, qy, rq = _attn_probs(
                kh, n, q_ref, kvc_ref, kvp_ref, qg_ref, kg_ref, tri_ref, bias_ref, sink_ref)
            dob = _stack_heads(do_ref, kh).astype(BF16)
            dp = jnp.where(from_prev, _dot_nt(vpb, dob), _dot_nt(vcb, dob))
            delta = jnp.sum(p * dp, axis=0, keepdims=True)
            dsp, dsc = _unfold((p * (dp - delta)).astype(BF16), from_prev)
            pp, pc = _unfold(p.astype(BF16), from_prev)
            dsink = -ps * delta
            dqh = (_dot_tn(dsp, kpb) + _dot_tn(dsc, kcb)) * SCALE
            dqg_ref[...] += jnp.sum(dqh * qy, axis=0, keepdims=True)
            dy = dqh * qg_ref[...]
            dq = (rq * (dy - qy * jnp.mean(dy * qy, axis=-1, keepdims=True))).astype(BF16)
            row = jnp.zeros((1, 128), F32)
            for g in range(GROUP):
                h = kh * GROUP + g
                dq_ref[:, h * HD:(h + 1) * HD] = dq[g * BLK:(g + 1) * BLK]
                row = row + jnp.where(lane == h, jnp.sum(dsink[:, g * BLK:(g + 1) * BLK], axis=1, keepdims=True), 0.0)
            dsk_ref[0:1, :] += row
            ks = slice(kh * HD, (kh + 1) * HD)
            vs = slice(2 * HD + kh * HD, 2 * HD + (kh + 1) * HD)
            dkp_ref[:, ks] = _dot(dsp, qhb)
            dkc_ref[:, ks] = _dot(dsc, qhb)
            dkp_ref[:, vs] = _dot(pp, dob)
            dkc_ref[:, vs] = _dot(pc, dob)

    return pl.pallas_call(
        body, name="attn_bwd", grid=(n_seq, nb),
        in_specs=[cur(D), cur(256), prev(256), cur(D)] + consts,
        out_specs=[cur(D), cur(256), cur(256), _whole((1, HD)), _whole((8, 128))],
        out_shape=[jax.ShapeDtypeStruct((T, D), BF16), jax.ShapeDtypeStruct((T, 256), F32),
                   jax.ShapeDtypeStruct((T, 256), F32), jax.ShapeDtypeStruct((1, HD), F32),
                   jax.ShapeDtypeStruct((8, 128), F32)],
        compiler_params=_params(("arbitrary", "arbitrary")),
    )(q, kv, kv, do, qg, kg, tri, bias, sink_rows)


def _kv_bwd(kv, dkc, dkp, kg, n_seq, S):
    T = n_seq * S
    nb, cur, _, _ = _attn_specs(n_seq, S)
    nxt = pl.BlockSpec((BLK, 256), lambda b, n: (b * nb + jnp.minimum(n + 1, nb - 1), 0))

    def body(kv_ref, dkc_ref, dkp_ref, kg_ref, dkv_ref, dkg_ref):
        n = pl.program_id(1)

        @pl.when((pl.program_id(0) == 0) & (n == 0))
        def _():
            dkg_ref[...] = jnp.zeros((1, HD), F32)

        has_next = jnp.where(n < nb - 1, 1.0, 0.0)
        d = dkc_ref[...] + has_next * dkp_ref[...]
        for kh in range(NKV):
            ks = slice(kh * HD, (kh + 1) * HD)
            k = kv_ref[:, ks]
            r = _rms64(k)
            y = k * r
            dkh = d[:, ks]
            dkg_ref[...] += jnp.sum(dkh * y, axis=0, keepdims=True)
            dy = dkh * kg_ref[...]
            dkv_ref[:, ks] = (r * (dy - y * jnp.mean(dy * y, axis=-1, keepdims=True))).astype(BF16)
        dkv_ref[:, 2 * HD:4 * HD] = d[:, 2 * HD:4 * HD].astype(BF16)

    return pl.pallas_call(
        body, name="kv_bwd", grid=(n_seq, nb),
        in_specs=[cur(256), cur(256), nxt, _resident((1, HD))],
        out_specs=[cur(256), _whole((1, HD))],
        out_shape=[jax.ShapeDtypeStruct((T, 256), BF16), jax.ShapeDtypeStruct((1, HD), F32)],
        compiler_params=_params(("arbitrary", "arbitrary")),
    )(kv, dkc, dkp, kg)


def _conv_bwd(h1, dh3, ag, dw, lng, lnb, n_seq, S):
    T = n_seq * S
    tc = min(256, S)
    nt = S // tc
    halo_blocks = tc // 32

    def body(h1_ref, dh3_ref, a_ref, gt_ref, ah_ref, gh_ref, dw_ref, lng_ref, lnb_ref, dag_ref, ddw_ref, st_ref,
             extd, ext0, acc8, shd, sh0):
        i = pl.program_id(1)
        ti = nt - 1 - i

        @pl.when((pl.program_id(0) == 0) & (i == 0))
        def _():
            acc8[...] = jnp.zeros((CW * 8, D), F32)
            st_ref[...] = jnp.zeros((8, D), F32)

        @pl.when(i == 0)
        def _():
            extd[tc:tc + 32, :] = jnp.zeros((32, D), F32)

        h1 = h1_ref[...]
        mu = jnp.mean(h1, axis=-1, keepdims=True)
        cen = h1 - mu
        rstd = lax.rsqrt(jnp.mean(cen * cen, axis=-1, keepdims=True) + EPS)
        xh = cen * rstd
        h2 = xh * lng_ref[...] + lnb_ref[...]
        sg = _sig(h2)
        dh2 = dh3_ref[...] * (sg * (1.0 + h2 * (1.0 - sg)))
        st_ref[1:2, :] += jnp.sum(dh2 * xh, axis=0, keepdims=True)
        st_ref[2:3, :] += jnp.sum(dh2, axis=0, keepdims=True)
        dxh = dh2 * lng_ref[...]
        dh1 = rstd * (dxh - jnp.mean(dxh, axis=-1, keepdims=True)
                      - xh * jnp.mean(dxh * xh, axis=-1, keepdims=True))
        st_ref[0:1, :] += jnp.sum(dh1, axis=0, keepdims=True)
        extd[0:tc, :] = dh1
        has_past = jnp.where(ti > 0, 1.0, 0.0)
        ext0[0:32, :] = has_past * (ah_ref[...] * _sig(gh_ref[...]))
        ext0[32:32 + tc, :] = a_ref[...] * _sig(gt_ref[...])
        _shift_copies(shd, extd, 0)
        _shift_copies(sh0, ext0, 2)
        for cb in range(D // 128):
            cs = slice(cb * 128, (cb + 1) * 128)
            for rb in range(tc // 128):
                rs = slice(rb * 128, (rb + 1) * 128)
                d = extd[rs, cs]
                acc = jnp.zeros((128, 128), F32)
                for phase, offs in _tap_phases():
                    for m, o in enumerate(offs):
                        j = CW - 1 - o
                        acc = acc + dw_ref[j:j + 1, cs] * shd[phase, rb * 128 + 8 * m:rb * 128 + 8 * m + 128, cs]
                for phase, taps in _tap_phases():
                    for m, j in enumerate(taps):
                        prod = d * sh0[phase, rb * 128 + 8 * m:rb * 128 + 8 * m + 128, cs]
                        acc8[j * 8:(j + 1) * 8, cs] += jnp.sum(prod.reshape(16, 8, 128), axis=0)
                sgt = _sig(gt_ref[rs, cs])
                dag_ref[rs, cs] = (acc * sgt).astype(BF16)
                dag_ref[rs, cb * 128 + D:(cb + 1) * 128 + D] = (acc * a_ref[rs, cs] * sgt * (1.0 - sgt)).astype(BF16)
        extd[tc:tc + 32, :] = extd[0:32, :]

        @pl.when((pl.program_id(0) == n_seq - 1) & (i == nt - 1))
        def _():
            for j in range(CW):
                ddw_ref[j:j + 1, :] = jnp.sum(acc8[j * 8:(j + 1) * 8, :], axis=0, keepdims=True)
            ddw_ref[CW:32, :] = jnp.zeros((32 - CW, D), F32)

    tile = lambda col: pl.BlockSpec((tc, D), lambda b, i: (b * nt + (nt - 1 - i), col))
    halo = lambda col: pl.BlockSpec(
        (32, D), lambda b, i: (jnp.maximum((b * nt + (nt - 1 - i)) * halo_blocks - 1, 0), col))
    return pl.pallas_call(
        body, name="conv_bwd", grid=(n_seq, nt),
        in_specs=[tile(0), tile(0), tile(0), tile(1), halo(0), halo(1), _resident((32, D)), _resident((1, D)),
                  _resident((1, D))],
        out_specs=[pl.BlockSpec((tc, 2 * D), lambda b, i: (b * nt + (nt - 1 - i), 0)), _whole((32, D)),
                   _whole((8, D))],
        out_shape=[jax.ShapeDtypeStruct((T, 2 * D), BF16), jax.ShapeDtypeStruct((32, D), F32),
                   jax.ShapeDtypeStruct((8, D), F32)],
        scratch_shapes=[pltpu.VMEM((tc + 32, D), F32), pltpu.VMEM((32 + tc, D), F32), pltpu.VMEM((CW * 8, D), F32),
                        pltpu.VMEM((8, tc + 24, D), F32), pltpu.VMEM((8, tc + 24, D), F32)],
        compiler_params=_params(("arbitrary", "arbitrary")),
    )(h1, dh3, ag, ag, ag, ag, dw, lng, lnb)


def _in_proj_bwd(dag, dq, dkv, dgg, dh, x2, g1, w_in):
    T = x2.shape[0]
    tm = min(256, T)

    def body(dag_ref, dq_ref, dkv_ref, dgg_ref, dh_ref, x_ref, g_ref, w_ref, dx_ref, dg_ref):
        @pl.when(pl.program_id(0) == 0)
        def _():
            dg_ref[...] = jnp.zeros((1, D), F32)

        dxn = (_dot_nt(dag_ref[...], w_ref[:, 0:2048]) + _dot_nt(dq_ref[...], w_ref[:, 2048:3072])
               + _dot_nt(dkv_ref[...], w_ref[:, 3072:3328]) + _dot_nt(dgg_ref[...], w_ref[:, 3328:5376]))
        x = x_ref[...]
        rstd = lax.rsqrt(jnp.mean(x * x, axis=-1, keepdims=True) + EPS)
        xh = x * rstd
        dg_ref[...] += jnp.sum(dxn * xh, axis=0, keepdims=True)
        dxh = dxn * g_ref[...]
        dx_ref[...] = dh_ref[...] + rstd * (dxh - xh * jnp.mean(dxh * xh, axis=-1, keepdims=True))

    return pl.pallas_call(
        body, name="in_proj_bwd", grid=(T // tm,),
        in_specs=[_rows(tm, 2 * D), _rows(tm, D), _rows(tm, 256), _rows(tm, 2 * D), _rows(tm, D), _rows(tm, D),
                  _resident((1, D)), _resident((D, IN_COLS))],
        out_specs=[_rows(tm, D), _whole((1, D))],
        out_shape=[jax.ShapeDtypeStruct((T, D), F32), jax.ShapeDtypeStruct((1, D), F32)],
        compiler_params=_params(("arbitrary",)),
    )(dag, dq, dkv, dgg, dh, x2, g1, w_in)


def _tn_matmul(a, b, name):
    T, K = a.shape
    N = b.shape[1]
    tk = K if K <= 1024 else K // 2
    tn = N if N <= 1024 else (1024 if N % 1024 == 0 else N // 4)
    tt = min(512, T)
    assert K % tk == 0 and N % tn == 0 and T % tt == 0 and tk % 128 == 0 and tn % 128 == 0

    def body(a_ref, b_ref, o_ref):
        @pl.when(pl.program_id(2) == 0)
        def _():
            o_ref[...] = jnp.zeros((tk, tn), F32)

        o_ref[...] += _dot_tn(a_ref[...], b_ref[...])

    return pl.pallas_call(
        body, name=name, grid=(K // tk, N // tn, T // tt),
        in_specs=[pl.BlockSpec((tt, tk), lambda i, j, t: (t, i)), pl.BlockSpec((tt, tn), lambda i, j, t: (t, j))],
        out_specs=pl.BlockSpec((tk, tn), lambda i, j, t: (i, j)),
        out_shape=jax.ShapeDtypeStruct((K, N), F32),
        compiler_params=_params(("parallel", "parallel", "arbitrary")),
    )(a, b)


def _place():
    x, y, c = lax.axis_index("x"), lax.axis_index("y"), lax.axis_index("c")
    chips = [(1 - x, y), (x, 1 - y), (1 - x, 1 - y)]
    return x, y, c, chips


def _gather_weights(pack):
    rows = pack.shape[0]
    half = rows // 2

    def body(src, dst, local_sem, send_sems, recv_sems):
        x, y, c, chips = _place()
        mine = pltpu.make_async_copy(src, dst.at[2 * x + y], local_sem)
        mine.start()

        def piece(px, py, pc):
            return dst.at[2 * px + py, pl.ds(pc * half, half), :]

        def copy(k, block, to, from_src=False):
            return pltpu.make_async_remote_copy(
                src_ref=src.at[pl.ds(c * half, half), :] if from_src else piece(*block), dst_ref=piece(*block),
                send_sem=send_sems.at[k], recv_sem=recv_sems.at[k], device_id=to, device_id_type=MESH)

        first = [copy(k, (x, y, c), (*chip, c), from_src=True) for k, chip in enumerate(chips)]
        for cp in first:
            cp.start()
        passed = [copy(3 + k, (*chip, c), (x, y, 1 - c)) for k, chip in enumerate(chips)]
        for k, chip in enumerate(chips):
            copy(k, (*chip, c), (x, y, c)).wait_recv()
            passed[k].start()
        for k, chip in enumerate(chips):
            copy(3 + k, (*chip, 1 - c), (x, y, c)).wait_recv()
        for cp in first + passed:
            cp.wait_send()
        mine.wait()

    return pl.pallas_call(
        body, name="gather_weights",
        in_specs=[pl.BlockSpec(memory_space=pl.ANY)], out_specs=pl.BlockSpec(memory_space=pl.ANY),
        out_shape=jax.ShapeDtypeStruct((4, rows, D), pack.dtype),
        scratch_shapes=[pltpu.SemaphoreType.DMA, pltpu.SemaphoreType.DMA((6,)), pltpu.SemaphoreType.DMA((6,))],
        compiler_params=pltpu.CompilerParams(has_side_effects=True),
    )(pack)


def _swap_halves(g):
    rows = g.shape[1]
    half = rows // 2

    def body(src, dst, send_sem, recv_sem):
        x, y, c, _ = _place()
        cp = pltpu.make_async_remote_copy(
            src_ref=src.at[:, pl.ds((1 - c) * half, half), :], dst_ref=dst, send_sem=send_sem, recv_sem=recv_sem,
            device_id=(x, y, 1 - c), device_id_type=MESH)
        cp.start()
        cp.wait()

    return pl.pallas_call(
        body, name="grad_swap_halves",
        in_specs=[pl.BlockSpec(memory_space=pl.ANY)], out_specs=pl.BlockSpec(memory_space=pl.ANY),
        out_shape=jax.ShapeDtypeStruct((4, half, D), g.dtype),
        scratch_shapes=[pltpu.SemaphoreType.DMA, pltpu.SemaphoreType.DMA],
        compiler_params=pltpu.CompilerParams(has_side_effects=True),
    )(g)


def _add_halves(g, got, c_idx):
    rows = g.shape[1]
    half = rows // 2
    tr = half // 4
    assert half % 4 == 0 and tr % 16 == 0

    def body(c_ref, g_ref, r_ref, o_ref):
        o_ref[...] = (g_ref[...] + r_ref[...]).astype(BF16)

    return pl.pallas_call(
        body, name="grad_add_halves",
        grid_spec=pltpu.PrefetchScalarGridSpec(
            num_scalar_prefetch=1, grid=(4, 4),
            in_specs=[pl.BlockSpec((1, tr, D), lambda q, i, c_ref: (q, c_ref[0] * 4 + i, 0)),
                      pl.BlockSpec((1, tr, D), lambda q, i, c_ref: (q, i, 0))],
            out_specs=pl.BlockSpec((1, tr, D), lambda q, i, c_ref: (q, i, 0))),
        out_shape=jax.ShapeDtypeStruct((4, half, D), BF16),
        compiler_params=_params(("parallel", "parallel")),
    )(c_idx, g, got)


def _scatter_chips(p):
    half = p.shape[1]

    def body(src, dst, local_sem, send_sems, recv_sems):
        x, y, c, chips = _place()
        me = 2 * x + y
        mine = pltpu.make_async_copy(src.at[me], dst.at[me], local_sem)
        mine.start()
        sends = [pltpu.make_async_remote_copy(
            src_ref=src.at[2 * cx + cy], dst_ref=dst.at[me], send_sem=send_sems.at[k], recv_sem=recv_sems.at[k],
            device_id=(cx, cy, c), device_id_type=MESH) for k, (cx, cy) in enumerate(chips)]
        for cp in sends:
            cp.start()
        for k, (cx, cy) in enumerate(chips):
            pltpu.make_async_remote_copy(
                src_ref=src.at[me], dst_ref=dst.at[2 * cx + cy], send_sem=send_sems.at[k], recv_sem=recv_sems.at[k],
                device_id=(cx, cy, c), device_id_type=MESH).wait_recv()
        for cp in sends:
            cp.wait_send()
        mine.wait()

    return pl.pallas_call(
        body, name="grad_scatter_chips",
        in_specs=[pl.BlockSpec(memory_space=pl.ANY)], out_specs=pl.BlockSpec(memory_space=pl.ANY),
        out_shape=jax.ShapeDtypeStruct((4, half, D), p.dtype),
        scratch_shapes=[pltpu.SemaphoreType.DMA, pltpu.SemaphoreType.DMA((3,)), pltpu.SemaphoreType.DMA((3,))],
        compiler_params=pltpu.CompilerParams(has_side_effects=True),
    )(p)


def _sum_chips(r):
    half = r.shape[1]
    tr = half // 4

    def body(r_ref, o_ref):
        acc = r_ref[0].astype(F32)
        for q in range(1, 4):
            acc = acc + r_ref[q].astype(F32)
        o_ref[...] = acc

    return pl.pallas_call(
        body, name="grad_sum_chips", grid=(4,),
        in_specs=[pl.BlockSpec((4, tr, D), lambda i: (0, i, 0))],
        out_specs=pl.BlockSpec((tr, D), lambda i: (i, 0)),
        out_shape=jax.ShapeDtypeStruct((half, D), F32),
        compiler_params=_params(("parallel",)),
    )(r)


def _join_halves(f):
    half = f.shape[0]

    def body(src, dst, local_sem, send_sem, recv_sem):
        x, y, c, _ = _place()
        mine = pltpu.make_async_copy(src, dst.at[pl.ds(c * half, half), :], local_sem)
        mine.start()
        cp = pltpu.make_async_remote_copy(
            src_ref=src, dst_ref=dst.at[pl.ds(c * half, half), :], send_sem=send_sem, recv_sem=recv_sem,
            device_id=(x, y, 1 - c), device_id_type=MESH)
        cp.start()
        pltpu.make_async_remote_copy(
            src_ref=src, dst_ref=dst.at[pl.ds((1 - c) * half, half), :], send_sem=send_sem, recv_sem=recv_sem,
            device_id=(x, y, 1 - c), device_id_type=MESH).wait_recv()
        cp.wait_send()
        mine.wait()

    return pl.pallas_call(
        body, name="grad_join_halves",
        in_specs=[pl.BlockSpec(memory_space=pl.ANY)], out_specs=pl.BlockSpec(memory_space=pl.ANY),
        out_shape=jax.ShapeDtypeStruct((2 * half, D), f.dtype),
        scratch_shapes=[pltpu.SemaphoreType.DMA, pltpu.SemaphoreType.DMA, pltpu.SemaphoreType.DMA],
        compiler_params=pltpu.CompilerParams(has_side_effects=True),
    )(f)


def _allreduce_small(vec):
    def body(v_ref, o_ref, gath, send_sems, recv_sems):
        x, y, c, _ = _place()
        me = 4 * x + 2 * y + c
        gath[me] = v_ref[...]
        sends = []
        for k in range(1, 8):
            peer = (x ^ (k >> 2), y ^ ((k >> 1) & 1), c ^ (k & 1))
            sends.append(pltpu.make_async_remote_copy(
                src_ref=v_ref, dst_ref=gath.at[me], send_sem=send_sems.at[k - 1], recv_sem=recv_sems.at[k - 1],
                device_id=peer, device_id_type=MESH))
        for cp in sends:
            cp.start()
        for k in range(1, 8):
            peer = (x ^ (k >> 2), y ^ ((k >> 1) & 1), c ^ (k & 1))
            pltpu.make_async_remote_copy(
                src_ref=v_ref, dst_ref=gath.at[4 * peer[0] + 2 * peer[1] + peer[2]], send_sem=send_sems.at[k - 1],
                recv_sem=recv_sems.at[k - 1], device_id=peer, device_id_type=MESH).wait_recv()
        for cp in sends:
            cp.wait_send()
        acc = gath[0]
        for d in range(1, 8):
            acc = acc + gath[d]
        o_ref[...] = acc

    return pl.pallas_call(
        body, name="allreduce_small",
        in_specs=[pl.BlockSpec(memory_space=pltpu.VMEM)], out_specs=pl.BlockSpec(memory_space=pltpu.VMEM),
        out_shape=jax.ShapeDtypeStruct(vec.shape, F32),
        scratch_shapes=[pltpu.VMEM((8,) + vec.shape, F32), pltpu.SemaphoreType.DMA((7,)),
                        pltpu.SemaphoreType.DMA((7,))],
    )(vec)


def _adamw(w, g, m, v, name):
    shape = w.shape
    if w.ndim == 1 or w.size <= 128 * 128:
        two_d = (1, w.size) if w.size % 128 else (w.size // 128, 128)
    else:
        two_d = (w.shape[0], w.size // w.shape[0])
    rows, cols = two_d
    tr = next((t for t in (256, 176, 128) if rows % t == 0), rows)

    def body(w_ref, g_ref, m_ref, v_ref, d_ref, nm_ref, nv_ref):
        gr = g_ref[...]
        nm = B1 * m_ref[...] + (1.0 - B1) * gr
        nv = B2 * v_ref[...] + (1.0 - B2) * (gr * gr)
        m_hat = nm / (1.0 - B1 ** STEP)
        v_hat = nv / (1.0 - B2 ** STEP)
        d_ref[...] = -LR * (m_hat / (jnp.sqrt(v_hat) + AEPS) + WD * w_ref[...])
        nm_ref[...] = nm
        nv_ref[...] = nv

    spec = pl.BlockSpec((tr, cols), lambda i: (i, 0))
    outs = pl.pallas_call(
        body, name=name, grid=(rows // tr,),
        in_specs=[spec] * 4, out_specs=[spec] * 3,
        out_shape=[jax.ShapeDtypeStruct(two_d, F32)] * 3,
        compiler_params=_params(("parallel",)),
    )(*[t.reshape(two_d) for t in (w, g, m, v)])
    return [o.reshape(shape) for o in outs]


def _pack_weights(w_in, w_conv_out, w_attn_out, w_merge_out, w_ffn_in, w_ffn_down, conv_dw_w):
    dw = jnp.pad(conv_dw_w.reshape(CW, 256), ((0, 1), (0, 0)))
    dw_bits = lax.bitcast_convert_type(dw, BF16).reshape(16, D)
    mats = [w_in.reshape(ROWS_W_IN, D), w_conv_out, w_attn_out, w_merge_out, w_ffn_in.reshape(ROWS_FFN_IN, D),
            w_ffn_down]
    return jnp.concatenate([t.astype(BF16) for t in mats] + [dw_bits, jnp.zeros((16, D), BF16)], axis=0)


def _unpack_weights(g):
    def cols(lo, n_rows, width):
        return jnp.concatenate([g[q, lo:lo + n_rows].reshape(D, width) for q in range(4)], axis=1)

    def rows(lo, n_rows):
        return g[:, lo:lo + n_rows].reshape(4 * n_rows, D)

    o = 0
    w_in = cols(o, ROWS_W_IN, ROWS_W_IN); o += ROWS_W_IN
    wc = rows(o, ROWS_SQ); o += ROWS_SQ
    wa = rows(o, ROWS_SQ); o += ROWS_SQ
    wm = rows(o, ROWS_SQ); o += ROWS_SQ
    wf = cols(o, ROWS_FFN_IN, ROWS_FFN_IN); o += ROWS_FFN_IN
    wd = rows(o, ROWS_DOWN); o += ROWS_DOWN
    dw = lax.bitcast_convert_type(g[:, o:o + 16].reshape(4, 32, 256, 2), F32)
    dw = jnp.transpose(dw, (1, 0, 2)).reshape(32, D)
    return w_in, wc, wa, wm, wf, wd, dw


def _pack_grads(d_w_in, d_wc, d_wa, d_wm, d_wf, d_wd):
    def cols(t, width):
        return jnp.transpose(t.reshape(D, 4, width), (1, 0, 2)).reshape(4, width, D)

    def rows(t):
        return t.reshape(4, t.shape[0] // 4, D)

    return jnp.concatenate([cols(d_w_in, ROWS_W_IN), rows(d_wc), rows(d_wa), rows(d_wm), cols(d_wf, ROWS_FFN_IN),
                            rows(d_wd)], axis=1)


def _unpack_shard(s):
    o = 0
    out = []
    for n_rows, shape in ((ROWS_W_IN, (D, ROWS_W_IN)), (ROWS_SQ, (ROWS_SQ, D)), (ROWS_SQ, (ROWS_SQ, D)),
                          (ROWS_SQ, (ROWS_SQ, D)), (ROWS_FFN_IN, (D, ROWS_FFN_IN)), (ROWS_DOWN, (ROWS_DOWN, D))):
        out.append(s[o:o + n_rows].reshape(shape))
        o += n_rows
    return out


def _local_grads(x, loss_target, norm_mix_g, conv_dw_b, conv_ln_g, conv_ln_b, q_norm_g, k_norm_g, sinks, norm_ffn_g,
                 w_in, wc, wa, wm, wf, wd, dw):
    n_seq, S, _ = x.shape
    T = n_seq * S
    x2 = x.reshape(T, D)
    tgt = loss_target.reshape(T, D)
    row = lambda t: t.reshape(1, -1)
    g1, g2 = row(norm_mix_g), row(norm_ffn_g)
    qg, kg = row(q_norm_g), row(k_norm_g)
    lng, lnb, dwb = row(conv_ln_g), row(conv_ln_b), row(conv_dw_b)
    sink_rows = jnp.repeat(sinks.reshape(NKV, GROUP), BLK, axis=1)

    xn, ag, q, kv, gg = _in_proj(x2, g1, w_in)
    h1, h3, yc = _conv_fwd(ag, dw, dwb, lng, lnb, wc, n_seq, S)
    o = _attn_fwd(q, kv, qg, kg, sink_rows, n_seq, S)
    ya, mix, h = _merge_fwd(x2, gg, yc, o, wa, wm)
    dh, dhb, hn, act, dout, dgu, ffn_stats = _ffn(h, tgt, g2, wf, wd)
    dgg, dyc, dya, do, dh3 = _merge_bwd(dhb, gg, yc, ya, wm, wa, wc)
    dq, dkc, dkp, dqg, dsk = _attn_bwd(q, kv, do, qg, kg, sink_rows, n_seq, S)
    dkv, dkg = _kv_bwd(kv, dkc, dkp, kg, n_seq, S)
    dag, ddw, conv_stats = _conv_bwd(h1, dh3, ag, dw, lng, lnb, n_seq, S)
    dx, dg1 = _in_proj_bwd(dag, dq, dkv, dgg, dh, x2, g1, w_in)

    d_w_in = jnp.concatenate([_tn_matmul(xn, dag, "dw_in_conv"), _tn_matmul(xn, dq, "dw_in_q"),
                              _tn_matmul(xn, dkv, "dw_in_kv"), _tn_matmul(xn, dgg, "dw_in_gates")], axis=1)
    d_wc = _tn_matmul(h3, dyc, "dw_conv_out")
    d_wa = _tn_matmul(o, dya, "dw_attn_out")
    d_wm = _tn_matmul(mix, dhb, "dw_merge")
    d_wf = _tn_matmul(hn, dgu, "dw_ffn_in")
    d_wd = _tn_matmul(act, dout, "dw_ffn_down")

    heads = jnp.concatenate([dqg[0], dkg[0], dsk[0, :NQ], jnp.zeros((D - 2 * HD - NQ,), F32)])
    vec = jnp.concatenate([dg1, conv_stats[0:3], ffn_stats[0:1], heads[None], jnp.zeros((2, D), F32), ddw], axis=0)
    return ffn_stats[1], dx.reshape(x.shape), (d_w_in, d_wc, d_wa, d_wm, d_wf, d_wd), vec


def kernel(x, norm_mix_g, w_in, conv_dw_w, conv_dw_b, conv_ln_g, conv_ln_b, w_conv_out, q_norm_g, k_norm_g, sinks, w_attn_out, w_merge_out, norm_ffn_g, w_ffn_in, w_ffn_down, loss_target, m_norm_mix_g, m_w_in, m_conv_dw_w, m_conv_dw_b, m_conv_ln_g, m_conv_ln_b, m_w_conv_out, m_q_norm_g, m_k_norm_g, m_sinks, m_w_attn_out, m_w_merge_out, m_norm_ffn_g, m_w_ffn_in, m_w_ffn_down, v_norm_mix_g, v_w_in, v_conv_dw_w, v_conv_dw_b, v_conv_ln_g, v_conv_ln_b, v_w_conv_out, v_q_norm_g, v_k_norm_g, v_sinks, v_w_attn_out, v_w_merge_out, v_norm_ffn_g, v_w_ffn_in, v_w_ffn_down):
    chip = 2 * lax.axis_index("x") + lax.axis_index("y")
    c_idx = lax.axis_index("c").astype(jnp.int32).reshape(1)

    gathered = _gather_weights(
        _pack_weights(w_in, w_conv_out, w_attn_out, w_merge_out, w_ffn_in, w_ffn_down, conv_dw_w))
    full = _unpack_weights(gathered)

    sq_cols, grad_x, mats, vec = _local_grads(x, loss_target, norm_mix_g, conv_dw_b, conv_ln_g, conv_ln_b, q_norm_g,
                                              k_norm_g, sinks, norm_ffn_g, *full)
    loss = lax.psum(0.5 / D * jnp.sum(sq_cols), ("x", "y", "c"))

    g = _pack_grads(*mats)
    p = _add_halves(g, _swap_halves(g), c_idx)
    shard = _join_halves(_sum_chips(_scatter_chips(p)))
    g_w_in, g_wc, g_wa, g_wm, g_wf, g_wd = _unpack_shard(shard)

    small = _allreduce_small(vec)
    g_dw = lax.dynamic_slice_in_dim(small[8:8 + CW], chip * 256, 256, axis=1).reshape(CW, 1, 256)
    grads = {
        "norm_mix_g": small[0], "w_in": g_w_in, "conv_dw_w": g_dw, "conv_dw_b": small[1], "conv_ln_g": small[2],
        "conv_ln_b": small[3], "w_conv_out": g_wc, "q_norm_g": small[5, 0:HD], "k_norm_g": small[5, HD:2 * HD],
        "sinks": small[5, 2 * HD:2 * HD + NQ], "w_attn_out": g_wa, "w_merge_out": g_wm, "norm_ffn_g": small[4],
        "w_ffn_in": g_wf, "w_ffn_down": g_wd,
    }
    weights = dict(norm_mix_g=norm_mix_g, w_in=w_in, conv_dw_w=conv_dw_w, conv_dw_b=conv_dw_b, conv_ln_g=conv_ln_g,
                   conv_ln_b=conv_ln_b, w_conv_out=w_conv_out, q_norm_g=q_norm_g, k_norm_g=k_norm_g, sinks=sinks,
                   w_attn_out=w_attn_out, w_merge_out=w_merge_out, norm_ffn_g=norm_ffn_g, w_ffn_in=w_ffn_in,
                   w_ffn_down=w_ffn_down)
    m_in = dict(norm_mix_g=m_norm_mix_g, w_in=m_w_in, conv_dw_w=m_conv_dw_w, conv_dw_b=m_conv_dw_b,
                conv_ln_g=m_conv_ln_g, conv_ln_b=m_conv_ln_b, w_conv_out=m_w_conv_out, q_norm_g=m_q_norm_g,
                k_norm_g=m_k_norm_g, sinks=m_sinks, w_attn_out=m_w_attn_out, w_merge_out=m_w_merge_out,
                norm_ffn_g=m_norm_ffn_g, w_ffn_in=m_w_ffn_in, w_ffn_down=m_w_ffn_down)
    v_in = dict(norm_mix_g=v_norm_mix_g, w_in=v_w_in, conv_dw_w=v_conv_dw_w, conv_dw_b=v_conv_dw_b,
                conv_ln_g=v_conv_ln_g, conv_ln_b=v_conv_ln_b, w_conv_out=v_w_conv_out, q_norm_g=v_q_norm_g,
                k_norm_g=v_k_norm_g, sinks=v_sinks, w_attn_out=v_w_attn_out, w_merge_out=v_w_merge_out,
                norm_ffn_g=v_norm_ffn_g, w_ffn_in=v_w_ffn_in, w_ffn_down=v_w_ffn_down)
    names = list(weights)
    deltas, new_m, new_v = [], [], []
    for n in names:
        d, nm, nv = _adamw(weights[n], grads[n], m_in[n], v_in[n], "adamw_" + n)
        deltas.append(d)
        new_m.append(nm)
        new_v.append(nv)
    return (loss, grad_x, *[grads[n] for n in names], *deltas, *new_m, *new_v)
```

```python
import functools
import math

import jax
import jax.numpy as jnp
import numpy as np
from jax import lax
from jax.experimental import pallas as pl
from jax.experimental.pallas import tpu as pltpu

F32 = jnp.float32
BF16 = jnp.bfloat16

D = 1024
CW = 31
HD = 64
NQ = 16
NKV = 2
GROUP = NQ // NKV
BLK = 128
DFF = 2816
EPS = 1e-6
NEG = -1e30
IN_COLS = 5376
SCALE = 1.0 / math.sqrt(HD)

LR, B1, B2, AEPS, WD, STEP = 0.001, 0.9, 0.999, 1e-08, 0.01, 10

MIB = 1024 * 1024
MESH = pl.DeviceIdType.MESH

ROWS_W_IN = 1344
ROWS_SQ = 256
ROWS_FFN_IN = 1408
ROWS_DOWN = 704
ROWS_MAT = ROWS_W_IN + 3 * ROWS_SQ + ROWS_FFN_IN + ROWS_DOWN
ROWS_DW = 32
ROWS_PACK = ROWS_MAT + ROWS_DW
VEC_ROWS = 40


def _sig(x):
    return 1.0 / (1.0 + jnp.exp(-x))


def _dot(a, b):
    return jnp.dot(a, b, preferred_element_type=F32)


def _dot_nt(a, b):
    return lax.dot_general(a, b, (((1,), (1,)), ((), ())), preferred_element_type=F32)


def _dot_tn(a, b):
    return lax.dot_general(a, b, (((0,), (0,)), ((), ())), preferred_element_type=F32)


def _params(sem, vmem_mib=48):
    return pltpu.CompilerParams(dimension_semantics=sem, vmem_limit_bytes=vmem_mib * MIB)


def _resident(shape):
    return pl.BlockSpec(shape, lambda *_: (0,) * len(shape), pipeline_mode=pl.Buffered(1))


def _whole(shape):
    return pl.BlockSpec(shape, lambda *_: (0,) * len(shape))


def _rows(tm, cols, col_block=0):
    return pl.BlockSpec((tm, cols), lambda i: (i, col_block))


def _in_proj(x2, g1, w_in):
    T = x2.shape[0]
    tm = min(256, T)

    def body(x_ref, g_ref, w_ref, xn_ref, ag_ref, q_ref, kv_ref, gg_ref):
        x = x_ref[...]
        rstd = lax.rsqrt(jnp.mean(x * x, axis=-1, keepdims=True) + EPS)
        xn = (x * rstd * g_ref[...]).astype(BF16)
        xn_ref[...] = xn
        ag_ref[...] = _dot(xn, w_ref[:, 0:2048])
        q_ref[...] = _dot(xn, w_ref[:, 2048:3072])
        kv_ref[...] = _dot(xn, w_ref[:, 3072:3328])
        gg_ref[...] = _dot(xn, w_ref[:, 3328:5376])

    return pl.pallas_call(
        body, name="in_proj", grid=(T // tm,),
        in_specs=[_rows(tm, D), _resident((1, D)), _resident((D, IN_COLS))],
        out_specs=[_rows(tm, D), _rows(tm, 2048), _rows(tm, D), _rows(tm, 256), _rows(tm, 2048)],
        out_shape=[jax.ShapeDtypeStruct((T, D), BF16), jax.ShapeDtypeStruct((T, 2048), F32),
                   jax.ShapeDtypeStruct((T, D), F32), jax.ShapeDtypeStruct((T, 256), F32),
                   jax.ShapeDtypeStruct((T, 2048), F32)],
        compiler_params=_params(("parallel",)),
    )(x2, g1, w_in)


def _tap_phases():
    return [(phase, list(range(phase, CW, 8))) for phase in range(8)]


def _shift_copies(dst, src, base):
    for phase, taps in _tap_phases():
        n = dst.shape[1] - 8 * (4 - len(taps))
        dst[phase, 0:n, :] = src[base + phase:base + phase + n, :]


def _conv_fwd(ag, dw, dwb, lng, lnb, wc, n_seq, S):
    T = n_seq * S
    tc = min(256, S)
    nt = S // tc

    def body(a_ref, gt_ref, dw_ref, dwb_ref, lng_ref, lnb_ref, wc_ref, h1_ref, h3_ref, yc_ref, ext, sh):
        i = pl.program_id(1)

        @pl.when(i == 0)
        def _():
            ext[0:32, :] = jnp.zeros((32, D), F32)

        ext[32:32 + tc, :] = a_ref[...] * _sig(gt_ref[...])
        _shift_copies(sh, ext, 2)
        for cb in range(D // 128):
            cs = slice(cb * 128, (cb + 1) * 128)
            acc = jnp.broadcast_to(dwb_ref[:, cs], (tc, 128))
            for phase, taps in _tap_phases():
                for m, j in enumerate(taps):
                    acc = acc + dw_ref[j:j + 1, cs] * sh[phase, 8 * m:8 * m + tc, cs]
            h1_ref[:, cs] = acc
        ext[0:32, :] = ext[tc:tc + 32, :]
        h1 = h1_ref[...]
        mu = jnp.mean(h1, axis=-1, keepdims=True)
        cen = h1 - mu
        var = jnp.mean(cen * cen, axis=-1, keepdims=True)
        h2 = cen * lax.rsqrt(var + EPS) * lng_ref[...] + lnb_ref[...]
        h3 = (h2 * _sig(h2)).astype(BF16)
        h3_ref[...] = h3
        yc_ref[...] = _dot(h3, wc_ref[...])

    tile = lambda col: pl.BlockSpec((tc, D), lambda b, i: (b * nt + i, col))
    return pl.pallas_call(
        body, name="conv_fwd", grid=(n_seq, nt),
        in_specs=[tile(0), tile(1), _resident((32, D)), _resident((1, D)), _resident((1, D)), _resident((1, D)),
                  _resident((D, D))],
        out_specs=[tile(0), tile(0), tile(0)],
        out_shape=[jax.ShapeDtypeStruct((T, D), F32), jax.ShapeDtypeStruct((T, D), BF16),
                   jax.ShapeDtypeStruct((T, D), F32)],
        scratch_shapes=[pltpu.VMEM((32 + tc, D), F32), pltpu.VMEM((8, tc + 24, D), F32)],
        compiler_params=_params(("parallel", "arbitrary")),
    )(ag, ag, dw, dwb, lng, lnb, wc)


def _attn_consts():
    k = np.arange(BLK)[:, None]
    i = np.arange(GROUP * BLK)[None, :] % BLK
    from_prev = k > i
    dist = np.where(from_prev, i + BLK - k, i - k).astype(np.float32)
    head = np.arange(GROUP * BLK)[None, :] // BLK
    bias = []
    for kh in range(NKV):
        slope = np.exp2(-8.0 * (kh * GROUP + head + 1) / NQ).astype(np.float32)
        bias.append(-slope * dist)
    return jnp.asarray(from_prev.astype(np.float32)), jnp.asarray(np.stack(bias))


def _stack_heads(ref, kh):
    return jnp.concatenate([ref[:, (kh * GROUP + g) * HD:(kh * GROUP + g + 1) * HD] for g in range(GROUP)], axis=0)


def _rms64(t):
    return lax.rsqrt(jnp.mean(t * t, axis=-1, keepdims=True) + EPS)


def _attn_probs(kh, n, q_ref, kvc_ref, kvp_ref, qg_ref, kg_ref, tri_ref, bias_ref, sink_ref):
    ks = slice(kh * HD, (kh + 1) * HD)
    vs = slice(2 * HD + kh * HD, 2 * HD + (kh + 1) * HD)
    kp, kc = kvp_ref[:, ks], kvc_ref[:, ks]
    kpb = (kp * _rms64(kp) * kg_ref[...]).astype(BF16)
    kcb = (kc * _rms64(kc) * kg_ref[...]).astype(BF16)
    qs = _stack_heads(q_ref, kh)
    rq = _rms64(qs)
    qy = qs * rq
    qhb = (qy * (qg_ref[...] * SCALE)).astype(BF16)
    from_prev = tri_ref[...] > 0.5
    no_prev = jnp.where(n > 0, 0.0, NEG)
    s = jnp.where(from_prev, _dot_nt(kpb, qhb) + no_prev, _dot_nt(kcb, qhb)) + bias_ref[kh]
    sink = sink_ref[kh:kh + 1, :]
    m = jnp.maximum(jnp.max(s, axis=0, keepdims=True), sink)
    e = jnp.exp(s - m)
    es = jnp.exp(sink - m)
    rz = 1.0 / (jnp.sum(e, axis=0, keepdims=True) + es)
    return e * rz, es * rz, from_prev, qhb, kpb, kcb, kvp_ref[:, vs].astype(BF16), kvc_ref[:, vs].astype(BF16), qy, rq


def _unfold(t, from_prev):
    zero = jnp.zeros_like(t)
    return jnp.where(from_prev, t, zero), jnp.where(from_prev, zero, t)


def _attn_specs(n_seq, S):
    nb = S // BLK
    cur = lambda cols: pl.BlockSpec((BLK, cols), lambda b, n: (b * nb + n, 0))
    prev = lambda cols: pl.BlockSpec((BLK, cols), lambda b, n: (b * nb + jnp.maximum(n - 1, 0), 0))
    consts = [_resident((1, HD)), _resident((1, HD)), _resident((BLK, GROUP * BLK)),
              _resident((NKV, BLK, GROUP * BLK)), _resident((NKV, GROUP * BLK))]
    return nb, cur, prev, consts


def _attn_fwd(q, kv, qg, kg, sink_rows, n_seq, S):
    T = n_seq * S
    nb, cur, prev, consts = _attn_specs(n_seq, S)
    tri, bias = _attn_consts()

    def body(q_ref, kvc_ref, kvp_ref, qg_ref, kg_ref, tri_ref, bias_ref, sink_ref, o_ref):
        n = pl.program_id(1)
        for kh in range(NKV):
            p, _, from_prev, _, _, _, vpb, vcb, _, _ = _attn_probs(kh, n, q_ref, kvc_ref, kvp_ref, qg_ref, kg_ref,
                                                                   tri_ref, bias_ref, sink_ref)
            pp, pc = _unfold(p.astype(BF16), from_prev)
            o = (_dot_tn(pp, vpb) + _dot_tn(pc, vcb)).astype(BF16)
            for g in range(GROUP):
                h = kh * GROUP + g
                o_ref[:, h * HD:(h + 1) * HD] = o[g * BLK:(g + 1) * BLK]

    return pl.pallas_call(
        body, name="attn_fwd", grid=(n_seq, nb),
        in_specs=[cur(D), cur(256), prev(256)] + consts,
        out_specs=cur(D),
        out_shape=jax.ShapeDtypeStruct((T, D), BF16),
        compiler_params=_params(("parallel", "parallel")),
    )(q, kv, kv, qg, kg, tri, bias, sink_rows)


def _merge_fwd(x2, gg, yc, o, wa, wm):
    T = x2.shape[0]
    tm = min(256, T)

    def body(x_ref, gg_ref, yc_ref, o_ref, wa_ref, wm_ref, ya_ref, mix_ref, h_ref):
        ya = _dot(o_ref[...], wa_ref[...])
        mix = (_sig(gg_ref[:, 0:D]) * yc_ref[...] + _sig(gg_ref[:, D:2 * D]) * ya).astype(BF16)
        ya_ref[...] = ya
        mix_ref[...] = mix
        h_ref[...] = x_ref[...] + _dot(mix, wm_ref[...])

    return pl.pallas_call(
        body, name="merge_fwd", grid=(T // tm,),
        in_specs=[_rows(tm, D), _rows(tm, 2 * D), _rows(tm, D), _rows(tm, D), _resident((D, D)), _resident((D, D))],
        out_specs=[_rows(tm, D), _rows(tm, D), _rows(tm, D)],
        out_shape=[jax.ShapeDtypeStruct((T, D), F32), jax.ShapeDtypeStruct((T, D), BF16),
                   jax.ShapeDtypeStruct((T, D), F32)],
        compiler_params=_params(("parallel",)),
    )(x2, gg, yc, o, wa, wm)


FF_CHUNK = DFF // 2


def _ffn(h, tgt, g2, wf, wd):
    T = h.shape[0]
    tm = min(256, T)

    def body(h_ref, t_ref, g_ref, wf_ref, wd_ref, dh_ref, dhb_ref, hn_ref, act_ref, dout_ref, dgu_ref, st_ref,
             gsc, usc):
        @pl.when(pl.program_id(0) == 0)
        def _():
            st_ref[...] = jnp.zeros((8, D), F32)

        hh = h_ref[...]
        rstd = lax.rsqrt(jnp.mean(hh * hh, axis=-1, keepdims=True) + EPS)
        hhat = hh * rstd
        hn = (hhat * g_ref[...]).astype(BF16)
        hn_ref[...] = hn
        out = hh
        for c in range(DFF // FF_CHUNK):
            cs = slice(c * FF_CHUNK, (c + 1) * FF_CHUNK)
            us = slice(DFF + c * FF_CHUNK, DFF + (c + 1) * FF_CHUNK)
            g = _dot(hn, wf_ref[:, cs])
            u = _dot(hn, wf_ref[:, us])
            gsc[:, cs] = g
            usc[:, cs] = u
            act = (g * _sig(g) * u).astype(BF16)
            act_ref[:, cs] = act
            out = out + _dot(act, wd_ref[cs, :])
        err = out - t_ref[...]
        dout = err * (1.0 / D)
        doutb = dout.astype(BF16)
        dout_ref[...] = doutb
        dhn = jnp.zeros((tm, D), F32)
        for c in range(DFF // FF_CHUNK):
            cs = slice(c * FF_CHUNK, (c + 1) * FF_CHUNK)
            us = slice(DFF + c * FF_CHUNK, DFF + (c + 1) * FF_CHUNK)
            g = gsc[:, cs]
            u = usc[:, cs]
            dact = _dot_nt(doutb, wd_ref[cs, :])
            sg = _sig(g)
            dg = (dact * u * (sg * (1.0 + g * (1.0 - sg)))).astype(BF16)
            du = (dact * (g * sg)).astype(BF16)
            dgu_ref[:, cs] = dg
            dgu_ref[:, us] = du
            dhn = dhn + _dot_nt(dg, wf_ref[:, cs]) + _dot_nt(du, wf_ref[:, us])
        st_ref[0:1, :] += jnp.sum(dhn * hhat, axis=0, keepdims=True)
        st_ref[1:2, :] += jnp.sum(err * err, axis=0, keepdims=True)
        dhh = dhn * g_ref[...]
        dh = dout + rstd * (dhh - hhat * jnp.mean(dhh * hhat, axis=-1, keepdims=True))
        dh_ref[...] = dh
        dhb_ref[...] = dh.astype(BF16)

    return pl.pallas_call(
        body, name="ffn_fwd_bwd", grid=(T // tm,),
        in_specs=[_rows(tm, D), _rows(tm, D), _resident((1, D)), _resident((D, 2 * DFF)), _resident((DFF, D))],
        out_specs=[_rows(tm, D), _rows(tm, D), _rows(tm, D), _rows(tm, DFF), _rows(tm, D), _rows(tm, 2 * DFF),
                   _whole((8, D))],
        out_shape=[jax.ShapeDtypeStruct((T, D), F32), jax.ShapeDtypeStruct((T, D), BF16),
                   jax.ShapeDtypeStruct((T, D), BF16), jax.ShapeDtypeStruct((T, DFF), BF16),
                   jax.ShapeDtypeStruct((T, D), BF16), jax.ShapeDtypeStruct((T, 2 * DFF), BF16),
                   jax.ShapeDtypeStruct((8, D), F32)],
        scratch_shapes=[pltpu.VMEM((tm, DFF), F32), pltpu.VMEM((tm, DFF), F32)],
        compiler_params=_params(("arbitrary",), 56),
    )(h, tgt, g2, wf, wd)


def _merge_bwd(dhb, gg, yc, ya, wm, wa, wc):
    T = dhb.shape[0]
    tm = min(256, T)

    def body(dh_ref, gg_ref, yc_ref, ya_ref, wm_ref, wa_ref, wc_ref, dgg_ref, dyc_ref, dya_ref, do_ref, dh3_ref):
        dmix = _dot_nt(dh_ref[...], wm_ref[...])
        gc = _sig(gg_ref[:, 0:D])
        ga = _sig(gg_ref[:, D:2 * D])
        yc = yc_ref[...]
        ya = ya_ref[...]
        dgg_ref[:, 0:D] = (dmix * yc * gc * (1.0 - gc)).astype(BF16)
        dgg_ref[:, D:2 * D] = (dmix * ya * ga * (1.0 - ga)).astype(BF16)
        dyc = (dmix * gc).astype(BF16)
        dya = (dmix * ga).astype(BF16)
        dyc_ref[...] = dyc
        dya_ref[...] = dya
        do_ref[...] = _dot_nt(dya, wa_ref[...])
        dh3_ref[...] = _dot_nt(dyc, wc_ref[...])

    return pl.pallas_call(
        body, name="merge_bwd", grid=(T // tm,),
        in_specs=[_rows(tm, D), _rows(tm, 2 * D), _rows(tm, D), _rows(tm, D), _resident((D, D)), _resident((D, D)),
                  _resident((D, D))],
        out_specs=[_rows(tm, 2 * D), _rows(tm, D), _rows(tm, D), _rows(tm, D), _rows(tm, D)],
        out_shape=[jax.ShapeDtypeStruct((T, 2 * D), BF16), jax.ShapeDtypeStruct((T, D), BF16),
                   jax.ShapeDtypeStruct((T, D), BF16), jax.ShapeDtypeStruct((T, D), F32),
                   jax.ShapeDtypeStruct((T, D), F32)],
        compiler_params=_params(("parallel",)),
    )(dhb, gg, yc, ya, wm, wa, wc)


def _attn_bwd(q, kv, do, qg, kg, sink_rows, n_seq, S):
    T = n_seq * S
    nb, cur, prev, consts = _attn_specs(n_seq, S)
    tri, bias = _attn_consts()

    def body(q_ref, kvc_ref, kvp_ref, do_ref, qg_ref, kg_ref, tri_ref, bias_ref, sink_ref, dq_ref, dkc_ref, dkp_ref,
             dqg_ref, dsk_ref):
        n = pl.program_id(1)

        @pl.when((pl.program_id(0) == 0) & (n == 0))
        def _():
            dqg_ref[...] = jnp.zeros((1, HD), F32)
            dsk_ref[...] = jnp.zeros((8, 128), F32)

        lane = lax.broadcasted_iota(jnp.int32, (1, 128), 1)
        for kh in range(NKV):
            p, ps, from_prev, qhb, kpb, kcb, vpb, vcb, qy, rq = _attn_probs(
                kh, n, q_ref, kvc_ref, kvp_ref, qg_ref, kg_ref, tri_ref, bias_ref, sink_ref)
            dob = _stack_heads(do_ref, kh).astype(BF16)
            dp = jnp.where(from_prev, _dot_nt(vpb, dob), _dot_nt(vcb, dob))
            delta = jnp.sum(p * dp, axis=0, keepdims=True)
            dsp, dsc = _unfold((p * (dp - delta)).astype(BF16), from_prev)
            pp, pc = _unfold(p.astype(BF16), from_prev)
            dsink = -ps * delta
            dqh = (_dot_tn(dsp, kpb) + _dot_tn(dsc, kcb)) * SCALE
            dqg_ref[...] += jnp.sum(dqh * qy, axis=0, keepdims=True)
            dy = dqh * qg_ref[...]
            dq = (rq * (dy - qy * jnp.mean(dy * qy, axis=-1, keepdims=True))).astype(BF16)
            row = jnp.zeros((1, 128), F32)
            for g in range(GROUP):
                h = kh * GROUP + g
                dq_ref[:, h * HD:(h + 1) * HD] = dq[g * BLK:(g + 1) * BLK]
                row = row + jnp.where(lane == h, jnp.sum(dsink[:, g * BLK:(g + 1) * BLK], axis=1, keepdims=True), 0.0)
            dsk_ref[0:1, :] += row
            ks = slice(kh * HD, (kh + 1) * HD)
            vs = slice(2 * HD + kh * HD, 2 * HD + (kh + 1) * HD)
            dkp_ref[:, ks] = _dot(dsp, qhb)
            dkc_ref[:, ks] = _dot(dsc, qhb)
            dkp_ref[:, vs] = _dot(pp, dob)
            dkc_ref[:, vs] = _dot(pc, dob)

    return pl.pallas_call(
        body, name="attn_bwd", grid=(n_seq, nb),
        in_specs=[cur(D), cur(256), prev(256), cur(D)] + consts,
        out_specs=[cur(D), cur(256), cur(256), _whole((1, HD)), _whole((8, 128))],
        out_shape=[jax.ShapeDtypeStruct((T, D), BF16), jax.ShapeDtypeStruct((T, 256), F32),
                   jax.ShapeDtypeStruct((T, 256), F32), jax.ShapeDtypeStruct((1, HD), F32),
                   jax.ShapeDtypeStruct((8, 128), F32)],
        compiler_params=_params(("arbitrary", "arbitrary")),
    )(q, kv, kv, do, qg, kg, tri, bias, sink_rows)


def _kv_bwd(kv, dkc, dkp, kg, n_seq, S):
    T = n_seq * S
    nb, cur, _, _ = _attn_specs(n_seq, S)
    nxt = pl.BlockSpec((BLK, 256), lambda b, n: (b * nb + jnp.minimum(n + 1, nb - 1), 0))

    def body(kv_ref, dkc_ref, dkp_ref, kg_ref, dkv_ref, dkg_ref):
        n = pl.program_id(1)

        @pl.when((pl.program_id(0) == 0) & (n == 0))
        def _():
            dkg_ref[...] = jnp.zeros((1, HD), F32)

        has_next = jnp.where(n < nb - 1, 1.0, 0.0)
        d = dkc_ref[...] + has_next * dkp_ref[...]
        for kh in range(NKV):
            ks = slice(kh * HD, (kh + 1) * HD)
            k = kv_ref[:, ks]
            r = _rms64(k)
            y = k * r
            dkh = d[:, ks]
            dkg_ref[...] += jnp.sum(dkh * y, axis=0, keepdims=True)
            dy = dkh * kg_ref[...]
            dkv_ref[:, ks] = (r * (dy - y * jnp.mean(dy * y, axis=-1, keepdims=True))).astype(BF16)
        dkv_ref[:, 2 * HD:4 * HD] = d[:, 2 * HD:4 * HD].astype(BF16)

    return pl.pallas_call(
        body, name="kv_bwd", grid=(n_seq, nb),
        in_specs=[cur(256), cur(256), nxt, _resident((1, HD))],
        out_specs=[cur(256), _whole((1, HD))],
        out_shape=[jax.ShapeDtypeStruct((T, 256), BF16), jax.ShapeDtypeStruct((1, HD), F32)],
        compiler_params=_params(("arbitrary", "arbitrary")),
    )(kv, dkc, dkp, kg)


def _conv_bwd(h1, dh3, ag, dw, lng, lnb, n_seq, S):
    T = n_seq * S
    tc = min(256, S)
    nt = S // tc
    halo_blocks = tc // 32

    def body(h1_ref, dh3_ref, a_ref, gt_ref, ah_ref, gh_ref, dw_ref, lng_ref, lnb_ref, dag_ref, ddw_ref, st_ref,
             extd, ext0, acc8, shd, sh0):
        i = pl.program_id(1)
        ti = nt - 1 - i

        @pl.when((pl.program_id(0) == 0) & (i == 0))
        def _():
            acc8[...] = jnp.zeros((CW * 8, D), F32)
            st_ref[...] = jnp.zeros((8, D), F32)

        @pl.when(i == 0)
        def _():
            extd[tc:tc + 32, :] = jnp.zeros((32, D), F32)

        h1 = h1_ref[...]
        mu = jnp.mean(h1, axis=-1, keepdims=True)
        cen = h1 - mu
        rstd = lax.rsqrt(jnp.mean(cen * cen, axis=-1, keepdims=True) + EPS)
        xh = cen * rstd
        h2 = xh * lng_ref[...] + lnb_ref[...]
        sg = _sig(h2)
        dh2 = dh3_ref[...] * (sg * (1.0 + h2 * (1.0 - sg)))
        st_ref[1:2, :] += jnp.sum(dh2 * xh, axis=0, keepdims=True)
        st_ref[2:3, :] += jnp.sum(dh2, axis=0, keepdims=True)
        dxh = dh2 * lng_ref[...]
        dh1 = rstd * (dxh - jnp.mean(dxh, axis=-1, keepdims=True)
                      - xh * jnp.mean(dxh * xh, axis=-1, keepdims=True))
        st_ref[0:1, :] += jnp.sum(dh1, axis=0, keepdims=True)
        extd[0:tc, :] = dh1
        has_past = jnp.where(ti > 0, 1.0, 0.0)
        ext0[0:32, :] = has_past * (ah_ref[...] * _sig(gh_ref[...]))
        ext0[32:32 + tc, :] = a_ref[...] * _sig(gt_ref[...])
        _shift_copies(shd, extd, 0)
        _shift_copies(sh0, ext0, 2)
        for cb in range(D // 128):
            cs = slice(cb * 128, (cb + 1) * 128)
            for rb in range(tc // 128):
                rs = slice(rb * 128, (rb + 1) * 128)
                d = extd[rs, cs]
                acc = jnp.zeros((128, 128), F32)
                for phase, offs in _tap_phases():
                    for m, o in enumerate(offs):
                        j = CW - 1 - o
                        acc = acc + dw_ref[j:j + 1, cs] * shd[phase, rb * 128 + 8 * m:rb * 128 + 8 * m + 128, cs]
                for phase, taps in _tap_phases():
                    for m, j in enumerate(taps):
                        prod = d * sh0[phase, rb * 128 + 8 * m:rb * 128 + 8 * m + 128, cs]
                        acc8[j * 8:(j + 1) * 8, cs] += jnp.sum(prod.reshape(16, 8, 128), axis=0)
                sgt = _sig(gt_ref[rs, cs])
                dag_ref[rs, cs] = (acc * sgt).astype(BF16)
                dag_ref[rs, cb * 128 + D:(cb + 1) * 128 + D] = (acc * a_ref[rs, cs] * sgt * (1.0 - sgt)).astype(BF16)
        extd[tc:tc + 32, :] = extd[0:32, :]

        @pl.when((pl.program_id(0) == n_seq - 1) & (i == nt - 1))
        def _():
            for j in range(CW):
                ddw_ref[j:j + 1, :] = jnp.sum(acc8[j * 8:(j + 1) * 8, :], axis=0, keepdims=True)
            ddw_ref[CW:32, :] = jnp.zeros((32 - CW, D), F32)

    tile = lambda col: pl.BlockSpec((tc, D), lambda b, i: (b * nt + (nt - 1 - i), col))
    halo = lambda col: pl.BlockSpec(
        (32, D), lambda b, i: (jnp.maximum((b * nt + (nt - 1 - i)) * halo_blocks - 1, 0), col))
    return pl.pallas_call(
        body, name="conv_bwd", grid=(n_seq, nt),
        in_specs=[tile(0), tile(0), tile(0), tile(1), halo(0), halo(1), _resident((32, D)), _resident((1, D)),
                  _resident((1, D))],
        out_specs=[pl.BlockSpec((tc, 2 * D), lambda b, i: (b * nt + (nt - 1 - i), 0)), _whole((32, D)),
                   _whole((8, D))],
        out_shape=[jax.ShapeDtypeStruct((T, 2 * D), BF16), jax.ShapeDtypeStruct((32, D), F32),
                   jax.ShapeDtypeStruct((8, D), F32)],
        scratch_shapes=[pltpu.VMEM((tc + 32, D), F32), pltpu.VMEM((32 + tc, D), F32), pltpu.VMEM((CW * 8, D), F32),
                        pltpu.VMEM((8, tc + 24, D), F32), pltpu.VMEM((8, tc + 24, D), F32)],
        compiler_params=_params(("arbitrary", "arbitrary")),
    )(h1, dh3, ag, ag, ag, ag, dw, lng, lnb)


def _in_proj_bwd(dag, dq, dkv, dgg, dh, x2, g1, w_in):
    T = x2.shape[0]
    tm = min(256, T)

    def body(dag_ref, dq_ref, dkv_ref, dgg_ref, dh_ref, x_ref, g_ref, w_ref, dx_ref, dg_ref):
        @pl.when(pl.program_id(0) == 0)
        def _():
            dg_ref[...] = jnp.zeros((1, D), F32)

        dxn = (_dot_nt(dag_ref[...], w_ref[:, 0:2048]) + _dot_nt(dq_ref[...], w_ref[:, 2048:3072])
               + _dot_nt(dkv_ref[...], w_ref[:, 3072:3328]) + _dot_nt(dgg_ref[...], w_ref[:, 3328:5376]))
        x = x_ref[...]
        rstd = lax.rsqrt(jnp.mean(x * x, axis=-1, keepdims=True) + EPS)
        xh = x * rstd
        dg_ref[...] += jnp.sum(dxn * xh, axis=0, keepdims=True)
        dxh = dxn * g_ref[...]
        dx_ref[...] = dh_ref[...] + rstd * (dxh - xh * jnp.mean(dxh * xh, axis=-1, keepdims=True))

    return pl.pallas_call(
        body, name="in_proj_bwd", grid=(T // tm,),
        in_specs=[_rows(tm, 2 * D), _rows(tm, D), _rows(tm, 256), _rows(tm, 2 * D), _rows(tm, D), _rows(tm, D),
                  _resident((1, D)), _resident((D, IN_COLS))],
        out_specs=[_rows(tm, D), _whole((1, D))],
        out_shape=[jax.ShapeDtypeStruct((T, D), F32), jax.ShapeDtypeStruct((1, D), F32)],
        compiler_params=_params(("arbitrary",)),
    )(dag, dq, dkv, dgg, dh, x2, g1, w_in)


def _tn_matmul(a, b, name):
    T, K = a.shape
    N = b.shape[1]
    tk = K if K <= 1024 else K // 2
    tn = N if N <= 1024 else (1024 if N % 1024 == 0 else N // 4)
    tt = min(512, T)
    assert K % tk == 0 and N % tn == 0 and T % tt == 0 and tk % 128 == 0 and tn % 128 == 0

    def body(a_ref, b_ref, o_ref):
        @pl.when(pl.program_id(2) == 0)
        def _():
            o_ref[...] = jnp.zeros((tk, tn), F32)

        o_ref[...] += _dot_tn(a_ref[...], b_ref[...])

    return pl.pallas_call(
        body, name=name, grid=(K // tk, N // tn, T // tt),
        in_specs=[pl.BlockSpec((tt, tk), lambda i, j, t: (t, i)), pl.BlockSpec((tt, tn), lambda i, j, t: (t, j))],
        out_specs=pl.BlockSpec((tk, tn), lambda i, j, t: (i, j)),
        out_shape=jax.ShapeDtypeStruct((K, N), F32),
        compiler_params=_params(("parallel", "parallel", "arbitrary")),
    )(a, b)


def _place():
    x, y, c = lax.axis_index("x"), lax.axis_index("y"), lax.axis_index("c")
    chips = [(1 - x, y), (x, 1 - y), (1 - x, 1 - y)]
    return x, y, c, chips


def _own_slot(slots, mine):
    chip = 2 * lax.axis_index("x") + lax.axis_index("y")
    return lax.dynamic_update_slice(slots, mine[None], (chip,) + (0,) * mine.ndim)


def _row_tile(rows, unit):
    return max(t for t in range(unit, 513, unit) if rows % t == 0)


def _gather_weights(pack):
    rows = pack.shape[0]
    half = rows // 2

    def body(src, dst, token, send_sems, recv_sems):
        x, y, c, chips = _place()

        def piece(px, py, pc):
            return dst.at[2 * px + py, pl.ds(pc * half, half), :]

        def copy(k, block, to, from_src=False):
            return pltpu.make_async_remote_copy(
                src_ref=src.at[pl.ds(c * half, half), :] if from_src else piece(*block), dst_ref=piece(*block),
                send_sem=send_sems.at[k], recv_sem=recv_sems.at[k], device_id=to, device_id_type=MESH)

        first = [copy(k, (x, y, c), (*chip, c), from_src=True) for k, chip in enumerate(chips)]
        for cp in first:
            cp.start()
        passed = [copy(3 + k, (*chip, c), (x, y, 1 - c)) for k, chip in enumerate(chips)]
        for k, chip in enumerate(chips):
            copy(k, (*chip, c), (x, y, c)).wait_recv()
            passed[k].start()
        for k, chip in enumerate(chips):
            copy(3 + k, (*chip, 1 - c), (x, y, c)).wait_recv()
        for cp in first + passed:
            cp.wait_send()
        token[...] = jnp.zeros((8, 128), F32)

    got, token = pl.pallas_call(
        body, name="gather_weights",
        in_specs=[pl.BlockSpec(memory_space=pl.ANY)],
        out_specs=[pl.BlockSpec(memory_space=pl.ANY), pl.BlockSpec(memory_space=pltpu.VMEM)],
        out_shape=[jax.ShapeDtypeStruct((4, rows, D), pack.dtype), jax.ShapeDtypeStruct((8, 128), F32)],
        scratch_shapes=[pltpu.SemaphoreType.DMA((6,)), pltpu.SemaphoreType.DMA((6,))],
        compiler_params=pltpu.CompilerParams(has_side_effects=True),
    )(pack)
    return _own_slot(got, pack), token[0, 0]


def _swap_halves(g):
    rows = g.shape[1]
    half = rows // 2

    def body(src, dst, send_sem, recv_sem):
        x, y, c, _ = _place()
        cp = pltpu.make_async_remote_copy(
            src_ref=src.at[:, pl.ds((1 - c) * half, half), :], dst_ref=dst, send_sem=send_sem, recv_sem=recv_sem,
            device_id=(x, y, 1 - c), device_id_type=MESH)
        cp.start()
        cp.wait()

    return pl.pallas_call(
        body, name="grad_swap_halves",
        in_specs=[pl.BlockSpec(memory_space=pl.ANY)], out_specs=pl.BlockSpec(memory_space=pl.ANY),
        out_shape=jax.ShapeDtypeStruct((4, half, D), g.dtype),
        scratch_shapes=[pltpu.SemaphoreType.DMA, pltpu.SemaphoreType.DMA],
        compiler_params=pltpu.CompilerParams(has_side_effects=True),
    )(g)


def _add_halves(g, got, c_idx, name="grad_add_halves"):
    rows = g.shape[1]
    half = rows // 2
    tr = _row_tile(half, 16)
    nt = half // tr

    def body(c_ref, g_ref, r_ref, o_ref):
        o_ref[...] = (g_ref[...] + r_ref[...]).astype(BF16)

    return pl.pallas_call(
        body, name=name,
        grid_spec=pltpu.PrefetchScalarGridSpec(
            num_scalar_prefetch=1, grid=(4, nt),
            in_specs=[pl.BlockSpec((1, tr, D), lambda q, i, c_ref: (q, c_ref[0] * nt + i, 0)),
                      pl.BlockSpec((1, tr, D), lambda q, i, c_ref: (q, i, 0))],
            out_specs=pl.BlockSpec((1, tr, D), lambda q, i, c_ref: (q, i, 0))),
        out_shape=jax.ShapeDtypeStruct((4, half, D), BF16),
        compiler_params=_params(("parallel", "parallel")),
    )(c_idx, g, got)


def _scatter_chips(p):
    half = p.shape[1]

    def body(src, dst, send_sems, recv_sems):
        x, y, c, chips = _place()
        me = 2 * x + y
        sends = [pltpu.make_async_remote_copy(
            src_ref=src.at[2 * cx + cy], dst_ref=dst.at[me], send_sem=send_sems.at[k], recv_sem=recv_sems.at[k],
            device_id=(cx, cy, c), device_id_type=MESH) for k, (cx, cy) in enumerate(chips)]
        for cp in sends:
            cp.start()
        for k, (cx, cy) in enumerate(chips):
            pltpu.make_async_remote_copy(
                src_ref=src.at[me], dst_ref=dst.at[2 * cx + cy], send_sem=send_sems.at[k], recv_sem=recv_sems.at[k],
                device_id=(cx, cy, c), device_id_type=MESH).wait_recv()
        for cp in sends:
            cp.wait_send()

    got = pl.pallas_call(
        body, name="grad_scatter_chips",
        in_specs=[pl.BlockSpec(memory_space=pl.ANY)], out_specs=pl.BlockSpec(memory_space=pl.ANY),
        out_shape=jax.ShapeDtypeStruct((4, half, D), p.dtype),
        scratch_shapes=[pltpu.SemaphoreType.DMA((3,)), pltpu.SemaphoreType.DMA((3,))],
        compiler_params=pltpu.CompilerParams(has_side_effects=True),
    )(p)
    return _own_slot(got, _own_piece(p))


def _own_piece(p):
    chip = 2 * lax.axis_index("x") + lax.axis_index("y")
    return lax.dynamic_index_in_dim(p, chip, axis=0, keepdims=False)


def _sum_chips(r, c_idx, name="grad_sum_chips"):
    half = r.shape[1]
    tr = _row_tile(half, 16)
    nt = half // tr

    def body(c_ref, r_ref, o_ref):
        acc = r_ref[0].astype(F32)
        for q in range(1, 4):
            acc = acc + r_ref[q].astype(F32)
        o_ref[...] = acc

    return pl.pallas_call(
        body, name=name,
        grid_spec=pltpu.PrefetchScalarGridSpec(
            num_scalar_prefetch=1, grid=(nt,),
            in_specs=[pl.BlockSpec((4, tr, D), lambda i, c_ref: (0, i, 0))],
            out_specs=pl.BlockSpec((tr, D), lambda i, c_ref: (c_ref[0] * nt + i, 0))),
        out_shape=jax.ShapeDtypeStruct((2 * half, D), F32),
        compiler_params=_params(("parallel",)),
    )(c_idx, r)


def _join_halves(f):
    half = f.shape[0] // 2

    def body(src, dst, send_sem, recv_sem):
        x, y, c, _ = _place()
        cp = pltpu.make_async_remote_copy(
            src_ref=src.at[pl.ds(c * half, half), :], dst_ref=dst.at[pl.ds(c * half, half), :], send_sem=send_sem,
            recv_sem=recv_sem, device_id=(x, y, 1 - c), device_id_type=MESH)
        cp.start()
        pltpu.make_async_remote_copy(
            src_ref=src.at[pl.ds(c * half, half), :], dst_ref=dst.at[pl.ds((1 - c) * half, half), :],
            send_sem=send_sem, recv_sem=recv_sem, device_id=(x, y, 1 - c), device_id_type=MESH).wait_recv()
        cp.wait_send()

    return pl.pallas_call(
        body, name="grad_join_halves",
        in_specs=[pl.BlockSpec(memory_space=pl.ANY)], out_specs=pl.BlockSpec(memory_space=pl.ANY),
        out_shape=jax.ShapeDtypeStruct(f.shape, f.dtype), input_output_aliases={0: 0},
        scratch_shapes=[pltpu.SemaphoreType.DMA, pltpu.SemaphoreType.DMA],
        compiler_params=pltpu.CompilerParams(has_side_effects=True),
    )(f)


_HBM = pl.BlockSpec(memory_space=pltpu.HBM)
_SEM = pl.BlockSpec(memory_space=pltpu.SEMAPHORE)
_EFFECT = pltpu.SideEffectType.DATAFLOW_SIDE_EFFECTING


def _start_copies(name, bufs, n_sems, plan):
    nb = len(bufs)

    def body(*refs):
        for cp in plan(refs[:nb], refs[nb], refs[nb + 1])[0]:
            cp.start()
        refs[-1][...] = jnp.zeros((8, 128), F32)

    out = pl.pallas_call(
        body, name=name,
        out_shape=(pltpu.SemaphoreType.DMA((n_sems,)), pltpu.SemaphoreType.DMA((n_sems,)),
                   *[pltpu.HBM(b.shape, b.dtype) for b in bufs], jax.ShapeDtypeStruct((8, 128), F32)),
        in_specs=[_HBM] * nb, out_specs=(_SEM, _SEM, *[_HBM] * nb, pl.BlockSpec(memory_space=pltpu.VMEM)),
        input_output_aliases={i: 2 + i for i in range(nb)},
        compiler_params=pltpu.CompilerParams(has_side_effects=_EFFECT),
    )(*[pltpu.with_memory_space_constraint(b, pltpu.HBM) for b in bufs])
    return out[0], out[1], list(out[2:2 + nb]), out[-1]


def _wait_copies(name, send_sems, recv_sems, bufs, after, plan):
    nb = len(bufs)

    def body(*refs):
        _, sends, recvs = plan(refs[:nb], refs[nb], refs[nb + 1])
        for cp in sends:
            cp.wait_send()
        for cp in recvs:
            cp.wait_recv()

    out = pl.pallas_call(
        body, name=name,
        out_shape=tuple(pltpu.HBM(b.shape, b.dtype) for b in bufs),
        in_specs=[_HBM] * nb + [_SEM, _SEM] + [pl.BlockSpec(memory_space=pl.ANY)] * len(after),
        out_specs=tuple([_HBM] * nb),
        input_output_aliases={i: i for i in range(nb)},
        compiler_params=pltpu.CompilerParams(has_side_effects=_EFFECT),
    )(*bufs, send_sems, recv_sems, *after)
    return list(out)


def _plan_gather_direct(half):
    def plan(refs, send_sems, recv_sems):
        src, land = refs
        x, y, c, chips = _place()
        starts, recvs = [], []
        for k, (cx, cy) in enumerate(chips):
            for d in range(2):
                other = c if d == 0 else 1 - c
                i = 2 * k + d
                starts.append(pltpu.make_async_remote_copy(
                    src_ref=src.at[pl.ds(c * half, half), :], dst_ref=land.at[2 * x + y, pl.ds(c * half, half), :],
                    send_sem=send_sems.at[i], recv_sem=recv_sems.at[i], device_id=(cx, cy, other),
                    device_id_type=MESH))
                recvs.append(pltpu.make_async_remote_copy(
                    src_ref=src.at[pl.ds(c * half, half), :],
                    dst_ref=land.at[2 * cx + cy, pl.ds(other * half, half), :],
                    send_sem=send_sems.at[i], recv_sem=recv_sems.at[i], device_id=(cx, cy, other),
                    device_id_type=MESH))
        return starts, starts, recvs
    return plan


def _plan_swap_halves(half):
    def plan(refs, send_sems, recv_sems):
        src, land = refs
        x, y, c, _ = _place()
        cp = pltpu.make_async_remote_copy(
            src_ref=src.at[:, pl.ds((1 - c) * half, half), :], dst_ref=land, send_sem=send_sems.at[0],
            recv_sem=recv_sems.at[0], device_id=(x, y, 1 - c), device_id_type=MESH)
        return [cp], [cp], [cp]
    return plan


def _plan_scatter_chips():
    def plan(refs, send_sems, recv_sems):
        src, land = refs
        x, y, c, chips = _place()
        me = 2 * x + y
        starts = [pltpu.make_async_remote_copy(
            src_ref=src.at[2 * cx + cy], dst_ref=land.at[me], send_sem=send_sems.at[k], recv_sem=recv_sems.at[k],
            device_id=(cx, cy, c), device_id_type=MESH) for k, (cx, cy) in enumerate(chips)]
        recvs = [pltpu.make_async_remote_copy(
            src_ref=src.at[me], dst_ref=land.at[2 * cx + cy], send_sem=send_sems.at[k], recv_sem=recv_sems.at[k],
            device_id=(cx, cy, c), device_id_type=MESH) for k, (cx, cy) in enumerate(chips)]
        return starts, starts, recvs
    return plan


def _plan_join_halves(half):
    def plan(refs, send_sems, recv_sems):
        (buf,) = refs
        x, y, c, _ = _place()
        cp = pltpu.make_async_remote_copy(
            src_ref=buf.at[pl.ds(c * half, half), :], dst_ref=buf.at[pl.ds(c * half, half), :],
            send_sem=send_sems.at[0], recv_sem=recv_sems.at[0], device_id=(x, y, 1 - c), device_id_type=MESH)
        got = pltpu.make_async_remote_copy(
            src_ref=buf.at[pl.ds(c * half, half), :], dst_ref=buf.at[pl.ds((1 - c) * half, half), :],
            send_sem=send_sems.at[0], recv_sem=recv_sems.at[0], device_id=(x, y, 1 - c), device_id_type=MESH)
        return [cp], [cp], [got]
    return plan


def _allreduce_small(vec):
    def body(v_ref, o_ref, gath, send_sems, recv_sems):
        x, y, c, _ = _place()
        me = 4 * x + 2 * y + c
        gath[me] = v_ref[...]
        sends = []
        for k in range(1, 8):
            peer = (x ^ (k >> 2), y ^ ((k >> 1) & 1), c ^ (k & 1))
            sends.append(pltpu.make_async_remote_copy(
                src_ref=v_ref, dst_ref=gath.at[me], send_sem=send_sems.at[k - 1], recv_sem=recv_sems.at[k - 1],
                device_id=peer, device_id_type=MESH))
        for cp in sends:
            cp.start()
        for k in range(1, 8):
            peer = (x ^ (k >> 2), y ^ ((k >> 1) & 1), c ^ (k & 1))
            pltpu.make_async_remote_copy(
                src_ref=v_ref, dst_ref=gath.at[4 * peer[0] + 2 * peer[1] + peer[2]], send_sem=send_sems.at[k - 1],
                recv_sem=recv_sems.at[k - 1], device_id=peer, device_id_type=MESH).wait_recv()
        for cp in sends:
            cp.wait_send()
        acc = gath[0]
        for d in range(1, 8):
            acc = acc + gath[d]
        o_ref[...] = acc

    return pl.pallas_call(
        body, name="allreduce_small",
        in_specs=[pl.BlockSpec(memory_space=pltpu.VMEM)], out_specs=pl.BlockSpec(memory_space=pltpu.VMEM),
        out_shape=jax.ShapeDtypeStruct(vec.shape, F32),
        scratch_shapes=[pltpu.VMEM((8,) + vec.shape, F32), pltpu.SemaphoreType.DMA((7,)),
                        pltpu.SemaphoreType.DMA((7,))],
    )(vec)


def _adamw(w, g, m, v, name):
    shape = w.shape
    if w.ndim == 1 or w.size <= 128 * 128:
        two_d = (1, w.size) if w.size % 128 else (w.size // 128, 128)
    else:
        two_d = (w.shape[0], w.size // w.shape[0])
    rows, cols = two_d
    tr = next((t for t in (256, 176, 128) if rows % t == 0), rows)

    def body(w_ref, g_ref, m_ref, v_ref, d_ref, nm_ref, nv_ref):
        gr = g_ref[...]
        nm = B1 * m_ref[...] + (1.0 - B1) * gr
        nv = B2 * v_ref[...] + (1.0 - B2) * (gr * gr)
        m_hat = nm / (1.0 - B1 ** STEP)
        v_hat = nv / (1.0 - B2 ** STEP)
        d_ref[...] = -LR * (m_hat / (jnp.sqrt(v_hat) + AEPS) + WD * w_ref[...])
        nm_ref[...] = nm
        nv_ref[...] = nv

    spec = pl.BlockSpec((tr, cols), lambda i: (i, 0))
    outs = pl.pallas_call(
        body, name=name, grid=(rows // tr,),
        in_specs=[spec] * 4, out_specs=[spec] * 3,
        out_shape=[jax.ShapeDtypeStruct(two_d, F32)] * 3,
        compiler_params=_params(("parallel",)),
    )(*[t.reshape(two_d) for t in (w, g, m, v)])
    return [o.reshape(shape) for o in outs]


def _cols_stacked(g, lo, n_rows, width):
    return jnp.concatenate([g[q, lo:lo + n_rows].reshape(D, width) for q in range(4)], axis=1)


def _rows_stacked(g, lo, n_rows):
    return g[:, lo:lo + n_rows].reshape(4 * n_rows, D)


def _cols_to_slots(t, width):
    return jnp.transpose(t.reshape(D, 4, width), (1, 0, 2)).reshape(4, width, D)


def _rows_to_slots(t):
    return t.reshape(4, t.shape[0] // 4, D)


def _pack_first(w_in, w_conv_out, conv_dw_w):
    dw = jnp.pad(conv_dw_w.reshape(CW, 256), ((0, 1), (0, 0)))
    dw_bits = lax.bitcast_convert_type(dw, BF16).reshape(16, D)
    return jnp.concatenate([w_in.reshape(ROWS_W_IN, D).astype(BF16), w_conv_out.astype(BF16), dw_bits,
                            jnp.zeros((16, D), BF16)], axis=0)


def _unpack_first(g):
    w_in = _cols_stacked(g, 0, ROWS_W_IN, ROWS_W_IN)
    wc = _rows_stacked(g, ROWS_W_IN, ROWS_SQ)
    o = ROWS_W_IN + ROWS_SQ
    dw = lax.bitcast_convert_type(g[:, o:o + 16].reshape(4, 32, 256, 2), F32)
    return w_in, wc, jnp.transpose(dw, (1, 0, 2)).reshape(32, D)


def _pack_late(w_attn_out, w_merge_out, w_ffn_in, w_ffn_down):
    return jnp.concatenate([w_attn_out.astype(BF16), w_merge_out.astype(BF16),
                            w_ffn_in.reshape(ROWS_FFN_IN, D).astype(BF16), w_ffn_down.astype(BF16)], axis=0)


def _unpack_late(g):
    wa = _rows_stacked(g, 0, ROWS_SQ)
    wm = _rows_stacked(g, ROWS_SQ, ROWS_SQ)
    wf = _cols_stacked(g, 2 * ROWS_SQ, ROWS_FFN_IN, ROWS_FFN_IN)
    wd = _rows_stacked(g, 2 * ROWS_SQ + ROWS_FFN_IN, ROWS_DOWN)
    return wa, wm, wf, wd


def _pack_early_grads(d_wd, d_wf, d_wm, d_wa, d_wc):
    return jnp.concatenate([_rows_to_slots(d_wd), _cols_to_slots(d_wf, ROWS_FFN_IN), _rows_to_slots(d_wm),
                            _rows_to_slots(d_wa), _rows_to_slots(d_wc)], axis=1)


def _unpack_early_shard(s):
    o = 0
    out = []
    for n_rows, shape in ((ROWS_DOWN, (ROWS_DOWN, D)), (ROWS_FFN_IN, (D, ROWS_FFN_IN)), (ROWS_SQ, (ROWS_SQ, D)),
                          (ROWS_SQ, (ROWS_SQ, D)), (ROWS_SQ, (ROWS_SQ, D))):
        out.append(s[o:o + n_rows].reshape(shape))
        o += n_rows
    return out


class _Exchanges:
    def __init__(self, late_pack):
        self.c_idx = lax.axis_index("c").astype(jnp.int32).reshape(1)
        rows = late_pack.shape[0]
        self.late_plan = _plan_gather_direct(rows // 2)
        slots = lax.empty((4, rows, D), BF16)
        self.late = _start_copies("gather_late_start", [late_pack, slots], 6, self.late_plan)
        self.first_token = self.late[3][0, 0]

    def late_weights(self, after):
        send_sems, recv_sems, bufs, _ = self.late
        pack, slots = _wait_copies("gather_late_wait", send_sems, recv_sems, bufs, after, self.late_plan)
        return _unpack_late(_own_slot(slots, pack))

    def reduce_start(self, mats):
        g = _pack_early_grads(*mats)
        self.half = g.shape[1] // 2
        self.swap_plan = _plan_swap_halves(self.half)
        self.swap = _start_copies("grad_swap_start", [g, lax.empty((4, self.half, D), F32)], 1, self.swap_plan)
        return self.swap[3][0, 0]

    def reduce_mid(self, after):
        send_sems, recv_sems, bufs, _ = self.swap
        g, got = _wait_copies("grad_swap_wait", send_sems, recv_sems, bufs, after, self.swap_plan)
        p = _add_halves(g, got, self.c_idx, "grad_add_halves_early")
        self.scatter_plan = _plan_scatter_chips()
        self.scatter = _start_copies("grad_scatter_start", [p, lax.empty(p.shape, BF16)], 3, self.scatter_plan)
        return self.scatter[3][0, 0]

    def reduce_late(self, after):
        send_sems, recv_sems, bufs, _ = self.scatter
        p, got = _wait_copies("grad_scatter_wait", send_sems, recv_sems, bufs, after, self.scatter_plan)
        f = _sum_chips(_own_slot(got, _own_piece(p)), self.c_idx, "grad_sum_chips_early")
        self.join_plan = _plan_join_halves(self.half)
        self.join = _start_copies("grad_join_start", [f], 1, self.join_plan)

    def reduce_end(self, after):
        send_sems, recv_sems, bufs, _ = self.join
        (shard,) = _wait_copies("grad_join_wait", send_sems, recv_sems, bufs, after, self.join_plan)
        return _unpack_early_shard(shard)

    def reduce_w_in(self, d_w_in):
        g = _cols_to_slots(d_w_in, ROWS_W_IN)
        p = _add_halves(g, _swap_halves(g), self.c_idx)
        return _join_halves(_sum_chips(_scatter_chips(p), self.c_idx)).reshape(D, ROWS_W_IN)


def _local_grads(x, loss_target, norm_mix_g, conv_dw_b, conv_ln_g, conv_ln_b, q_norm_g, k_norm_g, sinks, norm_ffn_g,
                 w_in, wc, dw, exchanges):
    n_seq, S, _ = x.shape
    T = n_seq * S
    x2 = x.reshape(T, D)
    tgt = loss_target.reshape(T, D)
    row = lambda t: t.reshape(1, -1)
    g1, g2 = row(norm_mix_g), row(norm_ffn_g)
    qg, kg = row(q_norm_g), row(k_norm_g)
    lng, lnb, dwb = row(conv_ln_g), row(conv_ln_b), row(conv_dw_b)
    sink_rows = jnp.repeat(sinks.reshape(NKV, GROUP), BLK, axis=1)

    xn, ag, q, kv, gg = _in_proj(x2, g1 + exchanges.first_token, w_in)
    h1, h3, yc = _conv_fwd(ag, dw, dwb, lng, lnb, wc, n_seq, S)
    o = _attn_fwd(q, kv, qg, kg, sink_rows, n_seq, S)
    wa, wm, wf, wd = exchanges.late_weights([o, yc])
    ya, mix, h = _merge_fwd(x2, gg, yc, o, wa, wm)
    dh, dhb, hn, act, dout, dgu, ffn_stats = _ffn(h, tgt, g2, wf, wd)
    d_wd = _tn_matmul(act, dout, "dw_ffn_down")
    d_wf = _tn_matmul(hn, dgu, "dw_ffn_in")
    d_wm = _tn_matmul(mix, dhb, "dw_merge")
    dgg, dyc, dya, do, dh3 = _merge_bwd(dhb, gg, yc, ya, wm, wa, wc)
    d_wa = _tn_matmul(o, dya, "dw_attn_out")
    d_wc = _tn_matmul(h3, dyc, "dw_conv_out")
    token = exchanges.reduce_start((d_wd, d_wf, d_wm, d_wa, d_wc))
    dq, dkc, dkp, dqg, dsk = _attn_bwd(q, kv, do, qg + token, kg, sink_rows, n_seq, S)
    token = exchanges.reduce_mid([dq])
    dkv, dkg = _kv_bwd(kv, dkc, dkp, kg, n_seq, S)
    dag, ddw, conv_stats = _conv_bwd(h1, dh3, ag, dw, lng + token, lnb, n_seq, S)
    dx, dg1 = _in_proj_bwd(dag, dq, dkv, dgg, dh, x2, g1, w_in)
    exchanges.reduce_late([dx])
    d_w_in = jnp.concatenate([_tn_matmul(xn, dag, "dw_in_conv"), _tn_matmul(xn, dq, "dw_in_q"),
                              _tn_matmul(xn, dkv, "dw_in_kv"), _tn_matmul(xn, dgg, "dw_in_gates")], axis=1)

    heads = jnp.concatenate([dqg[0], dkg[0], dsk[0, :NQ], jnp.zeros((D - 2 * HD - NQ,), F32)])
    vec = jnp.concatenate([dg1, conv_stats[0:3], ffn_stats[0:1], heads[None], jnp.zeros((2, D), F32), ddw], axis=0)
    return ffn_stats[1], dx.reshape(x.shape), d_w_in, vec


def kernel(x, norm_mix_g, w_in, conv_dw_w, conv_dw_b, conv_ln_g, conv_ln_b, w_conv_out, q_norm_g, k_norm_g, sinks, w_attn_out, w_merge_out, norm_ffn_g, w_ffn_in, w_ffn_down, loss_target, m_norm_mix_g, m_w_in, m_conv_dw_w, m_conv_dw_b, m_conv_ln_g, m_conv_ln_b, m_w_conv_out, m_q_norm_g, m_k_norm_g, m_sinks, m_w_attn_out, m_w_merge_out, m_norm_ffn_g, m_w_ffn_in, m_w_ffn_down, v_norm_mix_g, v_w_in, v_conv_dw_w, v_conv_dw_b, v_conv_ln_g, v_conv_ln_b, v_w_conv_out, v_q_norm_g, v_k_norm_g, v_sinks, v_w_attn_out, v_w_merge_out, v_norm_ffn_g, v_w_ffn_in, v_w_ffn_down):
    chip = 2 * lax.axis_index("x") + lax.axis_index("y")

    first, token = _gather_weights(_pack_first(w_in, w_conv_out, conv_dw_w))
    exchanges = _Exchanges(_pack_late(w_attn_out, w_merge_out, w_ffn_in, w_ffn_down) + token.astype(BF16))
    sq_cols, grad_x, d_w_in, vec = _local_grads(x, loss_target, norm_mix_g, conv_dw_b, conv_ln_g, conv_ln_b,
                                                q_norm_g, k_norm_g, sinks, norm_ffn_g, *_unpack_first(first),
                                                exchanges)
    loss = lax.psum(0.5 / D * jnp.sum(sq_cols), ("x", "y", "c"))

    g_w_in = exchanges.reduce_w_in(d_w_in)
    g_wd, g_wf, g_wm, g_wa, g_wc = exchanges.reduce_end([g_w_in])
    small = _allreduce_small(vec)
    g_dw = lax.dynamic_slice_in_dim(small[8:8 + CW], chip * 256, 256, axis=1).reshape(CW, 1, 256)
    grads = {
        "norm_mix_g": small[0], "w_in": g_w_in, "conv_dw_w": g_dw, "conv_dw_b": small[1], "conv_ln_g": small[2],
        "conv_ln_b": small[3], "w_conv_out": g_wc, "q_norm_g": small[5, 0:HD], "k_norm_g": small[5, HD:2 * HD],
        "sinks": small[5, 2 * HD:2 * HD + NQ], "w_attn_out": g_wa, "w_merge_out": g_wm, "norm_ffn_g": small[4],
        "w_ffn_in": g_wf, "w_ffn_down": g_wd,
    }
    weights = dict(norm_mix_g=norm_mix_g, w_in=w_in, conv_dw_w=conv_dw_w, conv_dw_b=conv_dw_b, conv_ln_g=conv_ln_g,
                   conv_ln_b=conv_ln_b, w_conv_out=w_conv_out, q_norm_g=q_norm_g, k_norm_g=k_norm_g, sinks=sinks,
                   w_attn_out=w_attn_out, w_merge_out=w_merge_out, norm_ffn_g=norm_ffn_g, w_ffn_in=w_ffn_in,
                   w_ffn_down=w_ffn_down)
    m_in = dict(norm_mix_g=m_norm_mix_g, w_in=m_w_in, conv_dw_w=m_conv_dw_w, conv_dw_b=m_conv_dw_b,
                conv_ln_g=m_conv_ln_g, conv_ln_b=m_conv_ln_b, w_conv_out=m_w_conv_out, q_norm_g=m_q_norm_g,
                k_norm_g=m_k_norm_g, sinks=m_sinks, w_attn_out=m_w_attn_out, w_merge_out=m_w_merge_out,
                norm_ffn_g=m_norm_ffn_g, w_ffn_in=m_w_ffn_in, w_ffn_down=m_w_ffn_down)
    v_in = dict(norm_mix_g=v_norm_mix_g, w_in=v_w_in, conv_dw_w=v_conv_dw_w, conv_dw_b=v_conv_dw_b,
                conv_ln_g=v_conv_ln_g, conv_ln_b=v_conv_ln_b, w_conv_out=v_w_conv_out, q_norm_g=v_q_norm_g,
                k_norm_g=v_k_norm_g, sinks=v_sinks, w_attn_out=v_w_attn_out, w_merge_out=v_w_merge_out,
                norm_ffn_g=v_norm_ffn_g, w_ffn_in=v_w_ffn_in, w_ffn_down=v_w_ffn_down)
    names = list(weights)
    deltas, new_m, new_v = [], [], []
    for n in names:
        d, nm, nv = _adamw(weights[n], grads[n], m_in[n], v_in[n], "adamw_" + n)
        deltas.append(d)
        new_m.append(nm)
        new_v.append(nv)
    return (loss, grad_x, *[grads[n] for n in names], *deltas, *new_m, *new_v)
```

```python
import functools
import math

import jax
import jax.numpy as jnp
import numpy as np
from jax import lax
from jax.experimental import pallas as pl
from jax.experimental.pallas import tpu as pltpu

F32 = jnp.float32
BF16 = jnp.bfloat16

D = 1024
CW = 31
HD = 64
NQ = 16
NKV = 2
GROUP = NQ // NKV
BLK = 128
DFF = 2816
EPS = 1e-6
NEG = -1e30
IN_COLS = 5376
SCALE = 1.0 / math.sqrt(HD)

LR, B1, B2, AEPS, WD, STEP = 0.001, 0.9, 0.999, 1e-08, 0.01, 10

MIB = 1024 * 1024
MESH = pl.DeviceIdType.MESH

ROWS_W_IN = 1344
ROWS_SQ = 256
ROWS_FFN_IN = 1408
ROWS_DOWN = 704
ROWS_MAT = ROWS_W_IN + 3 * ROWS_SQ + ROWS_FFN_IN + ROWS_DOWN
ROWS_DW = 32
ROWS_PACK = ROWS_MAT + ROWS_DW
VEC_ROWS = 40


def _sig(x):
    return 1.0 / (1.0 + jnp.exp(-x))


def _dot(a, b):
    return jnp.dot(a, b, preferred_element_type=F32)


def _dot_nt(a, b):
    return lax.dot_general(a, b, (((1,), (1,)), ((), ())), preferred_element_type=F32)


def _dot_tn(a, b):
    return lax.dot_general(a, b, (((0,), (0,)), ((), ())), preferred_element_type=F32)


def _params(sem, vmem_mib=48):
    return pltpu.CompilerParams(dimension_semantics=sem, vmem_limit_bytes=vmem_mib * MIB)


def _resident(shape):
    return pl.BlockSpec(shape, lambda *_: (0,) * len(shape), pipeline_mode=pl.Buffered(1))


def _whole(shape):
    return pl.BlockSpec(shape, lambda *_: (0,) * len(shape))


def _rows(tm, cols, col_block=0):
    return pl.BlockSpec((tm, cols), lambda i: (i, col_block))


def _tap_phases():
    return [(phase, list(range(phase, CW, 8))) for phase in range(8)]


def _shift_copies(dst, src, base):
    for phase, taps in _tap_phases():
        n = dst.shape[1] - 8 * (4 - len(taps))
        dst[phase, 0:n, :] = src[base + phase:base + phase + n, :]


def _in_proj_conv_fwd(x2, g1, w_in, dw, dwb, lng, lnb, wc, n_seq, S):
    T = n_seq * S
    tc = min(256, S)
    nt = S // tc

    def body(x_ref, g_ref, w_ref, dw_ref, dwb_ref, lng_ref, lnb_ref, wc_ref, xn_ref, ag_ref, q_ref, kv_ref, gg_ref,
             h1_ref, h3_ref, yc_ref, ext, sh):
        i = pl.program_id(1)

        @pl.when(i == 0)
        def _():
            ext[0:32, :] = jnp.zeros((32, D), F32)

        x = x_ref[...]
        rstd = lax.rsqrt(jnp.mean(x * x, axis=-1, keepdims=True) + EPS)
        xn = (x * rstd * g_ref[...]).astype(BF16)
        xn_ref[...] = xn
        ag = _dot_nt(xn, w_ref[0:2048, :])
        ag_ref[...] = ag
        ext[32:32 + tc, :] = ag[:, 0:D] * _sig(ag[:, D:2 * D])
        q_ref[...] = _dot_nt(xn, w_ref[2048:3072, :])
        kv_ref[...] = _dot_nt(xn, w_ref[3072:3328, :])
        gg_ref[...] = _dot_nt(xn, w_ref[3328:5376, :])
        _shift_copies(sh, ext, 2)
        for cb in range(D // 128):
            cs = slice(cb * 128, (cb + 1) * 128)
            acc = jnp.broadcast_to(dwb_ref[:, cs], (tc, 128))
            for phase, taps in _tap_phases():
                for m, j in enumerate(taps):
                    acc = acc + dw_ref[j:j + 1, cs] * sh[phase, 8 * m:8 * m + tc, cs]
            h1_ref[:, cs] = acc
        ext[0:32, :] = ext[tc:tc + 32, :]
        h1 = h1_ref[...]
        mu = jnp.mean(h1, axis=-1, keepdims=True)
        cen = h1 - mu
        var = jnp.mean(cen * cen, axis=-1, keepdims=True)
        h2 = cen * lax.rsqrt(var + EPS) * lng_ref[...] + lnb_ref[...]
        h3 = (h2 * _sig(h2)).astype(BF16)
        h3_ref[...] = h3
        yc_ref[...] = _dot(h3, wc_ref[...])

    tile = lambda cols: pl.BlockSpec((tc, cols), lambda b, i: (b * nt + i, 0))
    shape = lambda cols, dtype: jax.ShapeDtypeStruct((T, cols), dtype)
    return pl.pallas_call(
        body, name="in_proj_conv_fwd", grid=(n_seq, nt),
        in_specs=[tile(D), _resident((1, D)), _resident((IN_COLS, D)), _resident((32, D)), _resident((1, D)),
                  _resident((1, D)), _resident((1, D)), _resident((D, D))],
        out_specs=[tile(D), tile(2 * D), tile(D), tile(256), tile(2 * D), tile(D), tile(D), tile(D)],
        out_shape=[shape(D, BF16), shape(2 * D, F32), shape(D, F32), shape(256, F32), shape(2 * D, F32),
                   shape(D, F32), shape(D, BF16), shape(D, F32)],
        scratch_shapes=[pltpu.VMEM((32 + tc, D), F32), pltpu.VMEM((8, tc + 24, D), F32)],
        compiler_params=_params(("parallel", "arbitrary"), 56),
    )(x2, g1, w_in, dw, dwb, lng, lnb, wc)


def _attn_consts():
    k = np.arange(BLK)[:, None]
    i = np.arange(GROUP * BLK)[None, :] % BLK
    from_prev = k > i
    dist = np.where(from_prev, i + BLK - k, i - k).astype(np.float32)
    head = np.arange(GROUP * BLK)[None, :] // BLK
    bias = []
    for kh in range(NKV):
        slope = np.exp2(-8.0 * (kh * GROUP + head + 1) / NQ).astype(np.float32)
        bias.append(-slope * dist)
    return jnp.asarray(from_prev.astype(np.float32)), jnp.asarray(np.stack(bias))


def _stack_heads(ref, kh):
    return jnp.concatenate([ref[:, (kh * GROUP + g) * HD:(kh * GROUP + g + 1) * HD] for g in range(GROUP)], axis=0)


def _rms64(t):
    return lax.rsqrt(jnp.mean(t * t, axis=-1, keepdims=True) + EPS)


def _attn_probs(kh, n, q_ref, kvc_ref, kvp_ref, qg_ref, kg_ref, tri_ref, bias_ref, sink_ref):
    ks = slice(kh * HD, (kh + 1) * HD)
    vs = slice(2 * HD + kh * HD, 2 * HD + (kh + 1) * HD)
    kp, kc = kvp_ref[:, ks], kvc_ref[:, ks]
    kpb = (kp * _rms64(kp) * kg_ref[...]).astype(BF16)
    kcb = (kc * _rms64(kc) * kg_ref[...]).astype(BF16)
    qs = _stack_heads(q_ref, kh)
    rq = _rms64(qs)
    qy = qs * rq
    qhb = (qy * (qg_ref[...] * SCALE)).astype(BF16)
    from_prev = tri_ref[...] > 0.5
    no_prev = jnp.where(n > 0, 0.0, NEG)
    s = jnp.where(from_prev, _dot_nt(kpb, qhb) + no_prev, _dot_nt(kcb, qhb)) + bias_ref[kh]
    sink = sink_ref[kh:kh + 1, :]
    m = jnp.maximum(jnp.max(s, axis=0, keepdims=True), sink)
    e = jnp.exp(s - m)
    es = jnp.exp(sink - m)
    rz = 1.0 / (jnp.sum(e, axis=0, keepdims=True) + es)
    return e * rz, es * rz, from_prev, qhb, kpb, kcb, kvp_ref[:, vs].astype(BF16), kvc_ref[:, vs].astype(BF16), qy, rq


def _unfold(t, from_prev):
    zero = jnp.zeros_like(t)
    return jnp.where(from_prev, t, zero), jnp.where(from_prev, zero, t)


def _attn_specs(n_seq, S):
    nb = S // BLK
    cur = lambda cols: pl.BlockSpec((BLK, cols), lambda b, n: (b * nb + n, 0))
    prev = lambda cols: pl.BlockSpec((BLK, cols), lambda b, n: (b * nb + jnp.maximum(n - 1, 0), 0))
    consts = [_resident((1, HD)), _resident((1, HD)), _resident((BLK, GROUP * BLK)),
              _resident((NKV, BLK, GROUP * BLK)), _resident((NKV, GROUP * BLK))]
    return nb, cur, prev, consts


def _attn_fwd(q, kv, qg, kg, sink_rows, n_seq, S):
    T = n_seq * S
    nb, cur, prev, consts = _attn_specs(n_seq, S)
    tri, bias = _attn_consts()

    def body(q_ref, kvc_ref, kvp_ref, qg_ref, kg_ref, tri_ref, bias_ref, sink_ref, o_ref):
        n = pl.program_id(1)
        for kh in range(NKV):
            p, _, from_prev, _, _, _, vpb, vcb, _, _ = _attn_probs(kh, n, q_ref, kvc_ref, kvp_ref, qg_ref, kg_ref,
                                                                   tri_ref, bias_ref, sink_ref)
            pp, pc = _unfold(p.astype(BF16), from_prev)
            o = (_dot_tn(pp, vpb) + _dot_tn(pc, vcb)).astype(BF16)
            for g in range(GROUP):
                h = kh * GROUP + g
                o_ref[:, h * HD:(h + 1) * HD] = o[g * BLK:(g + 1) * BLK]

    return pl.pallas_call(
        body, name="attn_fwd", grid=(n_seq, nb),
        in_specs=[cur(D), cur(256), prev(256)] + consts,
        out_specs=cur(D),
        out_shape=jax.ShapeDtypeStruct((T, D), BF16),
        compiler_params=_params(("parallel", "parallel")),
    )(q, kv, kv, qg, kg, tri, bias, sink_rows)


def _merge_fwd(x2, gg, yc, o, wa, wm):
    T = x2.shape[0]
    tm = min(512, T)

    def body(x_ref, gg_ref, yc_ref, o_ref, wa_ref, wm_ref, ya_ref, mix_ref, h_ref):
        ya = _dot(o_ref[...], wa_ref[...])
        mix = (_sig(gg_ref[:, 0:D]) * yc_ref[...] + _sig(gg_ref[:, D:2 * D]) * ya).astype(BF16)
        ya_ref[...] = ya
        mix_ref[...] = mix
        h_ref[...] = x_ref[...] + _dot(mix, wm_ref[...])

    return pl.pallas_call(
        body, name="merge_fwd", grid=(T // tm,),
        in_specs=[_rows(tm, D), _rows(tm, 2 * D), _rows(tm, D), _rows(tm, D), _resident((D, D)), _resident((D, D))],
        out_specs=[_rows(tm, D), _rows(tm, D), _rows(tm, D)],
        out_shape=[jax.ShapeDtypeStruct((T, D), F32), jax.ShapeDtypeStruct((T, D), BF16),
                   jax.ShapeDtypeStruct((T, D), F32)],
        compiler_params=_params(("parallel",)),
    )(x2, gg, yc, o, wa, wm)


FF_CHUNK = DFF // 2


def _ffn(h, tgt, g2, wf, wd):
    T = h.shape[0]
    tm = min(256, T)

    def body(h_ref, t_ref, g_ref, wf_ref, wd_ref, dh_ref, dhb_ref, hn_ref, act_ref, dout_ref, dgu_ref, st_ref,
             gsc, usc):
        @pl.when(pl.program_id(0) == 0)
        def _():
            st_ref[...] = jnp.zeros((8, D), F32)

        hh = h_ref[...]
        rstd = lax.rsqrt(jnp.mean(hh * hh, axis=-1, keepdims=True) + EPS)
        hhat = hh * rstd
        hn = (hhat * g_ref[...]).astype(BF16)
        hn_ref[...] = hn
        out = hh
        for c in range(DFF // FF_CHUNK):
            cs = slice(c * FF_CHUNK, (c + 1) * FF_CHUNK)
            us = slice(DFF + c * FF_CHUNK, DFF + (c + 1) * FF_CHUNK)
            g = _dot(hn, wf_ref[c])
            u = _dot(hn, wf_ref[2 + c])
            gsc[:, cs] = g
            usc[:, cs] = u
            act = (g * _sig(g) * u).astype(BF16)
            act_ref[:, cs] = act
            out = out + _dot(act, wd_ref[cs, :])
        err = out - t_ref[...]
        dout = err * (1.0 / D)
        doutb = dout.astype(BF16)
        dout_ref[...] = doutb
        dhn = jnp.zeros((tm, D), F32)
        for c in range(DFF // FF_CHUNK):
            cs = slice(c * FF_CHUNK, (c + 1) * FF_CHUNK)
            us = slice(DFF + c * FF_CHUNK, DFF + (c + 1) * FF_CHUNK)
            g = gsc[:, cs]
            u = usc[:, cs]
            dact = _dot_nt(doutb, wd_ref[cs, :])
            sg = _sig(g)
            dg = (dact * u * (sg * (1.0 + g * (1.0 - sg)))).astype(BF16)
            du = (dact * (g * sg)).astype(BF16)
            dgu_ref[:, cs] = dg
            dgu_ref[:, us] = du
            dhn = dhn + _dot_nt(dg, wf_ref[c]) + _dot_nt(du, wf_ref[2 + c])
        st_ref[0:1, :] += jnp.sum(dhn * hhat, axis=0, keepdims=True)
        st_ref[1:2, :] += jnp.sum(err * err, axis=0, keepdims=True)
        dhh = dhn * g_ref[...]
        dh = dout + rstd * (dhh - hhat * jnp.mean(dhh * hhat, axis=-1, keepdims=True))
        dh_ref[...] = dh
        dhb_ref[...] = dh.astype(BF16)

    return pl.pallas_call(
        body, name="ffn_fwd_bwd", grid=(T // tm,),
        in_specs=[_rows(tm, D), _rows(tm, D), _resident((1, D)), _resident((4, D, FF_CHUNK)), _resident((DFF, D))],
        out_specs=[_rows(tm, D), _rows(tm, D), _rows(tm, D), _rows(tm, DFF), _rows(tm, D), _rows(tm, 2 * DFF),
                   _whole((8, D))],
        out_shape=[jax.ShapeDtypeStruct((T, D), F32), jax.ShapeDtypeStruct((T, D), BF16),
                   jax.ShapeDtypeStruct((T, D), BF16), jax.ShapeDtypeStruct((T, DFF), BF16),
                   jax.ShapeDtypeStruct((T, D), BF16), jax.ShapeDtypeStruct((T, 2 * DFF), BF16),
                   jax.ShapeDtypeStruct((8, D), F32)],
        scratch_shapes=[pltpu.VMEM((tm, DFF), F32), pltpu.VMEM((tm, DFF), F32)],
        compiler_params=_params(("arbitrary",), 56),
    )(h, tgt, g2, wf, wd)


def _merge_bwd(dhb, gg, yc, ya, wm, wa, wc):
    T = dhb.shape[0]
    tm = min(256, T)

    def body(dh_ref, gg_ref, yc_ref, ya_ref, wm_ref, wa_ref, wc_ref, dgg_ref, dyc_ref, dya_ref, do_ref, dh3_ref):
        dmix = _dot_nt(dh_ref[...], wm_ref[...])
        gc = _sig(gg_ref[:, 0:D])
        ga = _sig(gg_ref[:, D:2 * D])
        yc = yc_ref[...]
        ya = ya_ref[...]
        dgg_ref[:, 0:D] = (dmix * yc * gc * (1.0 - gc)).astype(BF16)
        dgg_ref[:, D:2 * D] = (dmix * ya * ga * (1.0 - ga)).astype(BF16)
        dyc = (dmix * gc).astype(BF16)
        dya = (dmix * ga).astype(BF16)
        dyc_ref[...] = dyc
        dya_ref[...] = dya
        do_ref[...] = _dot_nt(dya, wa_ref[...])
        dh3_ref[...] = _dot_nt(dyc, wc_ref[...])

    return pl.pallas_call(
        body, name="merge_bwd", grid=(T // tm,),
        in_specs=[_rows(tm, D), _rows(tm, 2 * D), _rows(tm, D), _rows(tm, D), _resident((D, D)), _resident((D, D)),
                  _resident((D, D))],
        out_specs=[_rows(tm, 2 * D), _rows(tm, D), _rows(tm, D), _rows(tm, D), _rows(tm, D)],
        out_shape=[jax.ShapeDtypeStruct((T, 2 * D), BF16), jax.ShapeDtypeStruct((T, D), BF16),
                   jax.ShapeDtypeStruct((T, D), BF16), jax.ShapeDtypeStruct((T, D), F32),
                   jax.ShapeDtypeStruct((T, D), F32)],
        compiler_params=_params(("parallel",)),
    )(dhb, gg, yc, ya, wm, wa, wc)


def _attn_bwd(q, kv, do, qg, kg, sink_rows, n_seq, S):
    T = n_seq * S
    nb, cur, prev, consts = _attn_specs(n_seq, S)
    tri, bias = _attn_consts()

    def body(q_ref, kvc_ref, kvp_ref, do_ref, qg_ref, kg_ref, tri_ref, bias_ref, sink_ref, dq_ref, dkc_ref, dkp_ref,
             dqg_ref, dsk_ref):
        n = pl.program_id(1)

        @pl.when((pl.program_id(0) == 0) & (n == 0))
        def _():
            dqg_ref[...] = jnp.zeros((1, HD), F32)
            dsk_ref[...] = jnp.zeros((8, 128), F32)

        lane = lax.broadcasted_iota(jnp.int32, (1, 128), 1)
        for kh in range(NKV):
            p, ps, from_prev, qhb, kpb, kcb, vpb, vcb, qy, rq = _attn_probs(
                kh, n, q_ref, kvc_ref, kvp_ref, qg_ref, kg_ref, tri_ref, bias_ref, sink_ref)
            dob = _stack_heads(do_ref, kh).astype(BF16)
            dp = jnp.where(from_prev, _dot_nt(vpb, dob), _dot_nt(vcb, dob))
            delta = jnp.sum(p * dp, axis=0, keepdims=True)
            dsp, dsc = _unfold((p * (dp - delta)).astype(BF16), from_prev)
            pp, pc = _unfold(p.astype(BF16), from_prev)
            dsink = -ps * delta
            dqh = (_dot_tn(dsp, kpb) + _dot_tn(dsc, kcb)) * SCALE
            dqg_ref[...] += jnp.sum(dqh * qy, axis=0, keepdims=True)
            dy = dqh * qg_ref[...]
            dq = (rq * (dy - qy * jnp.mean(dy * qy, axis=-1, keepdims=True))).astype(BF16)
            row = jnp.zeros((1, 128), F32)
            for g in range(GROUP):
                h = kh * GROUP + g
                dq_ref[:, h * HD:(h + 1) * HD] = dq[g * BLK:(g + 1) * BLK]
                row = row + jnp.where(lane == h, jnp.sum(dsink[:, g * BLK:(g + 1) * BLK], axis=1, keepdims=True), 0.0)
            dsk_ref[0:1, :] += row
            ks = slice(kh * HD, (kh + 1) * HD)
            vs = slice(2 * HD + kh * HD, 2 * HD + (kh + 1) * HD)
            dkp_ref[:, ks] = _dot(dsp, qhb)
            dkc_ref[:, ks] = _dot(dsc, qhb)
            dkp_ref[:, vs] = _dot(pp, dob)
            dkc_ref[:, vs] = _dot(pc, dob)

    return pl.pallas_call(
        body, name="attn_bwd", grid=(n_seq, nb),
        in_specs=[cur(D), cur(256), prev(256), cur(D)] + consts,
        out_specs=[cur(D), cur(256), cur(256), _whole((1, HD)), _whole((8, 128))],
        out_shape=[jax.ShapeDtypeStruct((T, D), BF16), jax.ShapeDtypeStruct((T, 256), F32),
                   jax.ShapeDtypeStruct((T, 256), F32), jax.ShapeDtypeStruct((1, HD), F32),
                   jax.ShapeDtypeStruct((8, 128), F32)],
        compiler_params=_params(("arbitrary", "arbitrary")),
    )(q, kv, kv, do, qg, kg, tri, bias, sink_rows)


def _kv_bwd(kv, dkc, dkp, kg, n_seq, S):
    T = n_seq * S
    nb, cur, _, _ = _attn_specs(n_seq, S)
    nxt = pl.BlockSpec((BLK, 256), lambda b, n: (b * nb + jnp.minimum(n + 1, nb - 1), 0))

    def body(kv_ref, dkc_ref, dkp_ref, kg_ref, dkv_ref, dkg_ref):
        n = pl.program_id(1)

        @pl.when((pl.program_id(0) == 0) & (n == 0))
        def _():
            dkg_ref[...] = jnp.zeros((1, HD), F32)

        has_next = jnp.where(n < nb - 1, 1.0, 0.0)
        d = dkc_ref[...] + has_next * dkp_ref[...]
        for kh in range(NKV):
            ks = slice(kh * HD, (kh + 1) * HD)
            k = kv_ref[:, ks]
            r = _rms64(k)
            y = k * r
            dkh = d[:, ks]
            dkg_ref[...] += jnp.sum(dkh * y, axis=0, keepdims=True)
            dy = dkh * kg_ref[...]
            dkv_ref[:, ks] = (r * (dy - y * jnp.mean(dy * y, axis=-1, keepdims=True))).astype(BF16)
        dkv_ref[:, 2 * HD:4 * HD] = d[:, 2 * HD:4 * HD].astype(BF16)

    return pl.pallas_call(
        body, name="kv_bwd", grid=(n_seq, nb),
        in_specs=[cur(256), cur(256), nxt, _resident((1, HD))],
        out_specs=[cur(256), _whole((1, HD))],
        out_shape=[jax.ShapeDtypeStruct((T, 256), BF16), jax.ShapeDtypeStruct((1, HD), F32)],
        compiler_params=_params(("arbitrary", "arbitrary")),
    )(kv, dkc, dkp, kg)


def _conv_bwd(h1, dh3, ag, dw, lng, lnb, n_seq, S):
    T = n_seq * S
    tc = min(256, S)
    nt = S // tc
    halo_blocks = tc // 32

    def body(h1_ref, dh3_ref, a_ref, gt_ref, ah_ref, gh_ref, dw_ref, lng_ref, lnb_ref, dag_ref, ddw_ref, st_ref,
             extd, ext0, acc8, shd, sh0):
        i = pl.program_id(1)
        ti = nt - 1 - i

        @pl.when((pl.program_id(0) == 0) & (i == 0))
        def _():
            acc8[...] = jnp.zeros((CW * 8, D), F32)
            st_ref[...] = jnp.zeros((8, D), F32)

        @pl.when(i == 0)
        def _():
            extd[tc:tc + 32, :] = jnp.zeros((32, D), F32)

        h1 = h1_ref[...]
        mu = jnp.mean(h1, axis=-1, keepdims=True)
        cen = h1 - mu
        rstd = lax.rsqrt(jnp.mean(cen * cen, axis=-1, keepdims=True) + EPS)
        xh = cen * rstd
        h2 = xh * lng_ref[...] + lnb_ref[...]
        sg = _sig(h2)
        dh2 = dh3_ref[...] * (sg * (1.0 + h2 * (1.0 - sg)))
        st_ref[1:2, :] += jnp.sum(dh2 * xh, axis=0, keepdims=True)
        st_ref[2:3, :] += jnp.sum(dh2, axis=0, keepdims=True)
        dxh = dh2 * lng_ref[...]
        dh1 = rstd * (dxh - jnp.mean(dxh, axis=-1, keepdims=True)
                      - xh * jnp.mean(dxh * xh, axis=-1, keepdims=True))
        st_ref[0:1, :] += jnp.sum(dh1, axis=0, keepdims=True)
        extd[0:tc, :] = dh1
        has_past = jnp.where(ti > 0, 1.0, 0.0)
        ext0[0:32, :] = has_past * (ah_ref[...] * _sig(gh_ref[...]))
        ext0[32:32 + tc, :] = a_ref[...] * _sig(gt_ref[...])
        _shift_copies(shd, extd, 0)
        _shift_copies(sh0, ext0, 2)
        for cb in range(D // 128):
            cs = slice(cb * 128, (cb + 1) * 128)
            for rb in range(tc // 128):
                rs = slice(rb * 128, (rb + 1) * 128)
                d = extd[rs, cs]
                acc = jnp.zeros((128, 128), F32)
                for phase, offs in _tap_phases():
                    for m, o in enumerate(offs):
                        j = CW - 1 - o
                        acc = acc + dw_ref[j:j + 1, cs] * shd[phase, rb * 128 + 8 * m:rb * 128 + 8 * m + 128, cs]
                for phase, taps in _tap_phases():
                    for m, j in enumerate(taps):
                        prod = d * sh0[phase, rb * 128 + 8 * m:rb * 128 + 8 * m + 128, cs]
                        acc8[j * 8:(j + 1) * 8, cs] += jnp.sum(prod.reshape(16, 8, 128), axis=0)
                sgt = _sig(gt_ref[rs, cs])
                dag_ref[rs, cs] = (acc * sgt).astype(BF16)
                dag_ref[rs, cb * 128 + D:(cb + 1) * 128 + D] = (acc * a_ref[rs, cs] * sgt * (1.0 - sgt)).astype(BF16)
        extd[tc:tc + 32, :] = extd[0:32, :]

        @pl.when((pl.program_id(0) == n_seq - 1) & (i == nt - 1))
        def _():
            for j in range(CW):
                ddw_ref[j:j + 1, :] = jnp.sum(acc8[j * 8:(j + 1) * 8, :], axis=0, keepdims=True)
            ddw_ref[CW:32, :] = jnp.zeros((32 - CW, D), F32)

    tile = lambda col: pl.BlockSpec((tc, D), lambda b, i: (b * nt + (nt - 1 - i), col))
    halo = lambda col: pl.BlockSpec(
        (32, D), lambda b, i: (jnp.maximum((b * nt + (nt - 1 - i)) * halo_blocks - 1, 0), col))
    return pl.pallas_call(
        body, name="conv_bwd", grid=(n_seq, nt),
        in_specs=[tile(0), tile(0), tile(0), tile(1), halo(0), halo(1), _resident((32, D)), _resident((1, D)),
                  _resident((1, D))],
        out_specs=[pl.BlockSpec((tc, 2 * D), lambda b, i: (b * nt + (nt - 1 - i), 0)), _whole((32, D)),
                   _whole((8, D))],
        out_shape=[jax.ShapeDtypeStruct((T, 2 * D), BF16), jax.ShapeDtypeStruct((32, D), F32),
                   jax.ShapeDtypeStruct((8, D), F32)],
        scratch_shapes=[pltpu.VMEM((tc + 32, D), F32), pltpu.VMEM((32 + tc, D), F32), pltpu.VMEM((CW * 8, D), F32),
                        pltpu.VMEM((8, tc + 24, D), F32), pltpu.VMEM((8, tc + 24, D), F32)],
        compiler_params=_params(("arbitrary", "arbitrary")),
    )(h1, dh3, ag, ag, ag, ag, dw, lng, lnb)


def _in_proj_bwd(dag, dq, dkv, dgg, dh, x2, g1, w_in):
    T = x2.shape[0]
    tm = min(512, T)

    def body(dag_ref, dq_ref, dkv_ref, dgg_ref, dh_ref, x_ref, g_ref, w_ref, dx_ref, dg_ref):
        @pl.when(pl.program_id(0) == 0)
        def _():
            dg_ref[...] = jnp.zeros((1, D), F32)

        dxn = (_dot(dag_ref[...], w_ref[0:2048, :]) + _dot(dq_ref[...], w_ref[2048:3072, :])
               + _dot(dkv_ref[...], w_ref[3072:3328, :]) + _dot(dgg_ref[...], w_ref[3328:5376, :]))
        x = x_ref[...]
        rstd = lax.rsqrt(jnp.mean(x * x, axis=-1, keepdims=True) + EPS)
        xh = x * rstd
        dg_ref[...] += jnp.sum(dxn * xh, axis=0, keepdims=True)
        dxh = dxn * g_ref[...]
        dx_ref[...] = dh_ref[...] + rstd * (dxh - xh * jnp.mean(dxh * xh, axis=-1, keepdims=True))

    return pl.pallas_call(
        body, name="in_proj_bwd", grid=(T // tm,),
        in_specs=[_rows(tm, 2 * D), _rows(tm, D), _rows(tm, 256), _rows(tm, 2 * D), _rows(tm, D), _rows(tm, D),
                  _resident((1, D)), _resident((IN_COLS, D))],
        out_specs=[_rows(tm, D), _whole((1, D))],
        out_shape=[jax.ShapeDtypeStruct((T, D), F32), jax.ShapeDtypeStruct((1, D), F32)],
        compiler_params=_params(("arbitrary",)),
    )(dag, dq, dkv, dgg, dh, x2, g1, w_in)


def _tn_matmul(a, b, name, column_blocks=False):
    T, K = a.shape
    N = b.shape[1]
    tk = K if K <= 1024 else K // 2
    tn = N if N <= 1024 else (1024 if N % 1024 == 0 and not column_blocks else N // 4)
    tt = min(2048, T)
    assert K % tk == 0 and N % tn == 0 and T % tt == 0 and tk % 128 == 0 and tn % 128 == 0

    def body(a_ref, b_ref, o_ref):
        @pl.when(pl.program_id(2) == 0)
        def _():
            o_ref[...] = jnp.zeros((tk, tn), F32)

        o_ref[...] += _dot_tn(a_ref[...], b_ref[...])

    return pl.pallas_call(
        body, name=name, grid=(K // tk, N // tn, T // tt),
        in_specs=[pl.BlockSpec((tt, tk), lambda i, j, t: (t, i)), pl.BlockSpec((tt, tn), lambda i, j, t: (t, j))],
        out_specs=(pl.BlockSpec((None, tk, tn), lambda i, j, t: (j, i, 0)) if column_blocks
                   else pl.BlockSpec((tk, tn), lambda i, j, t: (i, j))),
        out_shape=jax.ShapeDtypeStruct((N // tn, K, tn) if column_blocks else (K, N), F32),
        compiler_params=_params(("parallel", "parallel", "arbitrary")),
    )(a, b)


def _place():
    x, y, c = lax.axis_index("x"), lax.axis_index("y"), lax.axis_index("c")
    chips = [(1 - x, y), (x, 1 - y), (1 - x, 1 - y)]
    return x, y, c, chips


def _own_slot(slots, mine):
    chip = 2 * lax.axis_index("x") + lax.axis_index("y")
    return lax.dynamic_update_slice(slots, mine[None], (chip,) + (0,) * mine.ndim)


def _row_tile(rows, unit):
    return max(t for t in range(unit, 513, unit) if rows % t == 0)


def _gather_weights(pack):
    rows = pack.shape[0]
    half = rows // 2

    def body(src, dst, token, send_sems, recv_sems):
        x, y, c, chips = _place()

        def piece(px, py, pc):
            return dst.at[2 * px + py, pl.ds(pc * half, half), :]

        def copy(k, block, to, from_src=False):
            return pltpu.make_async_remote_copy(
                src_ref=src.at[pl.ds(c * half, half), :] if from_src else piece(*block), dst_ref=piece(*block),
                send_sem=send_sems.at[k], recv_sem=recv_sems.at[k], device_id=to, device_id_type=MESH)

        first = [copy(k, (x, y, c), (*chip, c), from_src=True) for k, chip in enumerate(chips)]
        for cp in first:
            cp.start()
        passed = [copy(3 + k, (*chip, c), (x, y, 1 - c)) for k, chip in enumerate(chips)]
        for k, chip in enumerate(chips):
            copy(k, (*chip, c), (x, y, c)).wait_recv()
            passed[k].start()
        for k, chip in enumerate(chips):
            copy(3 + k, (*chip, 1 - c), (x, y, c)).wait_recv()
        for cp in first + passed:
            cp.wait_send()
        token[...] = jnp.zeros((8, 128), F32)

    got, token = pl.pallas_call(
        body, name="gather_weights",
        in_specs=[pl.BlockSpec(memory_space=pl.ANY)],
        out_specs=[pl.BlockSpec(memory_space=pl.ANY), pl.BlockSpec(memory_space=pltpu.VMEM)],
        out_shape=[jax.ShapeDtypeStruct((4, rows, D), pack.dtype), jax.ShapeDtypeStruct((8, 128), F32)],
        scratch_shapes=[pltpu.SemaphoreType.DMA((6,)), pltpu.SemaphoreType.DMA((6,))],
        compiler_params=pltpu.CompilerParams(has_side_effects=True),
    )(pack)
    return _own_slot(got, pack), token[0, 0]


def _swap_halves(g):
    rows = g.shape[1]
    half = rows // 2

    def body(src, dst, send_sem, recv_sem):
        x, y, c, _ = _place()
        cp = pltpu.make_async_remote_copy(
            src_ref=src.at[:, pl.ds((1 - c) * half, half), :], dst_ref=dst, send_sem=send_sem, recv_sem=recv_sem,
            device_id=(x, y, 1 - c), device_id_type=MESH)
        cp.start()
        cp.wait()

    return pl.pallas_call(
        body, name="grad_swap_halves",
        in_specs=[pl.BlockSpec(memory_space=pl.ANY)], out_specs=pl.BlockSpec(memory_space=pl.ANY),
        out_shape=jax.ShapeDtypeStruct((4, half, D), g.dtype),
        scratch_shapes=[pltpu.SemaphoreType.DMA, pltpu.SemaphoreType.DMA],
        compiler_params=pltpu.CompilerParams(has_side_effects=True),
    )(g)


def _add_halves(g, got, c_idx, name="grad_add_halves"):
    rows, w = g.shape[1], g.shape[2]
    half = rows // 2
    tr = _row_tile(half, 16)
    nt = half // tr

    def body(c_ref, g_ref, r_ref, o_ref):
        o_ref[...] = (g_ref[...] + r_ref[...]).astype(BF16)

    return pl.pallas_call(
        body, name=name,
        grid_spec=pltpu.PrefetchScalarGridSpec(
            num_scalar_prefetch=1, grid=(4, nt),
            in_specs=[pl.BlockSpec((1, tr, w), lambda q, i, c_ref: (q, c_ref[0] * nt + i, 0)),
                      pl.BlockSpec((1, tr, w), lambda q, i, c_ref: (q, i, 0))],
            out_specs=pl.BlockSpec((1, tr, w), lambda q, i, c_ref: (q, i, 0))),
        out_shape=jax.ShapeDtypeStruct((4, half, w), BF16),
        compiler_params=_params(("parallel", "parallel")),
    )(c_idx, g, got)


def _scatter_chips(p):
    half = p.shape[1]

    def body(src, dst, send_sems, recv_sems):
        x, y, c, chips = _place()
        me = 2 * x + y
        sends = [pltpu.make_async_remote_copy(
            src_ref=src.at[2 * cx + cy], dst_ref=dst.at[me], send_sem=send_sems.at[k], recv_sem=recv_sems.at[k],
            device_id=(cx, cy, c), device_id_type=MESH) for k, (cx, cy) in enumerate(chips)]
        for cp in sends:
            cp.start()
        for k, (cx, cy) in enumerate(chips):
            pltpu.make_async_remote_copy(
                src_ref=src.at[me], dst_ref=dst.at[2 * cx + cy], send_sem=send_sems.at[k], recv_sem=recv_sems.at[k],
                device_id=(cx, cy, c), device_id_type=MESH).wait_recv()
        for cp in sends:
            cp.wait_send()

    got = pl.pallas_call(
        body, name="grad_scatter_chips",
        in_specs=[pl.BlockSpec(memory_space=pl.ANY)], out_specs=pl.BlockSpec(memory_space=pl.ANY),
        out_shape=jax.ShapeDtypeStruct((4, half, D), p.dtype),
        scratch_shapes=[pltpu.SemaphoreType.DMA((3,)), pltpu.SemaphoreType.DMA((3,))],
        compiler_params=pltpu.CompilerParams(has_side_effects=True),
    )(p)
    return _own_slot(got, _own_piece(p))


def _own_piece(p):
    chip = 2 * lax.axis_index("x") + lax.axis_index("y")
    return lax.dynamic_index_in_dim(p, chip, axis=0, keepdims=False)


def _sum_chips(r, c_idx, name="grad_sum_chips"):
    half, w = r.shape[1], r.shape[2]
    tr = _row_tile(half, 16)
    nt = half // tr

    def body(c_ref, r_ref, o_ref):
        acc = r_ref[0].astype(F32)
        for q in range(1, 4):
            acc = acc + r_ref[q].astype(F32)
        o_ref[...] = acc

    return pl.pallas_call(
        body, name=name,
        grid_spec=pltpu.PrefetchScalarGridSpec(
            num_scalar_prefetch=1, grid=(nt,),
            in_specs=[pl.BlockSpec((4, tr, w), lambda i, c_ref: (0, i, 0))],
            out_specs=pl.BlockSpec((tr, w), lambda i, c_ref: (c_ref[0] * nt + i, 0))),
        out_shape=jax.ShapeDtypeStruct((2 * half, w), F32),
        compiler_params=_params(("parallel",)),
    )(c_idx, r)


def _join_halves(f):
    half = f.shape[0] // 2

    def body(src, dst, send_sem, recv_sem):
        x, y, c, _ = _place()
        cp = pltpu.make_async_remote_copy(
            src_ref=src.at[pl.ds(c * half, half), :], dst_ref=dst.at[pl.ds(c * half, half), :], send_sem=send_sem,
            recv_sem=recv_sem, device_id=(x, y, 1 - c), device_id_type=MESH)
        cp.start()
        pltpu.make_async_remote_copy(
            src_ref=src.at[pl.ds(c * half, half), :], dst_ref=dst.at[pl.ds((1 - c) * half, half), :],
            send_sem=send_sem, recv_sem=recv_sem, device_id=(x, y, 1 - c), device_id_type=MESH).wait_recv()
        cp.wait_send()

    return pl.pallas_call(
        body, name="grad_join_halves",
        in_specs=[pl.BlockSpec(memory_space=pl.ANY)], out_specs=pl.BlockSpec(memory_space=pl.ANY),
        out_shape=jax.ShapeDtypeStruct(f.shape, f.dtype), input_output_aliases={0: 0},
        scratch_shapes=[pltpu.SemaphoreType.DMA, pltpu.SemaphoreType.DMA],
        compiler_params=pltpu.CompilerParams(has_side_effects=True),
    )(f)


_HBM = pl.BlockSpec(memory_space=pltpu.HBM)
_SEM = pl.BlockSpec(memory_space=pltpu.SEMAPHORE)
_EFFECT = pltpu.SideEffectType.DATAFLOW_SIDE_EFFECTING


def _start_copies(name, bufs, n_sems, plan):
    nb = len(bufs)

    def body(*refs):
        for cp in plan(refs[:nb], refs[nb], refs[nb + 1])[0]:
            cp.start()
        refs[-1][...] = jnp.zeros((8, 128), F32)

    out = pl.pallas_call(
        body, name=name,
        out_shape=(pltpu.SemaphoreType.DMA((n_sems,)), pltpu.SemaphoreType.DMA((n_sems,)),
                   *[pltpu.HBM(b.shape, b.dtype) for b in bufs], jax.ShapeDtypeStruct((8, 128), F32)),
        in_specs=[_HBM] * nb, out_specs=(_SEM, _SEM, *[_HBM] * nb, pl.BlockSpec(memory_space=pltpu.VMEM)),
        input_output_aliases={i: 2 + i for i in range(nb)},
        compiler_params=pltpu.CompilerParams(has_side_effects=_EFFECT),
    )(*[pltpu.with_memory_space_constraint(b, pltpu.HBM) for b in bufs])
    return out[0], out[1], list(out[2:2 + nb]), out[-1]


def _wait_copies(name, send_sems, recv_sems, bufs, after, plan):
    nb = len(bufs)

    def body(*refs):
        _, sends, recvs = plan(refs[:nb], refs[nb], refs[nb + 1])
        for cp in sends:
            cp.wait_send()
        for cp in recvs:
            cp.wait_recv()

    out = pl.pallas_call(
        body, name=name,
        out_shape=tuple(pltpu.HBM(b.shape, b.dtype) for b in bufs),
        in_specs=[_HBM] * nb + [_SEM, _SEM] + [pl.BlockSpec(memory_space=pl.ANY)] * len(after),
        out_specs=tuple([_HBM] * nb),
        input_output_aliases={i: i for i in range(nb)},
        compiler_params=pltpu.CompilerParams(has_side_effects=_EFFECT),
    )(*bufs, send_sems, recv_sems, *after)
    return list(out)


def _plan_gather_direct(halves):
    n = len(halves)

    def plan(refs, send_sems, recv_sems):
        x, y, c, chips = _place()
        starts, recvs = [], []
        for b, half in enumerate(halves):
            src, land = refs[b], refs[n + b]
            for k, (cx, cy) in enumerate(chips):
                for d in range(2):
                    other = c if d == 0 else 1 - c
                    i = 6 * b + 2 * k + d
                    starts.append(pltpu.make_async_remote_copy(
                        src_ref=src.at[pl.ds(c * half, half), :],
                        dst_ref=land.at[2 * x + y, pl.ds(c * half, half), :],
                        send_sem=send_sems.at[i], recv_sem=recv_sems.at[i], device_id=(cx, cy, other),
                        device_id_type=MESH))
                    recvs.append(pltpu.make_async_remote_copy(
                        src_ref=src.at[pl.ds(c * half, half), :],
                        dst_ref=land.at[2 * cx + cy, pl.ds(other * half, half), :],
                        send_sem=send_sems.at[i], recv_sem=recv_sems.at[i], device_id=(cx, cy, other),
                        device_id_type=MESH))
        return starts, starts, recvs
    return plan


def _plan_swap_halves(halves):
    n = len(halves)

    def plan(refs, send_sems, recv_sems):
        x, y, c, _ = _place()
        cps = [pltpu.make_async_remote_copy(
            src_ref=refs[b].at[:, pl.ds((1 - c) * half, half), :], dst_ref=refs[n + b], send_sem=send_sems.at[b],
            recv_sem=recv_sems.at[b], device_id=(x, y, 1 - c), device_id_type=MESH)
            for b, half in enumerate(halves)]
        return cps, cps, cps
    return plan


def _plan_scatter_chips(n):
    def plan(refs, send_sems, recv_sems):
        x, y, c, chips = _place()
        me = 2 * x + y
        starts, recvs = [], []
        for b in range(n):
            src, land = refs[b], refs[n + b]
            for k, (cx, cy) in enumerate(chips):
                i = 3 * b + k
                starts.append(pltpu.make_async_remote_copy(
                    src_ref=src.at[2 * cx + cy], dst_ref=land.at[me], send_sem=send_sems.at[i],
                    recv_sem=recv_sems.at[i], device_id=(cx, cy, c), device_id_type=MESH))
                recvs.append(pltpu.make_async_remote_copy(
                    src_ref=src.at[me], dst_ref=land.at[2 * cx + cy], send_sem=send_sems.at[i],
                    recv_sem=recv_sems.at[i], device_id=(cx, cy, c), device_id_type=MESH))
        return starts, starts, recvs
    return plan


def _plan_join_halves(halves):
    def plan(refs, send_sems, recv_sems):
        x, y, c, _ = _place()
        starts, recvs = [], []
        for b, half in enumerate(halves):
            mine, theirs = refs[b].at[pl.ds(c * half, half), :], refs[b].at[pl.ds((1 - c) * half, half), :]
            starts.append(pltpu.make_async_remote_copy(
                src_ref=mine, dst_ref=mine, send_sem=send_sems.at[b], recv_sem=recv_sems.at[b],
                device_id=(x, y, 1 - c), device_id_type=MESH))
            recvs.append(pltpu.make_async_remote_copy(
                src_ref=mine, dst_ref=theirs, send_sem=send_sems.at[b], recv_sem=recv_sems.at[b],
                device_id=(x, y, 1 - c), device_id_type=MESH))
        return starts, starts, recvs
    return plan


def _allreduce_small(vec):
    def body(v_ref, o_ref, gath, send_sems, recv_sems):
        x, y, c, _ = _place()
        me = 4 * x + 2 * y + c
        gath[me] = v_ref[...]
        sends = []
        for k in range(1, 8):
            peer = (x ^ (k >> 2), y ^ ((k >> 1) & 1), c ^ (k & 1))
            sends.append(pltpu.make_async_remote_copy(
                src_ref=v_ref, dst_ref=gath.at[me], send_sem=send_sems.at[k - 1], recv_sem=recv_sems.at[k - 1],
                device_id=peer, device_id_type=MESH))
        for cp in sends:
            cp.start()
        for k in range(1, 8):
            peer = (x ^ (k >> 2), y ^ ((k >> 1) & 1), c ^ (k & 1))
            pltpu.make_async_remote_copy(
                src_ref=v_ref, dst_ref=gath.at[4 * peer[0] + 2 * peer[1] + peer[2]], send_sem=send_sems.at[k - 1],
                recv_sem=recv_sems.at[k - 1], device_id=peer, device_id_type=MESH).wait_recv()
        for cp in sends:
            cp.wait_send()
        acc = gath[0]
        for d in range(1, 8):
            acc = acc + gath[d]
        o_ref[...] = acc

    return pl.pallas_call(
        body, name="allreduce_small",
        in_specs=[pl.BlockSpec(memory_space=pltpu.VMEM)], out_specs=pl.BlockSpec(memory_space=pltpu.VMEM),
        out_shape=jax.ShapeDtypeStruct(vec.shape, F32),
        scratch_shapes=[pltpu.VMEM((8,) + vec.shape, F32), pltpu.SemaphoreType.DMA((7,)),
                        pltpu.SemaphoreType.DMA((7,))],
    )(vec)


def _adamw(w, g, m, v, name):
    shape = w.shape
    if w.ndim == 1 or w.size <= 128 * 128:
        two_d = (1, w.size) if w.size % 128 else (w.size // 128, 128)
    else:
        two_d = (w.shape[0], w.size // w.shape[0])
    rows, cols = two_d
    tr = _row_tile(rows, 8) if rows % 8 == 0 and rows > 512 else rows

    def body(w_ref, g_ref, m_ref, v_ref, d_ref, nm_ref, nv_ref):
        gr = g_ref[...]
        nm = B1 * m_ref[...] + (1.0 - B1) * gr
        nv = B2 * v_ref[...] + (1.0 - B2) * (gr * gr)
        m_hat = nm / (1.0 - B1 ** STEP)
        v_hat = nv / (1.0 - B2 ** STEP)
        d_ref[...] = -LR * (m_hat / (jnp.sqrt(v_hat) + AEPS) + WD * w_ref[...])
        nm_ref[...] = nm
        nv_ref[...] = nv

    spec = pl.BlockSpec((tr, cols), lambda i: (i, 0))
    outs = pl.pallas_call(
        body, name=name, grid=(rows // tr,),
        in_specs=[spec] * 4, out_specs=[spec] * 3,
        out_shape=[jax.ShapeDtypeStruct(two_d, F32)] * 3,
        compiler_params=_params(("parallel",)),
    )(*[t.reshape(two_d) for t in (w, g, m, v)])
    return [o.reshape(shape) for o in outs]


def _rows_stacked(g, lo, n_rows):
    return g[:, lo:lo + n_rows].reshape(4 * n_rows, D)


def _rows_to_slots(t):
    return t.reshape(4, t.shape[0] // 4, D)


def _pack_first(w_in, w_conv_out, conv_dw_w):
    dw = jnp.pad(conv_dw_w.reshape(CW, 256), ((0, 1), (0, 0)))
    dw_bits = lax.bitcast_convert_type(dw, BF16).reshape(16, D)
    return jnp.concatenate([w_in.T.astype(BF16), w_conv_out.astype(BF16), dw_bits, jnp.zeros((16, D), BF16)],
                           axis=0)


def _unpack_first(g):
    w_in_t = _rows_stacked(g, 0, ROWS_W_IN)
    wc = _rows_stacked(g, ROWS_W_IN, ROWS_SQ)
    o = ROWS_W_IN + ROWS_SQ
    dw = lax.bitcast_convert_type(g[:, o:o + 16].reshape(4, 32, 256, 2), F32)
    return w_in_t, wc, jnp.transpose(dw, (1, 0, 2)).reshape(32, D)


def _pack_late(w_attn_out, w_merge_out, w_ffn_down):
    return jnp.concatenate([w_attn_out.astype(BF16), w_merge_out.astype(BF16), w_ffn_down.astype(BF16)], axis=0)


def _unpack_late(g):
    return (_rows_stacked(g, 0, ROWS_SQ), _rows_stacked(g, ROWS_SQ, ROWS_SQ),
            _rows_stacked(g, 2 * ROWS_SQ, ROWS_DOWN))


class _Exchanges:
    def __init__(self, late_pack, wf_shard):
        self.c_idx = lax.axis_index("c").astype(jnp.int32).reshape(1)
        packs = [late_pack, wf_shard]
        self.late_plan = _plan_gather_direct([p.shape[0] // 2 for p in packs])
        slots = [lax.empty((4,) + p.shape, BF16) for p in packs]
        self.late = _start_copies("gather_late_start", packs + slots, 6 * len(packs), self.late_plan)
        self.first_token = self.late[3][0, 0]

    def late_weights(self, after):
        send_sems, recv_sems, bufs, _ = self.late
        pack, wf_shard, slots, wf_slots = _wait_copies("gather_late_wait", send_sems, recv_sems, bufs, after,
                                                       self.late_plan)
        wa, wm, wd = _unpack_late(_own_slot(slots, pack))
        return wa, wm, _own_slot(wf_slots, wf_shard), wd

    def reduce_start(self, d_wd, d_wf4, d_wm, d_wa, d_wc):
        gs = [_rows_to_slots(d_wd), jnp.concatenate([_rows_to_slots(t) for t in (d_wm, d_wa, d_wc)], axis=1), d_wf4]
        self.halves = [g.shape[1] // 2 for g in gs]
        self.swap_plan = _plan_swap_halves(self.halves)
        lands = [lax.empty((4, h, g.shape[2]), F32) for g, h in zip(gs, self.halves)]
        self.swap = _start_copies("grad_swap_start", gs + lands, len(gs), self.swap_plan)
        return self.swap[3][0, 0]

    def reduce_mid(self, after):
        send_sems, recv_sems, bufs, _ = self.swap
        bufs = _wait_copies("grad_swap_wait", send_sems, recv_sems, bufs, after, self.swap_plan)
        n = len(self.halves)
        ps = [_add_halves(bufs[b], bufs[n + b], self.c_idx, "grad_add_halves_%d" % b) for b in range(n)]
        self.scatter_plan = _plan_scatter_chips(n)
        self.scatter = _start_copies("grad_scatter_start", ps + [lax.empty(p.shape, BF16) for p in ps], 3 * n,
                                     self.scatter_plan)
        return self.scatter[3][0, 0]

    def reduce_late(self, after):
        send_sems, recv_sems, bufs, _ = self.scatter
        bufs = _wait_copies("grad_scatter_wait", send_sems, recv_sems, bufs, after, self.scatter_plan)
        n = len(self.halves)
        fs = [_sum_chips(_own_slot(bufs[n + b], _own_piece(bufs[b])), self.c_idx, "grad_sum_chips_%d" % b)
              for b in range(n)]
        self.join_plan = _plan_join_halves(self.halves)
        self.join = _start_copies("grad_join_start", fs, n, self.join_plan)

    def reduce_end(self, after):
        send_sems, recv_sems, bufs, _ = self.join
        g_wd, sq, g_wf = _wait_copies("grad_join_wait", send_sems, recv_sems, bufs, after, self.join_plan)
        return g_wd, g_wf, sq[0:ROWS_SQ], sq[ROWS_SQ:2 * ROWS_SQ], sq[2 * ROWS_SQ:3 * ROWS_SQ]

    def reduce_w_in(self, d_w_in_t):
        g = _rows_to_slots(d_w_in_t)
        p = _add_halves(g, _swap_halves(g), self.c_idx)
        return _join_halves(_sum_chips(_scatter_chips(p), self.c_idx))


def _local_grads(x, loss_target, norm_mix_g, conv_dw_b, conv_ln_g, conv_ln_b, q_norm_g, k_norm_g, sinks, norm_ffn_g,
                 w_in, wc, dw, exchanges):
    n_seq, S, _ = x.shape
    T = n_seq * S
    x2 = x.reshape(T, D)
    tgt = loss_target.reshape(T, D)
    row = lambda t: t.reshape(1, -1)
    g1, g2 = row(norm_mix_g), row(norm_ffn_g)
    qg, kg = row(q_norm_g), row(k_norm_g)
    lng, lnb, dwb = row(conv_ln_g), row(conv_ln_b), row(conv_dw_b)
    sink_rows = jnp.repeat(sinks.reshape(NKV, GROUP), BLK, axis=1)

    xn, ag, q, kv, gg, h1, h3, yc = _in_proj_conv_fwd(x2, g1 + exchanges.first_token, w_in, dw, dwb, lng, lnb, wc,
                                                      n_seq, S)
    o = _attn_fwd(q, kv, qg, kg, sink_rows, n_seq, S)
    wa, wm, wf, wd = exchanges.late_weights([o, yc])
    ya, mix, h = _merge_fwd(x2, gg, yc, o, wa, wm)
    dh, dhb, hn, act, dout, dgu, ffn_stats = _ffn(h, tgt, g2, wf, wd)
    d_wd = _tn_matmul(act, dout, "dw_ffn_down")
    d_wf = _tn_matmul(hn, dgu, "dw_ffn_in", column_blocks=True)
    d_wm = _tn_matmul(mix, dhb, "dw_merge")
    dgg, dyc, dya, do, dh3 = _merge_bwd(dhb, gg, yc, ya, wm, wa, wc)
    d_wa = _tn_matmul(o, dya, "dw_attn_out")
    d_wc = _tn_matmul(h3, dyc, "dw_conv_out")
    token = exchanges.reduce_start(d_wd, d_wf, d_wm, d_wa, d_wc)
    dq, dkc, dkp, dqg, dsk = _attn_bwd(q, kv, do, qg + token, kg, sink_rows, n_seq, S)
    token = exchanges.reduce_mid([dq])
    dkv, dkg = _kv_bwd(kv, dkc, dkp, kg, n_seq, S)
    dag, ddw, conv_stats = _conv_bwd(h1, dh3, ag, dw, lng + token, lnb, n_seq, S)
    dx, dg1 = _in_proj_bwd(dag, dq, dkv, dgg, dh, x2, g1, w_in)
    exchanges.reduce_late([dx])
    d_w_in = jnp.concatenate([_tn_matmul(dag, xn, "dw_in_conv"), _tn_matmul(dq, xn, "dw_in_q"),
                              _tn_matmul(dkv, xn, "dw_in_kv"), _tn_matmul(dgg, xn, "dw_in_gates")], axis=0)

    heads = jnp.concatenate([dqg[0], dkg[0], dsk[0, :NQ], jnp.zeros((D - 2 * HD - NQ,), F32)])
    vec = jnp.concatenate([dg1, conv_stats[0:3], ffn_stats[0:1], heads[None], jnp.zeros((2, D), F32), ddw], axis=0)
    return ffn_stats[1], dx.reshape(x.shape), d_w_in, vec


def kernel(x, norm_mix_g, w_in, conv_dw_w, conv_dw_b, conv_ln_g, conv_ln_b, w_conv_out, q_norm_g, k_norm_g, sinks, w_attn_out, w_merge_out, norm_ffn_g, w_ffn_in, w_ffn_down, loss_target, m_norm_mix_g, m_w_in, m_conv_dw_w, m_conv_dw_b, m_conv_ln_g, m_conv_ln_b, m_w_conv_out, m_q_norm_g, m_k_norm_g, m_sinks, m_w_attn_out, m_w_merge_out, m_norm_ffn_g, m_w_ffn_in, m_w_ffn_down, v_norm_mix_g, v_w_in, v_conv_dw_w, v_conv_dw_b, v_conv_ln_g, v_conv_ln_b, v_w_conv_out, v_q_norm_g, v_k_norm_g, v_sinks, v_w_attn_out, v_w_merge_out, v_norm_ffn_g, v_w_ffn_in, v_w_ffn_down):
    chip = 2 * lax.axis_index("x") + lax.axis_index("y")

    first, token = _gather_weights(_pack_first(w_in, w_conv_out, conv_dw_w))
    exchanges = _Exchanges(_pack_late(w_attn_out, w_merge_out, w_ffn_down) + token.astype(BF16),
                           w_ffn_in.astype(BF16) + token.astype(BF16))
    sq_cols, grad_x, d_w_in, vec = _local_grads(x, loss_target, norm_mix_g, conv_dw_b, conv_ln_g, conv_ln_b,
                                                q_norm_g, k_norm_g, sinks, norm_ffn_g, *_unpack_first(first),
                                                exchanges)
    loss = lax.psum(0.5 / D * jnp.sum(sq_cols), ("x", "y", "c"))

    g_w_in_t = exchanges.reduce_w_in(d_w_in)
    g_wd, g_wf, g_wm, g_wa, g_wc = exchanges.reduce_end([g_w_in_t])
    g_w_in = g_w_in_t.T
    small = _allreduce_small(vec)
    g_dw = lax.dynamic_slice_in_dim(small[8:8 + CW], chip * 256, 256, axis=1).reshape(CW, 1, 256)
    grads = {
        "norm_mix_g": small[0], "w_in": g_w_in, "conv_dw_w": g_dw, "conv_dw_b": small[1], "conv_ln_g": small[2],
        "conv_ln_b": small[3], "w_conv_out": g_wc, "q_norm_g": small[5, 0:HD], "k_norm_g": small[5, HD:2 * HD],
        "sinks": small[5, 2 * HD:2 * HD + NQ], "w_attn_out": g_wa, "w_merge_out": g_wm, "norm_ffn_g": small[4],
        "w_ffn_in": g_wf, "w_ffn_down": g_wd,
    }
    weights = dict(norm_mix_g=norm_mix_g, w_in=w_in, conv_dw_w=conv_dw_w, conv_dw_b=conv_dw_b, conv_ln_g=conv_ln_g,
                   conv_ln_b=conv_ln_b, w_conv_out=w_conv_out, q_norm_g=q_norm_g, k_norm_g=k_norm_g, sinks=sinks,
                   w_attn_out=w_attn_out, w_merge_out=w_merge_out, norm_ffn_g=norm_ffn_g, w_ffn_in=w_ffn_in,
                   w_ffn_down=w_ffn_down)
    m_in = dict(norm_mix_g=m_norm_mix_g, w_in=m_w_in, conv_dw_w=m_conv_dw_w, conv_dw_b=m_conv_dw_b,
                conv_ln_g=m_conv_ln_g, conv_ln_b=m_conv_ln_b, w_conv_out=m_w_conv_out, q_norm_g=m_q_norm_g,
                k_norm_g=m_k_norm_g, sinks=m_sinks, w_attn_out=m_w_attn_out, w_merge_out=m_w_merge_out,
                norm_ffn_g=m_norm_ffn_g, w_ffn_in=m_w_ffn_in, w_ffn_down=m_w_ffn_down)
    v_in = dict(norm_mix_g=v_norm_mix_g, w_in=v_w_in, conv_dw_w=v_conv_dw_w, conv_dw_b=v_conv_dw_b,
                conv_ln_g=v_conv_ln_g, conv_ln_b=v_conv_ln_b, w_conv_out=v_w_conv_out, q_norm_g=v_q_norm_g,
                k_norm_g=v_k_norm_g, sinks=v_sinks, w_attn_out=v_w_attn_out, w_merge_out=v_w_merge_out,
                norm_ffn_g=v_norm_ffn_g, w_ffn_in=v_w_ffn_in, w_ffn_down=v_w_ffn_down)
    names = list(weights)
    deltas, new_m, new_v = [], [], []
    for n in names:
        if n == "w_in":
            d, nm, nv = [t.T for t in _adamw(w_in.T, g_w_in_t, m_w_in.T, v_w_in.T, "adamw_" + n)]
        else:
            d, nm, nv = _adamw(weights[n], grads[n], m_in[n], v_in[n], "adamw_" + n)
        deltas.append(d)
        new_m.append(nm)
        new_v.append(nv)
    return (loss, grad_x, *[grads[n] for n in names], *deltas, *new_m, *new_v)
```

```python
import functools
import math

import jax
import jax.numpy as jnp
import numpy as np
from jax import lax
from jax.experimental import pallas as pl
from jax.experimental.pallas import tpu as pltpu

F32 = jnp.float32
BF16 = jnp.bfloat16

D = 1024
CW = 31
HD = 64
NQ = 16
NKV = 2
GROUP = NQ // NKV
BLK = 128
DFF = 2816
EPS = 1e-6
NEG = -1e30
IN_COLS = 5376
SCALE = 1.0 / math.sqrt(HD)

LR, B1, B2, AEPS, WD, STEP = 0.001, 0.9, 0.999, 1e-08, 0.01, 10

MIB = 1024 * 1024
MESH = pl.DeviceIdType.MESH

ROWS_W_IN = 1344
ROWS_SQ = 256
ROWS_FFN_IN = 1408
ROWS_DOWN = 704
ROWS_MAT = ROWS_W_IN + 3 * ROWS_SQ + ROWS_FFN_IN + ROWS_DOWN
ROWS_DW = 32
ROWS_PACK = ROWS_MAT + ROWS_DW
VEC_ROWS = 40


def _sig(x):
    return 1.0 / (1.0 + jnp.exp(-x))


def _dot(a, b):
    return jnp.dot(a, b, preferred_element_type=F32)


def _dot_nt(a, b):
    return lax.dot_general(a, b, (((1,), (1,)), ((), ())), preferred_element_type=F32)


def _dot_tn(a, b):
    return lax.dot_general(a, b, (((0,), (0,)), ((), ())), preferred_element_type=F32)


def _params(sem, vmem_mib=48):
    return pltpu.CompilerParams(dimension_semantics=sem, vmem_limit_bytes=vmem_mib * MIB)


def _resident(shape):
    return pl.BlockSpec(shape, lambda *_: (0,) * len(shape), pipeline_mode=pl.Buffered(1))


def _whole(shape):
    return pl.BlockSpec(shape, lambda *_: (0,) * len(shape))


def _rows(tm, cols, col_block=0):
    return pl.BlockSpec((tm, cols), lambda i: (i, col_block))


def _tap_phases():
    return [(phase, list(range(phase, CW, 8))) for phase in range(8)]


def _shift_copies(dst, src, base):
    for phase, taps in _tap_phases():
        n = dst.shape[1] - 8 * (4 - len(taps))
        dst[phase, 0:n, :] = src[base + phase:base + phase + n, :]


def _in_proj_conv_fwd(x2, g1, w_in, dw, dwb, lng, lnb, wc, n_seq, S):
    T = n_seq * S
    tc = min(256, S)
    nt = S // tc

    def body(x_ref, g_ref, w_ref, dw_ref, dwb_ref, lng_ref, lnb_ref, wc_ref, xn_ref, ag_ref, q_ref, kv_ref, gg_ref,
             h1_ref, h3_ref, yc_ref, ext, sh):
        i = pl.program_id(1)

        @pl.when(i == 0)
        def _():
            ext[0:32, :] = jnp.zeros((32, D), F32)

        x = x_ref[...]
        rstd = lax.rsqrt(jnp.mean(x * x, axis=-1, keepdims=True) + EPS)
        xn = (x * rstd * g_ref[...]).astype(BF16)
        xn_ref[...] = xn
        ag = _dot_nt(xn, w_ref[0:2048, :])
        ag_ref[...] = ag
        ext[32:32 + tc, :] = ag[:, 0:D] * _sig(ag[:, D:2 * D])
        q_ref[...] = _dot_nt(xn, w_ref[2048:3072, :])
        kv_ref[...] = _dot_nt(xn, w_ref[3072:3328, :])
        gg_ref[...] = _dot_nt(xn, w_ref[3328:5376, :])
        _shift_copies(sh, ext, 2)
        for cb in range(D // 128):
            cs = slice(cb * 128, (cb + 1) * 128)
            acc = jnp.broadcast_to(dwb_ref[:, cs], (tc, 128))
            for phase, taps in _tap_phases():
                for m, j in enumerate(taps):
                    acc = acc + dw_ref[j:j + 1, cs] * sh[phase, 8 * m:8 * m + tc, cs]
            h1_ref[:, cs] = acc
        ext[0:32, :] = ext[tc:tc + 32, :]
        h1 = h1_ref[...]
        mu = jnp.mean(h1, axis=-1, keepdims=True)
        cen = h1 - mu
        var = jnp.mean(cen * cen, axis=-1, keepdims=True)
        h2 = cen * lax.rsqrt(var + EPS) * lng_ref[...] + lnb_ref[...]
        h3 = (h2 * _sig(h2)).astype(BF16)
        h3_ref[...] = h3
        yc_ref[...] = _dot(h3, wc_ref[...])

    tile = lambda cols: pl.BlockSpec((tc, cols), lambda b, i: (b * nt + i, 0))
    shape = lambda cols, dtype: jax.ShapeDtypeStruct((T, cols), dtype)
    return pl.pallas_call(
        body, name="in_proj_conv_fwd", grid=(n_seq, nt),
        in_specs=[tile(D), _resident((1, D)), _resident((IN_COLS, D)), _resident((32, D)), _resident((1, D)),
                  _resident((1, D)), _resident((1, D)), _resident((D, D))],
        out_specs=[tile(D), tile(2 * D), tile(D), tile(256), tile(2 * D), tile(D), tile(D), tile(D)],
        out_shape=[shape(D, BF16), shape(2 * D, F32), shape(D, F32), shape(256, F32), shape(2 * D, F32),
                   shape(D, F32), shape(D, BF16), shape(D, F32)],
        scratch_shapes=[pltpu.VMEM((32 + tc, D), F32), pltpu.VMEM((8, tc + 24, D), F32)],
        compiler_params=_params(("parallel", "arbitrary"), 56),
    )(x2, g1, w_in, dw, dwb, lng, lnb, wc)


def _attn_consts():
    k = np.arange(BLK)[:, None]
    i = np.arange(GROUP * BLK)[None, :] % BLK
    from_prev = k > i
    dist = np.where(from_prev, i + BLK - k, i - k).astype(np.float32)
    head = np.arange(GROUP * BLK)[None, :] // BLK
    bias = []
    for kh in range(NKV):
        slope = np.exp2(-8.0 * (kh * GROUP + head + 1) / NQ).astype(np.float32)
        bias.append(-slope * dist)
    return jnp.asarray(from_prev.astype(np.float32)), jnp.asarray(np.stack(bias))


def _stack_heads(ref, kh):
    return jnp.concatenate([ref[:, (kh * GROUP + g) * HD:(kh * GROUP + g + 1) * HD] for g in range(GROUP)], axis=0)


def _rms64(t):
    return lax.rsqrt(jnp.mean(t * t, axis=-1, keepdims=True) + EPS)


def _attn_probs(kh, n, q_ref, kvc_ref, kvp_ref, qg_ref, kg_ref, tri_ref, bias_ref, sink_ref):
    ks = slice(kh * HD, (kh + 1) * HD)
    vs = slice(2 * HD + kh * HD, 2 * HD + (kh + 1) * HD)
    kp, kc = kvp_ref[:, ks], kvc_ref[:, ks]
    kpb = (kp * _rms64(kp) * kg_ref[...]).astype(BF16)
    kcb = (kc * _rms64(kc) * kg_ref[...]).astype(BF16)
    qs = _stack_heads(q_ref, kh)
    rq = _rms64(qs)
    qy = qs * rq
    qhb = (qy * (qg_ref[...] * SCALE)).astype(BF16)
    from_prev = tri_ref[...] > 0.5
    no_prev = jnp.where(n > 0, 0.0, NEG)
    s = jnp.where(from_prev, _dot_nt(kpb, qhb) + no_prev, _dot_nt(kcb, qhb)) + bias_ref[kh]
    sink = sink_ref[kh:kh + 1, :]
    m = jnp.maximum(jnp.max(s, axis=0, keepdims=True), sink)
    e = jnp.exp(s - m)
    es = jnp.exp(sink - m)
    rz = 1.0 / (jnp.sum(e, axis=0, keepdims=True) + es)
    return e * rz, es * rz, from_prev, qhb, kpb, kcb, kvp_ref[:, vs].astype(BF16), kvc_ref[:, vs].astype(BF16), qy, rq


def _unfold(t, from_prev):
    zero = jnp.zeros_like(t)
    return jnp.where(from_prev, t, zero), jnp.where(from_prev, zero, t)


def _attn_specs(n_seq, S):
    nb = S // BLK
    cur = lambda cols: pl.BlockSpec((BLK, cols), lambda b, n: (b * nb + n, 0))
    prev = lambda cols: pl.BlockSpec((BLK, cols), lambda b, n: (b * nb + jnp.maximum(n - 1, 0), 0))
    consts = [_resident((1, HD)), _resident((1, HD)), _resident((BLK, GROUP * BLK)),
              _resident((NKV, BLK, GROUP * BLK)), _resident((NKV, GROUP * BLK))]
    return nb, cur, prev, consts


def _attn_fwd(q, kv, qg, kg, sink_rows, n_seq, S):
    T = n_seq * S
    nb, cur, prev, consts = _attn_specs(n_seq, S)
    tri, bias = _attn_consts()

    def body(q_ref, kvc_ref, kvp_ref, qg_ref, kg_ref, tri_ref, bias_ref, sink_ref, o_ref):
        n = pl.program_id(1)
        for kh in range(NKV):
            p, _, from_prev, _, _, _, vpb, vcb, _, _ = _attn_probs(kh, n, q_ref, kvc_ref, kvp_ref, qg_ref, kg_ref,
                                                                   tri_ref, bias_ref, sink_ref)
            pp, pc = _unfold(p.astype(BF16), from_prev)
            o = (_dot_tn(pp, vpb) + _dot_tn(pc, vcb)).astype(BF16)
            for g in range(GROUP):
                h = kh * GROUP + g
                o_ref[:, h * HD:(h + 1) * HD] = o[g * BLK:(g + 1) * BLK]

    return pl.pallas_call(
        body, name="attn_fwd", grid=(n_seq, nb),
        in_specs=[cur(D), cur(256), prev(256)] + consts,
        out_specs=cur(D),
        out_shape=jax.ShapeDtypeStruct((T, D), BF16),
        compiler_params=_params(("parallel", "parallel")),
    )(q, kv, kv, qg, kg, tri, bias, sink_rows)


def _merge_fwd(x2, gg, yc, o, wa, wm):
    T = x2.shape[0]
    tm = min(512, T)

    def body(x_ref, gg_ref, yc_ref, o_ref, wa_ref, wm_ref, ya_ref, mix_ref, h_ref):
        ya = _dot(o_ref[...], wa_ref[...])
        mix = (_sig(gg_ref[:, 0:D]) * yc_ref[...] + _sig(gg_ref[:, D:2 * D]) * ya).astype(BF16)
        ya_ref[...] = ya
        mix_ref[...] = mix
        h_ref[...] = x_ref[...] + _dot(mix, wm_ref[...])

    return pl.pallas_call(
        body, name="merge_fwd", grid=(T // tm,),
        in_specs=[_rows(tm, D), _rows(tm, 2 * D), _rows(tm, D), _rows(tm, D), _resident((D, D)), _resident((D, D))],
        out_specs=[_rows(tm, D), _rows(tm, D), _rows(tm, D)],
        out_shape=[jax.ShapeDtypeStruct((T, D), F32), jax.ShapeDtypeStruct((T, D), BF16),
                   jax.ShapeDtypeStruct((T, D), F32)],
        compiler_params=_params(("parallel",)),
    )(x2, gg, yc, o, wa, wm)


FF_CHUNK = DFF // 2


def _ffn(h, tgt, g2, wf, wd):
    T = h.shape[0]
    tm = min(256, T)

    def body(h_ref, t_ref, g_ref, wf_ref, wd_ref, dh_ref, dhb_ref, hn_ref, act_ref, dout_ref, dgu_ref, st_ref,
             gsc, usc):
        @pl.when(pl.program_id(0) == 0)
        def _():
            st_ref[...] = jnp.zeros((8, D), F32)

        hh = h_ref[...]
        rstd = lax.rsqrt(jnp.mean(hh * hh, axis=-1, keepdims=True) + EPS)
        hhat = hh * rstd
        hn = (hhat * g_ref[...]).astype(BF16)
        hn_ref[...] = hn
        out = hh
        for c in range(DFF // FF_CHUNK):
            cs = slice(c * FF_CHUNK, (c + 1) * FF_CHUNK)
            us = slice(DFF + c * FF_CHUNK, DFF + (c + 1) * FF_CHUNK)
            g = _dot(hn, wf_ref[c])
            u = _dot(hn, wf_ref[2 + c])
            gsc[:, cs] = g
            usc[:, cs] = u
            act = (g * _sig(g) * u).astype(BF16)
            act_ref[:, cs] = act
            out = out + _dot(act, wd_ref[cs, :])
        err = out - t_ref[...]
        dout = err * (1.0 / D)
        doutb = dout.astype(BF16)
        dout_ref[...] = doutb
        dhn = jnp.zeros((tm, D), F32)
        for c in range(DFF // FF_CHUNK):
            cs = slice(c * FF_CHUNK, (c + 1) * FF_CHUNK)
            us = slice(DFF + c * FF_CHUNK, DFF + (c + 1) * FF_CHUNK)
            g = gsc[:, cs]
            u = usc[:, cs]
            dact = _dot_nt(doutb, wd_ref[cs, :])
            sg = _sig(g)
            dg = (dact * u * (sg * (1.0 + g * (1.0 - sg)))).astype(BF16)
            du = (dact * (g * sg)).astype(BF16)
            dgu_ref[:, cs] = dg
            dgu_ref[:, us] = du
            dhn = dhn + _dot_nt(dg, wf_ref[c]) + _dot_nt(du, wf_ref[2 + c])
        st_ref[0:1, :] += jnp.sum(dhn * hhat, axis=0, keepdims=True)
        st_ref[1:2, :] += jnp.sum(err * err, axis=0, keepdims=True)
        dhh = dhn * g_ref[...]
        dh = dout + rstd * (dhh - hhat * jnp.mean(dhh * hhat, axis=-1, keepdims=True))
        dh_ref[...] = dh
        dhb_ref[...] = dh.astype(BF16)

    return pl.pallas_call(
        body, name="ffn_fwd_bwd", grid=(T // tm,),
        in_specs=[_rows(tm, D), _rows(tm, D), _resident((1, D)), _resident((4, D, FF_CHUNK)), _resident((DFF, D))],
        out_specs=[_rows(tm, D), _rows(tm, D), _rows(tm, D), _rows(tm, DFF), _rows(tm, D), _rows(tm, 2 * DFF),
                   _whole((8, D))],
        out_shape=[jax.ShapeDtypeStruct((T, D), F32), jax.ShapeDtypeStruct((T, D), BF16),
                   jax.ShapeDtypeStruct((T, D), BF16), jax.ShapeDtypeStruct((T, DFF), BF16),
                   jax.ShapeDtypeStruct((T, D), BF16), jax.ShapeDtypeStruct((T, 2 * DFF), BF16),
                   jax.ShapeDtypeStruct((8, D), F32)],
        scratch_shapes=[pltpu.VMEM((tm, DFF), F32), pltpu.VMEM((tm, DFF), F32)],
        compiler_params=_params(("arbitrary",), 56),
    )(h, tgt, g2, wf, wd)


def _merge_bwd(dhb, gg, yc, ya, wm, wa, wc):
    T = dhb.shape[0]
    tm = min(256, T)

    def body(dh_ref, gg_ref, yc_ref, ya_ref, wm_ref, wa_ref, wc_ref, dgg_ref, dyc_ref, dya_ref, do_ref, dh3_ref):
        dmix = _dot_nt(dh_ref[...], wm_ref[...])
        gc = _sig(gg_ref[:, 0:D])
        ga = _sig(gg_ref[:, D:2 * D])
        yc = yc_ref[...]
        ya = ya_ref[...]
        dgg_ref[:, 0:D] = (dmix * yc * gc * (1.0 - gc)).astype(BF16)
        dgg_ref[:, D:2 * D] = (dmix * ya * ga * (1.0 - ga)).astype(BF16)
        dyc = (dmix * gc).astype(BF16)
        dya = (dmix * ga).astype(BF16)
        dyc_ref[...] = dyc
        dya_ref[...] = dya
        do_ref[...] = _dot_nt(dya, wa_ref[...])
        dh3_ref[...] = _dot_nt(dyc, wc_ref[...])

    return pl.pallas_call(
        body, name="merge_bwd", grid=(T // tm,),
        in_specs=[_rows(tm, D), _rows(tm, 2 * D), _rows(tm, D), _rows(tm, D), _resident((D, D)), _resident((D, D)),
                  _resident((D, D))],
        out_specs=[_rows(tm, 2 * D), _rows(tm, D), _rows(tm, D), _rows(tm, D), _rows(tm, D)],
        out_shape=[jax.ShapeDtypeStruct((T, 2 * D), BF16), jax.ShapeDtypeStruct((T, D), BF16),
                   jax.ShapeDtypeStruct((T, D), BF16), jax.ShapeDtypeStruct((T, D), F32),
                   jax.ShapeDtypeStruct((T, D), F32)],
        compiler_params=_params(("parallel",)),
    )(dhb, gg, yc, ya, wm, wa, wc)


def _attn_bwd(q, kv, do, qg, kg, sink_rows, n_seq, S):
    T = n_seq * S
    nb, cur, prev, consts = _attn_specs(n_seq, S)
    tri, bias = _attn_consts()

    def body(q_ref, kvc_ref, kvp_ref, do_ref, qg_ref, kg_ref, tri_ref, bias_ref, sink_ref, dq_ref, dkc_ref, dkp_ref,
             dqg_ref, dsk_ref):
        n = pl.program_id(1)

        @pl.when((pl.program_id(0) == 0) & (n == 0))
        def _():
            dqg_ref[...] = jnp.zeros((1, HD), F32)
            dsk_ref[...] = jnp.zeros((8, 128), F32)

        lane = lax.broadcasted_iota(jnp.int32, (1, 128), 1)
        for kh in range(NKV):
            p, ps, from_prev, qhb, kpb, kcb, vpb, vcb, qy, rq = _attn_probs(
                kh, n, q_ref, kvc_ref, kvp_ref, qg_ref, kg_ref, tri_ref, bias_ref, sink_ref)
            dob = _stack_heads(do_ref, kh).astype(BF16)
            dp = jnp.where(from_prev, _dot_nt(vpb, dob), _dot_nt(vcb, dob))
            delta = jnp.sum(p * dp, axis=0, keepdims=True)
            dsp, dsc = _unfold((p * (dp - delta)).astype(BF16), from_prev)
            pp, pc = _unfold(p.astype(BF16), from_prev)
            dsink = -ps * delta
            dqh = (_dot_tn(dsp, kpb) + _dot_tn(dsc, kcb)) * SCALE
            dqg_ref[...] += jnp.sum(dqh * qy, axis=0, keepdims=True)
            dy = dqh * qg_ref[...]
            dq = (rq * (dy - qy * jnp.mean(dy * qy, axis=-1, keepdims=True))).astype(BF16)
            row = jnp.zeros((1, 128), F32)
            for g in range(GROUP):
                h = kh * GROUP + g
                dq_ref[:, h * HD:(h + 1) * HD] = dq[g * BLK:(g + 1) * BLK]
                row = row + jnp.where(lane == h, jnp.sum(dsink[:, g * BLK:(g + 1) * BLK], axis=1, keepdims=True), 0.0)
            dsk_ref[0:1, :] += row
            ks = slice(kh * HD, (kh + 1) * HD)
            vs = slice(2 * HD + kh * HD, 2 * HD + (kh + 1) * HD)
            dkp_ref[:, ks] = _dot(dsp, qhb)
            dkc_ref[:, ks] = _dot(dsc, qhb)
            dkp_ref[:, vs] = _dot(pp, dob)
            dkc_ref[:, vs] = _dot(pc, dob)

    return pl.pallas_call(
        body, name="attn_bwd", grid=(n_seq, nb),
        in_specs=[cur(D), cur(256), prev(256), cur(D)] + consts,
        out_specs=[cur(D), cur(256), cur(256), _whole((1, HD)), _whole((8, 128))],
        out_shape=[jax.ShapeDtypeStruct((T, D), BF16), jax.ShapeDtypeStruct((T, 256), F32),
                   jax.ShapeDtypeStruct((T, 256), F32), jax.ShapeDtypeStruct((1, HD), F32),
                   jax.ShapeDtypeStruct((8, 128), F32)],
        compiler_params=_params(("arbitrary", "arbitrary")),
    )(q, kv, kv, do, qg, kg, tri, bias, sink_rows)


def _kv_bwd(kv, dkc, dkp, kg, n_seq, S):
    T = n_seq * S
    nb, cur, _, _ = _attn_specs(n_seq, S)
    nxt = pl.BlockSpec((BLK, 256), lambda b, n: (b * nb + jnp.minimum(n + 1, nb - 1), 0))

    def body(kv_ref, dkc_ref, dkp_ref, kg_ref, dkv_ref, dkg_ref):
        n = pl.program_id(1)

        @pl.when((pl.program_id(0) == 0) & (n == 0))
        def _():
            dkg_ref[...] = jnp.zeros((1, HD), F32)

        has_next = jnp.where(n < nb - 1, 1.0, 0.0)
        d = dkc_ref[...] + has_next * dkp_ref[...]
        for kh in range(NKV):
            ks = slice(kh * HD, (kh + 1) * HD)
            k = kv_ref[:, ks]
            r = _rms64(k)
            y = k * r
            dkh = d[:, ks]
            dkg_ref[...] += jnp.sum(dkh * y, axis=0, keepdims=True)
            dy = dkh * kg_ref[...]
            dkv_ref[:, ks] = (r * (dy - y * jnp.mean(dy * y, axis=-1, keepdims=True))).astype(BF16)
        dkv_ref[:, 2 * HD:4 * HD] = d[:, 2 * HD:4 * HD].astype(BF16)

    return pl.pallas_call(
        body, name="kv_bwd", grid=(n_seq, nb),
        in_specs=[cur(256), cur(256), nxt, _resident((1, HD))],
        out_specs=[cur(256), _whole((1, HD))],
        out_shape=[jax.ShapeDtypeStruct((T, 256), BF16), jax.ShapeDtypeStruct((1, HD), F32)],
        compiler_params=_params(("arbitrary", "arbitrary")),
    )(kv, dkc, dkp, kg)


def _conv_bwd(h1, dh3, ag, dw, lng, lnb, n_seq, S):
    T = n_seq * S
    tc = min(256, S)
    nt = S // tc

    def body(h1_ref, dh3_ref, a_ref, gt_ref, dw_ref, lng_ref, lnb_ref, dag_ref, ddw_ref, st_ref, extd, acc8, shd):
        i = pl.program_id(1)

        @pl.when((pl.program_id(0) == 0) & (i == 0))
        def _():
            acc8[...] = jnp.zeros((CW * 8, D), F32)
            st_ref[...] = jnp.zeros((8, D), F32)

        @pl.when(i == 0)
        def _():
            extd[tc:tc + 32, :] = jnp.zeros((32, D), F32)

        h1 = h1_ref[...]
        mu = jnp.mean(h1, axis=-1, keepdims=True)
        cen = h1 - mu
        rstd = lax.rsqrt(jnp.mean(cen * cen, axis=-1, keepdims=True) + EPS)
        xh = cen * rstd
        h2 = xh * lng_ref[...] + lnb_ref[...]
        sg = _sig(h2)
        dh2 = dh3_ref[...] * (sg * (1.0 + h2 * (1.0 - sg)))
        st_ref[1:2, :] += jnp.sum(dh2 * xh, axis=0, keepdims=True)
        st_ref[2:3, :] += jnp.sum(dh2, axis=0, keepdims=True)
        dxh = dh2 * lng_ref[...]
        dh1 = rstd * (dxh - jnp.mean(dxh, axis=-1, keepdims=True)
                      - xh * jnp.mean(dxh * xh, axis=-1, keepdims=True))
        st_ref[0:1, :] += jnp.sum(dh1, axis=0, keepdims=True)
        extd[0:tc, :] = dh1
        _shift_copies(shd, extd, 0)
        for cb in range(D // 128):
            cs = slice(cb * 128, (cb + 1) * 128)
            for rb in range(tc // 128):
                rs = slice(rb * 128, (rb + 1) * 128)
                a = a_ref[rs, cs]
                sgt = _sig(gt_ref[rs, cs])
                h0 = a * sgt
                acc = jnp.zeros((128, 128), F32)
                for phase, offs in _tap_phases():
                    for m, o in enumerate(offs):
                        j = CW - 1 - o
                        ahead = shd[phase, rb * 128 + 8 * m:rb * 128 + 8 * m + 128, cs]
                        acc = acc + dw_ref[j:j + 1, cs] * ahead
                        acc8[j * 8:(j + 1) * 8, cs] += jnp.sum((h0 * ahead).reshape(16, 8, 128), axis=0)
                dag_ref[rs, cs] = (acc * sgt).astype(BF16)
                dag_ref[rs, cb * 128 + D:(cb + 1) * 128 + D] = (acc * a * sgt * (1.0 - sgt)).astype(BF16)
        extd[tc:tc + 32, :] = extd[0:32, :]

        @pl.when((pl.program_id(0) == n_seq - 1) & (i == nt - 1))
        def _():
            for j in range(CW):
                ddw_ref[j:j + 1, :] = jnp.sum(acc8[j * 8:(j + 1) * 8, :], axis=0, keepdims=True)
            ddw_ref[CW:32, :] = jnp.zeros((32 - CW, D), F32)

    tile = lambda col: pl.BlockSpec((tc, D), lambda b, i: (b * nt + (nt - 1 - i), col))
    return pl.pallas_call(
        body, name="conv_bwd", grid=(n_seq, nt),
        in_specs=[tile(0), tile(0), tile(0), tile(1), _resident((32, D)), _resident((1, D)), _resident((1, D))],
        out_specs=[pl.BlockSpec((tc, 2 * D), lambda b, i: (b * nt + (nt - 1 - i), 0)), _whole((32, D)),
                   _whole((8, D))],
        out_shape=[jax.ShapeDtypeStruct((T, 2 * D), BF16), jax.ShapeDtypeStruct((32, D), F32),
                   jax.ShapeDtypeStruct((8, D), F32)],
        scratch_shapes=[pltpu.VMEM((tc + 32, D), F32), pltpu.VMEM((CW * 8, D), F32),
                        pltpu.VMEM((8, tc + 24, D), F32)],
        compiler_params=_params(("arbitrary", "arbitrary")),
    )(h1, dh3, ag, ag, dw, lng, lnb)


def _in_proj_bwd(dag, dq, dkv, dgg, dh, x2, g1, w_in):
    T = x2.shape[0]
    tm = min(512, T)

    def body(dag_ref, dq_ref, dkv_ref, dgg_ref, dh_ref, x_ref, g_ref, w_ref, dx_ref, dg_ref):
        @pl.when(pl.program_id(0) == 0)
        def _():
            dg_ref[...] = jnp.zeros((1, D), F32)

        dxn = (_dot(dag_ref[...], w_ref[0:2048, :]) + _dot(dq_ref[...], w_ref[2048:3072, :])
               + _dot(dkv_ref[...], w_ref[3072:3328, :]) + _dot(dgg_ref[...], w_ref[3328:5376, :]))
        x = x_ref[...]
        rstd = lax.rsqrt(jnp.mean(x * x, axis=-1, keepdims=True) + EPS)
        xh = x * rstd
        dg_ref[...] += jnp.sum(dxn * xh, axis=0, keepdims=True)
        dxh = dxn * g_ref[...]
        dx_ref[...] = dh_ref[...] + rstd * (dxh - xh * jnp.mean(dxh * xh, axis=-1, keepdims=True))

    return pl.pallas_call(
        body, name="in_proj_bwd", grid=(T // tm,),
        in_specs=[_rows(tm, 2 * D), _rows(tm, D), _rows(tm, 256), _rows(tm, 2 * D), _rows(tm, D), _rows(tm, D),
                  _resident((1, D)), _resident((IN_COLS, D))],
        out_specs=[_rows(tm, D), _whole((1, D))],
        out_shape=[jax.ShapeDtypeStruct((T, D), F32), jax.ShapeDtypeStruct((1, D), F32)],
        compiler_params=_params(("arbitrary",)),
    )(dag, dq, dkv, dgg, dh, x2, g1, w_in)


def _tn_matmul(a, b, name, column_blocks=False):
    T, K = a.shape
    N = b.shape[1]
    tk = K if K <= 1024 else K // 2
    tn = N if N <= 1024 else (1024 if N % 1024 == 0 and not column_blocks else N // 4)
    tt = min(2048, T)
    assert K % tk == 0 and N % tn == 0 and T % tt == 0 and tk % 128 == 0 and tn % 128 == 0

    def body(a_ref, b_ref, o_ref):
        @pl.when(pl.program_id(2) == 0)
        def _():
            o_ref[...] = jnp.zeros((tk, tn), F32)

        o_ref[...] += _dot_tn(a_ref[...], b_ref[...])

    return pl.pallas_call(
        body, name=name, grid=(K // tk, N // tn, T // tt),
        in_specs=[pl.BlockSpec((tt, tk), lambda i, j, t: (t, i)), pl.BlockSpec((tt, tn), lambda i, j, t: (t, j))],
        out_specs=(pl.BlockSpec((None, tk, tn), lambda i, j, t: (j, i, 0)) if column_blocks
                   else pl.BlockSpec((tk, tn), lambda i, j, t: (i, j))),
        out_shape=jax.ShapeDtypeStruct((N // tn, K, tn) if column_blocks else (K, N), F32),
        compiler_params=_params(("parallel", "parallel", "arbitrary")),
    )(a, b)


def _place():
    x, y, c = lax.axis_index("x"), lax.axis_index("y"), lax.axis_index("c")
    chips = [(1 - x, y), (x, 1 - y), (1 - x, 1 - y)]
    return x, y, c, chips


def _own_slot(slots, mine):
    chip = 2 * lax.axis_index("x") + lax.axis_index("y")
    return lax.dynamic_update_slice(slots, mine[None], (chip,) + (0,) * mine.ndim)


def _row_tile(rows, unit):
    return max(t for t in range(unit, 513, unit) if rows % t == 0)


def _gather_weights(pack):
    rows = pack.shape[0]
    half = rows // 2

    def body(src, dst, token, send_sems, recv_sems):
        x, y, c, chips = _place()

        def piece(px, py, pc):
            return dst.at[2 * px + py, pl.ds(pc * half, half), :]

        def copy(k, block, to, from_src=False):
            return pltpu.make_async_remote_copy(
                src_ref=src.at[pl.ds(c * half, half), :] if from_src else piece(*block), dst_ref=piece(*block),
                send_sem=send_sems.at[k], recv_sem=recv_sems.at[k], device_id=to, device_id_type=MESH)

        first = [copy(k, (x, y, c), (*chip, c), from_src=True) for k, chip in enumerate(chips)]
        for cp in first:
            cp.start()
        passed = [copy(3 + k, (*chip, c), (x, y, 1 - c)) for k, chip in enumerate(chips)]
        for k, chip in enumerate(chips):
            copy(k, (*chip, c), (x, y, c)).wait_recv()
            passed[k].start()
        for k, chip in enumerate(chips):
            copy(3 + k, (*chip, 1 - c), (x, y, c)).wait_recv()
        for cp in first + passed:
            cp.wait_send()
        token[...] = jnp.zeros((8, 128), F32)

    got, token = pl.pallas_call(
        body, name="gather_weights",
        in_specs=[pl.BlockSpec(memory_space=pl.ANY)],
        out_specs=[pl.BlockSpec(memory_space=pl.ANY), pl.BlockSpec(memory_space=pltpu.VMEM)],
        out_shape=[jax.ShapeDtypeStruct((4, rows, D), pack.dtype), jax.ShapeDtypeStruct((8, 128), F32)],
        scratch_shapes=[pltpu.SemaphoreType.DMA((6,)), pltpu.SemaphoreType.DMA((6,))],
        compiler_params=pltpu.CompilerParams(has_side_effects=True),
    )(pack)
    return _own_slot(got, pack), token[0, 0]


def _add_halves(g, got, c_idx, name="grad_add_halves"):
    rows, w = g.shape[1], g.shape[2]
    half = rows // 2
    tr = _row_tile(half, 16)
    nt = half // tr

    def body(c_ref, g_ref, r_ref, o_ref):
        o_ref[...] = (g_ref[...] + r_ref[...]).astype(BF16)

    return pl.pallas_call(
        body, name=name,
        grid_spec=pltpu.PrefetchScalarGridSpec(
            num_scalar_prefetch=1, grid=(4, nt),
            in_specs=[pl.BlockSpec((1, tr, w), lambda q, i, c_ref: (q, c_ref[0] * nt + i, 0)),
                      pl.BlockSpec((1, tr, w), lambda q, i, c_ref: (q, i, 0))],
            out_specs=pl.BlockSpec((1, tr, w), lambda q, i, c_ref: (q, i, 0))),
        out_shape=jax.ShapeDtypeStruct((4, half, w), BF16),
        compiler_params=_params(("parallel", "parallel")),
    )(c_idx, g, got)


def _own_piece(p):
    chip = 2 * lax.axis_index("x") + lax.axis_index("y")
    return lax.dynamic_index_in_dim(p, chip, axis=0, keepdims=False)


def _sum_chips(r, c_idx, name="grad_sum_chips"):
    half, w = r.shape[1], r.shape[2]
    tr = _row_tile(half, 16)
    nt = half // tr

    def body(c_ref, r_ref, o_ref):
        acc = r_ref[0].astype(F32)
        for q in range(1, 4):
            acc = acc + r_ref[q].astype(F32)
        o_ref[...] = acc

    return pl.pallas_call(
        body, name=name,
        grid_spec=pltpu.PrefetchScalarGridSpec(
            num_scalar_prefetch=1, grid=(nt,),
            in_specs=[pl.BlockSpec((4, tr, w), lambda i, c_ref: (0, i, 0))],
            out_specs=pl.BlockSpec((tr, w), lambda i, c_ref: (c_ref[0] * nt + i, 0))),
        out_shape=jax.ShapeDtypeStruct((2 * half, w), F32),
        compiler_params=_params(("parallel",)),
    )(c_idx, r)


def _join_halves(f):
    half = f.shape[0] // 2

    def body(src, dst, send_sem, recv_sem):
        x, y, c, _ = _place()
        cp = pltpu.make_async_remote_copy(
            src_ref=src.at[pl.ds(c * half, half), :], dst_ref=dst.at[pl.ds(c * half, half), :], send_sem=send_sem,
            recv_sem=recv_sem, device_id=(x, y, 1 - c), device_id_type=MESH)
        cp.start()
        pltpu.make_async_remote_copy(
            src_ref=src.at[pl.ds(c * half, half), :], dst_ref=dst.at[pl.ds((1 - c) * half, half), :],
            send_sem=send_sem, recv_sem=recv_sem, device_id=(x, y, 1 - c), device_id_type=MESH).wait_recv()
        cp.wait_send()

    return pl.pallas_call(
        body, name="grad_join_halves",
        in_specs=[pl.BlockSpec(memory_space=pl.ANY)], out_specs=pl.BlockSpec(memory_space=pl.ANY),
        out_shape=jax.ShapeDtypeStruct(f.shape, f.dtype), input_output_aliases={0: 0},
        scratch_shapes=[pltpu.SemaphoreType.DMA, pltpu.SemaphoreType.DMA],
        compiler_params=pltpu.CompilerParams(has_side_effects=True),
    )(f)


_HBM = pl.BlockSpec(memory_space=pltpu.HBM)
_SEM = pl.BlockSpec(memory_space=pltpu.SEMAPHORE)
_EFFECT = pltpu.SideEffectType.DATAFLOW_SIDE_EFFECTING


def _start_copies(name, bufs, n_sems, plan):
    nb = len(bufs)

    def body(*refs):
        for cp in plan(refs[:nb], refs[nb], refs[nb + 1])[0]:
            cp.start()
        refs[-1][...] = jnp.zeros((8, 128), F32)

    out = pl.pallas_call(
        body, name=name,
        out_shape=(pltpu.SemaphoreType.DMA((n_sems,)), pltpu.SemaphoreType.DMA((n_sems,)),
                   *[pltpu.HBM(b.shape, b.dtype) for b in bufs], jax.ShapeDtypeStruct((8, 128), F32)),
        in_specs=[_HBM] * nb, out_specs=(_SEM, _SEM, *[_HBM] * nb, pl.BlockSpec(memory_space=pltpu.VMEM)),
        input_output_aliases={i: 2 + i for i in range(nb)},
        compiler_params=pltpu.CompilerParams(has_side_effects=_EFFECT),
    )(*[pltpu.with_memory_space_constraint(b, pltpu.HBM) for b in bufs])
    return out[0], out[1], list(out[2:2 + nb]), out[-1]


def _wait_copies(name, send_sems, recv_sems, bufs, after, plan):
    nb = len(bufs)

    def body(*refs):
        _, sends, recvs = plan(refs[:nb], refs[nb], refs[nb + 1])
        for cp in sends:
            cp.wait_send()
        for cp in recvs:
            cp.wait_recv()

    out = pl.pallas_call(
        body, name=name,
        out_shape=tuple(pltpu.HBM(b.shape, b.dtype) for b in bufs),
        in_specs=[_HBM] * nb + [_SEM, _SEM] + [pl.BlockSpec(memory_space=pl.ANY)] * len(after),
        out_specs=tuple([_HBM] * nb),
        input_output_aliases={i: i for i in range(nb)},
        compiler_params=pltpu.CompilerParams(has_side_effects=_EFFECT),
    )(*bufs, send_sems, recv_sems, *after)
    return list(out)


def _plan_gather_direct(halves):
    n = len(halves)

    def plan(refs, send_sems, recv_sems):
        x, y, c, chips = _place()
        starts, recvs = [], []
        for b, half in enumerate(halves):
            src, land = refs[b], refs[n + b]
            for k, (cx, cy) in enumerate(chips):
                for d in range(2):
                    other = c if d == 0 else 1 - c
                    i = 6 * b + 2 * k + d
                    starts.append(pltpu.make_async_remote_copy(
                        src_ref=src.at[pl.ds(c * half, half), :],
                        dst_ref=land.at[2 * x + y, pl.ds(c * half, half), :],
                        send_sem=send_sems.at[i], recv_sem=recv_sems.at[i], device_id=(cx, cy, other),
                        device_id_type=MESH))
                    recvs.append(pltpu.make_async_remote_copy(
                        src_ref=src.at[pl.ds(c * half, half), :],
                        dst_ref=land.at[2 * cx + cy, pl.ds(other * half, half), :],
                        send_sem=send_sems.at[i], recv_sem=recv_sems.at[i], device_id=(cx, cy, other),
                        device_id_type=MESH))
        return starts, starts, recvs
    return plan


def _plan_swap_halves(halves):
    n = len(halves)

    def plan(refs, send_sems, recv_sems):
        x, y, c, _ = _place()
        cps = [pltpu.make_async_remote_copy(
            src_ref=refs[b].at[:, pl.ds((1 - c) * half, half), :], dst_ref=refs[n + b], send_sem=send_sems.at[b],
            recv_sem=recv_sems.at[b], device_id=(x, y, 1 - c), device_id_type=MESH)
            for b, half in enumerate(halves)]
        return cps, cps, cps
    return plan


def _plan_scatter_chips(n):
    def plan(refs, send_sems, recv_sems):
        x, y, c, chips = _place()
        me = 2 * x + y
        starts, recvs = [], []
        for b in range(n):
            src, land = refs[b], refs[n + b]
            for k, (cx, cy) in enumerate(chips):
                i = 3 * b + k
                starts.append(pltpu.make_async_remote_copy(
                    src_ref=src.at[2 * cx + cy], dst_ref=land.at[me], send_sem=send_sems.at[i],
                    recv_sem=recv_sems.at[i], device_id=(cx, cy, c), device_id_type=MESH))
                recvs.append(pltpu.make_async_remote_copy(
                    src_ref=src.at[me], dst_ref=land.at[2 * cx + cy], send_sem=send_sems.at[i],
                    recv_sem=recv_sems.at[i], device_id=(cx, cy, c), device_id_type=MESH))
        return starts, starts, recvs
    return plan


def _plan_join_halves(halves):
    def plan(refs, send_sems, recv_sems):
        x, y, c, _ = _place()
        starts, recvs = [], []
        for b, half in enumerate(halves):
            mine, theirs = refs[b].at[pl.ds(c * half, half), :], refs[b].at[pl.ds((1 - c) * half, half), :]
            starts.append(pltpu.make_async_remote_copy(
                src_ref=mine, dst_ref=mine, send_sem=send_sems.at[b], recv_sem=recv_sems.at[b],
                device_id=(x, y, 1 - c), device_id_type=MESH))
            recvs.append(pltpu.make_async_remote_copy(
                src_ref=mine, dst_ref=theirs, send_sem=send_sems.at[b], recv_sem=recv_sems.at[b],
                device_id=(x, y, 1 - c), device_id_type=MESH))
        return starts, starts, recvs
    return plan


def _allreduce_small(vec):
    def body(v_ref, o_ref, gath, send_sems, recv_sems):
        x, y, c, _ = _place()
        me = 4 * x + 2 * y + c
        gath[me] = v_ref[...]
        sends = []
        for k in range(1, 8):
            peer = (x ^ (k >> 2), y ^ ((k >> 1) & 1), c ^ (k & 1))
            sends.append(pltpu.make_async_remote_copy(
                src_ref=v_ref, dst_ref=gath.at[me], send_sem=send_sems.at[k - 1], recv_sem=recv_sems.at[k - 1],
                device_id=peer, device_id_type=MESH))
        for cp in sends:
            cp.start()
        for k in range(1, 8):
            peer = (x ^ (k >> 2), y ^ ((k >> 1) & 1), c ^ (k & 1))
            pltpu.make_async_remote_copy(
                src_ref=v_ref, dst_ref=gath.at[4 * peer[0] + 2 * peer[1] + peer[2]], send_sem=send_sems.at[k - 1],
                recv_sem=recv_sems.at[k - 1], device_id=peer, device_id_type=MESH).wait_recv()
        for cp in sends:
            cp.wait_send()
        acc = gath[0]
        for d in range(1, 8):
            acc = acc + gath[d]
        o_ref[...] = acc

    return pl.pallas_call(
        body, name="allreduce_small",
        in_specs=[pl.BlockSpec(memory_space=pltpu.VMEM)], out_specs=pl.BlockSpec(memory_space=pltpu.VMEM),
        out_shape=jax.ShapeDtypeStruct(vec.shape, F32),
        scratch_shapes=[pltpu.VMEM((8,) + vec.shape, F32), pltpu.SemaphoreType.DMA((7,)),
                        pltpu.SemaphoreType.DMA((7,))],
    )(vec)


def _adamw(w, g, m, v, name, after):
    shape = w.shape
    if w.ndim == 1 or w.size <= 128 * 128:
        two_d = (1, w.size) if w.size % 128 else (w.size // 128, 128)
    else:
        two_d = (w.shape[0], w.size // w.shape[0])
    rows, cols = two_d
    tr = _row_tile(rows, 8) if rows % 8 == 0 and rows > 512 else rows

    def body(w_ref, g_ref, m_ref, v_ref, after_ref, d_ref, nm_ref, nv_ref):
        gr = g_ref[...]
        nm = B1 * m_ref[...] + (1.0 - B1) * gr
        nv = B2 * v_ref[...] + (1.0 - B2) * (gr * gr)
        m_hat = nm / (1.0 - B1 ** STEP)
        v_hat = nv / (1.0 - B2 ** STEP)
        d_ref[...] = -LR * (m_hat / (jnp.sqrt(v_hat) + AEPS) + WD * w_ref[...])
        nm_ref[...] = nm
        nv_ref[...] = nv

    spec = pl.BlockSpec((tr, cols), lambda i: (i, 0))
    outs = pl.pallas_call(
        body, name=name, grid=(rows // tr,),
        in_specs=[spec] * 4 + [pl.BlockSpec(memory_space=pl.ANY)], out_specs=[spec] * 3,
        out_shape=[jax.ShapeDtypeStruct(two_d, F32)] * 3,
        compiler_params=_params(("parallel",)),
    )(*[t.reshape(two_d) for t in (w, g, m, v)], after)
    return [o.reshape(shape) for o in outs]


def _rows_stacked(g, lo, n_rows):
    return g[:, lo:lo + n_rows].reshape(4 * n_rows, D)


def _rows_to_slots(t):
    return t.reshape(4, t.shape[0] // 4, D)


def _pack_first(w_in, w_conv_out, conv_dw_w):
    dw = jnp.pad(conv_dw_w.reshape(CW, 256), ((0, 1), (0, 0)))
    dw_bits = lax.bitcast_convert_type(dw, BF16).reshape(16, D)
    return jnp.concatenate([w_in.T.astype(BF16), w_conv_out.astype(BF16), dw_bits, jnp.zeros((16, D), BF16)],
                           axis=0)


def _unpack_first(g):
    w_in_t = _rows_stacked(g, 0, ROWS_W_IN)
    wc = _rows_stacked(g, ROWS_W_IN, ROWS_SQ)
    o = ROWS_W_IN + ROWS_SQ
    dw = lax.bitcast_convert_type(g[:, o:o + 16].reshape(4, 32, 256, 2), F32)
    return w_in_t, wc, jnp.transpose(dw, (1, 0, 2)).reshape(32, D)


def _pack_late(w_attn_out, w_merge_out, w_ffn_down):
    return jnp.concatenate([w_attn_out.astype(BF16), w_merge_out.astype(BF16), w_ffn_down.astype(BF16)], axis=0)


def _unpack_late(g):
    return (_rows_stacked(g, 0, ROWS_SQ), _rows_stacked(g, ROWS_SQ, ROWS_SQ),
            _rows_stacked(g, 2 * ROWS_SQ, ROWS_DOWN))


class _Exchanges:
    def __init__(self, late_pack, wf_shard):
        self.c_idx = lax.axis_index("c").astype(jnp.int32).reshape(1)
        packs = [late_pack, wf_shard]
        self.late_plan = _plan_gather_direct([p.shape[0] // 2 for p in packs])
        slots = [lax.empty((4,) + p.shape, BF16) for p in packs]
        self.late = _start_copies("gather_late_start", packs + slots, 6 * len(packs), self.late_plan)
        self.first_token = self.late[3][0, 0]

    def late_weights(self, after):
        send_sems, recv_sems, bufs, _ = self.late
        pack, wf_shard, slots, wf_slots = _wait_copies("gather_late_wait", send_sems, recv_sems, bufs, after,
                                                       self.late_plan)
        wa, wm, wd = _unpack_late(_own_slot(slots, pack))
        return wa, wm, _own_slot(wf_slots, wf_shard), wd

    def reduce_start(self, d_wd, d_wf4, d_wm, d_wa, d_wc):
        gs = [_rows_to_slots(d_wd), jnp.concatenate([_rows_to_slots(t) for t in (d_wm, d_wa, d_wc)], axis=1), d_wf4]
        self.halves = [g.shape[1] // 2 for g in gs]
        self.swap_plan = _plan_swap_halves(self.halves)
        lands = [lax.empty((4, h, g.shape[2]), F32) for g, h in zip(gs, self.halves)]
        self.swap = _start_copies("grad_swap_start", gs + lands, len(gs), self.swap_plan)
        return self.swap[3][0, 0]

    def reduce_mid(self, after):
        send_sems, recv_sems, bufs, _ = self.swap
        bufs = _wait_copies("grad_swap_wait", send_sems, recv_sems, bufs, after, self.swap_plan)
        n = len(self.halves)
        ps = [_add_halves(bufs[b], bufs[n + b], self.c_idx, "grad_add_halves_%d" % b) for b in range(n)]
        self.scatter_plan = _plan_scatter_chips(n)
        self.scatter = _start_copies("grad_scatter_start", ps + [lax.empty(p.shape, BF16) for p in ps], 3 * n,
                                     self.scatter_plan)
        return self.scatter[3][0, 0]

    def reduce_late(self, after):
        send_sems, recv_sems, bufs, _ = self.scatter
        bufs = _wait_copies("grad_scatter_wait", send_sems, recv_sems, bufs, after, self.scatter_plan)
        n = len(self.halves)
        fs = [_sum_chips(_own_slot(bufs[n + b], _own_piece(bufs[b])), self.c_idx, "grad_sum_chips_%d" % b)
              for b in range(n)]
        self.join_plan = _plan_join_halves(self.halves)
        self.join = _start_copies("grad_join_start", fs, n, self.join_plan)

    def reduce_end(self, after):
        send_sems, recv_sems, bufs, _ = self.join
        g_wd, sq, g_wf = _wait_copies("grad_join_wait", send_sems, recv_sems, bufs, after, self.join_plan)
        return g_wd, g_wf, sq[0:ROWS_SQ], sq[ROWS_SQ:2 * ROWS_SQ], sq[2 * ROWS_SQ:3 * ROWS_SQ]

    def w_in_start(self, d_w_in_t):
        g = _rows_to_slots(d_w_in_t)
        self.w_half = g.shape[1] // 2
        self.w_swap_plan = _plan_swap_halves([self.w_half])
        self.w_swap = _start_copies("grad_w_in_swap_start", [g, lax.empty((4, self.w_half, D), F32)], 1,
                                    self.w_swap_plan)
        return self.w_swap[3]

    def w_in_mid(self, after):
        send_sems, recv_sems, bufs, _ = self.w_swap
        g, got = _wait_copies("grad_w_in_swap_wait", send_sems, recv_sems, bufs, after, self.w_swap_plan)
        p = _add_halves(g, got, self.c_idx, "grad_add_halves_w_in")
        self.w_scatter_plan = _plan_scatter_chips(1)
        self.w_scatter = _start_copies("grad_w_in_scatter_start", [p, lax.empty(p.shape, BF16)], 3,
                                       self.w_scatter_plan)
        return self.w_scatter[3]

    def w_in_end(self, after):
        send_sems, recv_sems, bufs, _ = self.w_scatter
        p, got = _wait_copies("grad_w_in_scatter_wait", send_sems, recv_sems, bufs, after, self.w_scatter_plan)
        return _join_halves(_sum_chips(_own_slot(got, _own_piece(p)), self.c_idx, "grad_sum_chips_w_in"))


def _local_grads(x, loss_target, norm_mix_g, conv_dw_b, conv_ln_g, conv_ln_b, q_norm_g, k_norm_g, sinks, norm_ffn_g,
                 w_in, wc, dw, exchanges):
    n_seq, S, _ = x.shape
    T = n_seq * S
    x2 = x.reshape(T, D)
    tgt = loss_target.reshape(T, D)
    row = lambda t: t.reshape(1, -1)
    g1, g2 = row(norm_mix_g), row(norm_ffn_g)
    qg, kg = row(q_norm_g), row(k_norm_g)
    lng, lnb, dwb = row(conv_ln_g), row(conv_ln_b), row(conv_dw_b)
    sink_rows = jnp.repeat(sinks.reshape(NKV, GROUP), BLK, axis=1)

    xn, ag, q, kv, gg, h1, h3, yc = _in_proj_conv_fwd(x2, g1 + exchanges.first_token, w_in, dw, dwb, lng, lnb, wc,
                                                      n_seq, S)
    o = _attn_fwd(q, kv, qg, kg, sink_rows, n_seq, S)
    wa, wm, wf, wd = exchanges.late_weights([o, yc])
    ya, mix, h = _merge_fwd(x2, gg, yc, o, wa, wm)
    dh, dhb, hn, act, dout, dgu, ffn_stats = _ffn(h, tgt, g2, wf, wd)
    d_wd = _tn_matmul(act, dout, "dw_ffn_down")
    d_wf = _tn_matmul(hn, dgu, "dw_ffn_in", column_blocks=True)
    d_wm = _tn_matmul(mix, dhb, "dw_merge")
    dgg, dyc, dya, do, dh3 = _merge_bwd(dhb, gg, yc, ya, wm, wa, wc)
    d_wa = _tn_matmul(o, dya, "dw_attn_out")
    d_wc = _tn_matmul(h3, dyc, "dw_conv_out")
    token = exchanges.reduce_start(d_wd, d_wf, d_wm, d_wa, d_wc)
    dq, dkc, dkp, dqg, dsk = _attn_bwd(q, kv, do, qg + token, kg, sink_rows, n_seq, S)
    token = exchanges.reduce_mid([dq])
    dkv, dkg = _kv_bwd(kv, dkc, dkp, kg, n_seq, S)
    dag, ddw, conv_stats = _conv_bwd(h1, dh3, ag, dw, lng + token, lnb, n_seq, S)
    dx, dg1 = _in_proj_bwd(dag, dq, dkv, dgg, dh, x2, g1, w_in)
    exchanges.reduce_late([dx])
    d_w_in = jnp.concatenate([_tn_matmul(dag, xn, "dw_in_conv"), _tn_matmul(dq, xn, "dw_in_q"),
                              _tn_matmul(dkv, xn, "dw_in_kv"), _tn_matmul(dgg, xn, "dw_in_gates")], axis=0)

    heads = jnp.concatenate([dqg[0], dkg[0], dsk[0, :NQ], jnp.zeros((D - 2 * HD - NQ,), F32)])
    vec = jnp.concatenate([dg1, conv_stats[0:3], ffn_stats[0:1], heads[None], jnp.zeros((2, D), F32), ddw], axis=0)
    return ffn_stats[1], dx.reshape(x.shape), d_w_in, vec


def kernel(x, norm_mix_g, w_in, conv_dw_w, conv_dw_b, conv_ln_g, conv_ln_b, w_conv_out, q_norm_g, k_norm_g, sinks, w_attn_out, w_merge_out, norm_ffn_g, w_ffn_in, w_ffn_down, loss_target, m_norm_mix_g, m_w_in, m_conv_dw_w, m_conv_dw_b, m_conv_ln_g, m_conv_ln_b, m_w_conv_out, m_q_norm_g, m_k_norm_g, m_sinks, m_w_attn_out, m_w_merge_out, m_norm_ffn_g, m_w_ffn_in, m_w_ffn_down, v_norm_mix_g, v_w_in, v_conv_dw_w, v_conv_dw_b, v_conv_ln_g, v_conv_ln_b, v_w_conv_out, v_q_norm_g, v_k_norm_g, v_sinks, v_w_attn_out, v_w_merge_out, v_norm_ffn_g, v_w_ffn_in, v_w_ffn_down):
    chip = 2 * lax.axis_index("x") + lax.axis_index("y")

    first, token = _gather_weights(_pack_first(w_in, w_conv_out, conv_dw_w))
    exchanges = _Exchanges(_pack_late(w_attn_out, w_merge_out, w_ffn_down) + token.astype(BF16),
                           w_ffn_in.astype(BF16) + token.astype(BF16))
    sq_cols, grad_x, d_w_in, vec = _local_grads(x, loss_target, norm_mix_g, conv_dw_b, conv_ln_g, conv_ln_b,
                                                q_norm_g, k_norm_g, sinks, norm_ffn_g, *_unpack_first(first),
                                                exchanges)
    loss = lax.psum(0.5 / D * jnp.sum(sq_cols), ("x", "y", "c"))

    g_wd, g_wf, g_wm, g_wa, g_wc = exchanges.reduce_end([d_w_in])
    small = _allreduce_small(vec)
    g_dw = lax.dynamic_slice_in_dim(small[8:8 + CW], chip * 256, 256, axis=1).reshape(CW, 1, 256)
    grads = {
        "norm_mix_g": small[0], "conv_dw_w": g_dw, "conv_dw_b": small[1], "conv_ln_g": small[2],
        "conv_ln_b": small[3], "w_conv_out": g_wc, "q_norm_g": small[5, 0:HD], "k_norm_g": small[5, HD:2 * HD],
        "sinks": small[5, 2 * HD:2 * HD + NQ], "w_attn_out": g_wa, "w_merge_out": g_wm, "norm_ffn_g": small[4],
        "w_ffn_in": g_wf, "w_ffn_down": g_wd,
    }
    weights = dict(norm_mix_g=norm_mix_g, w_in=w_in, conv_dw_w=conv_dw_w, conv_dw_b=conv_dw_b, conv_ln_g=conv_ln_g,
                   conv_ln_b=conv_ln_b, w_conv_out=w_conv_out, q_norm_g=q_norm_g, k_norm_g=k_norm_g, sinks=sinks,
                   w_attn_out=w_attn_out, w_merge_out=w_merge_out, norm_ffn_g=norm_ffn_g, w_ffn_in=w_ffn_in,
                   w_ffn_down=w_ffn_down)
    m_in = dict(norm_mix_g=m_norm_mix_g, w_in=m_w_in, conv_dw_w=m_conv_dw_w, conv_dw_b=m_conv_dw_b,
                conv_ln_g=m_conv_ln_g, conv_ln_b=m_conv_ln_b, w_conv_out=m_w_conv_out, q_norm_g=m_q_norm_g,
                k_norm_g=m_k_norm_g, sinks=m_sinks, w_attn_out=m_w_attn_out, w_merge_out=m_w_merge_out,
                norm_ffn_g=m_norm_ffn_g, w_ffn_in=m_w_ffn_in, w_ffn_down=m_w_ffn_down)
    v_in = dict(norm_mix_g=v_norm_mix_g, w_in=v_w_in, conv_dw_w=v_conv_dw_w, conv_dw_b=v_conv_dw_b,
                conv_ln_g=v_conv_ln_g, conv_ln_b=v_conv_ln_b, w_conv_out=v_w_conv_out, q_norm_g=v_q_norm_g,
                k_norm_g=v_k_norm_g, sinks=v_sinks, w_attn_out=v_w_attn_out, w_merge_out=v_w_merge_out,
                norm_ffn_g=v_norm_ffn_g, w_ffn_in=v_w_ffn_in, w_ffn_down=v_w_ffn_down)
    names = list(weights)
    big = ("w_conv_out", "w_attn_out", "w_merge_out", "w_ffn_in", "w_ffn_down")
    updates = {}
    after = exchanges.w_in_start(d_w_in)
    for n in names:
        if n != "w_in" and n not in big:
            updates[n] = _adamw(weights[n], grads[n], m_in[n], v_in[n], "adamw_" + n, after)
    after = exchanges.w_in_mid([updates[n][0] for n in updates])
    for n in big:
        updates[n] = _adamw(weights[n], grads[n], m_in[n], v_in[n], "adamw_" + n, after)
    g_w_in_t = exchanges.w_in_end([updates[n][0] for n in big])
    grads["w_in"] = g_w_in_t.T
    updates["w_in"] = [t.T for t in _adamw(w_in.T, g_w_in_t, m_w_in.T, v_w_in.T, "adamw_w_in", g_w_in_t)]
    return (loss, grad_x, *[grads[n] for n in names], *[updates[n][0] for n in names],
            *[updates[n][1] for n in names], *[updates[n][2] for n in names])
```

```python
import functools
import math

import jax
import jax.numpy as jnp
import numpy as np
from jax import lax
from jax.experimental import pallas as pl
from jax.experimental.pallas import tpu as pltpu

F32 = jnp.float32
BF16 = jnp.bfloat16

D = 1024
CW = 31
HD = 64
NQ = 16
NKV = 2
GROUP = NQ // NKV
BLK = 128
DFF = 2816
EPS = 1e-6
NEG = -1e30
IN_COLS = 5376
SCALE = 1.0 / math.sqrt(HD)

LR, B1, B2, AEPS, WD, STEP = 0.001, 0.9, 0.999, 1e-08, 0.01, 10

MIB = 1024 * 1024
MESH = pl.DeviceIdType.MESH

ROWS_W_IN = 1344
ROWS_SQ = 256
ROWS_FFN_IN = 1408
ROWS_DOWN = 704
ROWS_MAT = ROWS_W_IN + 3 * ROWS_SQ + ROWS_FFN_IN + ROWS_DOWN
ROWS_DW = 32
ROWS_PACK = ROWS_MAT + ROWS_DW
VEC_ROWS = 40


def _sig(x):
    return 0.5 * jnp.tanh(0.5 * x) + 0.5


def _dot(a, b):
    return jnp.dot(a, b, preferred_element_type=F32)


def _dot_nt(a, b):
    return lax.dot_general(a, b, (((1,), (1,)), ((), ())), preferred_element_type=F32)


def _dot_tn(a, b):
    return lax.dot_general(a, b, (((0,), (0,)), ((), ())), preferred_element_type=F32)


def _params(sem, vmem_mib=48):
    return pltpu.CompilerParams(dimension_semantics=sem, vmem_limit_bytes=vmem_mib * MIB)


def _resident(shape):
    return pl.BlockSpec(shape, lambda *_: (0,) * len(shape), pipeline_mode=pl.Buffered(1))


def _whole(shape):
    return pl.BlockSpec(shape, lambda *_: (0,) * len(shape))


def _rows(tm, cols, col_block=0):
    return pl.BlockSpec((tm, cols), lambda i: (i, col_block))


def _tap_phases():
    return [(phase, list(range(phase, CW, 8))) for phase in range(8)]


def _shift_copies(dst, src, base):
    for phase, taps in _tap_phases():
        n = dst.shape[1] - 8 * (4 - len(taps))
        dst[phase, 0:n, :] = src[base + phase:base + phase + n, :]


def _in_proj_conv_fwd(x2, g1, w_in, dw, dwb, lng, lnb, wc, n_seq, S):
    T = n_seq * S
    tc = min(256, S)
    nt = S // tc

    def body(x_ref, g_ref, w_ref, dw_ref, dwb_ref, lng_ref, lnb_ref, wc_ref, xn_ref, ag_ref, q_ref, kv_ref, gg_ref,
             h1_ref, h3_ref, yc_ref, ext, sh):
        i = pl.program_id(1)

        @pl.when(i == 0)
        def _():
            ext[0:32, :] = jnp.zeros((32, D), F32)

        x = x_ref[...]
        rstd = lax.rsqrt(jnp.mean(x * x, axis=-1, keepdims=True) + EPS)
        xn = (x * rstd * g_ref[...]).astype(BF16)
        xn_ref[...] = xn
        ag = _dot_nt(xn, w_ref[0:2048, :])
        ag_ref[...] = ag
        ext[32:32 + tc, :] = ag[:, 0:D] * _sig(ag[:, D:2 * D])
        q_ref[...] = _dot_nt(xn, w_ref[2048:3072, :])
        kv_ref[...] = _dot_nt(xn, w_ref[3072:3328, :])
        gg_ref[...] = _dot_nt(xn, w_ref[3328:5376, :])
        _shift_copies(sh, ext, 2)
        for cb in range(D // 128):
            cs = slice(cb * 128, (cb + 1) * 128)
            acc = jnp.broadcast_to(dwb_ref[:, cs], (tc, 128))
            for phase, taps in _tap_phases():
                for m, j in enumerate(taps):
                    acc = acc + dw_ref[j:j + 1, cs] * sh[phase, 8 * m:8 * m + tc, cs]
            h1_ref[:, cs] = acc
        ext[0:32, :] = ext[tc:tc + 32, :]
        h1 = h1_ref[...]
        mu = jnp.mean(h1, axis=-1, keepdims=True)
        cen = h1 - mu
        var = jnp.mean(cen * cen, axis=-1, keepdims=True)
        h2 = cen * lax.rsqrt(var + EPS) * lng_ref[...] + lnb_ref[...]
        h3 = (h2 * _sig(h2)).astype(BF16)
        h3_ref[...] = h3
        yc_ref[...] = _dot(h3, wc_ref[...])

    tile = lambda cols: pl.BlockSpec((tc, cols), lambda b, i: (b * nt + i, 0))
    shape = lambda cols, dtype: jax.ShapeDtypeStruct((T, cols), dtype)
    return pl.pallas_call(
        body, name="in_proj_conv_fwd", grid=(n_seq, nt),
        in_specs=[tile(D), _resident((1, D)), _resident((IN_COLS, D)), _resident((32, D)), _resident((1, D)),
                  _resident((1, D)), _resident((1, D)), _resident((D, D))],
        out_specs=[tile(D), tile(2 * D), tile(D), tile(256), tile(2 * D), tile(D), tile(D), tile(D)],
        out_shape=[shape(D, BF16), shape(2 * D, F32), shape(D, F32), shape(256, F32), shape(2 * D, F32),
                   shape(D, F32), shape(D, BF16), shape(D, F32)],
        scratch_shapes=[pltpu.VMEM((32 + tc, D), F32), pltpu.VMEM((8, tc + 24, D), F32)],
        compiler_params=_params(("parallel", "arbitrary"), 56),
    )(x2, g1, w_in, dw, dwb, lng, lnb, wc)


def _attn_consts():
    k = np.arange(BLK)[:, None]
    i = np.arange(GROUP * BLK)[None, :] % BLK
    from_prev = k > i
    dist = np.where(from_prev, i + BLK - k, i - k).astype(np.float32)
    head = np.arange(GROUP * BLK)[None, :] // BLK
    bias = []
    for kh in range(NKV):
        slope = np.exp2(-8.0 * (kh * GROUP + head + 1) / NQ).astype(np.float32)
        bias.append(-slope * dist)
    return jnp.asarray(from_prev.astype(np.float32)), jnp.asarray(np.stack(bias))


def _stack_heads(ref, kh):
    return jnp.concatenate([ref[:, (kh * GROUP + g) * HD:(kh * GROUP + g + 1) * HD] for g in range(GROUP)], axis=0)


def _rms64(t):
    return lax.rsqrt(jnp.mean(t * t, axis=-1, keepdims=True) + EPS)


def _attn_probs(kh, n, q_ref, kvc_ref, kvp_ref, qg_ref, kg_ref, tri_ref, bias_ref, sink_ref):
    ks = slice(kh * HD, (kh + 1) * HD)
    vs = slice(2 * HD + kh * HD, 2 * HD + (kh + 1) * HD)
    kp, kc = kvp_ref[:, ks], kvc_ref[:, ks]
    kpb = (kp * _rms64(kp) * kg_ref[...]).astype(BF16)
    kcb = (kc * _rms64(kc) * kg_ref[...]).astype(BF16)
    qs = _stack_heads(q_ref, kh)
    rq = _rms64(qs)
    qy = qs * rq
    qhb = (qy * (qg_ref[...] * SCALE)).astype(BF16)
    from_prev = tri_ref[...] > 0.5
    no_prev = jnp.where(n > 0, 0.0, NEG)
    s = jnp.where(from_prev, _dot_nt(kpb, qhb) + no_prev, _dot_nt(kcb, qhb)) + bias_ref[kh]
    sink = sink_ref[kh:kh + 1, :]
    m = jnp.maximum(jnp.max(s, axis=0, keepdims=True), sink)
    e = jnp.exp(s - m)
    es = jnp.exp(sink - m)
    rz = 1.0 / (jnp.sum(e, axis=0, keepdims=True) + es)
    return e * rz, es * rz, from_prev, qhb, kpb, kcb, kvp_ref[:, vs].astype(BF16), kvc_ref[:, vs].astype(BF16), qy, rq


def _unfold(t, from_prev):
    zero = jnp.zeros_like(t)
    return jnp.where(from_prev, t, zero), jnp.where(from_prev, zero, t)


def _attn_specs(n_seq, S):
    nb = S // BLK
    cur = lambda cols: pl.BlockSpec((BLK, cols), lambda b, n: (b * nb + n, 0))
    prev = lambda cols: pl.BlockSpec((BLK, cols), lambda b, n: (b * nb + jnp.maximum(n - 1, 0), 0))
    consts = [_resident((1, HD)), _resident((1, HD)), _resident((BLK, GROUP * BLK)),
              _resident((NKV, BLK, GROUP * BLK)), _resident((NKV, GROUP * BLK))]
    return nb, cur, prev, consts


def _attn_fwd(q, kv, qg, kg, sink_rows, n_seq, S):
    T = n_seq * S
    nb, cur, prev, consts = _attn_specs(n_seq, S)
    tri, bias = _attn_consts()

    def body(q_ref, kvc_ref, kvp_ref, qg_ref, kg_ref, tri_ref, bias_ref, sink_ref, o_ref):
        n = pl.program_id(1)
        for kh in range(NKV):
            p, _, from_prev, _, _, _, vpb, vcb, _, _ = _attn_probs(kh, n, q_ref, kvc_ref, kvp_ref, qg_ref, kg_ref,
                                                                   tri_ref, bias_ref, sink_ref)
            pp, pc = _unfold(p.astype(BF16), from_prev)
            o = (_dot_tn(pp, vpb) + _dot_tn(pc, vcb)).astype(BF16)
            for g in range(GROUP):
                h = kh * GROUP + g
                o_ref[:, h * HD:(h + 1) * HD] = o[g * BLK:(g + 1) * BLK]

    return pl.pallas_call(
        body, name="attn_fwd", grid=(n_seq, nb),
        in_specs=[cur(D), cur(256), prev(256)] + consts,
        out_specs=cur(D),
        out_shape=jax.ShapeDtypeStruct((T, D), BF16),
        compiler_params=_params(("parallel", "parallel")),
    )(q, kv, kv, qg, kg, tri, bias, sink_rows)


def _merge_fwd(x2, gg, yc, o, wa, wm):
    T = x2.shape[0]
    tm = min(512, T)

    def body(x_ref, gg_ref, yc_ref, o_ref, wa_ref, wm_ref, ya_ref, mix_ref, h_ref):
        ya = _dot(o_ref[...], wa_ref[...])
        mix = (_sig(gg_ref[:, 0:D]) * yc_ref[...] + _sig(gg_ref[:, D:2 * D]) * ya).astype(BF16)
        ya_ref[...] = ya
        mix_ref[...] = mix
        h_ref[...] = x_ref[...] + _dot(mix, wm_ref[...])

    return pl.pallas_call(
        body, name="merge_fwd", grid=(T // tm,),
        in_specs=[_rows(tm, D), _rows(tm, 2 * D), _rows(tm, D), _rows(tm, D), _resident((D, D)), _resident((D, D))],
        out_specs=[_rows(tm, D), _rows(tm, D), _rows(tm, D)],
        out_shape=[jax.ShapeDtypeStruct((T, D), F32), jax.ShapeDtypeStruct((T, D), BF16),
                   jax.ShapeDtypeStruct((T, D), F32)],
        compiler_params=_params(("parallel",)),
    )(x2, gg, yc, o, wa, wm)


FF_CHUNK = DFF // 2


def _ffn(h, tgt, g2, wf, wd):
    T = h.shape[0]
    tm = min(256, T)

    def body(h_ref, t_ref, g_ref, wf_ref, wd_ref, dh_ref, dhb_ref, hn_ref, act_ref, dout_ref, dgu_ref, st_ref,
             gsc, usc):
        @pl.when(pl.program_id(0) == 0)
        def _():
            st_ref[...] = jnp.zeros((8, D), F32)

        hh = h_ref[...]
        rstd = lax.rsqrt(jnp.mean(hh * hh, axis=-1, keepdims=True) + EPS)
        hhat = hh * rstd
        hn = (hhat * g_ref[...]).astype(BF16)
        hn_ref[...] = hn
        out = hh
        for c in range(DFF // FF_CHUNK):
            cs = slice(c * FF_CHUNK, (c + 1) * FF_CHUNK)
            us = slice(DFF + c * FF_CHUNK, DFF + (c + 1) * FF_CHUNK)
            g = _dot(hn, wf_ref[c])
            u = _dot(hn, wf_ref[2 + c])
            gsc[:, cs] = g
            usc[:, cs] = u
            act = (g * _sig(g) * u).astype(BF16)
            act_ref[:, cs] = act
            out = out + _dot(act, wd_ref[cs, :])
        err = out - t_ref[...]
        dout = err * (1.0 / D)
        doutb = dout.astype(BF16)
        dout_ref[...] = doutb
        dhn = jnp.zeros((tm, D), F32)
        for c in range(DFF // FF_CHUNK):
            cs = slice(c * FF_CHUNK, (c + 1) * FF_CHUNK)
            us = slice(DFF + c * FF_CHUNK, DFF + (c + 1) * FF_CHUNK)
            g = gsc[:, cs]
            u = usc[:, cs]
            dact = _dot_nt(doutb, wd_ref[cs, :])
            sg = _sig(g)
            dg = (dact * u * (sg * (1.0 + g * (1.0 - sg)))).astype(BF16)
            du = (dact * (g * sg)).astype(BF16)
            dgu_ref[:, cs] = dg
            dgu_ref[:, us] = du
            dhn = dhn + _dot_nt(dg, wf_ref[c]) + _dot_nt(du, wf_ref[2 + c])
        st_ref[0:1, :] += jnp.sum(dhn * hhat, axis=0, keepdims=True)
        st_ref[1:2, :] += jnp.sum(err * err, axis=0, keepdims=True)
        dhh = dhn * g_ref[...]
        dh = dout + rstd * (dhh - hhat * jnp.mean(dhh * hhat, axis=-1, keepdims=True))
        dh_ref[...] = dh
        dhb_ref[...] = dh.astype(BF16)

    return pl.pallas_call(
        body, name="ffn_fwd_bwd", grid=(T // tm,),
        in_specs=[_rows(tm, D), _rows(tm, D), _resident((1, D)), _resident((4, D, FF_CHUNK)), _resident((DFF, D))],
        out_specs=[_rows(tm, D), _rows(tm, D), _rows(tm, D), _rows(tm, DFF), _rows(tm, D), _rows(tm, 2 * DFF),
                   _whole((8, D))],
        out_shape=[jax.ShapeDtypeStruct((T, D), F32), jax.ShapeDtypeStruct((T, D), BF16),
                   jax.ShapeDtypeStruct((T, D), BF16), jax.ShapeDtypeStruct((T, DFF), BF16),
                   jax.ShapeDtypeStruct((T, D), BF16), jax.ShapeDtypeStruct((T, 2 * DFF), BF16),
                   jax.ShapeDtypeStruct((8, D), F32)],
        scratch_shapes=[pltpu.VMEM((tm, DFF), F32), pltpu.VMEM((tm, DFF), F32)],
        compiler_params=_params(("arbitrary",), 56),
    )(h, tgt, g2, wf, wd)


def _merge_bwd(dhb, gg, yc, ya, wm, wa, wc):
    T = dhb.shape[0]
    tm = min(256, T)

    def body(dh_ref, gg_ref, yc_ref, ya_ref, wm_ref, wa_ref, wc_ref, dgg_ref, dyc_ref, dya_ref, do_ref, dh3_ref):
        dmix = _dot_nt(dh_ref[...], wm_ref[...])
        gc = _sig(gg_ref[:, 0:D])
        ga = _sig(gg_ref[:, D:2 * D])
        yc = yc_ref[...]
        ya = ya_ref[...]
        dgg_ref[:, 0:D] = (dmix * yc * gc * (1.0 - gc)).astype(BF16)
        dgg_ref[:, D:2 * D] = (dmix * ya * ga * (1.0 - ga)).astype(BF16)
        dyc = (dmix * gc).astype(BF16)
        dya = (dmix * ga).astype(BF16)
        dyc_ref[...] = dyc
        dya_ref[...] = dya
        do_ref[...] = _dot_nt(dya, wa_ref[...])
        dh3_ref[...] = _dot_nt(dyc, wc_ref[...])

    return pl.pallas_call(
        body, name="merge_bwd", grid=(T // tm,),
        in_specs=[_rows(tm, D), _rows(tm, 2 * D), _rows(tm, D), _rows(tm, D), _resident((D, D)), _resident((D, D)),
                  _resident((D, D))],
        out_specs=[_rows(tm, 2 * D), _rows(tm, D), _rows(tm, D), _rows(tm, D), _rows(tm, D)],
        out_shape=[jax.ShapeDtypeStruct((T, 2 * D), BF16), jax.ShapeDtypeStruct((T, D), BF16),
                   jax.ShapeDtypeStruct((T, D), BF16), jax.ShapeDtypeStruct((T, D), F32),
                   jax.ShapeDtypeStruct((T, D), F32)],
        compiler_params=_params(("parallel",)),
    )(dhb, gg, yc, ya, wm, wa, wc)


def _attn_bwd(q, kv, do, qg, kg, sink_rows, n_seq, S):
    T = n_seq * S
    nb, cur, prev, consts = _attn_specs(n_seq, S)
    tri, bias = _attn_consts()

    def body(q_ref, kvc_ref, kvp_ref, do_ref, qg_ref, kg_ref, tri_ref, bias_ref, sink_ref, dq_ref, dkc_ref, dkp_ref,
             dqg_ref, dsk_ref):
        n = pl.program_id(1)

        @pl.when((pl.program_id(0) == 0) & (n == 0))
        def _():
            dqg_ref[...] = jnp.zeros((1, HD), F32)
            dsk_ref[...] = jnp.zeros((8, 128), F32)

        lane = lax.broadcasted_iota(jnp.int32, (1, 128), 1)
        for kh in range(NKV):
            p, ps, from_prev, qhb, kpb, kcb, vpb, vcb, qy, rq = _attn_probs(
                kh, n, q_ref, kvc_ref, kvp_ref, qg_ref, kg_ref, tri_ref, bias_ref, sink_ref)
            dob = _stack_heads(do_ref, kh).astype(BF16)
            dp = jnp.where(from_prev, _dot_nt(vpb, dob), _dot_nt(vcb, dob))
            delta = jnp.sum(p * dp, axis=0, keepdims=True)
            dsp, dsc = _unfold((p * (dp - delta)).astype(BF16), from_prev)
            pp, pc = _unfold(p.astype(BF16), from_prev)
            dsink = -ps * delta
            dqh = (_dot_tn(dsp, kpb) + _dot_tn(dsc, kcb)) * SCALE
            dqg_ref[...] += jnp.sum(dqh * qy, axis=0, keepdims=True)
            dy = dqh * qg_ref[...]
            dq = (rq * (dy - qy * jnp.mean(dy * qy, axis=-1, keepdims=True))).astype(BF16)
            row = jnp.zeros((1, 128), F32)
            for g in range(GROUP):
                h = kh * GROUP + g
                dq_ref[:, h * HD:(h + 1) * HD] = dq[g * BLK:(g + 1) * BLK]
                row = row + jnp.where(lane == h, jnp.sum(dsink[:, g * BLK:(g + 1) * BLK], axis=1, keepdims=True), 0.0)
            dsk_ref[0:1, :] += row
            ks = slice(kh * HD, (kh + 1) * HD)
            vs = slice(2 * HD + kh * HD, 2 * HD + (kh + 1) * HD)
            dkp_ref[:, ks] = _dot(dsp, qhb)
            dkc_ref[:, ks] = _dot(dsc, qhb)
            dkp_ref[:, vs] = _dot(pp, dob)
            dkc_ref[:, vs] = _dot(pc, dob)

    return pl.pallas_call(
        body, name="attn_bwd", grid=(n_seq, nb),
        in_specs=[cur(D), cur(256), prev(256), cur(D)] + consts,
        out_specs=[cur(D), cur(256), cur(256), _whole((1, HD)), _whole((8, 128))],
        out_shape=[jax.ShapeDtypeStruct((T, D), BF16), jax.ShapeDtypeStruct((T, 256), F32),
                   jax.ShapeDtypeStruct((T, 256), F32), jax.ShapeDtypeStruct((1, HD), F32),
                   jax.ShapeDtypeStruct((8, 128), F32)],
        compiler_params=_params(("arbitrary", "arbitrary")),
    )(q, kv, kv, do, qg, kg, tri, bias, sink_rows)


def _kv_bwd(kv, dkc, dkp, kg, n_seq, S):
    T = n_seq * S
    nb, cur, _, _ = _attn_specs(n_seq, S)
    nxt = pl.BlockSpec((BLK, 256), lambda b, n: (b * nb + jnp.minimum(n + 1, nb - 1), 0))

    def body(kv_ref, dkc_ref, dkp_ref, kg_ref, dkv_ref, dkg_ref):
        n = pl.program_id(1)

        @pl.when((pl.program_id(0) == 0) & (n == 0))
        def _():
            dkg_ref[...] = jnp.zeros((1, HD), F32)

        has_next = jnp.where(n < nb - 1, 1.0, 0.0)
        d = dkc_ref[...] + has_next * dkp_ref[...]
        for kh in range(NKV):
            ks = slice(kh * HD, (kh + 1) * HD)
            k = kv_ref[:, ks]
            r = _rms64(k)
            y = k * r
            dkh = d[:, ks]
            dkg_ref[...] += jnp.sum(dkh * y, axis=0, keepdims=True)
            dy = dkh * kg_ref[...]
            dkv_ref[:, ks] = (r * (dy - y * jnp.mean(dy * y, axis=-1, keepdims=True))).astype(BF16)
        dkv_ref[:, 2 * HD:4 * HD] = d[:, 2 * HD:4 * HD].astype(BF16)

    return pl.pallas_call(
        body, name="kv_bwd", grid=(n_seq, nb),
        in_specs=[cur(256), cur(256), nxt, _resident((1, HD))],
        out_specs=[cur(256), _whole((1, HD))],
        out_shape=[jax.ShapeDtypeStruct((T, 256), BF16), jax.ShapeDtypeStruct((1, HD), F32)],
        compiler_params=_params(("arbitrary", "arbitrary")),
    )(kv, dkc, dkp, kg)


def _conv_bwd(h1, dh3, ag, dw, lng, lnb, n_seq, S):
    T = n_seq * S
    tc = min(256, S)
    nt = S // tc

    def body(h1_ref, dh3_ref, a_ref, gt_ref, dw_ref, lng_ref, lnb_ref, dag_ref, ddw_ref, st_ref, extd, acc8, shd):
        i = pl.program_id(1)

        @pl.when((pl.program_id(0) == 0) & (i == 0))
        def _():
            acc8[...] = jnp.zeros((CW * 8, D), F32)
            st_ref[...] = jnp.zeros((8, D), F32)

        @pl.when(i == 0)
        def _():
            extd[tc:tc + 32, :] = jnp.zeros((32, D), F32)

        h1 = h1_ref[...]
        mu = jnp.mean(h1, axis=-1, keepdims=True)
        cen = h1 - mu
        rstd = lax.rsqrt(jnp.mean(cen * cen, axis=-1, keepdims=True) + EPS)
        xh = cen * rstd
        h2 = xh * lng_ref[...] + lnb_ref[...]
        sg = _sig(h2)
        dh2 = dh3_ref[...] * (sg * (1.0 + h2 * (1.0 - sg)))
        st_ref[1:2, :] += jnp.sum(dh2 * xh, axis=0, keepdims=True)
        st_ref[2:3, :] += jnp.sum(dh2, axis=0, keepdims=True)
        dxh = dh2 * lng_ref[...]
        dh1 = rstd * (dxh - jnp.mean(dxh, axis=-1, keepdims=True)
                      - xh * jnp.mean(dxh * xh, axis=-1, keepdims=True))
        st_ref[0:1, :] += jnp.sum(dh1, axis=0, keepdims=True)
        extd[0:tc, :] = dh1
        _shift_copies(shd, extd, 0)
        for cb in range(D // 128):
            cs = slice(cb * 128, (cb + 1) * 128)
            for rb in range(tc // 128):
                rs = slice(rb * 128, (rb + 1) * 128)
                a = a_ref[rs, cs]
                sgt = _sig(gt_ref[rs, cs])
                h0 = a * sgt
                acc = jnp.zeros((128, 128), F32)
                for phase, offs in _tap_phases():
                    for m, o in enumerate(offs):
                        j = CW - 1 - o
                        ahead = shd[phase, rb * 128 + 8 * m:rb * 128 + 8 * m + 128, cs]
                        acc = acc + dw_ref[j:j + 1, cs] * ahead
                        acc8[j * 8:(j + 1) * 8, cs] += jnp.sum((h0 * ahead).reshape(16, 8, 128), axis=0)
                dag_ref[rs, cs] = (acc * sgt).astype(BF16)
                dag_ref[rs, cb * 128 + D:(cb + 1) * 128 + D] = (acc * a * sgt * (1.0 - sgt)).astype(BF16)
        extd[tc:tc + 32, :] = extd[0:32, :]

        @pl.when((pl.program_id(0) == n_seq - 1) & (i == nt - 1))
        def _():
            for j in range(CW):
                ddw_ref[j:j + 1, :] = jnp.sum(acc8[j * 8:(j + 1) * 8, :], axis=0, keepdims=True)
            ddw_ref[CW:32, :] = jnp.zeros((32 - CW, D), F32)

    tile = lambda col: pl.BlockSpec((tc, D), lambda b, i: (b * nt + (nt - 1 - i), col))
    return pl.pallas_call(
        body, name="conv_bwd", grid=(n_seq, nt),
        in_specs=[tile(0), tile(0), tile(0), tile(1), _resident((32, D)), _resident((1, D)), _resident((1, D))],
        out_specs=[pl.BlockSpec((tc, 2 * D), lambda b, i: (b * nt + (nt - 1 - i), 0)), _whole((32, D)),
                   _whole((8, D))],
        out_shape=[jax.ShapeDtypeStruct((T, 2 * D), BF16), jax.ShapeDtypeStruct((32, D), F32),
                   jax.ShapeDtypeStruct((8, D), F32)],
        scratch_shapes=[pltpu.VMEM((tc + 32, D), F32), pltpu.VMEM((CW * 8, D), F32),
                        pltpu.VMEM((8, tc + 24, D), F32)],
        compiler_params=_params(("arbitrary", "arbitrary")),
    )(h1, dh3, ag, ag, dw, lng, lnb)


def _in_proj_bwd(dag, dq, dkv, dgg, dh, x2, g1, w_in):
    T = x2.shape[0]
    tm = min(512, T)

    def body(dag_ref, dq_ref, dkv_ref, dgg_ref, dh_ref, x_ref, g_ref, w_ref, dx_ref, dg_ref):
        @pl.when(pl.program_id(0) == 0)
        def _():
            dg_ref[...] = jnp.zeros((1, D), F32)

        dxn = (_dot(dag_ref[...], w_ref[0:2048, :]) + _dot(dq_ref[...], w_ref[2048:3072, :])
               + _dot(dkv_ref[...], w_ref[3072:3328, :]) + _dot(dgg_ref[...], w_ref[3328:5376, :]))
        x = x_ref[...]
        rstd = lax.rsqrt(jnp.mean(x * x, axis=-1, keepdims=True) + EPS)
        xh = x * rstd
        dg_ref[...] += jnp.sum(dxn * xh, axis=0, keepdims=True)
        dxh = dxn * g_ref[...]
        dx_ref[...] = dh_ref[...] + rstd * (dxh - xh * jnp.mean(dxh * xh, axis=-1, keepdims=True))

    return pl.pallas_call(
        body, name="in_proj_bwd", grid=(T // tm,),
        in_specs=[_rows(tm, 2 * D), _rows(tm, D), _rows(tm, 256), _rows(tm, 2 * D), _rows(tm, D), _rows(tm, D),
                  _resident((1, D)), _resident((IN_COLS, D))],
        out_specs=[_rows(tm, D), _whole((1, D))],
        out_shape=[jax.ShapeDtypeStruct((T, D), F32), jax.ShapeDtypeStruct((1, D), F32)],
        compiler_params=_params(("arbitrary",)),
    )(dag, dq, dkv, dgg, dh, x2, g1, w_in)


def _tn_matmul(a, b, name, column_blocks=False):
    T, K = a.shape
    N = b.shape[1]
    tk = K if K <= 1024 else K // 2
    tn = N if N <= 1024 else (1024 if N % 1024 == 0 and not column_blocks else N // 4)
    tt = min(2048, T)
    assert K % tk == 0 and N % tn == 0 and T % tt == 0 and tk % 128 == 0 and tn % 128 == 0

    def body(a_ref, b_ref, o_ref):
        @pl.when(pl.program_id(2) == 0)
        def _():
            o_ref[...] = jnp.zeros((tk, tn), F32)

        o_ref[...] += _dot_tn(a_ref[...], b_ref[...])

    return pl.pallas_call(
        body, name=name, grid=(K // tk, N // tn, T // tt),
        in_specs=[pl.BlockSpec((tt, tk), lambda i, j, t: (t, i)), pl.BlockSpec((tt, tn), lambda i, j, t: (t, j))],
        out_specs=(pl.BlockSpec((None, tk, tn), lambda i, j, t: (j, i, 0)) if column_blocks
                   else pl.BlockSpec((tk, tn), lambda i, j, t: (i, j))),
        out_shape=jax.ShapeDtypeStruct((N // tn, K, tn) if column_blocks else (K, N), F32),
        compiler_params=_params(("parallel", "parallel", "arbitrary")),
    )(a, b)


def _place():
    x, y, c = lax.axis_index("x"), lax.axis_index("y"), lax.axis_index("c")
    chips = [(1 - x, y), (x, 1 - y), (1 - x, 1 - y)]
    return x, y, c, chips


def _own_slot(slots, mine):
    chip = 2 * lax.axis_index("x") + lax.axis_index("y")
    return lax.dynamic_update_slice(slots, mine[None], (chip,) + (0,) * mine.ndim)


def _row_tile(rows, unit):
    return max(t for t in range(unit, 513, unit) if rows % t == 0)


def _gather_weights(pack):
    rows = pack.shape[0]
    half = rows // 2

    def body(src, dst, token, send_sems, recv_sems):
        x, y, c, chips = _place()

        def piece(px, py, pc):
            return dst.at[2 * px + py, pl.ds(pc * half, half), :]

        def copy(k, block, to, from_src=False):
            return pltpu.make_async_remote_copy(
                src_ref=src.at[pl.ds(c * half, half), :] if from_src else piece(*block), dst_ref=piece(*block),
                send_sem=send_sems.at[k], recv_sem=recv_sems.at[k], device_id=to, device_id_type=MESH)

        first = [copy(k, (x, y, c), (*chip, c), from_src=True) for k, chip in enumerate(chips)]
        for cp in first:
            cp.start()
        passed = [copy(3 + k, (*chip, c), (x, y, 1 - c)) for k, chip in enumerate(chips)]
        for k, chip in enumerate(chips):
            copy(k, (*chip, c), (x, y, c)).wait_recv()
            passed[k].start()
        for k, chip in enumerate(chips):
            copy(3 + k, (*chip, 1 - c), (x, y, c)).wait_recv()
        for cp in first + passed:
            cp.wait_send()
        token[...] = jnp.zeros((8, 128), F32)

    got, token = pl.pallas_call(
        body, name="gather_weights",
        in_specs=[pl.BlockSpec(memory_space=pl.ANY)],
        out_specs=[pl.BlockSpec(memory_space=pl.ANY), pl.BlockSpec(memory_space=pltpu.VMEM)],
        out_shape=[jax.ShapeDtypeStruct((4, rows, D), pack.dtype), jax.ShapeDtypeStruct((8, 128), F32)],
        scratch_shapes=[pltpu.SemaphoreType.DMA((6,)), pltpu.SemaphoreType.DMA((6,))],
        compiler_params=pltpu.CompilerParams(has_side_effects=True),
    )(pack)
    return _own_slot(got, pack), token[0, 0]


def _add_halves(g, got, c_idx, name="grad_add_halves"):
    rows, w = g.shape[1], g.shape[2]
    half = rows // 2
    tr = _row_tile(half, 16)
    nt = half // tr

    def body(c_ref, g_ref, r_ref, o_ref):
        o_ref[...] = (g_ref[...] + r_ref[...]).astype(BF16)

    return pl.pallas_call(
        body, name=name,
        grid_spec=pltpu.PrefetchScalarGridSpec(
            num_scalar_prefetch=1, grid=(4, nt),
            in_specs=[pl.BlockSpec((1, tr, w), lambda q, i, c_ref: (q, c_ref[0] * nt + i, 0)),
                      pl.BlockSpec((1, tr, w), lambda q, i, c_ref: (q, i, 0))],
            out_specs=pl.BlockSpec((1, tr, w), lambda q, i, c_ref: (q, i, 0))),
        out_shape=jax.ShapeDtypeStruct((4, half, w), BF16),
        compiler_params=_params(("parallel", "parallel")),
    )(c_idx, g, got)


def _own_piece(p):
    chip = 2 * lax.axis_index("x") + lax.axis_index("y")
    return lax.dynamic_index_in_dim(p, chip, axis=0, keepdims=False)


def _sum_chips(r, c_idx, name="grad_sum_chips"):
    half, w = r.shape[1], r.shape[2]
    tr = _row_tile(half, 16)
    nt = half // tr

    def body(c_ref, r_ref, o_ref):
        acc = r_ref[0].astype(F32)
        for q in range(1, 4):
            acc = acc + r_ref[q].astype(F32)
        o_ref[...] = acc

    return pl.pallas_call(
        body, name=name,
        grid_spec=pltpu.PrefetchScalarGridSpec(
            num_scalar_prefetch=1, grid=(nt,),
            in_specs=[pl.BlockSpec((4, tr, w), lambda i, c_ref: (0, i, 0))],
            out_specs=pl.BlockSpec((tr, w), lambda i, c_ref: (c_ref[0] * nt + i, 0))),
        out_shape=jax.ShapeDtypeStruct((2 * half, w), F32),
        compiler_params=_params(("parallel",)),
    )(c_idx, r)


def _join_halves(f):
    half = f.shape[0] // 2

    def body(src, dst, send_sem, recv_sem):
        x, y, c, _ = _place()
        cp = pltpu.make_async_remote_copy(
            src_ref=src.at[pl.ds(c * half, half), :], dst_ref=dst.at[pl.ds(c * half, half), :], send_sem=send_sem,
            recv_sem=recv_sem, device_id=(x, y, 1 - c), device_id_type=MESH)
        cp.start()
        pltpu.make_async_remote_copy(
            src_ref=src.at[pl.ds(c * half, half), :], dst_ref=dst.at[pl.ds((1 - c) * half, half), :],
            send_sem=send_sem, recv_sem=recv_sem, device_id=(x, y, 1 - c), device_id_type=MESH).wait_recv()
        cp.wait_send()

    return pl.pallas_call(
        body, name="grad_join_halves",
        in_specs=[pl.BlockSpec(memory_space=pl.ANY)], out_specs=pl.BlockSpec(memory_space=pl.ANY),
        out_shape=jax.ShapeDtypeStruct(f.shape, f.dtype), input_output_aliases={0: 0},
        scratch_shapes=[pltpu.SemaphoreType.DMA, pltpu.SemaphoreType.DMA],
        compiler_params=pltpu.CompilerParams(has_side_effects=True),
    )(f)


_HBM = pl.BlockSpec(memory_space=pltpu.HBM)
_SEM = pl.BlockSpec(memory_space=pltpu.SEMAPHORE)
_EFFECT = pltpu.SideEffectType.DATAFLOW_SIDE_EFFECTING


def _start_copies(name, bufs, n_sems, plan):
    nb = len(bufs)

    def body(*refs):
        for cp in plan(refs[:nb], refs[nb], refs[nb + 1])[0]:
            cp.start()
        refs[-1][...] = jnp.zeros((8, 128), F32)

    out = pl.pallas_call(
        body, name=name,
        out_shape=(pltpu.SemaphoreType.DMA((n_sems,)), pltpu.SemaphoreType.DMA((n_sems,)),
                   *[pltpu.HBM(b.shape, b.dtype) for b in bufs], jax.ShapeDtypeStruct((8, 128), F32)),
        in_specs=[_HBM] * nb, out_specs=(_SEM, _SEM, *[_HBM] * nb, pl.BlockSpec(memory_space=pltpu.VMEM)),
        input_output_aliases={i: 2 + i for i in range(nb)},
        compiler_params=pltpu.CompilerParams(has_side_effects=_EFFECT),
    )(*[pltpu.with_memory_space_constraint(b, pltpu.HBM) for b in bufs])
    return out[0], out[1], list(out[2:2 + nb]), out[-1]


def _wait_copies(name, send_sems, recv_sems, bufs, after, plan):
    nb = len(bufs)

    def body(*refs):
        _, sends, recvs = plan(refs[:nb], refs[nb], refs[nb + 1])
        for cp in sends:
            cp.wait_send()
        for cp in recvs:
            cp.wait_recv()

    out = pl.pallas_call(
        body, name=name,
        out_shape=tuple(pltpu.HBM(b.shape, b.dtype) for b in bufs),
        in_specs=[_HBM] * nb + [_SEM, _SEM] + [pl.BlockSpec(memory_space=pl.ANY)] * len(after),
        out_specs=tuple([_HBM] * nb),
        input_output_aliases={i: i for i in range(nb)},
        compiler_params=pltpu.CompilerParams(has_side_effects=_EFFECT),
    )(*bufs, send_sems, recv_sems, *after)
    return list(out)


def _plan_gather_direct(halves):
    n = len(halves)

    def plan(refs, send_sems, recv_sems):
        x, y, c, chips = _place()
        starts, recvs = [], []
        for b, half in enumerate(halves):
            src, land = refs[b], refs[n + b]
            for k, (cx, cy) in enumerate(chips):
                for d in range(2):
                    other = c if d == 0 else 1 - c
                    i = 6 * b + 2 * k + d
                    starts.append(pltpu.make_async_remote_copy(
                        src_ref=src.at[pl.ds(c * half, half), :],
                        dst_ref=land.at[2 * x + y, pl.ds(c * half, half), :],
                        send_sem=send_sems.at[i], recv_sem=recv_sems.at[i], device_id=(cx, cy, other),
                        device_id_type=MESH))
                    recvs.append(pltpu.make_async_remote_copy(
                        src_ref=src.at[pl.ds(c * half, half), :],
                        dst_ref=land.at[2 * cx + cy, pl.ds(other * half, half), :],
                        send_sem=send_sems.at[i], recv_sem=recv_sems.at[i], device_id=(cx, cy, other),
                        device_id_type=MESH))
        return starts, starts, recvs
    return plan


def _plan_swap_halves(halves):
    n = len(halves)

    def plan(refs, send_sems, recv_sems):
        x, y, c, _ = _place()
        cps = [pltpu.make_async_remote_copy(
            src_ref=refs[b].at[:, pl.ds((1 - c) * half, half), :], dst_ref=refs[n + b], send_sem=send_sems.at[b],
            recv_sem=recv_sems.at[b], device_id=(x, y, 1 - c), device_id_type=MESH)
            for b, half in enumerate(halves)]
        return cps, cps, cps
    return plan


def _plan_scatter_chips(n):
    def plan(refs, send_sems, recv_sems):
        x, y, c, chips = _place()
        me = 2 * x + y
        starts, recvs = [], []
        for b in range(n):
            src, land = refs[b], refs[n + b]
            for k, (cx, cy) in enumerate(chips):
                i = 3 * b + k
                starts.append(pltpu.make_async_remote_copy(
                    src_ref=src.at[2 * cx + cy], dst_ref=land.at[me], send_sem=send_sems.at[i],
                    recv_sem=recv_sems.at[i], device_id=(cx, cy, c), device_id_type=MESH))
                recvs.append(pltpu.make_async_remote_copy(
                    src_ref=src.at[me], dst_ref=land.at[2 * cx + cy], send_sem=send_sems.at[i],
                    recv_sem=recv_sems.at[i], device_id=(cx, cy, c), device_id_type=MESH))
        return starts, starts, recvs
    return plan


def _plan_join_halves(halves):
    def plan(refs, send_sems, recv_sems):
        x, y, c, _ = _place()
        starts, recvs = [], []
        for b, half in enumerate(halves):
            mine, theirs = refs[b].at[pl.ds(c * half, half), :], refs[b].at[pl.ds((1 - c) * half, half), :]
            starts.append(pltpu.make_async_remote_copy(
                src_ref=mine, dst_ref=mine, send_sem=send_sems.at[b], recv_sem=recv_sems.at[b],
                device_id=(x, y, 1 - c), device_id_type=MESH))
            recvs.append(pltpu.make_async_remote_copy(
                src_ref=mine, dst_ref=theirs, send_sem=send_sems.at[b], recv_sem=recv_sems.at[b],
                device_id=(x, y, 1 - c), device_id_type=MESH))
        return starts, starts, recvs
    return plan


def _allreduce_small(vec):
    def body(v_ref, o_ref, gath, send_sems, recv_sems):
        x, y, c, _ = _place()
        me = 4 * x + 2 * y + c
        gath[me] = v_ref[...]
        sends = []
        for k in range(1, 8):
            peer = (x ^ (k >> 2), y ^ ((k >> 1) & 1), c ^ (k & 1))
            sends.append(pltpu.make_async_remote_copy(
                src_ref=v_ref, dst_ref=gath.at[me], send_sem=send_sems.at[k - 1], recv_sem=recv_sems.at[k - 1],
                device_id=peer, device_id_type=MESH))
        for cp in sends:
            cp.start()
        for k in range(1, 8):
            peer = (x ^ (k >> 2), y ^ ((k >> 1) & 1), c ^ (k & 1))
            pltpu.make_async_remote_copy(
                src_ref=v_ref, dst_ref=gath.at[4 * peer[0] + 2 * peer[1] + peer[2]], send_sem=send_sems.at[k - 1],
                recv_sem=recv_sems.at[k - 1], device_id=peer, device_id_type=MESH).wait_recv()
        for cp in sends:
            cp.wait_send()
        acc = gath[0]
        for d in range(1, 8):
            acc = acc + gath[d]
        o_ref[...] = acc

    return pl.pallas_call(
        body, name="allreduce_small",
        in_specs=[pl.BlockSpec(memory_space=pltpu.VMEM)], out_specs=pl.BlockSpec(memory_space=pltpu.VMEM),
        out_shape=jax.ShapeDtypeStruct(vec.shape, F32),
        scratch_shapes=[pltpu.VMEM((8,) + vec.shape, F32), pltpu.SemaphoreType.DMA((7,)),
                        pltpu.SemaphoreType.DMA((7,))],
    )(vec)


def _adamw(w, g, m, v, name, after):
    shape = w.shape
    if w.ndim == 1 or w.size <= 128 * 128:
        two_d = (1, w.size) if w.size % 128 else (w.size // 128, 128)
    else:
        two_d = (w.shape[0], w.size // w.shape[0])
    rows, cols = two_d
    tr = _row_tile(rows, 8) if rows % 8 == 0 and rows > 512 else rows

    def body(w_ref, g_ref, m_ref, v_ref, after_ref, d_ref, nm_ref, nv_ref):
        gr = g_ref[...]
        nm = B1 * m_ref[...] + (1.0 - B1) * gr
        nv = B2 * v_ref[...] + (1.0 - B2) * (gr * gr)
        m_hat = nm / (1.0 - B1 ** STEP)
        v_hat = nv / (1.0 - B2 ** STEP)
        d_ref[...] = -LR * (m_hat / (jnp.sqrt(v_hat) + AEPS) + WD * w_ref[...])
        nm_ref[...] = nm
        nv_ref[...] = nv

    spec = pl.BlockSpec((tr, cols), lambda i: (i, 0))
    outs = pl.pallas_call(
        body, name=name, grid=(rows // tr,),
        in_specs=[spec] * 4 + [pl.BlockSpec(memory_space=pl.ANY)], out_specs=[spec] * 3,
        out_shape=[jax.ShapeDtypeStruct(two_d, F32)] * 3,
        compiler_params=_params(("parallel",)),
    )(*[t.reshape(two_d) for t in (w, g, m, v)], after)
    return [o.reshape(shape) for o in outs]


def _rows_stacked(g, lo, n_rows):
    return g[:, lo:lo + n_rows].reshape(4 * n_rows, D)


def _rows_to_slots(t):
    return t.reshape(4, t.shape[0] // 4, D)


def _pack_first(w_in, w_conv_out, conv_dw_w):
    dw = jnp.pad(conv_dw_w.reshape(CW, 256), ((0, 1), (0, 0)))
    dw_bits = lax.bitcast_convert_type(dw, BF16).reshape(16, D)
    return jnp.concatenate([w_in.T.astype(BF16), w_conv_out.astype(BF16), dw_bits, jnp.zeros((16, D), BF16)],
                           axis=0)


def _unpack_first(g):
    w_in_t = _rows_stacked(g, 0, ROWS_W_IN)
    wc = _rows_stacked(g, ROWS_W_IN, ROWS_SQ)
    o = ROWS_W_IN + ROWS_SQ
    dw = lax.bitcast_convert_type(g[:, o:o + 16].reshape(4, 32, 256, 2), F32)
    return w_in_t, wc, jnp.transpose(dw, (1, 0, 2)).reshape(32, D)


def _pack_late(w_attn_out, w_merge_out, w_ffn_down):
    return jnp.concatenate([w_attn_out.astype(BF16), w_merge_out.astype(BF16), w_ffn_down.astype(BF16)], axis=0)


def _unpack_late(g):
    return (_rows_stacked(g, 0, ROWS_SQ), _rows_stacked(g, ROWS_SQ, ROWS_SQ),
            _rows_stacked(g, 2 * ROWS_SQ, ROWS_DOWN))


class _Exchanges:
    def __init__(self, late_pack, wf_shard):
        self.c_idx = lax.axis_index("c").astype(jnp.int32).reshape(1)
        packs = [late_pack, wf_shard]
        self.late_plan = _plan_gather_direct([p.shape[0] // 2 for p in packs])
        slots = [lax.empty((4,) + p.shape, BF16) for p in packs]
        self.late = _start_copies("gather_late_start", packs + slots, 6 * len(packs), self.late_plan)
        self.first_token = self.late[3][0, 0]

    def late_weights(self, after):
        send_sems, recv_sems, bufs, _ = self.late
        pack, wf_shard, slots, wf_slots = _wait_copies("gather_late_wait", send_sems, recv_sems, bufs, after,
                                                       self.late_plan)
        wa, wm, wd = _unpack_late(_own_slot(slots, pack))
        return wa, wm, _own_slot(wf_slots, wf_shard), wd

    def reduce_start(self, d_wd, d_wf4, d_wm, d_wa, d_wc):
        gs = [_rows_to_slots(d_wd), jnp.concatenate([_rows_to_slots(t) for t in (d_wm, d_wa, d_wc)], axis=1), d_wf4]
        self.halves = [g.shape[1] // 2 for g in gs]
        self.swap_plan = _plan_swap_halves(self.halves)
        lands = [lax.empty((4, h, g.shape[2]), F32) for g, h in zip(gs, self.halves)]
        self.swap = _start_copies("grad_swap_start", gs + lands, len(gs), self.swap_plan)
        return self.swap[3][0, 0]

    def reduce_mid(self, after):
        send_sems, recv_sems, bufs, _ = self.swap
        bufs = _wait_copies("grad_swap_wait", send_sems, recv_sems, bufs, after, self.swap_plan)
        n = len(self.halves)
        ps = [_add_halves(bufs[b], bufs[n + b], self.c_idx, "grad_add_halves_%d" % b) for b in range(n)]
        self.scatter_plan = _plan_scatter_chips(n)
        self.scatter = _start_copies("grad_scatter_start", ps + [lax.empty(p.shape, BF16) for p in ps], 3 * n,
                                     self.scatter_plan)
        return self.scatter[3][0, 0]

    def reduce_late(self, after):
        send_sems, recv_sems, bufs, _ = self.scatter
        bufs = _wait_copies("grad_scatter_wait", send_sems, recv_sems, bufs, after, self.scatter_plan)
        n = len(self.halves)
        fs = [_sum_chips(_own_slot(bufs[n + b], _own_piece(bufs[b])), self.c_idx, "grad_sum_chips_%d" % b)
              for b in range(n)]
        self.join_plan = _plan_join_halves(self.halves)
        self.join = _start_copies("grad_join_start", fs, n, self.join_plan)

    def reduce_end(self, after):
        send_sems, recv_sems, bufs, _ = self.join
        g_wd, sq, g_wf = _wait_copies("grad_join_wait", send_sems, recv_sems, bufs, after, self.join_plan)
        return g_wd, g_wf, sq[0:ROWS_SQ], sq[ROWS_SQ:2 * ROWS_SQ], sq[2 * ROWS_SQ:3 * ROWS_SQ]

    def w_in_start(self, d_w_in_t):
        g = _rows_to_slots(d_w_in_t)
        self.w_half = g.shape[1] // 2
        self.w_swap_plan = _plan_swap_halves([self.w_half])
        self.w_swap = _start_copies("grad_w_in_swap_start", [g, lax.empty((4, self.w_half, D), F32)], 1,
                                    self.w_swap_plan)
        return self.w_swap[3]

    def w_in_mid(self, after):
        send_sems, recv_sems, bufs, _ = self.w_swap
        g, got = _wait_copies("grad_w_in_swap_wait", send_sems, recv_sems, bufs, after, self.w_swap_plan)
        p = _add_halves(g, got, self.c_idx, "grad_add_halves_w_in")
        self.w_scatter_plan = _plan_scatter_chips(1)
        self.w_scatter = _start_copies("grad_w_in_scatter_start", [p, lax.empty(p.shape, BF16)], 3,
                                       self.w_scatter_plan)
        return self.w_scatter[3]

    def w_in_end(self, after):
        send_sems, recv_sems, bufs, _ = self.w_scatter
        p, got = _wait_copies("grad_w_in_scatter_wait", send_sems, recv_sems, bufs, after, self.w_scatter_plan)
        return _join_halves(_sum_chips(_own_slot(got, _own_piece(p)), self.c_idx, "grad_sum_chips_w_in"))


def _local_grads(x, loss_target, norm_mix_g, conv_dw_b, conv_ln_g, conv_ln_b, q_norm_g, k_norm_g, sinks, norm_ffn_g,
                 w_in, wc, dw, exchanges):
    n_seq, S, _ = x.shape
    T = n_seq * S
    x2 = x.reshape(T, D)
    tgt = loss_target.reshape(T, D)
    row = lambda t: t.reshape(1, -1)
    g1, g2 = row(norm_mix_g), row(norm_ffn_g)
    qg, kg = row(q_norm_g), row(k_norm_g)
    lng, lnb, dwb = row(conv_ln_g), row(conv_ln_b), row(conv_dw_b)
    sink_rows = jnp.repeat(sinks.reshape(NKV, GROUP), BLK, axis=1)

    xn, ag, q, kv, gg, h1, h3, yc = _in_proj_conv_fwd(x2, g1 + exchanges.first_token, w_in, dw, dwb, lng, lnb, wc,
                                                      n_seq, S)
    o = _attn_fwd(q, kv, qg, kg, sink_rows, n_seq, S)
    wa, wm, wf, wd = exchanges.late_weights([o, yc])
    ya, mix, h = _merge_fwd(x2, gg, yc, o, wa, wm)
    dh, dhb, hn, act, dout, dgu, ffn_stats = _ffn(h, tgt, g2, wf, wd)
    d_wd = _tn_matmul(act, dout, "dw_ffn_down")
    d_wf = _tn_matmul(hn, dgu, "dw_ffn_in", column_blocks=True)
    d_wm = _tn_matmul(mix, dhb, "dw_merge")
    dgg, dyc, dya, do, dh3 = _merge_bwd(dhb, gg, yc, ya, wm, wa, wc)
    d_wa = _tn_matmul(o, dya, "dw_attn_out")
    d_wc = _tn_matmul(h3, dyc, "dw_conv_out")
    token = exchanges.reduce_start(d_wd, d_wf, d_wm, d_wa, d_wc)
    dq, dkc, dkp, dqg, dsk = _attn_bwd(q, kv, do, qg + token, kg, sink_rows, n_seq, S)
    token = exchanges.reduce_mid([dq])
    dkv, dkg = _kv_bwd(kv, dkc, dkp, kg, n_seq, S)
    dag, ddw, conv_stats = _conv_bwd(h1, dh3, ag, dw, lng + token, lnb, n_seq, S)
    dx, dg1 = _in_proj_bwd(dag, dq, dkv, dgg, dh, x2, g1, w_in)
    exchanges.reduce_late([dx])
    d_w_in = jnp.concatenate([_tn_matmul(dag, xn, "dw_in_conv"), _tn_matmul(dq, xn, "dw_in_q"),
                              _tn_matmul(dkv, xn, "dw_in_kv"), _tn_matmul(dgg, xn, "dw_in_gates")], axis=0)

    heads = jnp.concatenate([dqg[0], dkg[0], dsk[0, :NQ], jnp.zeros((D - 2 * HD - NQ,), F32)])
    vec = jnp.concatenate([dg1, conv_stats[0:3], ffn_stats[0:1], heads[None], ffn_stats[1:2], jnp.zeros((1, D), F32),
                           ddw], axis=0)
    return ffn_stats[1], dx.reshape(x.shape), d_w_in, vec


def kernel(x, norm_mix_g, w_in, conv_dw_w, conv_dw_b, conv_ln_g, conv_ln_b, w_conv_out, q_norm_g, k_norm_g, sinks, w_attn_out, w_merge_out, norm_ffn_g, w_ffn_in, w_ffn_down, loss_target, m_norm_mix_g, m_w_in, m_conv_dw_w, m_conv_dw_b, m_conv_ln_g, m_conv_ln_b, m_w_conv_out, m_q_norm_g, m_k_norm_g, m_sinks, m_w_attn_out, m_w_merge_out, m_norm_ffn_g, m_w_ffn_in, m_w_ffn_down, v_norm_mix_g, v_w_in, v_conv_dw_w, v_conv_dw_b, v_conv_ln_g, v_conv_ln_b, v_w_conv_out, v_q_norm_g, v_k_norm_g, v_sinks, v_w_attn_out, v_w_merge_out, v_norm_ffn_g, v_w_ffn_in, v_w_ffn_down):
    chip = 2 * lax.axis_index("x") + lax.axis_index("y")

    first, token = _gather_weights(_pack_first(w_in, w_conv_out, conv_dw_w))
    exchanges = _Exchanges(_pack_late(w_attn_out, w_merge_out, w_ffn_down) + token.astype(BF16),
                           w_ffn_in.astype(BF16) + token.astype(BF16))
    _, grad_x, d_w_in, vec = _local_grads(x, loss_target, norm_mix_g, conv_dw_b, conv_ln_g, conv_ln_b, q_norm_g,
                                          k_norm_g, sinks, norm_ffn_g, *_unpack_first(first), exchanges)

    g_wd, g_wf, g_wm, g_wa, g_wc = exchanges.reduce_end([d_w_in])
    small = _allreduce_small(vec)
    loss = 0.5 / D * jnp.sum(small[6])
    g_dw = lax.dynamic_slice_in_dim(small[8:8 + CW], chip * 256, 256, axis=1).reshape(CW, 1, 256)
    grads = {
        "norm_mix_g": small[0], "conv_dw_w": g_dw, "conv_dw_b": small[1], "conv_ln_g": small[2],
        "conv_ln_b": small[3], "w_conv_out": g_wc, "q_norm_g": small[5, 0:HD], "k_norm_g": small[5, HD:2 * HD],
        "sinks": small[5, 2 * HD:2 * HD + NQ], "w_attn_out": g_wa, "w_merge_out": g_wm, "norm_ffn_g": small[4],
        "w_ffn_in": g_wf, "w_ffn_down": g_wd,
    }
    weights = dict(norm_mix_g=norm_mix_g, w_in=w_in, conv_dw_w=conv_dw_w, conv_dw_b=conv_dw_b, conv_ln_g=conv_ln_g,
                   conv_ln_b=conv_ln_b, w_conv_out=w_conv_out, q_norm_g=q_norm_g, k_norm_g=k_norm_g, sinks=sinks,
                   w_attn_out=w_attn_out, w_merge_out=w_merge_out, norm_ffn_g=norm_ffn_g, w_ffn_in=w_ffn_in,
                   w_ffn_down=w_ffn_down)
    m_in = dict(norm_mix_g=m_norm_mix_g, w_in=m_w_in, conv_dw_w=m_conv_dw_w, conv_dw_b=m_conv_dw_b,
                conv_ln_g=m_conv_ln_g, conv_ln_b=m_conv_ln_b, w_conv_out=m_w_conv_out, q_norm_g=m_q_norm_g,
                k_norm_g=m_k_norm_g, sinks=m_sinks, w_attn_out=m_w_attn_out, w_merge_out=m_w_merge_out,
                norm_ffn_g=m_norm_ffn_g, w_ffn_in=m_w_ffn_in, w_ffn_down=m_w_ffn_down)
    v_in = dict(norm_mix_g=v_norm_mix_g, w_in=v_w_in, conv_dw_w=v_conv_dw_w, conv_dw_b=v_conv_dw_b,
                conv_ln_g=v_conv_ln_g, conv_ln_b=v_conv_ln_b, w_conv_out=v_w_conv_out, q_norm_g=v_q_norm_g,
                k_norm_g=v_k_norm_g, sinks=v_sinks, w_attn_out=v_w_attn_out, w_merge_out=v_w_merge_out,
                norm_ffn_g=v_norm_ffn_g, w_ffn_in=v_w_ffn_in, w_ffn_down=v_w_ffn_down)
    names = list(weights)
    big = ("w_conv_out", "w_attn_out", "w_merge_out", "w_ffn_in", "w_ffn_down")
    updates = {}
    after = exchanges.w_in_start(d_w_in)
    for n in names:
        if n != "w_in" and n not in big:
            updates[n] = _adamw(weights[n], grads[n], m_in[n], v_in[n], "adamw_" + n, after)
    after = exchanges.w_in_mid([updates[n][0] for n in updates])
    for n in big:
        updates[n] = _adamw(weights[n], grads[n], m_in[n], v_in[n], "adamw_" + n, after)
    g_w_in_t = exchanges.w_in_end([updates[n][0] for n in big])
    grads["w_in"] = g_w_in_t.T
    updates["w_in"] = [t.T for t in _adamw(w_in.T, g_w_in_t, m_w_in.T, v_w_in.T, "adamw_w_in", g_w_in_t)]
    return (loss, grad_x, *[grads[n] for n in names], *[updates[n][0] for n in names],
            *[updates[n][1] for n in names], *[updates[n][2] for n in names])
```

```python
import functools
import math

import jax
import jax.numpy as jnp
import numpy as np
from jax import lax
from jax.experimental import pallas as pl
from jax.experimental.pallas import tpu as pltpu

F32 = jnp.float32
BF16 = jnp.bfloat16

D = 1024
CW = 31
HD = 64
NQ = 16
NKV = 2
GROUP = NQ // NKV
BLK = 128
DFF = 2816
EPS = 1e-6
NEG = -1e30
IN_COLS = 5376
SCALE = 1.0 / math.sqrt(HD)

LR, B1, B2, AEPS, WD, STEP = 0.001, 0.9, 0.999, 1e-08, 0.01, 10

MIB = 1024 * 1024
MESH = pl.DeviceIdType.MESH

ROWS_W_IN = 1344
ROWS_SQ = 256
ROWS_FFN_IN = 1408
ROWS_DOWN = 704
ROWS_MAT = ROWS_W_IN + 3 * ROWS_SQ + ROWS_FFN_IN + ROWS_DOWN
ROWS_DW = 32
ROWS_PACK = ROWS_MAT + ROWS_DW
VEC_ROWS = 40


def _sig(x):
    return 1.0 / (1.0 + jnp.exp(-x))


def _dot(a, b):
    return jnp.dot(a, b, preferred_element_type=F32)


def _dot_nt(a, b):
    return lax.dot_general(a, b, (((1,), (1,)), ((), ())), preferred_element_type=F32)


def _dot_tn(a, b):
    return lax.dot_general(a, b, (((0,), (0,)), ((), ())), preferred_element_type=F32)


def _params(sem, vmem_mib=48):
    return pltpu.CompilerParams(dimension_semantics=sem, vmem_limit_bytes=vmem_mib * MIB)


def _resident(shape):
    return pl.BlockSpec(shape, lambda *_: (0,) * len(shape), pipeline_mode=pl.Buffered(1))


def _whole(shape):
    return pl.BlockSpec(shape, lambda *_: (0,) * len(shape))


def _rows(tm, cols, col_block=0):
    return pl.BlockSpec((tm, cols), lambda i: (i, col_block))


def _tap_phases():
    return [(phase, list(range(phase, CW, 8))) for phase in range(8)]


def _shift_copies(dst, src, base):
    for phase, taps in _tap_phases():
        n = dst.shape[1] - 8 * (4 - len(taps))
        dst[phase, 0:n, :] = src[base + phase:base + phase + n, :]


def _in_proj_conv_fwd(x2, g1, w_in, dw, dwb, lng, lnb, wc, n_seq, S):
    T = n_seq * S
    tc = min(256, S)
    nt = S // tc

    def body(x_ref, g_ref, w_ref, dw_ref, dwb_ref, lng_ref, lnb_ref, wc_ref, xn_ref, ag_ref, q_ref, kv_ref, gg_ref,
             h1_ref, h3_ref, yc_ref, ext, sh):
        i = pl.program_id(1)

        @pl.when(i == 0)
        def _():
            ext[0:32, :] = jnp.zeros((32, D), F32)

        x = x_ref[...]
        rstd = lax.rsqrt(jnp.mean(x * x, axis=-1, keepdims=True) + EPS)
        xn = (x * rstd * g_ref[...]).astype(BF16)
        xn_ref[...] = xn
        ag = _dot_nt(xn, w_ref[0:2048, :])
        ag_ref[...] = ag
        ext[32:32 + tc, :] = ag[:, 0:D] * _sig(ag[:, D:2 * D])
        q_ref[...] = _dot_nt(xn, w_ref[2048:3072, :])
        kv_ref[...] = _dot_nt(xn, w_ref[3072:3328, :])
        gg_ref[...] = _dot_nt(xn, w_ref[3328:5376, :])
        _shift_copies(sh, ext, 2)
        for cb in range(D // 128):
            cs = slice(cb * 128, (cb + 1) * 128)
            acc = jnp.broadcast_to(dwb_ref[:, cs], (tc, 128))
            for phase, taps in _tap_phases():
                for m, j in enumerate(taps):
                    acc = acc + dw_ref[j:j + 1, cs] * sh[phase, 8 * m:8 * m + tc, cs]
            h1_ref[:, cs] = acc
        ext[0:32, :] = ext[tc:tc + 32, :]
        h1 = h1_ref[...]
        mu = jnp.mean(h1, axis=-1, keepdims=True)
        cen = h1 - mu
        var = jnp.mean(cen * cen, axis=-1, keepdims=True)
        h2 = cen * lax.rsqrt(var + EPS) * lng_ref[...] + lnb_ref[...]
        h3 = (h2 * _sig(h2)).astype(BF16)
        h3_ref[...] = h3
        yc_ref[...] = _dot(h3, wc_ref[...])

    tile = lambda cols: pl.BlockSpec((tc, cols), lambda b, i: (b * nt + i, 0))
    shape = lambda cols, dtype: jax.ShapeDtypeStruct((T, cols), dtype)
    return pl.pallas_call(
        body, name="in_proj_conv_fwd", grid=(n_seq, nt),
        in_specs=[tile(D), _resident((1, D)), _resident((IN_COLS, D)), _resident((32, D)), _resident((1, D)),
                  _resident((1, D)), _resident((1, D)), _resident((D, D))],
        out_specs=[tile(D), tile(2 * D), tile(D), tile(256), tile(2 * D), tile(D), tile(D), tile(D)],
        out_shape=[shape(D, BF16), shape(2 * D, F32), shape(D, F32), shape(256, F32), shape(2 * D, F32),
                   shape(D, F32), shape(D, BF16), shape(D, F32)],
        scratch_shapes=[pltpu.VMEM((32 + tc, D), F32), pltpu.VMEM((8, tc + 24, D), F32)],
        compiler_params=_params(("parallel", "arbitrary"), 56),
    )(x2, g1, w_in, dw, dwb, lng, lnb, wc)


def _attn_consts():
    k = np.arange(BLK)[:, None]
    i = np.arange(GROUP * BLK)[None, :] % BLK
    from_prev = k > i
    dist = np.where(from_prev, i + BLK - k, i - k).astype(np.float32)
    head = np.arange(GROUP * BLK)[None, :] // BLK
    bias = []
    for kh in range(NKV):
        slope = np.exp2(-8.0 * (kh * GROUP + head + 1) / NQ).astype(np.float32)
        bias.append(-slope * dist)
    return jnp.asarray(from_prev.astype(np.float32)), jnp.asarray(np.stack(bias))


def _heads_to_lanes(t, kh):
    return jnp.concatenate([t[(kh * GROUP + g) * HD:(kh * GROUP + g + 1) * HD, :] for g in range(GROUP)], axis=1)


def _lanes_to_heads(t):
    return jnp.concatenate([t[:, g * BLK:(g + 1) * BLK] for g in range(GROUP)], axis=0)


def _rms64(t):
    return lax.rsqrt(jnp.mean(t * t, axis=-1, keepdims=True) + EPS)


def _attn_probs(kh, n, q_t, kvc_ref, kvp_ref, qg_ref, kg_ref, tri_ref, bias_ref, sink_ref):
    ks = slice(kh * HD, (kh + 1) * HD)
    vs = slice(2 * HD + kh * HD, 2 * HD + (kh + 1) * HD)
    kp, kc = kvp_ref[:, ks], kvc_ref[:, ks]
    kpb = (kp * _rms64(kp) * kg_ref[...]).astype(BF16)
    kcb = (kc * _rms64(kc) * kg_ref[...]).astype(BF16)
    qs = _heads_to_lanes(q_t, kh)
    rq = lax.rsqrt(jnp.mean(qs * qs, axis=0, keepdims=True) + EPS)
    qy = qs * rq
    qhb = (qy * (qg_ref[...] * SCALE)).astype(BF16)
    from_prev = tri_ref[...] > 0.5
    no_prev = jnp.where(n > 0, 0.0, NEG)
    s = jnp.where(from_prev, _dot(kpb, qhb) + no_prev, _dot(kcb, qhb)) + bias_ref[kh]
    sink = sink_ref[kh:kh + 1, :]
    m = jnp.maximum(jnp.max(s, axis=0, keepdims=True), sink)
    e = jnp.exp(s - m)
    es = jnp.exp(sink - m)
    rz = 1.0 / (jnp.sum(e, axis=0, keepdims=True) + es)
    return e * rz, es * rz, from_prev, qhb, kpb, kcb, kvp_ref[:, vs].astype(BF16), kvc_ref[:, vs].astype(BF16), qy, rq


def _unfold(t, from_prev):
    zero = jnp.zeros_like(t)
    return jnp.where(from_prev, t, zero), jnp.where(from_prev, zero, t)


def _attn_specs(n_seq, S):
    nb = S // BLK
    cur = lambda cols: pl.BlockSpec((BLK, cols), lambda b, n: (b * nb + n, 0))
    prev = lambda cols: pl.BlockSpec((BLK, cols), lambda b, n: (b * nb + jnp.maximum(n - 1, 0), 0))
    consts = [_resident((HD, GROUP * BLK)), _resident((1, HD)), _resident((BLK, GROUP * BLK)),
              _resident((NKV, BLK, GROUP * BLK)), _resident((NKV, GROUP * BLK))]
    return nb, cur, prev, consts


def _attn_fwd(q, kv, qg_cols, kg, sink_rows, n_seq, S):
    T = n_seq * S
    nb, cur, prev, consts = _attn_specs(n_seq, S)
    tri, bias = _attn_consts()

    def body(q_ref, kvc_ref, kvp_ref, qg_ref, kg_ref, tri_ref, bias_ref, sink_ref, o_ref):
        n = pl.program_id(1)
        q_t = q_ref[...].T
        o_t = []
        for kh in range(NKV):
            p, _, from_prev, _, _, _, vpb, vcb, _, _ = _attn_probs(kh, n, q_t, kvc_ref, kvp_ref, qg_ref, kg_ref,
                                                                   tri_ref, bias_ref, sink_ref)
            pp, pc = _unfold(p.astype(BF16), from_prev)
            o_t.append(_lanes_to_heads(_dot_tn(vpb, pp) + _dot_tn(vcb, pc)))
        o_ref[...] = jnp.concatenate(o_t, axis=0).T.astype(BF16)

    return pl.pallas_call(
        body, name="attn_fwd", grid=(n_seq, nb),
        in_specs=[cur(D), cur(256), prev(256)] + consts,
        out_specs=cur(D),
        out_shape=jax.ShapeDtypeStruct((T, D), BF16),
        compiler_params=_params(("parallel", "parallel")),
    )(q, kv, kv, qg_cols, kg, tri, bias, sink_rows)


def _merge_fwd(x2, gg, yc, o, wa, wm):
    T = x2.shape[0]
    tm = min(512, T)

    def body(x_ref, gg_ref, yc_ref, o_ref, wa_ref, wm_ref, ya_ref, mix_ref, h_ref):
        ya = _dot(o_ref[...], wa_ref[...])
        mix = (_sig(gg_ref[:, 0:D]) * yc_ref[...] + _sig(gg_ref[:, D:2 * D]) * ya).astype(BF16)
        ya_ref[...] = ya
        mix_ref[...] = mix
        h_ref[...] = x_ref[...] + _dot(mix, wm_ref[...])

    return pl.pallas_call(
        body, name="merge_fwd", grid=(T // tm,),
        in_specs=[_rows(tm, D), _rows(tm, 2 * D), _rows(tm, D), _rows(tm, D), _resident((D, D)), _resident((D, D))],
        out_specs=[_rows(tm, D), _rows(tm, D), _rows(tm, D)],
        out_shape=[jax.ShapeDtypeStruct((T, D), F32), jax.ShapeDtypeStruct((T, D), BF16),
                   jax.ShapeDtypeStruct((T, D), F32)],
        compiler_params=_params(("parallel",)),
    )(x2, gg, yc, o, wa, wm)


FF_CHUNK = DFF // 2


def _ffn(h, tgt, g2, wf, wd):
    T = h.shape[0]
    tm = min(256, T)

    def body(h_ref, t_ref, g_ref, wf_ref, wd_ref, dh_ref, dhb_ref, hn_ref, act_ref, dout_ref, dgu_ref, st_ref,
             gsc, usc):
        @pl.when(pl.program_id(0) == 0)
        def _():
            st_ref[...] = jnp.zeros((8, D), F32)

        hh = h_ref[...]
        rstd = lax.rsqrt(jnp.mean(hh * hh, axis=-1, keepdims=True) + EPS)
        hhat = hh * rstd
        hn = (hhat * g_ref[...]).astype(BF16)
        hn_ref[...] = hn
        out = hh
        for c in range(DFF // FF_CHUNK):
            cs = slice(c * FF_CHUNK, (c + 1) * FF_CHUNK)
            us = slice(DFF + c * FF_CHUNK, DFF + (c + 1) * FF_CHUNK)
            g = _dot(hn, wf_ref[c])
            u = _dot(hn, wf_ref[2 + c])
            gsc[:, cs] = g
            usc[:, cs] = u
            act = (g * _sig(g) * u).astype(BF16)
            act_ref[:, cs] = act
            out = out + _dot(act, wd_ref[cs, :])
        err = out - t_ref[...]
        dout = err * (1.0 / D)
        doutb = dout.astype(BF16)
        dout_ref[...] = doutb
        dhn = jnp.zeros((tm, D), F32)
        for c in range(DFF // FF_CHUNK):
            cs = slice(c * FF_CHUNK, (c + 1) * FF_CHUNK)
            us = slice(DFF + c * FF_CHUNK, DFF + (c + 1) * FF_CHUNK)
            g = gsc[:, cs]
            u = usc[:, cs]
            dact = _dot_nt(doutb, wd_ref[cs, :])
            sg = _sig(g)
            dg = (dact * u * (sg * (1.0 + g * (1.0 - sg)))).astype(BF16)
            du = (dact * (g * sg)).astype(BF16)
            dgu_ref[:, cs] = dg
            dgu_ref[:, us] = du
            dhn = dhn + _dot_nt(dg, wf_ref[c]) + _dot_nt(du, wf_ref[2 + c])
        st_ref[0:1, :] += jnp.sum(dhn * hhat, axis=0, keepdims=True)
        st_ref[1:2, :] += jnp.sum(err * err, axis=0, keepdims=True)
        dhh = dhn * g_ref[...]
        dh = dout + rstd * (dhh - hhat * jnp.mean(dhh * hhat, axis=-1, keepdims=True))
        dh_ref[...] = dh
        dhb_ref[...] = dh.astype(BF16)

    return pl.pallas_call(
        body, name="ffn_fwd_bwd", grid=(T // tm,),
        in_specs=[_rows(tm, D), _rows(tm, D), _resident((1, D)), _resident((4, D, FF_CHUNK)), _resident((DFF, D))],
        out_specs=[_rows(tm, D), _rows(tm, D), _rows(tm, D), _rows(tm, DFF), _rows(tm, D), _rows(tm, 2 * DFF),
                   _whole((8, D))],
        out_shape=[jax.ShapeDtypeStruct((T, D), F32), jax.ShapeDtypeStruct((T, D), BF16),
                   jax.ShapeDtypeStruct((T, D), BF16), jax.ShapeDtypeStruct((T, DFF), BF16),
                   jax.ShapeDtypeStruct((T, D), BF16), jax.ShapeDtypeStruct((T, 2 * DFF), BF16),
                   jax.ShapeDtypeStruct((8, D), F32)],
        scratch_shapes=[pltpu.VMEM((tm, DFF), F32), pltpu.VMEM((tm, DFF), F32)],
        compiler_params=_params(("arbitrary",), 56),
    )(h, tgt, g2, wf, wd)


def _merge_bwd(dhb, gg, yc, ya, wm, wa, wc):
    T = dhb.shape[0]
    tm = min(256, T)

    def body(dh_ref, gg_ref, yc_ref, ya_ref, wm_ref, wa_ref, wc_ref, dgg_ref, dyc_ref, dya_ref, do_ref, dh3_ref):
        dmix = _dot_nt(dh_ref[...], wm_ref[...])
        gc = _sig(gg_ref[:, 0:D])
        ga = _sig(gg_ref[:, D:2 * D])
        yc = yc_ref[...]
        ya = ya_ref[...]
        dgg_ref[:, 0:D] = (dmix * yc * gc * (1.0 - gc)).astype(BF16)
        dgg_ref[:, D:2 * D] = (dmix * ya * ga * (1.0 - ga)).astype(BF16)
        dyc = (dmix * gc).astype(BF16)
        dya = (dmix * ga).astype(BF16)
        dyc_ref[...] = dyc
        dya_ref[...] = dya
        do_ref[...] = _dot_nt(dya, wa_ref[...])
        dh3_ref[...] = _dot_nt(dyc, wc_ref[...])

    return pl.pallas_call(
        body, name="merge_bwd", grid=(T // tm,),
        in_specs=[_rows(tm, D), _rows(tm, 2 * D), _rows(tm, D), _rows(tm, D), _resident((D, D)), _resident((D, D)),
                  _resident((D, D))],
        out_specs=[_rows(tm, 2 * D), _rows(tm, D), _rows(tm, D), _rows(tm, D), _rows(tm, D)],
        out_shape=[jax.ShapeDtypeStruct((T, 2 * D), BF16), jax.ShapeDtypeStruct((T, D), BF16),
                   jax.ShapeDtypeStruct((T, D), BF16), jax.ShapeDtypeStruct((T, D), F32),
                   jax.ShapeDtypeStruct((T, D), F32)],
        compiler_params=_params(("parallel",)),
    )(dhb, gg, yc, ya, wm, wa, wc)


def _attn_bwd(q, kv, do, qg_cols, kg, sink_rows, n_seq, S):
    T = n_seq * S
    nb, cur, prev, consts = _attn_specs(n_seq, S)
    tri, bias = _attn_consts()

    def body(q_ref, kvc_ref, kvp_ref, do_ref, qg_ref, kg_ref, tri_ref, bias_ref, sink_ref, dq_ref, dkc_ref, dkp_ref,
             dqg_ref, dsk_ref):
        n = pl.program_id(1)

        @pl.when((pl.program_id(0) == 0) & (n == 0))
        def _():
            dqg_ref[...] = jnp.zeros((HD, BLK), F32)
            dsk_ref[...] = jnp.zeros((8, 128), F32)

        lane = lax.broadcasted_iota(jnp.int32, (1, 128), 1)
        q_t = q_ref[...].T
        do_t = do_ref[...].T
        dq_t = []
        for kh in range(NKV):
            p, ps, from_prev, qhb, kpb, kcb, vpb, vcb, qy, rq = _attn_probs(
                kh, n, q_t, kvc_ref, kvp_ref, qg_ref, kg_ref, tri_ref, bias_ref, sink_ref)
            dob = _heads_to_lanes(do_t, kh).astype(BF16)
            dp = jnp.where(from_prev, _dot(vpb, dob), _dot(vcb, dob))
            delta = jnp.sum(p * dp, axis=0, keepdims=True)
            dsp, dsc = _unfold((p * (dp - delta)).astype(BF16), from_prev)
            pp, pc = _unfold(p.astype(BF16), from_prev)
            dsink = -ps * delta
            dqh = (_dot_tn(kpb, dsp) + _dot_tn(kcb, dsc)) * SCALE
            dqg = dqh * qy
            dqg_ref[...] += sum(dqg[:, g * BLK:(g + 1) * BLK] for g in range(GROUP))
            dy = dqh * qg_ref[...]
            dq_t.append(_lanes_to_heads(rq * (dy - qy * jnp.mean(dy * qy, axis=0, keepdims=True))))
            row = jnp.zeros((1, 128), F32)
            for g in range(GROUP):
                h = kh * GROUP + g
                row = row + jnp.where(lane == h, jnp.sum(dsink[:, g * BLK:(g + 1) * BLK], axis=1, keepdims=True), 0.0)
            dsk_ref[0:1, :] += row
            ks = slice(kh * HD, (kh + 1) * HD)
            vs = slice(2 * HD + kh * HD, 2 * HD + (kh + 1) * HD)
            dkp_ref[:, ks] = _dot_nt(dsp, qhb)
            dkc_ref[:, ks] = _dot_nt(dsc, qhb)
            dkp_ref[:, vs] = _dot_nt(pp, dob)
            dkc_ref[:, vs] = _dot_nt(pc, dob)
        dq_ref[...] = jnp.concatenate(dq_t, axis=0).T.astype(BF16)

    return pl.pallas_call(
        body, name="attn_bwd", grid=(n_seq, nb),
        in_specs=[cur(D), cur(256), prev(256), cur(D)] + consts,
        out_specs=[cur(D), cur(256), cur(256), _whole((HD, BLK)), _whole((8, 128))],
        out_shape=[jax.ShapeDtypeStruct((T, D), BF16), jax.ShapeDtypeStruct((T, 256), F32),
                   jax.ShapeDtypeStruct((T, 256), F32), jax.ShapeDtypeStruct((HD, BLK), F32),
                   jax.ShapeDtypeStruct((8, 128), F32)],
        compiler_params=_params(("arbitrary", "arbitrary")),
    )(q, kv, kv, do, qg_cols, kg, tri, bias, sink_rows)


def _kv_bwd(kv, dkc, dkp, kg, n_seq, S):
    T = n_seq * S
    nb, cur, _, _ = _attn_specs(n_seq, S)
    nxt = pl.BlockSpec((BLK, 256), lambda b, n: (b * nb + jnp.minimum(n + 1, nb - 1), 0))

    def body(kv_ref, dkc_ref, dkp_ref, kg_ref, dkv_ref, dkg_ref):
        n = pl.program_id(1)

        @pl.when((pl.program_id(0) == 0) & (n == 0))
        def _():
            dkg_ref[...] = jnp.zeros((1, HD), F32)

        has_next = jnp.where(n < nb - 1, 1.0, 0.0)
        d = dkc_ref[...] + has_next * dkp_ref[...]
        for kh in range(NKV):
            ks = slice(kh * HD, (kh + 1) * HD)
            k = kv_ref[:, ks]
            r = _rms64(k)
            y = k * r
            dkh = d[:, ks]
            dkg_ref[...] += jnp.sum(dkh * y, axis=0, keepdims=True)
            dy = dkh * kg_ref[...]
            dkv_ref[:, ks] = (r * (dy - y * jnp.mean(dy * y, axis=-1, keepdims=True))).astype(BF16)
        dkv_ref[:, 2 * HD:4 * HD] = d[:, 2 * HD:4 * HD].astype(BF16)

    return pl.pallas_call(
        body, name="kv_bwd", grid=(n_seq, nb),
        in_specs=[cur(256), cur(256), nxt, _resident((1, HD))],
        out_specs=[cur(256), _whole((1, HD))],
        out_shape=[jax.ShapeDtypeStruct((T, 256), BF16), jax.ShapeDtypeStruct((1, HD), F32)],
        compiler_params=_params(("arbitrary", "arbitrary")),
    )(kv, dkc, dkp, kg)


def _conv_bwd(h1, dh3, ag, dw, lng, lnb, n_seq, S):
    T = n_seq * S
    tc = min(256, S)
    nt = S // tc

    def body(h1_ref, dh3_ref, a_ref, gt_ref, dw_ref, lng_ref, lnb_ref, dag_ref, ddw_ref, st_ref, extd, acc8, shd):
        i = pl.program_id(1)

        @pl.when((pl.program_id(0) == 0) & (i == 0))
        def _():
            acc8[...] = jnp.zeros((CW * 8, D), F32)
            st_ref[...] = jnp.zeros((8, D), F32)

        @pl.when(i == 0)
        def _():
            extd[tc:tc + 32, :] = jnp.zeros((32, D), F32)

        h1 = h1_ref[...]
        mu = jnp.mean(h1, axis=-1, keepdims=True)
        cen = h1 - mu
        rstd = lax.rsqrt(jnp.mean(cen * cen, axis=-1, keepdims=True) + EPS)
        xh = cen * rstd
        h2 = xh * lng_ref[...] + lnb_ref[...]
        sg = _sig(h2)
        dh2 = dh3_ref[...] * (sg * (1.0 + h2 * (1.0 - sg)))
        st_ref[1:2, :] += jnp.sum(dh2 * xh, axis=0, keepdims=True)
        st_ref[2:3, :] += jnp.sum(dh2, axis=0, keepdims=True)
        dxh = dh2 * lng_ref[...]
        dh1 = rstd * (dxh - jnp.mean(dxh, axis=-1, keepdims=True)
                      - xh * jnp.mean(dxh * xh, axis=-1, keepdims=True))
        st_ref[0:1, :] += jnp.sum(dh1, axis=0, keepdims=True)
        extd[0:tc, :] = dh1
        _shift_copies(shd, extd, 0)
        for cb in range(D // 128):
            cs = slice(cb * 128, (cb + 1) * 128)
            for rb in range(tc // 128):
                rs = slice(rb * 128, (rb + 1) * 128)
                a = a_ref[rs, cs]
                sgt = _sig(gt_ref[rs, cs])
                h0 = a * sgt
                acc = jnp.zeros((128, 128), F32)
                for phase, offs in _tap_phases():
                    for m, o in enumerate(offs):
                        j = CW - 1 - o
                        ahead = shd[phase, rb * 128 + 8 * m:rb * 128 + 8 * m + 128, cs]
                        acc = acc + dw_ref[j:j + 1, cs] * ahead
                        acc8[j * 8:(j + 1) * 8, cs] += jnp.sum((h0 * ahead).reshape(16, 8, 128), axis=0)
                dag_ref[rs, cs] = (acc * sgt).astype(BF16)
                dag_ref[rs, cb * 128 + D:(cb + 1) * 128 + D] = (acc * a * sgt * (1.0 - sgt)).astype(BF16)
        extd[tc:tc + 32, :] = extd[0:32, :]

        @pl.when((pl.program_id(0) == n_seq - 1) & (i == nt - 1))
        def _():
            for j in range(CW):
                ddw_ref[j:j + 1, :] = jnp.sum(acc8[j * 8:(j + 1) * 8, :], axis=0, keepdims=True)
            ddw_ref[CW:32, :] = jnp.zeros((32 - CW, D), F32)

    tile = lambda col: pl.BlockSpec((tc, D), lambda b, i: (b * nt + (nt - 1 - i), col))
    return pl.pallas_call(
        body, name="conv_bwd", grid=(n_seq, nt),
        in_specs=[tile(0), tile(0), tile(0), tile(1), _resident((32, D)), _resident((1, D)), _resident((1, D))],
        out_specs=[pl.BlockSpec((tc, 2 * D), lambda b, i: (b * nt + (nt - 1 - i), 0)), _whole((32, D)),
                   _whole((8, D))],
        out_shape=[jax.ShapeDtypeStruct((T, 2 * D), BF16), jax.ShapeDtypeStruct((32, D), F32),
                   jax.ShapeDtypeStruct((8, D), F32)],
        scratch_shapes=[pltpu.VMEM((tc + 32, D), F32), pltpu.VMEM((CW * 8, D), F32),
                        pltpu.VMEM((8, tc + 24, D), F32)],
        compiler_params=_params(("arbitrary", "arbitrary")),
    )(h1, dh3, ag, ag, dw, lng, lnb)


def _in_proj_bwd(dag, dq, dkv, dgg, dh, x2, g1, w_in):
    T = x2.shape[0]
    tm = min(512, T)

    def body(dag_ref, dq_ref, dkv_ref, dgg_ref, dh_ref, x_ref, g_ref, w_ref, dx_ref, dg_ref):
        @pl.when(pl.program_id(0) == 0)
        def _():
            dg_ref[...] = jnp.zeros((1, D), F32)

        dxn = (_dot(dag_ref[...], w_ref[0:2048, :]) + _dot(dq_ref[...], w_ref[2048:3072, :])
               + _dot(dkv_ref[...], w_ref[3072:3328, :]) + _dot(dgg_ref[...], w_ref[3328:5376, :]))
        x = x_ref[...]
        rstd = lax.rsqrt(jnp.mean(x * x, axis=-1, keepdims=True) + EPS)
        xh = x * rstd
        dg_ref[...] += jnp.sum(dxn * xh, axis=0, keepdims=True)
        dxh = dxn * g_ref[...]
        dx_ref[...] = dh_ref[...] + rstd * (dxh - xh * jnp.mean(dxh * xh, axis=-1, keepdims=True))

    return pl.pallas_call(
        body, name="in_proj_bwd", grid=(T // tm,),
        in_specs=[_rows(tm, 2 * D), _rows(tm, D), _rows(tm, 256), _rows(tm, 2 * D), _rows(tm, D), _rows(tm, D),
                  _resident((1, D)), _resident((IN_COLS, D))],
        out_specs=[_rows(tm, D), _whole((1, D))],
        out_shape=[jax.ShapeDtypeStruct((T, D), F32), jax.ShapeDtypeStruct((1, D), F32)],
        compiler_params=_params(("arbitrary",)),
    )(dag, dq, dkv, dgg, dh, x2, g1, w_in)


def _tn_matmul(a, b, name, column_blocks=False):
    T, K = a.shape
    N = b.shape[1]
    tk = K if K <= 1024 else K // 2
    tn = N if N <= 1024 else (1024 if N % 1024 == 0 and not column_blocks else N // 4)
    tt = min(2048, T)
    assert K % tk == 0 and N % tn == 0 and T % tt == 0 and tk % 128 == 0 and tn % 128 == 0

    def body(a_ref, b_ref, o_ref):
        @pl.when(pl.program_id(2) == 0)
        def _():
            o_ref[...] = jnp.zeros((tk, tn), F32)

        o_ref[...] += _dot_tn(a_ref[...], b_ref[...])

    return pl.pallas_call(
        body, name=name, grid=(K // tk, N // tn, T // tt),
        in_specs=[pl.BlockSpec((tt, tk), lambda i, j, t: (t, i)), pl.BlockSpec((tt, tn), lambda i, j, t: (t, j))],
        out_specs=(pl.BlockSpec((None, tk, tn), lambda i, j, t: (j, i, 0)) if column_blocks
                   else pl.BlockSpec((tk, tn), lambda i, j, t: (i, j))),
        out_shape=jax.ShapeDtypeStruct((N // tn, K, tn) if column_blocks else (K, N), F32),
        compiler_params=_params(("parallel", "parallel", "arbitrary")),
    )(a, b)


def _place():
    x, y, c = lax.axis_index("x"), lax.axis_index("y"), lax.axis_index("c")
    chips = [(1 - x, y), (x, 1 - y), (1 - x, 1 - y)]
    return x, y, c, chips


def _own_slot(slots, mine):
    chip = 2 * lax.axis_index("x") + lax.axis_index("y")
    return lax.dynamic_update_slice(slots, mine[None], (chip,) + (0,) * mine.ndim)


def _row_tile(rows, unit):
    return max(t for t in range(unit, 513, unit) if rows % t == 0)


def _gather_weights(pack):
    rows = pack.shape[0]
    half = rows // 2

    def body(src, dst, token, send_sems, recv_sems):
        x, y, c, chips = _place()

        def piece(px, py, pc):
            return dst.at[2 * px + py, pl.ds(pc * half, half), :]

        def copy(k, block, to, from_src=False):
            return pltpu.make_async_remote_copy(
                src_ref=src.at[pl.ds(c * half, half), :] if from_src else piece(*block), dst_ref=piece(*block),
                send_sem=send_sems.at[k], recv_sem=recv_sems.at[k], device_id=to, device_id_type=MESH)

        first = [copy(k, (x, y, c), (*chip, c), from_src=True) for k, chip in enumerate(chips)]
        for cp in first:
            cp.start()
        passed = [copy(3 + k, (*chip, c), (x, y, 1 - c)) for k, chip in enumerate(chips)]
        for k, chip in enumerate(chips):
            copy(k, (*chip, c), (x, y, c)).wait_recv()
            passed[k].start()
        for k, chip in enumerate(chips):
            copy(3 + k, (*chip, 1 - c), (x, y, c)).wait_recv()
        for cp in first + passed:
            cp.wait_send()
        token[...] = jnp.zeros((8, 128), F32)

    got, token = pl.pallas_call(
        body, name="gather_weights",
        in_specs=[pl.BlockSpec(memory_space=pl.ANY)],
        out_specs=[pl.BlockSpec(memory_space=pl.ANY), pl.BlockSpec(memory_space=pltpu.VMEM)],
        out_shape=[jax.ShapeDtypeStruct((4, rows, D), pack.dtype), jax.ShapeDtypeStruct((8, 128), F32)],
        scratch_shapes=[pltpu.SemaphoreType.DMA((6,)), pltpu.SemaphoreType.DMA((6,))],
        compiler_params=pltpu.CompilerParams(has_side_effects=True),
    )(pack)
    return _own_slot(got, pack), token[0, 0]


def _add_halves(g, got, c_idx, name="grad_add_halves"):
    rows, w = g.shape[1], g.shape[2]
    half = rows // 2
    tr = _row_tile(half, 16)
    nt = half // tr

    def body(c_ref, g_ref, r_ref, o_ref):
        o_ref[...] = (g_ref[...] + r_ref[...]).astype(BF16)

    return pl.pallas_call(
        body, name=name,
        grid_spec=pltpu.PrefetchScalarGridSpec(
            num_scalar_prefetch=1, grid=(4, nt),
            in_specs=[pl.BlockSpec((1, tr, w), lambda q, i, c_ref: (q, c_ref[0] * nt + i, 0)),
                      pl.BlockSpec((1, tr, w), lambda q, i, c_ref: (q, i, 0))],
            out_specs=pl.BlockSpec((1, tr, w), lambda q, i, c_ref: (q, i, 0))),
        out_shape=jax.ShapeDtypeStruct((4, half, w), BF16),
        compiler_params=_params(("parallel", "parallel")),
    )(c_idx, g, got)


def _own_piece(p):
    chip = 2 * lax.axis_index("x") + lax.axis_index("y")
    return lax.dynamic_index_in_dim(p, chip, axis=0, keepdims=False)


def _sum_chips(r, c_idx, name="grad_sum_chips"):
    half, w = r.shape[1], r.shape[2]
    tr = _row_tile(half, 16)
    nt = half // tr

    def body(c_ref, r_ref, o_ref):
        acc = r_ref[0].astype(F32)
        for q in range(1, 4):
            acc = acc + r_ref[q].astype(F32)
        o_ref[...] = acc

    return pl.pallas_call(
        body, name=name,
        grid_spec=pltpu.PrefetchScalarGridSpec(
            num_scalar_prefetch=1, grid=(nt,),
            in_specs=[pl.BlockSpec((4, tr, w), lambda i, c_ref: (0, i, 0))],
            out_specs=pl.BlockSpec((tr, w), lambda i, c_ref: (c_ref[0] * nt + i, 0))),
        out_shape=jax.ShapeDtypeStruct((2 * half, w), F32),
        compiler_params=_params(("parallel",)),
    )(c_idx, r)


def _join_halves(f):
    half = f.shape[0] // 2

    def body(src, dst, send_sem, recv_sem):
        x, y, c, _ = _place()
        cp = pltpu.make_async_remote_copy(
            src_ref=src.at[pl.ds(c * half, half), :], dst_ref=dst.at[pl.ds(c * half, half), :], send_sem=send_sem,
            recv_sem=recv_sem, device_id=(x, y, 1 - c), device_id_type=MESH)
        cp.start()
        pltpu.make_async_remote_copy(
            src_ref=src.at[pl.ds(c * half, half), :], dst_ref=dst.at[pl.ds((1 - c) * half, half), :],
            send_sem=send_sem, recv_sem=recv_sem, device_id=(x, y, 1 - c), device_id_type=MESH).wait_recv()
        cp.wait_send()

    return pl.pallas_call(
        body, name="grad_join_halves",
        in_specs=[pl.BlockSpec(memory_space=pl.ANY)], out_specs=pl.BlockSpec(memory_space=pl.ANY),
        out_shape=jax.ShapeDtypeStruct(f.shape, f.dtype), input_output_aliases={0: 0},
        scratch_shapes=[pltpu.SemaphoreType.DMA, pltpu.SemaphoreType.DMA],
        compiler_params=pltpu.CompilerParams(has_side_effects=True),
    )(f)


_HBM = pl.BlockSpec(memory_space=pltpu.HBM)
_SEM = pl.BlockSpec(memory_space=pltpu.SEMAPHORE)
_EFFECT = pltpu.SideEffectType.DATAFLOW_SIDE_EFFECTING


def _start_copies(name, bufs, n_sems, plan):
    nb = len(bufs)

    def body(*refs):
        for cp in plan(refs[:nb], refs[nb], refs[nb + 1])[0]:
            cp.start()
        refs[-1][...] = jnp.zeros((8, 128), F32)

    out = pl.pallas_call(
        body, name=name,
        out_shape=(pltpu.SemaphoreType.DMA((n_sems,)), pltpu.SemaphoreType.DMA((n_sems,)),
                   *[pltpu.HBM(b.shape, b.dtype) for b in bufs], jax.ShapeDtypeStruct((8, 128), F32)),
        in_specs=[_HBM] * nb, out_specs=(_SEM, _SEM, *[_HBM] * nb, pl.BlockSpec(memory_space=pltpu.VMEM)),
        input_output_aliases={i: 2 + i for i in range(nb)},
        compiler_params=pltpu.CompilerParams(has_side_effects=_EFFECT),
    )(*[pltpu.with_memory_space_constraint(b, pltpu.HBM) for b in bufs])
    return out[0], out[1], list(out[2:2 + nb]), out[-1]


def _wait_copies(name, send_sems, recv_sems, bufs, after, plan):
    nb = len(bufs)

    def body(*refs):
        _, sends, recvs = plan(refs[:nb], refs[nb], refs[nb + 1])
        for cp in sends:
            cp.wait_send()
        for cp in recvs:
            cp.wait_recv()

    out = pl.pallas_call(
        body, name=name,
        out_shape=tuple(pltpu.HBM(b.shape, b.dtype) for b in bufs),
        in_specs=[_HBM] * nb + [_SEM, _SEM] + [pl.BlockSpec(memory_space=pl.ANY)] * len(after),
        out_specs=tuple([_HBM] * nb),
        input_output_aliases={i: i for i in range(nb)},
        compiler_params=pltpu.CompilerParams(has_side_effects=_EFFECT),
    )(*bufs, send_sems, recv_sems, *after)
    return list(out)


def _plan_gather_direct(halves):
    n = len(halves)

    def plan(refs, send_sems, recv_sems):
        x, y, c, chips = _place()
        starts, recvs = [], []
        for b, half in enumerate(halves):
            src, land = refs[b], refs[n + b]
            for k, (cx, cy) in enumerate(chips):
                for d in range(2):
                    other = c if d == 0 else 1 - c
                    i = 6 * b + 2 * k + d
                    starts.append(pltpu.make_async_remote_copy(
                        src_ref=src.at[pl.ds(c * half, half), :],
                        dst_ref=land.at[2 * x + y, pl.ds(c * half, half), :],
                        send_sem=send_sems.at[i], recv_sem=recv_sems.at[i], device_id=(cx, cy, other),
                        device_id_type=MESH))
                    recvs.append(pltpu.make_async_remote_copy(
                        src_ref=src.at[pl.ds(c * half, half), :],
                        dst_ref=land.at[2 * cx + cy, pl.ds(other * half, half), :],
                        send_sem=send_sems.at[i], recv_sem=recv_sems.at[i], device_id=(cx, cy, other),
                        device_id_type=MESH))
        return starts, starts, recvs
    return plan


def _plan_swap_halves(halves):
    n = len(halves)

    def plan(refs, send_sems, recv_sems):
        x, y, c, _ = _place()
        cps = [pltpu.make_async_remote_copy(
            src_ref=refs[b].at[:, pl.ds((1 - c) * half, half), :], dst_ref=refs[n + b], send_sem=send_sems.at[b],
            recv_sem=recv_sems.at[b], device_id=(x, y, 1 - c), device_id_type=MESH)
            for b, half in enumerate(halves)]
        return cps, cps, cps
    return plan


def _plan_scatter_chips(n):
    def plan(refs, send_sems, recv_sems):
        x, y, c, chips = _place()
        me = 2 * x + y
        starts, recvs = [], []
        for b in range(n):
            src, land = refs[b], refs[n + b]
            for k, (cx, cy) in enumerate(chips):
                i = 3 * b + k
                starts.append(pltpu.make_async_remote_copy(
                    src_ref=src.at[2 * cx + cy], dst_ref=land.at[me], send_sem=send_sems.at[i],
                    recv_sem=recv_sems.at[i], device_id=(cx, cy, c), device_id_type=MESH))
                recvs.append(pltpu.make_async_remote_copy(
                    src_ref=src.at[me], dst_ref=land.at[2 * cx + cy], send_sem=send_sems.at[i],
                    recv_sem=recv_sems.at[i], device_id=(cx, cy, c), device_id_type=MESH))
        return starts, starts, recvs
    return plan


def _plan_join_halves(halves):
    def plan(refs, send_sems, recv_sems):
        x, y, c, _ = _place()
        starts, recvs = [], []
        for b, half in enumerate(halves):
            mine, theirs = refs[b].at[pl.ds(c * half, half), :], refs[b].at[pl.ds((1 - c) * half, half), :]
            starts.append(pltpu.make_async_remote_copy(
                src_ref=mine, dst_ref=mine, send_sem=send_sems.at[b], recv_sem=recv_sems.at[b],
                device_id=(x, y, 1 - c), device_id_type=MESH))
            recvs.append(pltpu.make_async_remote_copy(
                src_ref=mine, dst_ref=theirs, send_sem=send_sems.at[b], recv_sem=recv_sems.at[b],
                device_id=(x, y, 1 - c), device_id_type=MESH))
        return starts, starts, recvs
    return plan


def _allreduce_small(vec):
    def body(v_ref, o_ref, gath, send_sems, recv_sems):
        x, y, c, _ = _place()
        me = 4 * x + 2 * y + c
        gath[me] = v_ref[...]
        sends = []
        for k in range(1, 8):
            peer = (x ^ (k >> 2), y ^ ((k >> 1) & 1), c ^ (k & 1))
            sends.append(pltpu.make_async_remote_copy(
                src_ref=v_ref, dst_ref=gath.at[me], send_sem=send_sems.at[k - 1], recv_sem=recv_sems.at[k - 1],
                device_id=peer, device_id_type=MESH))
        for cp in sends:
            cp.start()
        for k in range(1, 8):
            peer = (x ^ (k >> 2), y ^ ((k >> 1) & 1), c ^ (k & 1))
            pltpu.make_async_remote_copy(
                src_ref=v_ref, dst_ref=gath.at[4 * peer[0] + 2 * peer[1] + peer[2]], send_sem=send_sems.at[k - 1],
                recv_sem=recv_sems.at[k - 1], device_id=peer, device_id_type=MESH).wait_recv()
        for cp in sends:
            cp.wait_send()
        acc = gath[0]
        for d in range(1, 8):
            acc = acc + gath[d]
        o_ref[...] = acc

    return pl.pallas_call(
        body, name="allreduce_small",
        in_specs=[pl.BlockSpec(memory_space=pltpu.VMEM)], out_specs=pl.BlockSpec(memory_space=pltpu.VMEM),
        out_shape=jax.ShapeDtypeStruct(vec.shape, F32),
        scratch_shapes=[pltpu.VMEM((8,) + vec.shape, F32), pltpu.SemaphoreType.DMA((7,)),
                        pltpu.SemaphoreType.DMA((7,))],
    )(vec)


def _adamw(w, g, m, v, name, after):
    shape = w.shape
    if w.ndim == 1 or w.size <= 128 * 128:
        two_d = (1, w.size) if w.size % 128 else (w.size // 128, 128)
    else:
        two_d = (w.shape[0], w.size // w.shape[0])
    rows, cols = two_d
    tr = _row_tile(rows, 8) if rows % 8 == 0 and rows > 512 else rows

    def body(w_ref, g_ref, m_ref, v_ref, after_ref, d_ref, nm_ref, nv_ref):
        gr = g_ref[...]
        nm = B1 * m_ref[...] + (1.0 - B1) * gr
        nv = B2 * v_ref[...] + (1.0 - B2) * (gr * gr)
        m_hat = nm / (1.0 - B1 ** STEP)
        v_hat = nv / (1.0 - B2 ** STEP)
        d_ref[...] = -LR * (m_hat / (jnp.sqrt(v_hat) + AEPS) + WD * w_ref[...])
        nm_ref[...] = nm
        nv_ref[...] = nv

    spec = pl.BlockSpec((tr, cols), lambda i: (i, 0))
    outs = pl.pallas_call(
        body, name=name, grid=(rows // tr,),
        in_specs=[spec] * 4 + [pl.BlockSpec(memory_space=pl.ANY)], out_specs=[spec] * 3,
        out_shape=[jax.ShapeDtypeStruct(two_d, F32)] * 3,
        compiler_params=_params(("parallel",)),
    )(*[t.reshape(two_d) for t in (w, g, m, v)], after)
    return [o.reshape(shape) for o in outs]


def _rows_stacked(g, lo, n_rows):
    return g[:, lo:lo + n_rows].reshape(4 * n_rows, D)


def _rows_to_slots(t):
    return t.reshape(4, t.shape[0] // 4, D)


def _pack_first(w_in, w_conv_out, conv_dw_w):
    dw = jnp.pad(conv_dw_w.reshape(CW, 256), ((0, 1), (0, 0)))
    dw_bits = lax.bitcast_convert_type(dw, BF16).reshape(16, D)
    return jnp.concatenate([w_in.T.astype(BF16), w_conv_out.astype(BF16), dw_bits, jnp.zeros((16, D), BF16)],
                           axis=0)


def _unpack_first(g):
    w_in_t = _rows_stacked(g, 0, ROWS_W_IN)
    wc = _rows_stacked(g, ROWS_W_IN, ROWS_SQ)
    o = ROWS_W_IN + ROWS_SQ
    dw = lax.bitcast_convert_type(g[:, o:o + 16].reshape(4, 32, 256, 2), F32)
    return w_in_t, wc, jnp.transpose(dw, (1, 0, 2)).reshape(32, D)


def _pack_late(w_attn_out, w_merge_out, w_ffn_down):
    return jnp.concatenate([w_attn_out.astype(BF16), w_merge_out.astype(BF16), w_ffn_down.astype(BF16)], axis=0)


def _unpack_late(g):
    return (_rows_stacked(g, 0, ROWS_SQ), _rows_stacked(g, ROWS_SQ, ROWS_SQ),
            _rows_stacked(g, 2 * ROWS_SQ, ROWS_DOWN))


class _Exchanges:
    def __init__(self, late_pack, wf_shard):
        self.c_idx = lax.axis_index("c").astype(jnp.int32).reshape(1)
        packs = [late_pack, wf_shard]
        self.late_plan = _plan_gather_direct([p.shape[0] // 2 for p in packs])
        slots = [lax.empty((4,) + p.shape, BF16) for p in packs]
        self.late = _start_copies("gather_late_start", packs + slots, 6 * len(packs), self.late_plan)
        self.first_token = self.late[3][0, 0]

    def late_weights(self, after):
        send_sems, recv_sems, bufs, _ = self.late
        pack, wf_shard, slots, wf_slots = _wait_copies("gather_late_wait", send_sems, recv_sems, bufs, after,
                                                       self.late_plan)
        wa, wm, wd = _unpack_late(_own_slot(slots, pack))
        return wa, wm, _own_slot(wf_slots, wf_shard), wd

    def reduce_start(self, d_wd, d_wf4, d_wm, d_wa, d_wc):
        gs = [_rows_to_slots(d_wd), jnp.concatenate([_rows_to_slots(t) for t in (d_wm, d_wa, d_wc)], axis=1), d_wf4]
        self.halves = [g.shape[1] // 2 for g in gs]
        self.swap_plan = _plan_swap_halves(self.halves)
        lands = [lax.empty((4, h, g.shape[2]), F32) for g, h in zip(gs, self.halves)]
        self.swap = _start_copies("grad_swap_start", gs + lands, len(gs), self.swap_plan)
        return self.swap[3][0, 0]

    def reduce_mid(self, after):
        send_sems, recv_sems, bufs, _ = self.swap
        bufs = _wait_copies("grad_swap_wait", send_sems, recv_sems, bufs, after, self.swap_plan)
        n = len(self.halves)
        ps = [_add_halves(bufs[b], bufs[n + b], self.c_idx, "grad_add_halves_%d" % b) for b in range(n)]
        self.scatter_plan = _plan_scatter_chips(n)
        self.scatter = _start_copies("grad_scatter_start", ps + [lax.empty(p.shape, BF16) for p in ps], 3 * n,
                                     self.scatter_plan)
        return self.scatter[3][0, 0]

    def reduce_late(self, after):
        send_sems, recv_sems, bufs, _ = self.scatter
        bufs = _wait_copies("grad_scatter_wait", send_sems, recv_sems, bufs, after, self.scatter_plan)
        n = len(self.halves)
        fs = [_sum_chips(_own_slot(bufs[n + b], _own_piece(bufs[b])), self.c_idx, "grad_sum_chips_%d" % b)
              for b in range(n)]
        self.join_plan = _plan_join_halves(self.halves)
        self.join = _start_copies("grad_join_start", fs, n, self.join_plan)

    def reduce_end(self, after):
        send_sems, recv_sems, bufs, _ = self.join
        g_wd, sq, g_wf = _wait_copies("grad_join_wait", send_sems, recv_sems, bufs, after, self.join_plan)
        return g_wd, g_wf, sq[0:ROWS_SQ], sq[ROWS_SQ:2 * ROWS_SQ], sq[2 * ROWS_SQ:3 * ROWS_SQ]

    def w_in_start(self, d_w_in_t):
        g = _rows_to_slots(d_w_in_t)
        self.w_half = g.shape[1] // 2
        self.w_swap_plan = _plan_swap_halves([self.w_half])
        self.w_swap = _start_copies("grad_w_in_swap_start", [g, lax.empty((4, self.w_half, D), F32)], 1,
                                    self.w_swap_plan)
        return self.w_swap[3]

    def w_in_mid(self, after):
        send_sems, recv_sems, bufs, _ = self.w_swap
        g, got = _wait_copies("grad_w_in_swap_wait", send_sems, recv_sems, bufs, after, self.w_swap_plan)
        p = _add_halves(g, got, self.c_idx, "grad_add_halves_w_in")
        self.w_scatter_plan = _plan_scatter_chips(1)
        self.w_scatter = _start_copies("grad_w_in_scatter_start", [p, lax.empty(p.shape, BF16)], 3,
                                       self.w_scatter_plan)
        return self.w_scatter[3]

    def w_in_end(self, after):
        send_sems, recv_sems, bufs, _ = self.w_scatter
        p, got = _wait_copies("grad_w_in_scatter_wait", send_sems, recv_sems, bufs, after, self.w_scatter_plan)
        return _join_halves(_sum_chips(_own_slot(got, _own_piece(p)), self.c_idx, "grad_sum_chips_w_in"))


def _local_grads(x, loss_target, norm_mix_g, conv_dw_b, conv_ln_g, conv_ln_b, q_norm_g, k_norm_g, sinks, norm_ffn_g,
                 w_in, wc, dw, exchanges):
    n_seq, S, _ = x.shape
    T = n_seq * S
    x2 = x.reshape(T, D)
    tgt = loss_target.reshape(T, D)
    row = lambda t: t.reshape(1, -1)
    g1, g2 = row(norm_mix_g), row(norm_ffn_g)
    qg, kg = jnp.broadcast_to(q_norm_g.reshape(HD, 1), (HD, GROUP * BLK)), row(k_norm_g)
    lng, lnb, dwb = row(conv_ln_g), row(conv_ln_b), row(conv_dw_b)
    sink_rows = jnp.repeat(sinks.reshape(NKV, GROUP), BLK, axis=1)

    xn, ag, q, kv, gg, h1, h3, yc = _in_proj_conv_fwd(x2, g1 + exchanges.first_token, w_in, dw, dwb, lng, lnb, wc,
                                                      n_seq, S)
    o = _attn_fwd(q, kv, qg, kg, sink_rows, n_seq, S)
    wa, wm, wf, wd = exchanges.late_weights([o, yc])
    ya, mix, h = _merge_fwd(x2, gg, yc, o, wa, wm)
    dh, dhb, hn, act, dout, dgu, ffn_stats = _ffn(h, tgt, g2, wf, wd)
    d_wd = _tn_matmul(act, dout, "dw_ffn_down")
    d_wf = _tn_matmul(hn, dgu, "dw_ffn_in", column_blocks=True)
    d_wm = _tn_matmul(mix, dhb, "dw_merge")
    dgg, dyc, dya, do, dh3 = _merge_bwd(dhb, gg, yc, ya, wm, wa, wc)
    d_wa = _tn_matmul(o, dya, "dw_attn_out")
    d_wc = _tn_matmul(h3, dyc, "dw_conv_out")
    token = exchanges.reduce_start(d_wd, d_wf, d_wm, d_wa, d_wc)
    dq, dkc, dkp, dqg, dsk = _attn_bwd(q, kv, do, qg + token, kg, sink_rows, n_seq, S)
    token = exchanges.reduce_mid([dq])
    dkv, dkg = _kv_bwd(kv, dkc, dkp, kg, n_seq, S)
    dag, ddw, conv_stats = _conv_bwd(h1, dh3, ag, dw, lng + token, lnb, n_seq, S)
    dx, dg1 = _in_proj_bwd(dag, dq, dkv, dgg, dh, x2, g1, w_in)
    exchanges.reduce_late([dx])
    d_w_in = jnp.concatenate([_tn_matmul(dag, xn, "dw_in_conv"), _tn_matmul(dq, xn, "dw_in_q"),
                              _tn_matmul(dkv, xn, "dw_in_kv"), _tn_matmul(dgg, xn, "dw_in_gates")], axis=0)

    heads = jnp.concatenate([jnp.sum(dqg, axis=1), dkg[0], dsk[0, :NQ], jnp.zeros((D - 2 * HD - NQ,), F32)])
    vec = jnp.concatenate([dg1, conv_stats[0:3], ffn_stats[0:1], heads[None], ffn_stats[1:2], jnp.zeros((1, D), F32),
                           ddw], axis=0)
    return ffn_stats[1], dx.reshape(x.shape), d_w_in, vec


def kernel(x, norm_mix_g, w_in, conv_dw_w, conv_dw_b, conv_ln_g, conv_ln_b, w_conv_out, q_norm_g, k_norm_g, sinks, w_attn_out, w_merge_out, norm_ffn_g, w_ffn_in, w_ffn_down, loss_target, m_norm_mix_g, m_w_in, m_conv_dw_w, m_conv_dw_b, m_conv_ln_g, m_conv_ln_b, m_w_conv_out, m_q_norm_g, m_k_norm_g, m_sinks, m_w_attn_out, m_w_merge_out, m_norm_ffn_g, m_w_ffn_in, m_w_ffn_down, v_norm_mix_g, v_w_in, v_conv_dw_w, v_conv_dw_b, v_conv_ln_g, v_conv_ln_b, v_w_conv_out, v_q_norm_g, v_k_norm_g, v_sinks, v_w_attn_out, v_w_merge_out, v_norm_ffn_g, v_w_ffn_in, v_w_ffn_down):
    chip = 2 * lax.axis_index("x") + lax.axis_index("y")

    first, token = _gather_weights(_pack_first(w_in, w_conv_out, conv_dw_w))
    exchanges = _Exchanges(_pack_late(w_attn_out, w_merge_out, w_ffn_down) + token.astype(BF16),
                           w_ffn_in.astype(BF16) + token.astype(BF16))
    _, grad_x, d_w_in, vec = _local_grads(x, loss_target, norm_mix_g, conv_dw_b, conv_ln_g, conv_ln_b, q_norm_g,
                                          k_norm_g, sinks, norm_ffn_g, *_unpack_first(first), exchanges)

    g_wd, g_wf, g_wm, g_wa, g_wc = exchanges.reduce_end([d_w_in])
    small = _allreduce_small(vec)
    loss = 0.5 / D * jnp.sum(small[6])
    g_dw = lax.dynamic_slice_in_dim(small[8:8 + CW], chip * 256, 256, axis=1).reshape(CW, 1, 256)
    grads = {
        "norm_mix_g": small[0], "conv_dw_w": g_dw, "conv_dw_b": small[1], "conv_ln_g": small[2],
        "conv_ln_b": small[3], "w_conv_out": g_wc, "q_norm_g": small[5, 0:HD], "k_norm_g": small[5, HD:2 * HD],
        "sinks": small[5, 2 * HD:2 * HD + NQ], "w_attn_out": g_wa, "w_merge_out": g_wm, "norm_ffn_g": small[4],
        "w_ffn_in": g_wf, "w_ffn_down": g_wd,
    }
    weights = dict(norm_mix_g=norm_mix_g, w_in=w_in, conv_dw_w=conv_dw_w, conv_dw_b=conv_dw_b, conv_ln_g=conv_ln_g,
                   conv_ln_b=conv_ln_b, w_conv_out=w_conv_out, q_norm_g=q_norm_g, k_norm_g=k_norm_g, sinks=sinks,
                   w_attn_out=w_attn_out, w_merge_out=w_merge_out, norm_ffn_g=norm_ffn_g, w_ffn_in=w_ffn_in,
                   w_ffn_down=w_ffn_down)
    m_in = dict(norm_mix_g=m_norm_mix_g, w_in=m_w_in, conv_dw_w=m_conv_dw_w, conv_dw_b=m_conv_dw_b,
                conv_ln_g=m_conv_ln_g, conv_ln_b=m_conv_ln_b, w_conv_out=m_w_conv_out, q_norm_g=m_q_norm_g,
                k_norm_g=m_k_norm_g, sinks=m_sinks, w_attn_out=m_w_attn_out, w_merge_out=m_w_merge_out,
                norm_ffn_g=m_norm_ffn_g, w_ffn_in=m_w_ffn_in, w_ffn_down=m_w_ffn_down)
    v_in = dict(norm_mix_g=v_norm_mix_g, w_in=v_w_in, conv_dw_w=v_conv_dw_w, conv_dw_b=v_conv_dw_b,
                conv_ln_g=v_conv_ln_g, conv_ln_b=v_conv_ln_b, w_conv_out=v_w_conv_out, q_norm_g=v_q_norm_g,
                k_norm_g=v_k_norm_g, sinks=v_sinks, w_attn_out=v_w_attn_out, w_merge_out=v_w_merge_out,
                norm_ffn_g=v_norm_ffn_g, w_ffn_in=v_w_ffn_in, w_ffn_down=v_w_ffn_down)
    names = list(weights)
    big = ("w_conv_out", "w_attn_out", "w_merge_out", "w_ffn_in", "w_ffn_down")
    updates = {}
    after = exchanges.w_in_start(d_w_in)
    for n in names:
        if n != "w_in" and n not in big:
            updates[n] = _adamw(weights[n], grads[n], m_in[n], v_in[n], "adamw_" + n, after)
    after = exchanges.w_in_mid([updates[n][0] for n in updates])
    for n in big:
        updates[n] = _adamw(weights[n], grads[n], m_in[n], v_in[n], "adamw_" + n, after)
    g_w_in_t = exchanges.w_in_end([updates[n][0] for n in big])
    grads["w_in"] = g_w_in_t.T
    updates["w_in"] = [t.T for t in _adamw(w_in.T, g_w_in_t, m_w_in.T, v_w_in.T, "adamw_w_in", g_w_in_t)]
    return (loss, grad_x, *[grads[n] for n in names], *[updates[n][0] for n in names],
            *[updates[n][1] for n in names], *[updates[n][2] for n in names])
```

```python
import functools
import math

import jax
import jax.numpy as jnp
import numpy as np
from jax import lax
from jax.experimental import pallas as pl
from jax.experimental.pallas import tpu as pltpu

F32 = jnp.float32
BF16 = jnp.bfloat16

D = 1024
CW = 31
HD = 64
NQ = 16
NKV = 2
GROUP = NQ // NKV
BLK = 128
DFF = 2816
EPS = 1e-6
NEG = -1e30
IN_COLS = 5376
SCALE = 1.0 / math.sqrt(HD)

LR, B1, B2, AEPS, WD, STEP = 0.001, 0.9, 0.999, 1e-08, 0.01, 10

MIB = 1024 * 1024
MESH = pl.DeviceIdType.MESH

ROWS_W_IN = 1344
ROWS_SQ = 256
ROWS_FFN_IN = 1408
ROWS_DOWN = 704
ROWS_MAT = ROWS_W_IN + 3 * ROWS_SQ + ROWS_FFN_IN + ROWS_DOWN
ROWS_DW = 32
ROWS_PACK = ROWS_MAT + ROWS_DW
VEC_ROWS = 40


def _sig(x):
    return 1.0 / (1.0 + jnp.exp(-x))


def _dot(a, b):
    return jnp.dot(a, b, preferred_element_type=F32)


def _dot_nt(a, b):
    return lax.dot_general(a, b, (((1,), (1,)), ((), ())), preferred_element_type=F32)


def _dot_tn(a, b):
    return lax.dot_general(a, b, (((0,), (0,)), ((), ())), preferred_element_type=F32)


def _params(sem, vmem_mib=48):
    return pltpu.CompilerParams(dimension_semantics=sem, vmem_limit_bytes=vmem_mib * MIB)


def _resident(shape):
    return pl.BlockSpec(shape, lambda *_: (0,) * len(shape), pipeline_mode=pl.Buffered(1))


def _whole(shape):
    return pl.BlockSpec(shape, lambda *_: (0,) * len(shape))


def _rows(tm, cols, col_block=0):
    return pl.BlockSpec((tm, cols), lambda i: (i, col_block))


def _tap_phases():
    return [(phase, list(range(phase, CW, 8))) for phase in range(8)]


def _shift_copies(dst, src, base):
    for phase, taps in _tap_phases():
        n = dst.shape[1] - 8 * (4 - len(taps))
        dst[phase, 0:n, :] = src[base + phase:base + phase + n, :]


def _in_proj_conv_fwd(x2, g1, w_in, dw, dwb, lng, lnb, wc, n_seq, S):
    T = n_seq * S
    tc = min(256, S)
    nt = S // tc

    def body(x_ref, g_ref, w_ref, dw_ref, dwb_ref, lng_ref, lnb_ref, wc_ref, xn_ref, ag_ref, q_ref, kv_ref, gg_ref,
             h1_ref, h3_ref, yc_ref, ext, sh):
        i = pl.program_id(1)

        @pl.when(i == 0)
        def _():
            ext[0:32, :] = jnp.zeros((32, D), F32)

        x = x_ref[...]
        rstd = lax.rsqrt(jnp.mean(x * x, axis=-1, keepdims=True) + EPS)
        xn = (x * rstd * g_ref[...]).astype(BF16)
        xn_ref[...] = xn
        ag = _dot_nt(xn, w_ref[0:2048, :])
        ag_ref[...] = ag
        ext[32:32 + tc, :] = ag[:, 0:D] * _sig(ag[:, D:2 * D])
        q_ref[...] = _dot_nt(xn, w_ref[2048:3072, :])
        kv_ref[...] = _dot_nt(xn, w_ref[3072:3328, :])
        gg_ref[...] = _dot_nt(xn, w_ref[3328:5376, :])
        _shift_copies(sh, ext, 2)
        for cb in range(D // 128):
            cs = slice(cb * 128, (cb + 1) * 128)
            acc = jnp.broadcast_to(dwb_ref[:, cs], (tc, 128))
            for phase, taps in _tap_phases():
                for m, j in enumerate(taps):
                    acc = acc + dw_ref[j:j + 1, cs] * sh[phase, 8 * m:8 * m + tc, cs]
            h1_ref[:, cs] = acc
        ext[0:32, :] = ext[tc:tc + 32, :]
        h1 = h1_ref[...]
        mu = jnp.mean(h1, axis=-1, keepdims=True)
        cen = h1 - mu
        var = jnp.mean(cen * cen, axis=-1, keepdims=True)
        h2 = cen * lax.rsqrt(var + EPS) * lng_ref[...] + lnb_ref[...]
        h3 = (h2 * _sig(h2)).astype(BF16)
        h3_ref[...] = h3
        yc_ref[...] = _dot(h3, wc_ref[...])

    tile = lambda cols: pl.BlockSpec((tc, cols), lambda b, i: (b * nt + i, 0))
    shape = lambda cols, dtype: jax.ShapeDtypeStruct((T, cols), dtype)
    return pl.pallas_call(
        body, name="in_proj_conv_fwd", grid=(n_seq, nt),
        in_specs=[tile(D), _resident((1, D)), _resident((IN_COLS, D)), _resident((32, D)), _resident((1, D)),
                  _resident((1, D)), _resident((1, D)), _resident((D, D))],
        out_specs=[tile(D), tile(2 * D), tile(D), tile(256), tile(2 * D), tile(D), tile(D), tile(D)],
        out_shape=[shape(D, BF16), shape(2 * D, F32), shape(D, F32), shape(256, F32), shape(2 * D, F32),
                   shape(D, F32), shape(D, BF16), shape(D, F32)],
        scratch_shapes=[pltpu.VMEM((32 + tc, D), F32), pltpu.VMEM((8, tc + 24, D), F32)],
        compiler_params=_params(("parallel", "arbitrary"), 56),
    )(x2, g1, w_in, dw, dwb, lng, lnb, wc)


def _attn_consts():
    k = np.arange(BLK)[:, None]
    i = np.arange(GROUP * BLK)[None, :] % BLK
    from_prev = k > i
    dist = np.where(from_prev, i + BLK - k, i - k).astype(np.float32)
    head = np.arange(GROUP * BLK)[None, :] // BLK
    bias = []
    for kh in range(NKV):
        slope = np.exp2(-8.0 * (kh * GROUP + head + 1) / NQ).astype(np.float32)
        bias.append(-slope * dist)
    return jnp.asarray(from_prev.astype(np.float32)), jnp.asarray(np.stack(bias))


def _heads_to_lanes(t, kh):
    return jnp.concatenate([t[(kh * GROUP + g) * HD:(kh * GROUP + g + 1) * HD, :] for g in range(GROUP)], axis=1)


def _lanes_to_heads(t):
    return jnp.concatenate([t[:, g * BLK:(g + 1) * BLK] for g in range(GROUP)], axis=0)


def _rms64(t):
    return lax.rsqrt(jnp.mean(t * t, axis=-1, keepdims=True) + EPS)


def _attn_probs(kh, n, q_t, kvc_ref, kvp_ref, qg_ref, kg_ref, tri_ref, bias_ref, sink_ref):
    ks = slice(kh * HD, (kh + 1) * HD)
    vs = slice(2 * HD + kh * HD, 2 * HD + (kh + 1) * HD)
    kp, kc = kvp_ref[:, ks], kvc_ref[:, ks]
    kpb = (kp * _rms64(kp) * kg_ref[...]).astype(BF16)
    kcb = (kc * _rms64(kc) * kg_ref[...]).astype(BF16)
    qs = _heads_to_lanes(q_t, kh)
    rq = lax.rsqrt(jnp.mean(qs * qs, axis=0, keepdims=True) + EPS)
    qy = qs * rq
    qhb = (qy * (qg_ref[...] * SCALE)).astype(BF16)
    from_prev = tri_ref[...] > 0.5
    no_prev = jnp.where(n > 0, 0.0, NEG)
    s = jnp.where(from_prev, _dot(kpb, qhb) + no_prev, _dot(kcb, qhb)) + bias_ref[kh]
    sink = sink_ref[kh:kh + 1, :]
    m = jnp.maximum(jnp.max(s, axis=0, keepdims=True), sink)
    e = jnp.exp(s - m)
    es = jnp.exp(sink - m)
    rz = 1.0 / (jnp.sum(e, axis=0, keepdims=True) + es)
    prev_mask = tri_ref[...].astype(BF16)
    return e * rz, es * rz, prev_mask, qhb, kpb, kcb, kvp_ref[:, vs].astype(BF16), kvc_ref[:, vs].astype(BF16), qy, rq


def _unfold(t, prev_mask):
    prev = t * prev_mask
    return prev, t - prev


def _attn_specs(n_seq, S):
    nb = S // BLK
    cur = lambda cols: pl.BlockSpec((BLK, cols), lambda b, n: (b * nb + n, 0))
    prev = lambda cols: pl.BlockSpec((BLK, cols), lambda b, n: (b * nb + jnp.maximum(n - 1, 0), 0))
    consts = [_resident((HD, GROUP * BLK)), _resident((1, HD)), _resident((BLK, GROUP * BLK)),
              _resident((NKV, BLK, GROUP * BLK)), _resident((NKV, GROUP * BLK))]
    return nb, cur, prev, consts


def _attn_fwd(q, kv, qg_cols, kg, sink_rows, n_seq, S):
    T = n_seq * S
    nb, cur, prev, consts = _attn_specs(n_seq, S)
    tri, bias = _attn_consts()

    def body(q_ref, kvc_ref, kvp_ref, qg_ref, kg_ref, tri_ref, bias_ref, sink_ref, o_ref):
        n = pl.program_id(1)
        q_t = q_ref[...].T
        o_t = []
        for kh in range(NKV):
            p, _, prev_mask, _, _, _, vpb, vcb, _, _ = _attn_probs(kh, n, q_t, kvc_ref, kvp_ref, qg_ref, kg_ref,
                                                                   tri_ref, bias_ref, sink_ref)
            pp, pc = _unfold(p.astype(BF16), prev_mask)
            o_t.append(_lanes_to_heads(_dot_tn(vpb, pp) + _dot_tn(vcb, pc)))
        o_ref[...] = jnp.concatenate(o_t, axis=0).T.astype(BF16)

    return pl.pallas_call(
        body, name="attn_fwd", grid=(n_seq, nb),
        in_specs=[cur(D), cur(256), prev(256)] + consts,
        out_specs=cur(D),
        out_shape=jax.ShapeDtypeStruct((T, D), BF16),
        compiler_params=_params(("parallel", "parallel")),
    )(q, kv, kv, qg_cols, kg, tri, bias, sink_rows)


def _merge_fwd(x2, gg, yc, o, wa, wm):
    T = x2.shape[0]
    tm = min(512, T)

    def body(x_ref, gg_ref, yc_ref, o_ref, wa_ref, wm_ref, ya_ref, mix_ref, h_ref):
        ya = _dot(o_ref[...], wa_ref[...])
        mix = (_sig(gg_ref[:, 0:D]) * yc_ref[...] + _sig(gg_ref[:, D:2 * D]) * ya).astype(BF16)
        ya_ref[...] = ya
        mix_ref[...] = mix
        h_ref[...] = x_ref[...] + _dot(mix, wm_ref[...])

    return pl.pallas_call(
        body, name="merge_fwd", grid=(T // tm,),
        in_specs=[_rows(tm, D), _rows(tm, 2 * D), _rows(tm, D), _rows(tm, D), _resident((D, D)), _resident((D, D))],
        out_specs=[_rows(tm, D), _rows(tm, D), _rows(tm, D)],
        out_shape=[jax.ShapeDtypeStruct((T, D), F32), jax.ShapeDtypeStruct((T, D), BF16),
                   jax.ShapeDtypeStruct((T, D), F32)],
        compiler_params=_params(("parallel",)),
    )(x2, gg, yc, o, wa, wm)


FF_CHUNK = DFF // 2


def _ffn(h, tgt, g2, wf, wd):
    T = h.shape[0]
    tm = min(256, T)

    def body(h_ref, t_ref, g_ref, wf_ref, wd_ref, dh_ref, dhb_ref, hn_ref, act_ref, dout_ref, dgu_ref, st_ref,
             gsc, usc):
        @pl.when(pl.program_id(0) == 0)
        def _():
            st_ref[...] = jnp.zeros((8, D), F32)

        hh = h_ref[...]
        rstd = lax.rsqrt(jnp.mean(hh * hh, axis=-1, keepdims=True) + EPS)
        hhat = hh * rstd
        hn = (hhat * g_ref[...]).astype(BF16)
        hn_ref[...] = hn
        out = hh
        for c in range(DFF // FF_CHUNK):
            cs = slice(c * FF_CHUNK, (c + 1) * FF_CHUNK)
            us = slice(DFF + c * FF_CHUNK, DFF + (c + 1) * FF_CHUNK)
            g = _dot(hn, wf_ref[c])
            u = _dot(hn, wf_ref[2 + c])
            gsc[:, cs] = g
            usc[:, cs] = u
            act = (g * _sig(g) * u).astype(BF16)
            act_ref[:, cs] = act
            out = out + _dot(act, wd_ref[cs, :])
        err = out - t_ref[...]
        dout = err * (1.0 / D)
        doutb = dout.astype(BF16)
        dout_ref[...] = doutb
        dhn = jnp.zeros((tm, D), F32)
        for c in range(DFF // FF_CHUNK):
            cs = slice(c * FF_CHUNK, (c + 1) * FF_CHUNK)
            us = slice(DFF + c * FF_CHUNK, DFF + (c + 1) * FF_CHUNK)
            g = gsc[:, cs]
            u = usc[:, cs]
            dact = _dot_nt(doutb, wd_ref[cs, :])
            sg = _sig(g)
            dg = (dact * u * (sg * (1.0 + g * (1.0 - sg)))).astype(BF16)
            du = (dact * (g * sg)).astype(BF16)
            dgu_ref[:, cs] = dg
            dgu_ref[:, us] = du
            dhn = dhn + _dot_nt(dg, wf_ref[c]) + _dot_nt(du, wf_ref[2 + c])
        st_ref[0:1, :] += jnp.sum(dhn * hhat, axis=0, keepdims=True)
        st_ref[1:2, :] += jnp.sum(err * err, axis=0, keepdims=True)
        dhh = dhn * g_ref[...]
        dh = dout + rstd * (dhh - hhat * jnp.mean(dhh * hhat, axis=-1, keepdims=True))
        dh_ref[...] = dh
        dhb_ref[...] = dh.astype(BF16)

    return pl.pallas_call(
        body, name="ffn_fwd_bwd", grid=(T // tm,),
        in_specs=[_rows(tm, D), _rows(tm, D), _resident((1, D)), _resident((4, D, FF_CHUNK)), _resident((DFF, D))],
        out_specs=[_rows(tm, D), _rows(tm, D), _rows(tm, D), _rows(tm, DFF), _rows(tm, D), _rows(tm, 2 * DFF),
                   _whole((8, D))],
        out_shape=[jax.ShapeDtypeStruct((T, D), F32), jax.ShapeDtypeStruct((T, D), BF16),
                   jax.ShapeDtypeStruct((T, D), BF16), jax.ShapeDtypeStruct((T, DFF), BF16),
                   jax.ShapeDtypeStruct((T, D), BF16), jax.ShapeDtypeStruct((T, 2 * DFF), BF16),
                   jax.ShapeDtypeStruct((8, D), F32)],
        scratch_shapes=[pltpu.VMEM((tm, DFF), F32), pltpu.VMEM((tm, DFF), F32)],
        compiler_params=_params(("arbitrary",), 56),
    )(h, tgt, g2, wf, wd)


def _merge_bwd(dhb, gg, yc, ya, wm, wa, wc):
    T = dhb.shape[0]
    tm = min(256, T)

    def body(dh_ref, gg_ref, yc_ref, ya_ref, wm_ref, wa_ref, wc_ref, dgg_ref, dyc_ref, dya_ref, do_ref, dh3_ref):
        dmix = _dot_nt(dh_ref[...], wm_ref[...])
        gc = _sig(gg_ref[:, 0:D])
        ga = _sig(gg_ref[:, D:2 * D])
        yc = yc_ref[...]
        ya = ya_ref[...]
        dgg_ref[:, 0:D] = (dmix * yc * gc * (1.0 - gc)).astype(BF16)
        dgg_ref[:, D:2 * D] = (dmix * ya * ga * (1.0 - ga)).astype(BF16)
        dyc = (dmix * gc).astype(BF16)
        dya = (dmix * ga).astype(BF16)
        dyc_ref[...] = dyc
        dya_ref[...] = dya
        do_ref[...] = _dot_nt(dya, wa_ref[...]).astype(BF16)
        dh3_ref[...] = _dot_nt(dyc, wc_ref[...])

    return pl.pallas_call(
        body, name="merge_bwd", grid=(T // tm,),
        in_specs=[_rows(tm, D), _rows(tm, 2 * D), _rows(tm, D), _rows(tm, D), _resident((D, D)), _resident((D, D)),
                  _resident((D, D))],
        out_specs=[_rows(tm, 2 * D), _rows(tm, D), _rows(tm, D), _rows(tm, D), _rows(tm, D)],
        out_shape=[jax.ShapeDtypeStruct((T, 2 * D), BF16), jax.ShapeDtypeStruct((T, D), BF16),
                   jax.ShapeDtypeStruct((T, D), BF16), jax.ShapeDtypeStruct((T, D), BF16),
                   jax.ShapeDtypeStruct((T, D), F32)],
        compiler_params=_params(("parallel",)),
    )(dhb, gg, yc, ya, wm, wa, wc)


def _attn_bwd(q, kv, do, qg_cols, kg, sink_rows, n_seq, S):
    T = n_seq * S
    nb, cur, prev, consts = _attn_specs(n_seq, S)
    tri, bias = _attn_consts()

    def body(q_ref, kvc_ref, kvp_ref, do_ref, qg_ref, kg_ref, tri_ref, bias_ref, sink_ref, dq_ref, dkc_ref, dkp_ref,
             dqg_ref, dsk_ref):
        n = pl.program_id(1)

        @pl.when((pl.program_id(0) == 0) & (n == 0))
        def _():
            dqg_ref[...] = jnp.zeros((HD, BLK), F32)
            dsk_ref[...] = jnp.zeros((8, 128), F32)

        lane = lax.broadcasted_iota(jnp.int32, (1, 128), 1)
        q_t = q_ref[...].T
        do_t = do_ref[...].astype(F32).T
        dq_t = []
        for kh in range(NKV):
            p, ps, prev_mask, qhb, kpb, kcb, vpb, vcb, qy, rq = _attn_probs(
                kh, n, q_t, kvc_ref, kvp_ref, qg_ref, kg_ref, tri_ref, bias_ref, sink_ref)
            dob = _heads_to_lanes(do_t, kh).astype(BF16)
            dp = jnp.where(tri_ref[...] > 0.5, _dot(vpb, dob), _dot(vcb, dob))
            delta = jnp.sum(p * dp, axis=0, keepdims=True)
            dsp, dsc = _unfold((p * (dp - delta)).astype(BF16), prev_mask)
            pp, pc = _unfold(p.astype(BF16), prev_mask)
            dsink = -ps * delta
            dqh = (_dot_tn(kpb, dsp) + _dot_tn(kcb, dsc)) * SCALE
            dqg = dqh * qy
            dqg_ref[...] += sum(dqg[:, g * BLK:(g + 1) * BLK] for g in range(GROUP))
            dy = dqh * qg_ref[...]
            dq_t.append(_lanes_to_heads(rq * (dy - qy * jnp.mean(dy * qy, axis=0, keepdims=True))))
            row = jnp.zeros((1, 128), F32)
            for g in range(GROUP):
                h = kh * GROUP + g
                row = row + jnp.where(lane == h, jnp.sum(dsink[:, g * BLK:(g + 1) * BLK], axis=1, keepdims=True), 0.0)
            dsk_ref[0:1, :] += row
            ks = slice(kh * HD, (kh + 1) * HD)
            vs = slice(2 * HD + kh * HD, 2 * HD + (kh + 1) * HD)
            dkp_ref[:, ks] = _dot_nt(dsp, qhb)
            dkc_ref[:, ks] = _dot_nt(dsc, qhb)
            dkp_ref[:, vs] = _dot_nt(pp, dob)
            dkc_ref[:, vs] = _dot_nt(pc, dob)
        dq_ref[...] = jnp.concatenate(dq_t, axis=0).T.astype(BF16)

    return pl.pallas_call(
        body, name="attn_bwd", grid=(n_seq, nb),
        in_specs=[cur(D), cur(256), prev(256), cur(D)] + consts,
        out_specs=[cur(D), cur(256), cur(256), _whole((HD, BLK)), _whole((8, 128))],
        out_shape=[jax.ShapeDtypeStruct((T, D), BF16), jax.ShapeDtypeStruct((T, 256), F32),
                   jax.ShapeDtypeStruct((T, 256), F32), jax.ShapeDtypeStruct((HD, BLK), F32),
                   jax.ShapeDtypeStruct((8, 128), F32)],
        compiler_params=_params(("arbitrary", "arbitrary")),
    )(q, kv, kv, do, qg_cols, kg, tri, bias, sink_rows)


def _kv_bwd(kv, dkc, dkp, kg_cols, n_seq, S):
    T = n_seq * S
    seq = lambda cols: pl.BlockSpec((S, cols), lambda b: (b, 0))

    def body(kv_ref, dkc_ref, dkp_ref, kg_ref, dkv_ref, dkg_ref):
        @pl.when(pl.program_id(0) == 0)
        def _():
            dkg_ref[...] = jnp.zeros((HD, BLK), F32)

        from_next = jnp.concatenate([dkp_ref[BLK:S, :], jnp.zeros((BLK, 256), F32)], axis=0)
        d = dkc_ref[...] + from_next
        d_t = d[:, 0:2 * HD].T
        k_t = kv_ref[:, 0:2 * HD].T
        kg = jnp.concatenate([kg_ref[...]] * (S // BLK), axis=1)
        out = []
        for kh in range(NKV):
            k = k_t[kh * HD:(kh + 1) * HD, :]
            r = lax.rsqrt(jnp.mean(k * k, axis=0, keepdims=True) + EPS)
            y = k * r
            dkh = d_t[kh * HD:(kh + 1) * HD, :]
            dkg = dkh * y
            dkg_ref[...] += sum(dkg[:, j * BLK:(j + 1) * BLK] for j in range(S // BLK))
            dy = dkh * kg
            out.append(r * (dy - y * jnp.mean(dy * y, axis=0, keepdims=True)))
        dkv_ref[:, 0:2 * HD] = jnp.concatenate(out, axis=0).T.astype(BF16)
        dkv_ref[:, 2 * HD:4 * HD] = d[:, 2 * HD:4 * HD].astype(BF16)

    return pl.pallas_call(
        body, name="kv_bwd", grid=(n_seq,),
        in_specs=[seq(256), seq(256), seq(256), _resident((HD, BLK))],
        out_specs=[seq(256), _whole((HD, BLK))],
        out_shape=[jax.ShapeDtypeStruct((T, 256), BF16), jax.ShapeDtypeStruct((HD, BLK), F32)],
        compiler_params=_params(("arbitrary",)),
    )(kv, dkc, dkp, kg_cols)


def _conv_bwd(h1, dh3, ag, dw, lng, lnb, n_seq, S):
    T = n_seq * S
    tc = min(256, S)
    nt = S // tc

    def body(h1_ref, dh3_ref, a_ref, gt_ref, dw_ref, lng_ref, lnb_ref, dag_ref, ddw_ref, st_ref, extd, acc8, shd):
        i = pl.program_id(1)

        @pl.when((pl.program_id(0) == 0) & (i == 0))
        def _():
            acc8[...] = jnp.zeros((CW * 8, D), F32)
            st_ref[...] = jnp.zeros((8, D), F32)

        @pl.when(i == 0)
        def _():
            extd[tc:tc + 32, :] = jnp.zeros((32, D), F32)

        h1 = h1_ref[...]
        mu = jnp.mean(h1, axis=-1, keepdims=True)
        cen = h1 - mu
        rstd = lax.rsqrt(jnp.mean(cen * cen, axis=-1, keepdims=True) + EPS)
        xh = cen * rstd
        h2 = xh * lng_ref[...] + lnb_ref[...]
        sg = _sig(h2)
        dh2 = dh3_ref[...] * (sg * (1.0 + h2 * (1.0 - sg)))
        st_ref[1:2, :] += jnp.sum(dh2 * xh, axis=0, keepdims=True)
        st_ref[2:3, :] += jnp.sum(dh2, axis=0, keepdims=True)
        dxh = dh2 * lng_ref[...]
        dh1 = rstd * (dxh - jnp.mean(dxh, axis=-1, keepdims=True)
                      - xh * jnp.mean(dxh * xh, axis=-1, keepdims=True))
        st_ref[0:1, :] += jnp.sum(dh1, axis=0, keepdims=True)
        extd[0:tc, :] = dh1
        _shift_copies(shd, extd, 0)
        for cb in range(D // 128):
            cs = slice(cb * 128, (cb + 1) * 128)
            for rb in range(tc // 128):
                rs = slice(rb * 128, (rb + 1) * 128)
                a = a_ref[rs, cs]
                sgt = _sig(gt_ref[rs, cs])
                h0 = a * sgt
                acc = jnp.zeros((128, 128), F32)
                for phase, offs in _tap_phases():
                    for m, o in enumerate(offs):
                        j = CW - 1 - o
                        ahead = shd[phase, rb * 128 + 8 * m:rb * 128 + 8 * m + 128, cs]
                        acc = acc + dw_ref[j:j + 1, cs] * ahead
                        acc8[j * 8:(j + 1) * 8, cs] += jnp.sum((h0 * ahead).reshape(16, 8, 128), axis=0)
                dag_ref[rs, cs] = (acc * sgt).astype(BF16)
                dag_ref[rs, cb * 128 + D:(cb + 1) * 128 + D] = (acc * a * sgt * (1.0 - sgt)).astype(BF16)
        extd[tc:tc + 32, :] = extd[0:32, :]

        @pl.when((pl.program_id(0) == n_seq - 1) & (i == nt - 1))
        def _():
            for j in range(CW):
                ddw_ref[j:j + 1, :] = jnp.sum(acc8[j * 8:(j + 1) * 8, :], axis=0, keepdims=True)
            ddw_ref[CW:32, :] = jnp.zeros((32 - CW, D), F32)

    tile = lambda col: pl.BlockSpec((tc, D), lambda b, i: (b * nt + (nt - 1 - i), col))
    return pl.pallas_call(
        body, name="conv_bwd", grid=(n_seq, nt),
        in_specs=[tile(0), tile(0), tile(0), tile(1), _resident((32, D)), _resident((1, D)), _resident((1, D))],
        out_specs=[pl.BlockSpec((tc, 2 * D), lambda b, i: (b * nt + (nt - 1 - i), 0)), _whole((32, D)),
                   _whole((8, D))],
        out_shape=[jax.ShapeDtypeStruct((T, 2 * D), BF16), jax.ShapeDtypeStruct((32, D), F32),
                   jax.ShapeDtypeStruct((8, D), F32)],
        scratch_shapes=[pltpu.VMEM((tc + 32, D), F32), pltpu.VMEM((CW * 8, D), F32),
                        pltpu.VMEM((8, tc + 24, D), F32)],
        compiler_params=_params(("arbitrary", "arbitrary")),
    )(h1, dh3, ag, ag, dw, lng, lnb)


def _in_proj_bwd(dag, dq, dkv, dgg, dh, x2, g1, w_in):
    T = x2.shape[0]
    tm = min(512, T)

    def body(dag_ref, dq_ref, dkv_ref, dgg_ref, dh_ref, x_ref, g_ref, w_ref, dx_ref, dg_ref):
        @pl.when(pl.program_id(0) == 0)
        def _():
            dg_ref[...] = jnp.zeros((1, D), F32)

        dxn = (_dot(dag_ref[...], w_ref[0:2048, :]) + _dot(dq_ref[...], w_ref[2048:3072, :])
               + _dot(dkv_ref[...], w_ref[3072:3328, :]) + _dot(dgg_ref[...], w_ref[3328:5376, :]))
        x = x_ref[...]
        rstd = lax.rsqrt(jnp.mean(x * x, axis=-1, keepdims=True) + EPS)
        xh = x * rstd
        dg_ref[...] += jnp.sum(dxn * xh, axis=0, keepdims=True)
        dxh = dxn * g_ref[...]
        dx_ref[...] = dh_ref[...] + rstd * (dxh - xh * jnp.mean(dxh * xh, axis=-1, keepdims=True))

    return pl.pallas_call(
        body, name="in_proj_bwd", grid=(T // tm,),
        in_specs=[_rows(tm, 2 * D), _rows(tm, D), _rows(tm, 256), _rows(tm, 2 * D), _rows(tm, D), _rows(tm, D),
                  _resident((1, D)), _resident((IN_COLS, D))],
        out_specs=[_rows(tm, D), _whole((1, D))],
        out_shape=[jax.ShapeDtypeStruct((T, D), F32), jax.ShapeDtypeStruct((1, D), F32)],
        compiler_params=_params(("arbitrary",)),
    )(dag, dq, dkv, dgg, dh, x2, g1, w_in)


def _tn_matmul(a, b, name, column_blocks=False):
    T, K = a.shape
    N = b.shape[1]
    tk = K if K <= 1024 else K // 2
    tn = N if N <= 1024 else (1024 if N % 1024 == 0 and not column_blocks else N // 4)
    tt = min(2048, T)
    assert K % tk == 0 and N % tn == 0 and T % tt == 0 and tk % 128 == 0 and tn % 128 == 0

    def body(a_ref, b_ref, o_ref):
        @pl.when(pl.program_id(2) == 0)
        def _():
            o_ref[...] = jnp.zeros((tk, tn), F32)

        o_ref[...] += _dot_tn(a_ref[...], b_ref[...])

    return pl.pallas_call(
        body, name=name, grid=(K // tk, N // tn, T // tt),
        in_specs=[pl.BlockSpec((tt, tk), lambda i, j, t: (t, i)), pl.BlockSpec((tt, tn), lambda i, j, t: (t, j))],
        out_specs=(pl.BlockSpec((None, tk, tn), lambda i, j, t: (j, i, 0)) if column_blocks
                   else pl.BlockSpec((tk, tn), lambda i, j, t: (i, j))),
        out_shape=jax.ShapeDtypeStruct((N // tn, K, tn) if column_blocks else (K, N), F32),
        compiler_params=_params(("parallel", "parallel", "arbitrary")),
    )(a, b)


def _place():
    x, y, c = lax.axis_index("x"), lax.axis_index("y"), lax.axis_index("c")
    chips = [(1 - x, y), (x, 1 - y), (1 - x, 1 - y)]
    return x, y, c, chips


def _own_slot(slots, mine):
    chip = 2 * lax.axis_index("x") + lax.axis_index("y")
    return lax.dynamic_update_slice(slots, mine[None], (chip,) + (0,) * mine.ndim)


def _row_tile(rows, unit):
    return max(t for t in range(unit, 513, unit) if rows % t == 0)


def _gather_weights(pack):
    rows = pack.shape[0]
    half = rows // 2

    def body(src, dst, token, send_sems, recv_sems):
        x, y, c, chips = _place()

        def piece(px, py, pc):
            return dst.at[2 * px + py, pl.ds(pc * half, half), :]

        def copy(k, block, to, from_src=False):
            return pltpu.make_async_remote_copy(
                src_ref=src.at[pl.ds(c * half, half), :] if from_src else piece(*block), dst_ref=piece(*block),
                send_sem=send_sems.at[k], recv_sem=recv_sems.at[k], device_id=to, device_id_type=MESH)

        first = [copy(k, (x, y, c), (*chip, c), from_src=True) for k, chip in enumerate(chips)]
        for cp in first:
            cp.start()
        passed = [copy(3 + k, (*chip, c), (x, y, 1 - c)) for k, chip in enumerate(chips)]
        for k, chip in enumerate(chips):
            copy(k, (*chip, c), (x, y, c)).wait_recv()
            passed[k].start()
        for k, chip in enumerate(chips):
            copy(3 + k, (*chip, 1 - c), (x, y, c)).wait_recv()
        for cp in first + passed:
            cp.wait_send()
        token[...] = jnp.zeros((8, 128), F32)

    got, token = pl.pallas_call(
        body, name="gather_weights",
        in_specs=[pl.BlockSpec(memory_space=pl.ANY)],
        out_specs=[pl.BlockSpec(memory_space=pl.ANY), pl.BlockSpec(memory_space=pltpu.VMEM)],
        out_shape=[jax.ShapeDtypeStruct((4, rows, D), pack.dtype), jax.ShapeDtypeStruct((8, 128), F32)],
        scratch_shapes=[pltpu.SemaphoreType.DMA((6,)), pltpu.SemaphoreType.DMA((6,))],
        compiler_params=pltpu.CompilerParams(has_side_effects=True),
    )(pack)
    return _own_slot(got, pack), token[0, 0]


def _add_halves(g, got, c_idx, name="grad_add_halves"):
    rows, w = g.shape[1], g.shape[2]
    half = rows // 2
    tr = _row_tile(half, 16)
    nt = half // tr

    def body(c_ref, g_ref, r_ref, o_ref):
        o_ref[...] = (g_ref[...] + r_ref[...]).astype(BF16)

    return pl.pallas_call(
        body, name=name,
        grid_spec=pltpu.PrefetchScalarGridSpec(
            num_scalar_prefetch=1, grid=(4, nt),
            in_specs=[pl.BlockSpec((1, tr, w), lambda q, i, c_ref: (q, c_ref[0] * nt + i, 0)),
                      pl.BlockSpec((1, tr, w), lambda q, i, c_ref: (q, i, 0))],
            out_specs=pl.BlockSpec((1, tr, w), lambda q, i, c_ref: (q, i, 0))),
        out_shape=jax.ShapeDtypeStruct((4, half, w), BF16),
        compiler_params=_params(("parallel", "parallel")),
    )(c_idx, g, got)


def _own_piece(p):
    chip = 2 * lax.axis_index("x") + lax.axis_index("y")
    return lax.dynamic_index_in_dim(p, chip, axis=0, keepdims=False)


def _sum_chips(r, c_idx, name="grad_sum_chips"):
    half, w = r.shape[1], r.shape[2]
    tr = _row_tile(half, 16)
    nt = half // tr

    def body(c_ref, r_ref, o_ref):
        acc = r_ref[0].astype(F32)
        for q in range(1, 4):
            acc = acc + r_ref[q].astype(F32)
        o_ref[...] = acc

    return pl.pallas_call(
        body, name=name,
        grid_spec=pltpu.PrefetchScalarGridSpec(
            num_scalar_prefetch=1, grid=(nt,),
            in_specs=[pl.BlockSpec((4, tr, w), lambda i, c_ref: (0, i, 0))],
            out_specs=pl.BlockSpec((tr, w), lambda i, c_ref: (c_ref[0] * nt + i, 0))),
        out_shape=jax.ShapeDtypeStruct((2 * half, w), F32),
        compiler_params=_params(("parallel",)),
    )(c_idx, r)


def _join_halves(f):
    half = f.shape[0] // 2

    def body(src, dst, send_sem, recv_sem):
        x, y, c, _ = _place()
        cp = pltpu.make_async_remote_copy(
            src_ref=src.at[pl.ds(c * half, half), :], dst_ref=dst.at[pl.ds(c * half, half), :], send_sem=send_sem,
            recv_sem=recv_sem, device_id=(x, y, 1 - c), device_id_type=MESH)
        cp.start()
        pltpu.make_async_remote_copy(
            src_ref=src.at[pl.ds(c * half, half), :], dst_ref=dst.at[pl.ds((1 - c) * half, half), :],
            send_sem=send_sem, recv_sem=recv_sem, device_id=(x, y, 1 - c), device_id_type=MESH).wait_recv()
        cp.wait_send()

    return pl.pallas_call(
        body, name="grad_join_halves",
        in_specs=[pl.BlockSpec(memory_space=pl.ANY)], out_specs=pl.BlockSpec(memory_space=pl.ANY),
        out_shape=jax.ShapeDtypeStruct(f.shape, f.dtype), input_output_aliases={0: 0},
        scratch_shapes=[pltpu.SemaphoreType.DMA, pltpu.SemaphoreType.DMA],
        compiler_params=pltpu.CompilerParams(has_side_effects=True),
    )(f)


_HBM = pl.BlockSpec(memory_space=pltpu.HBM)
_SEM = pl.BlockSpec(memory_space=pltpu.SEMAPHORE)
_EFFECT = pltpu.SideEffectType.DATAFLOW_SIDE_EFFECTING


def _start_copies(name, bufs, n_sems, plan):
    nb = len(bufs)

    def body(*refs):
        for cp in plan(refs[:nb], refs[nb], refs[nb + 1])[0]:
            cp.start()
        refs[-1][...] = jnp.zeros((8, 128), F32)

    out = pl.pallas_call(
        body, name=name,
        out_shape=(pltpu.SemaphoreType.DMA((n_sems,)), pltpu.SemaphoreType.DMA((n_sems,)),
                   *[pltpu.HBM(b.shape, b.dtype) for b in bufs], jax.ShapeDtypeStruct((8, 128), F32)),
        in_specs=[_HBM] * nb, out_specs=(_SEM, _SEM, *[_HBM] * nb, pl.BlockSpec(memory_space=pltpu.VMEM)),
        input_output_aliases={i: 2 + i for i in range(nb)},
        compiler_params=pltpu.CompilerParams(has_side_effects=_EFFECT),
    )(*[pltpu.with_memory_space_constraint(b, pltpu.HBM) for b in bufs])
    return out[0], out[1], list(out[2:2 + nb]), out[-1]


def _wait_copies(name, send_sems, recv_sems, bufs, after, plan):
    nb = len(bufs)

    def body(*refs):
        _, sends, recvs = plan(refs[:nb], refs[nb], refs[nb + 1])
        for cp in sends:
            cp.wait_send()
        for cp in recvs:
            cp.wait_recv()

    out = pl.pallas_call(
        body, name=name,
        out_shape=tuple(pltpu.HBM(b.shape, b.dtype) for b in bufs),
        in_specs=[_HBM] * nb + [_SEM, _SEM] + [pl.BlockSpec(memory_space=pl.ANY)] * len(after),
        out_specs=tuple([_HBM] * nb),
        input_output_aliases={i: i for i in range(nb)},
        compiler_params=pltpu.CompilerParams(has_side_effects=_EFFECT),
    )(*bufs, send_sems, recv_sems, *after)
    return list(out)


def _plan_gather_direct(halves):
    n = len(halves)

    def plan(refs, send_sems, recv_sems):
        x, y, c, chips = _place()
        starts, recvs = [], []
        for b, half in enumerate(halves):
            src, land = refs[b], refs[n + b]
            for k, (cx, cy) in enumerate(chips):
                for d in range(2):
                    other = c if d == 0 else 1 - c
                    i = 6 * b + 2 * k + d
                    starts.append(pltpu.make_async_remote_copy(
                        src_ref=src.at[pl.ds(c * half, half), :],
                        dst_ref=land.at[2 * x + y, pl.ds(c * half, half), :],
                        send_sem=send_sems.at[i], recv_sem=recv_sems.at[i], device_id=(cx, cy, other),
                        device_id_type=MESH))
                    recvs.append(pltpu.make_async_remote_copy(
                        src_ref=src.at[pl.ds(c * half, half), :],
                        dst_ref=land.at[2 * cx + cy, pl.ds(other * half, half), :],
                        send_sem=send_sems.at[i], recv_sem=recv_sems.at[i], device_id=(cx, cy, other),
                        device_id_type=MESH))
        return starts, starts, recvs
    return plan


def _plan_swap_halves(halves):
    n = len(halves)

    def plan(refs, send_sems, recv_sems):
        x, y, c, _ = _place()
        cps = [pltpu.make_async_remote_copy(
            src_ref=refs[b].at[:, pl.ds((1 - c) * half, half), :], dst_ref=refs[n + b], send_sem=send_sems.at[b],
            recv_sem=recv_sems.at[b], device_id=(x, y, 1 - c), device_id_type=MESH)
            for b, half in enumerate(halves)]
        return cps, cps, cps
    return plan


def _plan_scatter_chips(n):
    def plan(refs, send_sems, recv_sems):
        x, y, c, chips = _place()
        me = 2 * x + y
        starts, recvs = [], []
        for b in range(n):
            src, land = refs[b], refs[n + b]
            for k, (cx, cy) in enumerate(chips):
                i = 3 * b + k
                starts.append(pltpu.make_async_remote_copy(
                    src_ref=src.at[2 * cx + cy], dst_ref=land.at[me], send_sem=send_sems.at[i],
                    recv_sem=recv_sems.at[i], device_id=(cx, cy, c), device_id_type=MESH))
                recvs.append(pltpu.make_async_remote_copy(
                    src_ref=src.at[me], dst_ref=land.at[2 * cx + cy], send_sem=send_sems.at[i],
                    recv_sem=recv_sems.at[i], device_id=(cx, cy, c), device_id_type=MESH))
        return starts, starts, recvs
    return plan


def _plan_join_halves(halves):
    def plan(refs, send_sems, recv_sems):
        x, y, c, _ = _place()
        starts, recvs = [], []
        for b, half in enumerate(halves):
            mine, theirs = refs[b].at[pl.ds(c * half, half), :], refs[b].at[pl.ds((1 - c) * half, half), :]
            starts.append(pltpu.make_async_remote_copy(
                src_ref=mine, dst_ref=mine, send_sem=send_sems.at[b], recv_sem=recv_sems.at[b],
                device_id=(x, y, 1 - c), device_id_type=MESH))
            recvs.append(pltpu.make_async_remote_copy(
                src_ref=mine, dst_ref=theirs, send_sem=send_sems.at[b], recv_sem=recv_sems.at[b],
                device_id=(x, y, 1 - c), device_id_type=MESH))
        return starts, starts, recvs
    return plan


def _allreduce_small(vec):
    def body(v_ref, o_ref, gath, send_sems, recv_sems):
        x, y, c, _ = _place()
        me = 4 * x + 2 * y + c
        gath[me] = v_ref[...]
        sends = []
        for k in range(1, 8):
            peer = (x ^ (k >> 2), y ^ ((k >> 1) & 1), c ^ (k & 1))
            sends.append(pltpu.make_async_remote_copy(
                src_ref=v_ref, dst_ref=gath.at[me], send_sem=send_sems.at[k - 1], recv_sem=recv_sems.at[k - 1],
                device_id=peer, device_id_type=MESH))
        for cp in sends:
            cp.start()
        for k in range(1, 8):
            peer = (x ^ (k >> 2), y ^ ((k >> 1) & 1), c ^ (k & 1))
            pltpu.make_async_remote_copy(
                src_ref=v_ref, dst_ref=gath.at[4 * peer[0] + 2 * peer[1] + peer[2]], send_sem=send_sems.at[k - 1],
                recv_sem=recv_sems.at[k - 1], device_id=peer, device_id_type=MESH).wait_recv()
        for cp in sends:
            cp.wait_send()
        acc = gath[0]
        for d in range(1, 8):
            acc = acc + gath[d]
        o_ref[...] = acc

    return pl.pallas_call(
        body, name="allreduce_small",
        in_specs=[pl.BlockSpec(memory_space=pltpu.VMEM)], out_specs=pl.BlockSpec(memory_space=pltpu.VMEM),
        out_shape=jax.ShapeDtypeStruct(vec.shape, F32),
        scratch_shapes=[pltpu.VMEM((8,) + vec.shape, F32), pltpu.SemaphoreType.DMA((7,)),
                        pltpu.SemaphoreType.DMA((7,))],
    )(vec)


def _adamw(w, g, m, v, name, after):
    shape = w.shape
    if w.ndim == 1 or w.size <= 128 * 128:
        two_d = (1, w.size) if w.size % 128 else (w.size // 128, 128)
    else:
        two_d = (w.shape[0], w.size // w.shape[0])
    rows, cols = two_d
    tr = _row_tile(rows, 8) if rows % 8 == 0 and rows > 512 else rows

    def body(w_ref, g_ref, m_ref, v_ref, after_ref, d_ref, nm_ref, nv_ref):
        gr = g_ref[...]
        nm = B1 * m_ref[...] + (1.0 - B1) * gr
        nv = B2 * v_ref[...] + (1.0 - B2) * (gr * gr)
        m_hat = nm / (1.0 - B1 ** STEP)
        v_hat = nv / (1.0 - B2 ** STEP)
        d_ref[...] = -LR * (m_hat / (jnp.sqrt(v_hat) + AEPS) + WD * w_ref[...])
        nm_ref[...] = nm
        nv_ref[...] = nv

    spec = pl.BlockSpec((tr, cols), lambda i: (i, 0))
    outs = pl.pallas_call(
        body, name=name, grid=(rows // tr,),
        in_specs=[spec] * 4 + [pl.BlockSpec(memory_space=pl.ANY)], out_specs=[spec] * 3,
        out_shape=[jax.ShapeDtypeStruct(two_d, F32)] * 3,
        compiler_params=_params(("parallel",)),
    )(*[t.reshape(two_d) for t in (w, g, m, v)], after)
    return [o.reshape(shape) for o in outs]


def _rows_stacked(g, lo, n_rows):
    return g[:, lo:lo + n_rows].reshape(4 * n_rows, D)


def _rows_to_slots(t):
    return t.reshape(4, t.shape[0] // 4, D)


def _pack_first(w_in, w_conv_out, conv_dw_w):
    dw = jnp.pad(conv_dw_w.reshape(CW, 256), ((0, 1), (0, 0)))
    dw_bits = lax.bitcast_convert_type(dw, BF16).reshape(16, D)
    return jnp.concatenate([w_in.T.astype(BF16), w_conv_out.astype(BF16), dw_bits, jnp.zeros((16, D), BF16)],
                           axis=0)


def _unpack_first(g):
    w_in_t = _rows_stacked(g, 0, ROWS_W_IN)
    wc = _rows_stacked(g, ROWS_W_IN, ROWS_SQ)
    o = ROWS_W_IN + ROWS_SQ
    dw = lax.bitcast_convert_type(g[:, o:o + 16].reshape(4, 32, 256, 2), F32)
    return w_in_t, wc, jnp.transpose(dw, (1, 0, 2)).reshape(32, D)


def _pack_late(w_attn_out, w_merge_out, w_ffn_down):
    return jnp.concatenate([w_attn_out.astype(BF16), w_merge_out.astype(BF16), w_ffn_down.astype(BF16)], axis=0)


def _unpack_late(g):
    return (_rows_stacked(g, 0, ROWS_SQ), _rows_stacked(g, ROWS_SQ, ROWS_SQ),
            _rows_stacked(g, 2 * ROWS_SQ, ROWS_DOWN))


class _Exchanges:
    def __init__(self, late_pack, wf_shard):
        self.c_idx = lax.axis_index("c").astype(jnp.int32).reshape(1)
        packs = [late_pack, wf_shard]
        self.late_plan = _plan_gather_direct([p.shape[0] // 2 for p in packs])
        slots = [lax.empty((4,) + p.shape, BF16) for p in packs]
        self.late = _start_copies("gather_late_start", packs + slots, 6 * len(packs), self.late_plan)
        self.first_token = self.late[3][0, 0]

    def late_weights(self, after):
        send_sems, recv_sems, bufs, _ = self.late
        pack, wf_shard, slots, wf_slots = _wait_copies("gather_late_wait", send_sems, recv_sems, bufs, after,
                                                       self.late_plan)
        wa, wm, wd = _unpack_late(_own_slot(slots, pack))
        return wa, wm, _own_slot(wf_slots, wf_shard), wd

    def reduce_start(self, d_wd, d_wf4, d_wm, d_wa, d_wc):
        gs = [_rows_to_slots(d_wd), jnp.concatenate([_rows_to_slots(t) for t in (d_wm, d_wa, d_wc)], axis=1), d_wf4]
        self.halves = [g.shape[1] // 2 for g in gs]
        self.swap_plan = _plan_swap_halves(self.halves)
        lands = [lax.empty((4, h, g.shape[2]), F32) for g, h in zip(gs, self.halves)]
        self.swap = _start_copies("grad_swap_start", gs + lands, len(gs), self.swap_plan)
        return self.swap[3][0, 0]

    def reduce_mid(self, after):
        send_sems, recv_sems, bufs, _ = self.swap
        bufs = _wait_copies("grad_swap_wait", send_sems, recv_sems, bufs, after, self.swap_plan)
        n = len(self.halves)
        ps = [_add_halves(bufs[b], bufs[n + b], self.c_idx, "grad_add_halves_%d" % b) for b in range(n)]
        self.scatter_plan = _plan_scatter_chips(n)
        self.scatter = _start_copies("grad_scatter_start", ps + [lax.empty(p.shape, BF16) for p in ps], 3 * n,
                                     self.scatter_plan)
        return self.scatter[3][0, 0]

    def reduce_late(self, after):
        send_sems, recv_sems, bufs, _ = self.scatter
        bufs = _wait_copies("grad_scatter_wait", send_sems, recv_sems, bufs, after, self.scatter_plan)
        n = len(self.halves)
        fs = [_sum_chips(_own_slot(bufs[n + b], _own_piece(bufs[b])), self.c_idx, "grad_sum_chips_%d" % b)
              for b in range(n)]
        self.join_plan = _plan_join_halves(self.halves)
        self.join = _start_copies("grad_join_start", fs, n, self.join_plan)

    def reduce_end(self, after):
        send_sems, recv_sems, bufs, _ = self.join
        g_wd, sq, g_wf = _wait_copies("grad_join_wait", send_sems, recv_sems, bufs, after, self.join_plan)
        return g_wd, g_wf, sq[0:ROWS_SQ], sq[ROWS_SQ:2 * ROWS_SQ], sq[2 * ROWS_SQ:3 * ROWS_SQ]

    def w_in_start(self, d_w_in_t):
        g = _rows_to_slots(d_w_in_t)
        self.w_half = g.shape[1] // 2
        self.w_swap_plan = _plan_swap_halves([self.w_half])
        self.w_swap = _start_copies("grad_w_in_swap_start", [g, lax.empty((4, self.w_half, D), F32)], 1,
                                    self.w_swap_plan)
        return self.w_swap[3]

    def w_in_mid(self, after):
        send_sems, recv_sems, bufs, _ = self.w_swap
        g, got = _wait_copies("grad_w_in_swap_wait", send_sems, recv_sems, bufs, after, self.w_swap_plan)
        p = _add_halves(g, got, self.c_idx, "grad_add_halves_w_in")
        self.w_scatter_plan = _plan_scatter_chips(1)
        self.w_scatter = _start_copies("grad_w_in_scatter_start", [p, lax.empty(p.shape, BF16)], 3,
                                       self.w_scatter_plan)
        return self.w_scatter[3]

    def w_in_end(self, after):
        send_sems, recv_sems, bufs, _ = self.w_scatter
        p, got = _wait_copies("grad_w_in_scatter_wait", send_sems, recv_sems, bufs, after, self.w_scatter_plan)
        return _join_halves(_sum_chips(_own_slot(got, _own_piece(p)), self.c_idx, "grad_sum_chips_w_in"))


def _local_grads(x, loss_target, norm_mix_g, conv_dw_b, conv_ln_g, conv_ln_b, q_norm_g, k_norm_g, sinks, norm_ffn_g,
                 w_in, wc, dw, exchanges):
    n_seq, S, _ = x.shape
    T = n_seq * S
    x2 = x.reshape(T, D)
    tgt = loss_target.reshape(T, D)
    row = lambda t: t.reshape(1, -1)
    g1, g2 = row(norm_mix_g), row(norm_ffn_g)
    qg, kg = jnp.broadcast_to(q_norm_g.reshape(HD, 1), (HD, GROUP * BLK)), row(k_norm_g)
    lng, lnb, dwb = row(conv_ln_g), row(conv_ln_b), row(conv_dw_b)
    sink_rows = jnp.repeat(sinks.reshape(NKV, GROUP), BLK, axis=1)

    xn, ag, q, kv, gg, h1, h3, yc = _in_proj_conv_fwd(x2, g1 + exchanges.first_token, w_in, dw, dwb, lng, lnb, wc,
                                                      n_seq, S)
    o = _attn_fwd(q, kv, qg, kg, sink_rows, n_seq, S)
    wa, wm, wf, wd = exchanges.late_weights([o, yc])
    ya, mix, h = _merge_fwd(x2, gg, yc, o, wa, wm)
    dh, dhb, hn, act, dout, dgu, ffn_stats = _ffn(h, tgt, g2, wf, wd)
    d_wd = _tn_matmul(act, dout, "dw_ffn_down")
    d_wf = _tn_matmul(hn, dgu, "dw_ffn_in", column_blocks=True)
    d_wm = _tn_matmul(mix, dhb, "dw_merge")
    dgg, dyc, dya, do, dh3 = _merge_bwd(dhb, gg, yc, ya, wm, wa, wc)
    d_wa = _tn_matmul(o, dya, "dw_attn_out")
    d_wc = _tn_matmul(h3, dyc, "dw_conv_out")
    token = exchanges.reduce_start(d_wd, d_wf, d_wm, d_wa, d_wc)
    dq, dkc, dkp, dqg, dsk = _attn_bwd(q, kv, do, qg + token, kg, sink_rows, n_seq, S)
    token = exchanges.reduce_mid([dq])
    dkv, dkg = _kv_bwd(kv, dkc, dkp, jnp.broadcast_to(k_norm_g.reshape(HD, 1), (HD, BLK)), n_seq, S)
    dag, ddw, conv_stats = _conv_bwd(h1, dh3, ag, dw, lng + token, lnb, n_seq, S)
    dx, dg1 = _in_proj_bwd(dag, dq, dkv, dgg, dh, x2, g1, w_in)
    exchanges.reduce_late([dx])
    d_w_in = jnp.concatenate([_tn_matmul(dag, xn, "dw_in_conv"), _tn_matmul(dq, xn, "dw_in_q"),
                              _tn_matmul(dkv, xn, "dw_in_kv"), _tn_matmul(dgg, xn, "dw_in_gates")], axis=0)

    heads = jnp.concatenate([jnp.sum(dqg, axis=1), jnp.sum(dkg, axis=1), dsk[0, :NQ],
                             jnp.zeros((D - 2 * HD - NQ,), F32)])
    vec = jnp.concatenate([dg1, conv_stats[0:3], ffn_stats[0:1], heads[None], ffn_stats[1:2], jnp.zeros((1, D), F32),
                           ddw], axis=0)
    return ffn_stats[1], dx.reshape(x.shape), d_w_in, vec


def kernel(x, norm_mix_g, w_in, conv_dw_w, conv_dw_b, conv_ln_g, conv_ln_b, w_conv_out, q_norm_g, k_norm_g, sinks, w_attn_out, w_merge_out, norm_ffn_g, w_ffn_in, w_ffn_down, loss_target, m_norm_mix_g, m_w_in, m_conv_dw_w, m_conv_dw_b, m_conv_ln_g, m_conv_ln_b, m_w_conv_out, m_q_norm_g, m_k_norm_g, m_sinks, m_w_attn_out, m_w_merge_out, m_norm_ffn_g, m_w_ffn_in, m_w_ffn_down, v_norm_mix_g, v_w_in, v_conv_dw_w, v_conv_dw_b, v_conv_ln_g, v_conv_ln_b, v_w_conv_out, v_q_norm_g, v_k_norm_g, v_sinks, v_w_attn_out, v_w_merge_out, v_norm_ffn_g, v_w_ffn_in, v_w_ffn_down):
    chip = 2 * lax.axis_index("x") + lax.axis_index("y")

    first, token = _gather_weights(_pack_first(w_in, w_conv_out, conv_dw_w))
    exchanges = _Exchanges(_pack_late(w_attn_out, w_merge_out, w_ffn_down) + token.astype(BF16),
                           w_ffn_in.astype(BF16) + token.astype(BF16))
    _, grad_x, d_w_in, vec = _local_grads(x, loss_target, norm_mix_g, conv_dw_b, conv_ln_g, conv_ln_b, q_norm_g,
                                          k_norm_g, sinks, norm_ffn_g, *_unpack_first(first), exchanges)

    g_wd, g_wf, g_wm, g_wa, g_wc = exchanges.reduce_end([d_w_in])
    small = _allreduce_small(vec)
    loss = 0.5 / D * jnp.sum(small[6])
    g_dw = lax.dynamic_slice_in_dim(small[8:8 + CW], chip * 256, 256, axis=1).reshape(CW, 1, 256)
    grads = {
        "norm_mix_g": small[0], "conv_dw_w": g_dw, "conv_dw_b": small[1], "conv_ln_g": small[2],
        "conv_ln_b": small[3], "w_conv_out": g_wc, "q_norm_g": small[5, 0:HD], "k_norm_g": small[5, HD:2 * HD],
        "sinks": small[5, 2 * HD:2 * HD + NQ], "w_attn_out": g_wa, "w_merge_out": g_wm, "norm_ffn_g": small[4],
        "w_ffn_in": g_wf, "w_ffn_down": g_wd,
    }
    weights = dict(norm_mix_g=norm_mix_g, w_in=w_in, conv_dw_w=conv_dw_w, conv_dw_b=conv_dw_b, conv_ln_g=conv_ln_g,
                   conv_ln_b=conv_ln_b, w_conv_out=w_conv_out, q_norm_g=q_norm_g, k_norm_g=k_norm_g, sinks=sinks,
                   w_attn_out=w_attn_out, w_merge_out=w_merge_out, norm_ffn_g=norm_ffn_g, w_ffn_in=w_ffn_in,
                   w_ffn_down=w_ffn_down)
    m_in = dict(norm_mix_g=m_norm_mix_g, w_in=m_w_in, conv_dw_w=m_conv_dw_w, conv_dw_b=m_conv_dw_b,
                conv_ln_g=m_conv_ln_g, conv_ln_b=m_conv_ln_b, w_conv_out=m_w_conv_out, q_norm_g=m_q_norm_g,
                k_norm_g=m_k_norm_g, sinks=m_sinks, w_attn_out=m_w_attn_out, w_merge_out=m_w_merge_out,
                norm_ffn_g=m_norm_ffn_g, w_ffn_in=m_w_ffn_in, w_ffn_down=m_w_ffn_down)
    v_in = dict(norm_mix_g=v_norm_mix_g, w_in=v_w_in, conv_dw_w=v_conv_dw_w, conv_dw_b=v_conv_dw_b,
                conv_ln_g=v_conv_ln_g, conv_ln_b=v_conv_ln_b, w_conv_out=v_w_conv_out, q_norm_g=v_q_norm_g,
                k_norm_g=v_k_norm_g, sinks=v_sinks, w_attn_out=v_w_attn_out, w_merge_out=v_w_merge_out,
                norm_ffn_g=v_norm_ffn_g, w_ffn_in=v_w_ffn_in, w_ffn_down=v_w_ffn_down)
    names = list(weights)
    big = ("w_conv_out", "w_attn_out", "w_merge_out", "w_ffn_in", "w_ffn_down")
    updates = {}
    after = exchanges.w_in_start(d_w_in)
    for n in names:
        if n != "w_in" and n not in big:
            updates[n] = _adamw(weights[n], grads[n], m_in[n], v_in[n], "adamw_" + n, after)
    after = exchanges.w_in_mid([updates[n][0] for n in updates])
    for n in big:
        updates[n] = _adamw(weights[n], grads[n], m_in[n], v_in[n], "adamw_" + n, after)
    g_w_in_t = exchanges.w_in_end([updates[n][0] for n in big])
    grads["w_in"] = g_w_in_t.T
    updates["w_in"] = [t.T for t in _adamw(w_in.T, g_w_in_t, m_w_in.T, v_w_in.T, "adamw_w_in", g_w_in_t)]
    return (loss, grad_x, *[grads[n] for n in names], *[updates[n][0] for n in names],
            *[updates[n][1] for n in names], *[updates[n][2] for n in names])
```

```python
import functools
import math

import jax
import jax.numpy as jnp
import numpy as np
from jax import lax
from jax.experimental import pallas as pl
from jax.experimental.pallas import tpu as pltpu

F32 = jnp.float32
BF16 = jnp.bfloat16

D = 1024
CW = 31
HD = 64
NQ = 16
NKV = 2
GROUP = NQ // NKV
BLK = 128
DFF = 2816
EPS = 1e-6
NEG = -1e30
IN_COLS = 5376
SCALE = 1.0 / math.sqrt(HD)

LR, B1, B2, AEPS, WD, STEP = 0.001, 0.9, 0.999, 1e-08, 0.01, 10

MIB = 1024 * 1024
MESH = pl.DeviceIdType.MESH

ROWS_W_IN = 1344
ROWS_SQ = 256
ROWS_FFN_IN = 1408
ROWS_DOWN = 704
ROWS_MAT = ROWS_W_IN + 3 * ROWS_SQ + ROWS_FFN_IN + ROWS_DOWN
ROWS_DW = 32
ROWS_PACK = ROWS_MAT + ROWS_DW
VEC_ROWS = 40


def _sig(x):
    return 1.0 / (1.0 + jnp.exp(-x))


def _dot(a, b):
    return jnp.dot(a, b, preferred_element_type=F32)


def _dot_nt(a, b):
    return lax.dot_general(a, b, (((1,), (1,)), ((), ())), preferred_element_type=F32)


def _dot_tn(a, b):
    return lax.dot_general(a, b, (((0,), (0,)), ((), ())), preferred_element_type=F32)


def _params(sem, vmem_mib=48):
    return pltpu.CompilerParams(dimension_semantics=sem, vmem_limit_bytes=vmem_mib * MIB)


def _resident(shape):
    return pl.BlockSpec(shape, lambda *_: (0,) * len(shape), pipeline_mode=pl.Buffered(1))


def _whole(shape):
    return pl.BlockSpec(shape, lambda *_: (0,) * len(shape))


def _rows(tm, cols, col_block=0):
    return pl.BlockSpec((tm, cols), lambda i: (i, col_block))


def _tap_phases():
    return [(phase, list(range(phase, CW, 8))) for phase in range(8)]


def _shift_copies(dst, src, base):
    for phase, taps in _tap_phases():
        n = dst.shape[1] - 8 * (4 - len(taps))
        dst[phase, 0:n, :] = src[base + phase:base + phase + n, :]


def _in_proj_conv_fwd(x2, g1, w_in, dw, dwb, lng, lnb, wc, n_seq, S):
    T = n_seq * S
    tc = min(256, S)
    nt = S // tc

    def body(x_ref, g_ref, w_ref, dw_ref, dwb_ref, lng_ref, lnb_ref, wc_ref, xn_ref, ag_ref, q_ref, kv_ref, gg_ref,
             h1_ref, h3_ref, yc_ref, ext, sh):
        i = pl.program_id(1)

        @pl.when(i == 0)
        def _():
            ext[0:32, :] = jnp.zeros((32, D), F32)

        x = x_ref[...]
        rstd = lax.rsqrt(jnp.mean(x * x, axis=-1, keepdims=True) + EPS)
        xn = (x * rstd * g_ref[...]).astype(BF16)
        xn_ref[...] = xn
        ag = _dot_nt(xn, w_ref[0:2048, :])
        ag_ref[...] = ag
        ext[32:32 + tc, :] = ag[:, 0:D] * _sig(ag[:, D:2 * D])
        q_ref[...] = _dot_nt(xn, w_ref[2048:3072, :])
        kv_ref[...] = _dot_nt(xn, w_ref[3072:3328, :])
        gg_ref[...] = _dot_nt(xn, w_ref[3328:5376, :])
        _shift_copies(sh, ext, 2)
        for cb in range(D // 128):
            cs = slice(cb * 128, (cb + 1) * 128)
            acc = jnp.broadcast_to(dwb_ref[:, cs], (tc, 128))
            for phase, taps in _tap_phases():
                for m, j in enumerate(taps):
                    acc = acc + dw_ref[j:j + 1, cs] * sh[phase, 8 * m:8 * m + tc, cs]
            h1_ref[:, cs] = acc
        ext[0:32, :] = ext[tc:tc + 32, :]
        h1 = h1_ref[...]
        mu = jnp.mean(h1, axis=-1, keepdims=True)
        cen = h1 - mu
        var = jnp.mean(cen * cen, axis=-1, keepdims=True)
        h2 = cen * lax.rsqrt(var + EPS) * lng_ref[...] + lnb_ref[...]
        h3 = (h2 * _sig(h2)).astype(BF16)
        h3_ref[...] = h3
        yc_ref[...] = _dot(h3, wc_ref[...])

    tile = lambda cols: pl.BlockSpec((tc, cols), lambda b, i: (b * nt + i, 0))
    shape = lambda cols, dtype: jax.ShapeDtypeStruct((T, cols), dtype)
    return pl.pallas_call(
        body, name="in_proj_conv_fwd", grid=(n_seq, nt),
        in_specs=[tile(D), _resident((1, D)), _resident((IN_COLS, D)), _resident((32, D)), _resident((1, D)),
                  _resident((1, D)), _resident((1, D)), _resident((D, D))],
        out_specs=[tile(D), tile(2 * D), tile(D), tile(256), tile(2 * D), tile(D), tile(D), tile(D)],
        out_shape=[shape(D, BF16), shape(2 * D, F32), shape(D, F32), shape(256, F32), shape(2 * D, F32),
                   shape(D, F32), shape(D, BF16), shape(D, F32)],
        scratch_shapes=[pltpu.VMEM((32 + tc, D), F32), pltpu.VMEM((8, tc + 24, D), F32)],
        compiler_params=_params(("parallel", "arbitrary"), 56),
    )(x2, g1, w_in, dw, dwb, lng, lnb, wc)


def _attn_consts():
    k = np.arange(BLK)[:, None]
    i = np.arange(GROUP * BLK)[None, :] % BLK
    from_prev = k > i
    dist = np.where(from_prev, i + BLK - k, i - k).astype(np.float32)
    head = np.arange(GROUP * BLK)[None, :] // BLK
    bias = []
    for kh in range(NKV):
        slope = np.exp2(-8.0 * (kh * GROUP + head + 1) / NQ).astype(np.float32)
        bias.append(-slope * dist)
    return jnp.asarray(from_prev.astype(np.float32)), jnp.asarray(np.stack(bias))


def _heads_to_lanes(t, kh):
    return jnp.concatenate([t[(kh * GROUP + g) * HD:(kh * GROUP + g + 1) * HD, :] for g in range(GROUP)], axis=1)


def _lanes_to_heads(t):
    return jnp.concatenate([t[:, g * BLK:(g + 1) * BLK] for g in range(GROUP)], axis=0)


def _rms64(t):
    return lax.rsqrt(jnp.mean(t * t, axis=-1, keepdims=True) + EPS)


def _attn_probs(kh, n, q_t, kvc_ref, kvp_ref, qg_ref, kg_ref, tri_ref, bias_ref, sink_ref):
    ks = slice(kh * HD, (kh + 1) * HD)
    vs = slice(2 * HD + kh * HD, 2 * HD + (kh + 1) * HD)
    kp, kc = kvp_ref[:, ks], kvc_ref[:, ks]
    kpb = (kp * _rms64(kp) * kg_ref[...]).astype(BF16)
    kcb = (kc * _rms64(kc) * kg_ref[...]).astype(BF16)
    qs = _heads_to_lanes(q_t, kh)
    rq = lax.rsqrt(jnp.mean(qs * qs, axis=0, keepdims=True) + EPS)
    qy = qs * rq
    qhb = (qy * (qg_ref[...] * SCALE)).astype(BF16)
    from_prev = tri_ref[...] > 0.5
    no_prev = jnp.where(n > 0, 0.0, NEG)
    s = jnp.where(from_prev, _dot(kpb, qhb) + no_prev, _dot(kcb, qhb)) + bias_ref[kh]
    sink = sink_ref[kh:kh + 1, :]
    m = jnp.maximum(jnp.max(s, axis=0, keepdims=True), sink)
    e = jnp.exp(s - m)
    es = jnp.exp(sink - m)
    rz = 1.0 / (jnp.sum(e, axis=0, keepdims=True) + es)
    prev_mask = tri_ref[...].astype(BF16)
    return e * rz, es * rz, prev_mask, qhb, kpb, kcb, kvp_ref[:, vs].astype(BF16), kvc_ref[:, vs].astype(BF16), qy, rq


def _unfold(t, prev_mask):
    prev = t * prev_mask
    return prev, t - prev


def _attn_specs(n_seq, S):
    nb = S // BLK
    cur = lambda cols: pl.BlockSpec((BLK, cols), lambda b, n: (b * nb + n, 0))
    prev = lambda cols: pl.BlockSpec((BLK, cols), lambda b, n: (b * nb + jnp.maximum(n - 1, 0), 0))
    consts = [_resident((HD, GROUP * BLK)), _resident((1, HD)), _resident((BLK, GROUP * BLK)),
              _resident((NKV, BLK, GROUP * BLK)), _resident((NKV, GROUP * BLK))]
    return nb, cur, prev, consts


def _attn_fwd(q, kv, qg_cols, kg, sink_rows, n_seq, S):
    T = n_seq * S
    nb, cur, prev, consts = _attn_specs(n_seq, S)
    tri, bias = _attn_consts()

    def body(q_ref, kvc_ref, kvp_ref, qg_ref, kg_ref, tri_ref, bias_ref, sink_ref, o_ref):
        n = pl.program_id(1)
        q_t = q_ref[...].T
        o_t = []
        for kh in range(NKV):
            p, _, prev_mask, _, _, _, vpb, vcb, _, _ = _attn_probs(kh, n, q_t, kvc_ref, kvp_ref, qg_ref, kg_ref,
                                                                   tri_ref, bias_ref, sink_ref)
            pp, pc = _unfold(p.astype(BF16), prev_mask)
            o_t.append(_lanes_to_heads(_dot_tn(vpb, pp) + _dot_tn(vcb, pc)))
        o_ref[...] = jnp.concatenate(o_t, axis=0).T.astype(BF16)

    return pl.pallas_call(
        body, name="attn_fwd", grid=(n_seq, nb),
        in_specs=[cur(D), cur(256), prev(256)] + consts,
        out_specs=cur(D),
        out_shape=jax.ShapeDtypeStruct((T, D), BF16),
        compiler_params=_params(("parallel", "parallel")),
    )(q, kv, kv, qg_cols, kg, tri, bias, sink_rows)


def _merge_fwd(x2, gg, yc, o, wa, wm):
    T = x2.shape[0]
    tm = min(512, T)

    def body(x_ref, gg_ref, yc_ref, o_ref, wa_ref, wm_ref, mix_ref, h_ref):
        ya = _dot(o_ref[...], wa_ref[...])
        mix = (_sig(gg_ref[:, 0:D]) * yc_ref[...] + _sig(gg_ref[:, D:2 * D]) * ya).astype(BF16)
        mix_ref[...] = mix
        h_ref[...] = x_ref[...] + _dot(mix, wm_ref[...])

    return pl.pallas_call(
        body, name="merge_fwd", grid=(T // tm,),
        in_specs=[_rows(tm, D), _rows(tm, 2 * D), _rows(tm, D), _rows(tm, D), _resident((D, D)), _resident((D, D))],
        out_specs=[_rows(tm, D), _rows(tm, D)],
        out_shape=[jax.ShapeDtypeStruct((T, D), BF16), jax.ShapeDtypeStruct((T, D), F32)],
        compiler_params=_params(("parallel",)),
    )(x2, gg, yc, o, wa, wm)


FF_CHUNK = DFF // 2


def _ffn(h, tgt, g2, wf, wd):
    T = h.shape[0]
    tm = min(256, T)

    def body(h_ref, t_ref, g_ref, wf_ref, wd_ref, dh_ref, dhb_ref, hn_ref, act_ref, dout_ref, dgu_ref, st_ref,
             gsc, usc):
        @pl.when(pl.program_id(0) == 0)
        def _():
            st_ref[...] = jnp.zeros((8, D), F32)

        hh = h_ref[...]
        rstd = lax.rsqrt(jnp.mean(hh * hh, axis=-1, keepdims=True) + EPS)
        hhat = hh * rstd
        hn = (hhat * g_ref[...]).astype(BF16)
        hn_ref[...] = hn
        out = hh
        for c in range(DFF // FF_CHUNK):
            cs = slice(c * FF_CHUNK, (c + 1) * FF_CHUNK)
            us = slice(DFF + c * FF_CHUNK, DFF + (c + 1) * FF_CHUNK)
            g = _dot(hn, wf_ref[c])
            u = _dot(hn, wf_ref[2 + c])
            gsc[:, cs] = g
            usc[:, cs] = u
            act = (g * _sig(g) * u).astype(BF16)
            act_ref[:, cs] = act
            out = out + _dot(act, wd_ref[cs, :])
        err = out - t_ref[...]
        dout = err * (1.0 / D)
        doutb = dout.astype(BF16)
        dout_ref[...] = doutb
        dhn = jnp.zeros((tm, D), F32)
        for c in range(DFF // FF_CHUNK):
            cs = slice(c * FF_CHUNK, (c + 1) * FF_CHUNK)
            us = slice(DFF + c * FF_CHUNK, DFF + (c + 1) * FF_CHUNK)
            g = gsc[:, cs]
            u = usc[:, cs]
            dact = _dot_nt(doutb, wd_ref[cs, :])
            sg = _sig(g)
            dg = (dact * u * (sg * (1.0 + g * (1.0 - sg)))).astype(BF16)
            du = (dact * (g * sg)).astype(BF16)
            dgu_ref[:, cs] = dg
            dgu_ref[:, us] = du
            dhn = dhn + _dot_nt(dg, wf_ref[c]) + _dot_nt(du, wf_ref[2 + c])
        st_ref[0:1, :] += jnp.sum(dhn * hhat, axis=0, keepdims=True)
        st_ref[1:2, :] += jnp.sum(err * err, axis=0, keepdims=True)
        dhh = dhn * g_ref[...]
        dh = dout + rstd * (dhh - hhat * jnp.mean(dhh * hhat, axis=-1, keepdims=True))
        dh_ref[...] = dh
        dhb_ref[...] = dh.astype(BF16)

    return pl.pallas_call(
        body, name="ffn_fwd_bwd", grid=(T // tm,),
        in_specs=[_rows(tm, D), _rows(tm, D), _resident((1, D)), _resident((4, D, FF_CHUNK)), _resident((DFF, D))],
        out_specs=[_rows(tm, D), _rows(tm, D), _rows(tm, D), _rows(tm, DFF), _rows(tm, D), _rows(tm, 2 * DFF),
                   _whole((8, D))],
        out_shape=[jax.ShapeDtypeStruct((T, D), F32), jax.ShapeDtypeStruct((T, D), BF16),
                   jax.ShapeDtypeStruct((T, D), BF16), jax.ShapeDtypeStruct((T, DFF), BF16),
                   jax.ShapeDtypeStruct((T, D), BF16), jax.ShapeDtypeStruct((T, 2 * DFF), BF16),
                   jax.ShapeDtypeStruct((8, D), F32)],
        scratch_shapes=[pltpu.VMEM((tm, DFF), F32), pltpu.VMEM((tm, DFF), F32)],
        compiler_params=_params(("arbitrary",), 56),
    )(h, tgt, g2, wf, wd)


def _merge_bwd(dhb, gg, yc, o, wm, wa, wc):
    T = dhb.shape[0]
    tm = min(256, T)

    def body(dh_ref, gg_ref, yc_ref, o_ref, wm_ref, wa_ref, wc_ref, dgg_ref, dyc_ref, dya_ref, do_ref, dh3_ref):
        dmix = _dot_nt(dh_ref[...], wm_ref[...])
        gc = _sig(gg_ref[:, 0:D])
        ga = _sig(gg_ref[:, D:2 * D])
        yc = yc_ref[...]
        ya = _dot(o_ref[...], wa_ref[...])
        dgg_ref[:, 0:D] = (dmix * yc * gc * (1.0 - gc)).astype(BF16)
        dgg_ref[:, D:2 * D] = (dmix * ya * ga * (1.0 - ga)).astype(BF16)
        dyc = (dmix * gc).astype(BF16)
        dya = (dmix * ga).astype(BF16)
        dyc_ref[...] = dyc
        dya_ref[...] = dya
        do_ref[...] = _dot_nt(dya, wa_ref[...]).astype(BF16)
        dh3_ref[...] = _dot_nt(dyc, wc_ref[...])

    return pl.pallas_call(
        body, name="merge_bwd", grid=(T // tm,),
        in_specs=[_rows(tm, D), _rows(tm, 2 * D), _rows(tm, D), _rows(tm, D), _resident((D, D)), _resident((D, D)),
                  _resident((D, D))],
        out_specs=[_rows(tm, 2 * D), _rows(tm, D), _rows(tm, D), _rows(tm, D), _rows(tm, D)],
        out_shape=[jax.ShapeDtypeStruct((T, 2 * D), BF16), jax.ShapeDtypeStruct((T, D), BF16),
                   jax.ShapeDtypeStruct((T, D), BF16), jax.ShapeDtypeStruct((T, D), BF16),
                   jax.ShapeDtypeStruct((T, D), F32)],
        compiler_params=_params(("parallel",)),
    )(dhb, gg, yc, o, wm, wa, wc)


def _attn_bwd(q, kv, do, qg_cols, kg, sink_rows, n_seq, S):
    T = n_seq * S
    nb, cur, prev, consts = _attn_specs(n_seq, S)
    tri, bias = _attn_consts()

    def body(q_ref, kvc_ref, kvp_ref, do_ref, qg_ref, kg_ref, tri_ref, bias_ref, sink_ref, dq_ref, dkc_ref, dkp_ref,
             dqg_ref, dsk_ref):
        n = pl.program_id(1)

        @pl.when((pl.program_id(0) == 0) & (n == 0))
        def _():
            dqg_ref[...] = jnp.zeros((HD, BLK), F32)
            dsk_ref[...] = jnp.zeros((8, 128), F32)

        lane = lax.broadcasted_iota(jnp.int32, (1, 128), 1)
        q_t = q_ref[...].T
        do_t = do_ref[...].astype(F32).T
        dq_t = []
        for kh in range(NKV):
            p, ps, prev_mask, qhb, kpb, kcb, vpb, vcb, qy, rq = _attn_probs(
                kh, n, q_t, kvc_ref, kvp_ref, qg_ref, kg_ref, tri_ref, bias_ref, sink_ref)
            dob = _heads_to_lanes(do_t, kh).astype(BF16)
            dp = jnp.where(tri_ref[...] > 0.5, _dot(vpb, dob), _dot(vcb, dob))
            delta = jnp.sum(p * dp, axis=0, keepdims=True)
            dsp, dsc = _unfold((p * (dp - delta)).astype(BF16), prev_mask)
            pp, pc = _unfold(p.astype(BF16), prev_mask)
            dsink = -ps * delta
            dqh = (_dot_tn(kpb, dsp) + _dot_tn(kcb, dsc)) * SCALE
            dqg = dqh * qy
            dqg_ref[...] += sum(dqg[:, g * BLK:(g + 1) * BLK] for g in range(GROUP))
            dy = dqh * qg_ref[...]
            dq_t.append(_lanes_to_heads(rq * (dy - qy * jnp.mean(dy * qy, axis=0, keepdims=True))))
            row = jnp.zeros((1, 128), F32)
            for g in range(GROUP):
                h = kh * GROUP + g
                row = row + jnp.where(lane == h, jnp.sum(dsink[:, g * BLK:(g + 1) * BLK], axis=1, keepdims=True), 0.0)
            dsk_ref[0:1, :] += row
            ks = slice(kh * HD, (kh + 1) * HD)
            vs = slice(2 * HD + kh * HD, 2 * HD + (kh + 1) * HD)
            dkp_ref[:, ks] = _dot_nt(dsp, qhb)
            dkc_ref[:, ks] = _dot_nt(dsc, qhb)
            dkp_ref[:, vs] = _dot_nt(pp, dob)
            dkc_ref[:, vs] = _dot_nt(pc, dob)
        dq_ref[...] = jnp.concatenate(dq_t, axis=0).T.astype(BF16)

    return pl.pallas_call(
        body, name="attn_bwd", grid=(n_seq, nb),
        in_specs=[cur(D), cur(256), prev(256), cur(D)] + consts,
        out_specs=[cur(D), cur(256), cur(256), _whole((HD, BLK)), _whole((8, 128))],
        out_shape=[jax.ShapeDtypeStruct((T, D), BF16), jax.ShapeDtypeStruct((T, 256), F32),
                   jax.ShapeDtypeStruct((T, 256), F32), jax.ShapeDtypeStruct((HD, BLK), F32),
                   jax.ShapeDtypeStruct((8, 128), F32)],
        compiler_params=_params(("arbitrary", "arbitrary")),
    )(q, kv, kv, do, qg_cols, kg, tri, bias, sink_rows)


def _kv_bwd(kv, dkc, dkp, kg_cols, n_seq, S):
    T = n_seq * S
    seq = lambda cols: pl.BlockSpec((S, cols), lambda b: (b, 0))

    def body(kv_ref, dkc_ref, dkp_ref, kg_ref, dkv_ref, dkg_ref):
        @pl.when(pl.program_id(0) == 0)
        def _():
            dkg_ref[...] = jnp.zeros((HD, BLK), F32)

        from_next = jnp.concatenate([dkp_ref[BLK:S, :], jnp.zeros((BLK, 256), F32)], axis=0)
        d = dkc_ref[...] + from_next
        d_t = d[:, 0:2 * HD].T
        k_t = kv_ref[:, 0:2 * HD].T
        kg = jnp.concatenate([kg_ref[...]] * (S // BLK), axis=1)
        out = []
        for kh in range(NKV):
            k = k_t[kh * HD:(kh + 1) * HD, :]
            r = lax.rsqrt(jnp.mean(k * k, axis=0, keepdims=True) + EPS)
            y = k * r
            dkh = d_t[kh * HD:(kh + 1) * HD, :]
            dkg = dkh * y
            dkg_ref[...] += sum(dkg[:, j * BLK:(j + 1) * BLK] for j in range(S // BLK))
            dy = dkh * kg
            out.append(r * (dy - y * jnp.mean(dy * y, axis=0, keepdims=True)))
        dkv_ref[:, 0:2 * HD] = jnp.concatenate(out, axis=0).T.astype(BF16)
        dkv_ref[:, 2 * HD:4 * HD] = d[:, 2 * HD:4 * HD].astype(BF16)

    return pl.pallas_call(
        body, name="kv_bwd", grid=(n_seq,),
        in_specs=[seq(256), seq(256), seq(256), _resident((HD, BLK))],
        out_specs=[seq(256), _whole((HD, BLK))],
        out_shape=[jax.ShapeDtypeStruct((T, 256), BF16), jax.ShapeDtypeStruct((HD, BLK), F32)],
        compiler_params=_params(("arbitrary",)),
    )(kv, dkc, dkp, kg_cols)


def _conv_bwd(h1, dh3, ag, dw, lng, lnb, n_seq, S):
    T = n_seq * S
    tc = min(256, S)
    nt = S // tc

    def body(h1_ref, dh3_ref, a_ref, gt_ref, dw_ref, lng_ref, lnb_ref, dag_ref, ddw_ref, st_ref, extd, acc8, shd):
        i = pl.program_id(1)

        @pl.when((pl.program_id(0) == 0) & (i == 0))
        def _():
            acc8[...] = jnp.zeros((CW * 8, D), F32)
            st_ref[...] = jnp.zeros((8, D), F32)

        @pl.when(i == 0)
        def _():
            extd[tc:tc + 32, :] = jnp.zeros((32, D), F32)

        h1 = h1_ref[...]
        mu = jnp.mean(h1, axis=-1, keepdims=True)
        cen = h1 - mu
        rstd = lax.rsqrt(jnp.mean(cen * cen, axis=-1, keepdims=True) + EPS)
        xh = cen * rstd
        h2 = xh * lng_ref[...] + lnb_ref[...]
        sg = _sig(h2)
        dh2 = dh3_ref[...] * (sg * (1.0 + h2 * (1.0 - sg)))
        st_ref[1:2, :] += jnp.sum(dh2 * xh, axis=0, keepdims=True)
        st_ref[2:3, :] += jnp.sum(dh2, axis=0, keepdims=True)
        dxh = dh2 * lng_ref[...]
        dh1 = rstd * (dxh - jnp.mean(dxh, axis=-1, keepdims=True)
                      - xh * jnp.mean(dxh * xh, axis=-1, keepdims=True))
        st_ref[0:1, :] += jnp.sum(dh1, axis=0, keepdims=True)
        extd[0:tc, :] = dh1
        _shift_copies(shd, extd, 0)
        for cb in range(D // 128):
            cs = slice(cb * 128, (cb + 1) * 128)
            for rb in range(tc // 128):
                rs = slice(rb * 128, (rb + 1) * 128)
                a = a_ref[rs, cs]
                sgt = _sig(gt_ref[rs, cs])
                h0 = a * sgt
                acc = jnp.zeros((128, 128), F32)
                for phase, offs in _tap_phases():
                    for m, o in enumerate(offs):
                        j = CW - 1 - o
                        ahead = shd[phase, rb * 128 + 8 * m:rb * 128 + 8 * m + 128, cs]
                        acc = acc + dw_ref[j:j + 1, cs] * ahead
                        acc8[j * 8:(j + 1) * 8, cs] += jnp.sum((h0 * ahead).reshape(16, 8, 128), axis=0)
                dag_ref[rs, cs] = (acc * sgt).astype(BF16)
                dag_ref[rs, cb * 128 + D:(cb + 1) * 128 + D] = (acc * a * sgt * (1.0 - sgt)).astype(BF16)
        extd[tc:tc + 32, :] = extd[0:32, :]

        @pl.when((pl.program_id(0) == n_seq - 1) & (i == nt - 1))
        def _():
            for j in range(CW):
                ddw_ref[j:j + 1, :] = jnp.sum(acc8[j * 8:(j + 1) * 8, :], axis=0, keepdims=True)
            ddw_ref[CW:32, :] = jnp.zeros((32 - CW, D), F32)

    tile = lambda col: pl.BlockSpec((tc, D), lambda b, i: (b * nt + (nt - 1 - i), col))
    return pl.pallas_call(
        body, name="conv_bwd", grid=(n_seq, nt),
        in_specs=[tile(0), tile(0), tile(0), tile(1), _resident((32, D)), _resident((1, D)), _resident((1, D))],
        out_specs=[pl.BlockSpec((tc, 2 * D), lambda b, i: (b * nt + (nt - 1 - i), 0)), _whole((32, D)),
                   _whole((8, D))],
        out_shape=[jax.ShapeDtypeStruct((T, 2 * D), BF16), jax.ShapeDtypeStruct((32, D), F32),
                   jax.ShapeDtypeStruct((8, D), F32)],
        scratch_shapes=[pltpu.VMEM((tc + 32, D), F32), pltpu.VMEM((CW * 8, D), F32),
                        pltpu.VMEM((8, tc + 24, D), F32)],
        compiler_params=_params(("arbitrary", "arbitrary")),
    )(h1, dh3, ag, ag, dw, lng, lnb)


def _in_proj_bwd(dag, dq, dkv, dgg, dh, x2, g1, w_in):
    T = x2.shape[0]
    tm = min(512, T)

    def body(dag_ref, dq_ref, dkv_ref, dgg_ref, dh_ref, x_ref, g_ref, w_ref, dx_ref, dg_ref):
        @pl.when(pl.program_id(0) == 0)
        def _():
            dg_ref[...] = jnp.zeros((1, D), F32)

        dxn = (_dot(dag_ref[...], w_ref[0:2048, :]) + _dot(dq_ref[...], w_ref[2048:3072, :])
               + _dot(dkv_ref[...], w_ref[3072:3328, :]) + _dot(dgg_ref[...], w_ref[3328:5376, :]))
        x = x_ref[...]
        rstd = lax.rsqrt(jnp.mean(x * x, axis=-1, keepdims=True) + EPS)
        xh = x * rstd
        dg_ref[...] += jnp.sum(dxn * xh, axis=0, keepdims=True)
        dxh = dxn * g_ref[...]
        dx_ref[...] = dh_ref[...] + rstd * (dxh - xh * jnp.mean(dxh * xh, axis=-1, keepdims=True))

    return pl.pallas_call(
        body, name="in_proj_bwd", grid=(T // tm,),
        in_specs=[_rows(tm, 2 * D), _rows(tm, D), _rows(tm, 256), _rows(tm, 2 * D), _rows(tm, D), _rows(tm, D),
                  _resident((1, D)), _resident((IN_COLS, D))],
        out_specs=[_rows(tm, D), _whole((1, D))],
        out_shape=[jax.ShapeDtypeStruct((T, D), F32), jax.ShapeDtypeStruct((1, D), F32)],
        compiler_params=_params(("arbitrary",)),
    )(dag, dq, dkv, dgg, dh, x2, g1, w_in)


def _tn_matmul(a, b, name, column_blocks=False):
    T, K = a.shape
    N = b.shape[1]
    tk = K if K <= 1024 else K // 2
    tn = N if N <= 1024 else (1024 if N % 1024 == 0 and not column_blocks else N // 4)
    tt = min(2048, T)
    assert K % tk == 0 and N % tn == 0 and T % tt == 0 and tk % 128 == 0 and tn % 128 == 0

    def body(a_ref, b_ref, o_ref):
        @pl.when(pl.program_id(2) == 0)
        def _():
            o_ref[...] = jnp.zeros((tk, tn), F32)

        o_ref[...] += _dot_tn(a_ref[...], b_ref[...])

    return pl.pallas_call(
        body, name=name, grid=(K // tk, N // tn, T // tt),
        in_specs=[pl.BlockSpec((tt, tk), lambda i, j, t: (t, i)), pl.BlockSpec((tt, tn), lambda i, j, t: (t, j))],
        out_specs=(pl.BlockSpec((None, tk, tn), lambda i, j, t: (j, i, 0)) if column_blocks
                   else pl.BlockSpec((tk, tn), lambda i, j, t: (i, j))),
        out_shape=jax.ShapeDtypeStruct((N // tn, K, tn) if column_blocks else (K, N), F32),
        compiler_params=_params(("parallel", "parallel", "arbitrary")),
    )(a, b)


def _place():
    x, y, c = lax.axis_index("x"), lax.axis_index("y"), lax.axis_index("c")
    chips = [(1 - x, y), (x, 1 - y), (1 - x, 1 - y)]
    return x, y, c, chips


def _own_slot(slots, mine):
    chip = 2 * lax.axis_index("x") + lax.axis_index("y")
    return lax.dynamic_update_slice(slots, mine[None], (chip,) + (0,) * mine.ndim)


def _row_tile(rows, unit):
    return max(t for t in range(unit, 513, unit) if rows % t == 0)


def _gather_weights(pack):
    rows = pack.shape[0]
    half = rows // 2

    def body(src, dst, token, send_sems, recv_sems):
        x, y, c, chips = _place()

        def piece(px, py, pc):
            return dst.at[2 * px + py, pl.ds(pc * half, half), :]

        def copy(k, block, to, from_src=False):
            return pltpu.make_async_remote_copy(
                src_ref=src.at[pl.ds(c * half, half), :] if from_src else piece(*block), dst_ref=piece(*block),
                send_sem=send_sems.at[k], recv_sem=recv_sems.at[k], device_id=to, device_id_type=MESH)

        first = [copy(k, (x, y, c), (*chip, c), from_src=True) for k, chip in enumerate(chips)]
        for cp in first:
            cp.start()
        passed = [copy(3 + k, (*chip, c), (x, y, 1 - c)) for k, chip in enumerate(chips)]
        for k, chip in enumerate(chips):
            copy(k, (*chip, c), (x, y, c)).wait_recv()
            passed[k].start()
        for k, chip in enumerate(chips):
            copy(3 + k, (*chip, 1 - c), (x, y, c)).wait_recv()
        for cp in first + passed:
            cp.wait_send()
        token[...] = jnp.zeros((8, 128), F32)

    got, token = pl.pallas_call(
        body, name="gather_weights",
        in_specs=[pl.BlockSpec(memory_space=pl.ANY)],
        out_specs=[pl.BlockSpec(memory_space=pl.ANY), pl.BlockSpec(memory_space=pltpu.VMEM)],
        out_shape=[jax.ShapeDtypeStruct((4, rows, D), pack.dtype), jax.ShapeDtypeStruct((8, 128), F32)],
        scratch_shapes=[pltpu.SemaphoreType.DMA((6,)), pltpu.SemaphoreType.DMA((6,))],
        compiler_params=pltpu.CompilerParams(has_side_effects=True),
    )(pack)
    return _own_slot(got, pack), token[0, 0]


def _add_halves(g, got, c_idx, name="grad_add_halves"):
    rows, w = g.shape[1], g.shape[2]
    half = rows // 2
    tr = _row_tile(half, 16)
    nt = half // tr

    def body(c_ref, g_ref, r_ref, o_ref):
        o_ref[...] = (g_ref[...] + r_ref[...]).astype(BF16)

    return pl.pallas_call(
        body, name=name,
        grid_spec=pltpu.PrefetchScalarGridSpec(
            num_scalar_prefetch=1, grid=(4, nt),
            in_specs=[pl.BlockSpec((1, tr, w), lambda q, i, c_ref: (q, c_ref[0] * nt + i, 0)),
                      pl.BlockSpec((1, tr, w), lambda q, i, c_ref: (q, i, 0))],
            out_specs=pl.BlockSpec((1, tr, w), lambda q, i, c_ref: (q, i, 0))),
        out_shape=jax.ShapeDtypeStruct((4, half, w), BF16),
        compiler_params=_params(("parallel", "parallel")),
    )(c_idx, g, got)


def _own_piece(p):
    chip = 2 * lax.axis_index("x") + lax.axis_index("y")
    return lax.dynamic_index_in_dim(p, chip, axis=0, keepdims=False)


def _sum_chips(r, c_idx, name="grad_sum_chips"):
    half, w = r.shape[1], r.shape[2]
    tr = _row_tile(half, 16)
    nt = half // tr

    def body(c_ref, r_ref, o_ref):
        acc = r_ref[0].astype(F32)
        for q in range(1, 4):
            acc = acc + r_ref[q].astype(F32)
        o_ref[...] = acc

    return pl.pallas_call(
        body, name=name,
        grid_spec=pltpu.PrefetchScalarGridSpec(
            num_scalar_prefetch=1, grid=(nt,),
            in_specs=[pl.BlockSpec((4, tr, w), lambda i, c_ref: (0, i, 0))],
            out_specs=pl.BlockSpec((tr, w), lambda i, c_ref: (c_ref[0] * nt + i, 0))),
        out_shape=jax.ShapeDtypeStruct((2 * half, w), F32),
        compiler_params=_params(("parallel",)),
    )(c_idx, r)


def _join_halves(f):
    half = f.shape[0] // 2

    def body(src, dst, send_sem, recv_sem):
        x, y, c, _ = _place()
        cp = pltpu.make_async_remote_copy(
            src_ref=src.at[pl.ds(c * half, half), :], dst_ref=dst.at[pl.ds(c * half, half), :], send_sem=send_sem,
            recv_sem=recv_sem, device_id=(x, y, 1 - c), device_id_type=MESH)
        cp.start()
        pltpu.make_async_remote_copy(
            src_ref=src.at[pl.ds(c * half, half), :], dst_ref=dst.at[pl.ds((1 - c) * half, half), :],
            send_sem=send_sem, recv_sem=recv_sem, device_id=(x, y, 1 - c), device_id_type=MESH).wait_recv()
        cp.wait_send()

    return pl.pallas_call(
        body, name="grad_join_halves",
        in_specs=[pl.BlockSpec(memory_space=pl.ANY)], out_specs=pl.BlockSpec(memory_space=pl.ANY),
        out_shape=jax.ShapeDtypeStruct(f.shape, f.dtype), input_output_aliases={0: 0},
        scratch_shapes=[pltpu.SemaphoreType.DMA, pltpu.SemaphoreType.DMA],
        compiler_params=pltpu.CompilerParams(has_side_effects=True),
    )(f)


_HBM = pl.BlockSpec(memory_space=pltpu.HBM)
_SEM = pl.BlockSpec(memory_space=pltpu.SEMAPHORE)
_EFFECT = pltpu.SideEffectType.DATAFLOW_SIDE_EFFECTING


def _start_copies(name, bufs, n_sems, plan):
    nb = len(bufs)

    def body(*refs):
        for cp in plan(refs[:nb], refs[nb], refs[nb + 1])[0]:
            cp.start()
        refs[-1][...] = jnp.zeros((8, 128), F32)

    out = pl.pallas_call(
        body, name=name,
        out_shape=(pltpu.SemaphoreType.DMA((n_sems,)), pltpu.SemaphoreType.DMA((n_sems,)),
                   *[pltpu.HBM(b.shape, b.dtype) for b in bufs], jax.ShapeDtypeStruct((8, 128), F32)),
        in_specs=[_HBM] * nb, out_specs=(_SEM, _SEM, *[_HBM] * nb, pl.BlockSpec(memory_space=pltpu.VMEM)),
        input_output_aliases={i: 2 + i for i in range(nb)},
        compiler_params=pltpu.CompilerParams(has_side_effects=_EFFECT),
    )(*[pltpu.with_memory_space_constraint(b, pltpu.HBM) for b in bufs])
    return out[0], out[1], list(out[2:2 + nb]), out[-1]


def _wait_copies(name, send_sems, recv_sems, bufs, after, plan):
    nb = len(bufs)

    def body(*refs):
        _, sends, recvs = plan(refs[:nb], refs[nb], refs[nb + 1])
        for cp in sends:
            cp.wait_send()
        for cp in recvs:
            cp.wait_recv()

    out = pl.pallas_call(
        body, name=name,
        out_shape=tuple(pltpu.HBM(b.shape, b.dtype) for b in bufs),
        in_specs=[_HBM] * nb + [_SEM, _SEM] + [pl.BlockSpec(memory_space=pl.ANY)] * len(after),
        out_specs=tuple([_HBM] * nb),
        input_output_aliases={i: i for i in range(nb)},
        compiler_params=pltpu.CompilerParams(has_side_effects=_EFFECT),
    )(*bufs, send_sems, recv_sems, *after)
    return list(out)


def _plan_gather_direct(halves):
    n = len(halves)

    def plan(refs, send_sems, recv_sems):
        x, y, c, chips = _place()
        starts, recvs = [], []
        for b, half in enumerate(halves):
            src, land = refs[b], refs[n + b]
            for k, (cx, cy) in enumerate(chips):
                for d in range(2):
                    other = c if d == 0 else 1 - c
                    i = 6 * b + 2 * k + d
                    starts.append(pltpu.make_async_remote_copy(
                        src_ref=src.at[pl.ds(c * half, half), :],
                        dst_ref=land.at[2 * x + y, pl.ds(c * half, half), :],
                        send_sem=send_sems.at[i], recv_sem=recv_sems.at[i], device_id=(cx, cy, other),
                        device_id_type=MESH))
                    recvs.append(pltpu.make_async_remote_copy(
                        src_ref=src.at[pl.ds(c * half, half), :],
                        dst_ref=land.at[2 * cx + cy, pl.ds(other * half, half), :],
                        send_sem=send_sems.at[i], recv_sem=recv_sems.at[i], device_id=(cx, cy, other),
                        device_id_type=MESH))
        return starts, starts, recvs
    return plan


def _plan_swap_halves(halves):
    n = len(halves)

    def plan(refs, send_sems, recv_sems):
        x, y, c, _ = _place()
        cps = [pltpu.make_async_remote_copy(
            src_ref=refs[b].at[:, pl.ds((1 - c) * half, half), :], dst_ref=refs[n + b], send_sem=send_sems.at[b],
            recv_sem=recv_sems.at[b], device_id=(x, y, 1 - c), device_id_type=MESH)
            for b, half in enumerate(halves)]
        return cps, cps, cps
    return plan


def _plan_scatter_chips(n):
    def plan(refs, send_sems, recv_sems):
        x, y, c, chips = _place()
        me = 2 * x + y
        starts, recvs = [], []
        for b in range(n):
            src, land = refs[b], refs[n + b]
            for k, (cx, cy) in enumerate(chips):
                i = 3 * b + k
                starts.append(pltpu.make_async_remote_copy(
                    src_ref=src.at[2 * cx + cy], dst_ref=land.at[me], send_sem=send_sems.at[i],
                    recv_sem=recv_sems.at[i], device_id=(cx, cy, c), device_id_type=MESH))
                recvs.append(pltpu.make_async_remote_copy(
                    src_ref=src.at[me], dst_ref=land.at[2 * cx + cy], send_sem=send_sems.at[i],
                    recv_sem=recv_sems.at[i], device_id=(cx, cy, c), device_id_type=MESH))
        return starts, starts, recvs
    return plan


def _plan_join_halves(halves):
    def plan(refs, send_sems, recv_sems):
        x, y, c, _ = _place()
        starts, recvs = [], []
        for b, half in enumerate(halves):
            mine, theirs = refs[b].at[pl.ds(c * half, half), :], refs[b].at[pl.ds((1 - c) * half, half), :]
            starts.append(pltpu.make_async_remote_copy(
                src_ref=mine, dst_ref=mine, send_sem=send_sems.at[b], recv_sem=recv_sems.at[b],
                device_id=(x, y, 1 - c), device_id_type=MESH))
            recvs.append(pltpu.make_async_remote_copy(
                src_ref=mine, dst_ref=theirs, send_sem=send_sems.at[b], recv_sem=recv_sems.at[b],
                device_id=(x, y, 1 - c), device_id_type=MESH))
        return starts, starts, recvs
    return plan


def _allreduce_small(vec):
    def body(v_ref, o_ref, gath, send_sems, recv_sems):
        x, y, c, _ = _place()
        me = 4 * x + 2 * y + c
        gath[me] = v_ref[...]
        sends = []
        for k in range(1, 8):
            peer = (x ^ (k >> 2), y ^ ((k >> 1) & 1), c ^ (k & 1))
            sends.append(pltpu.make_async_remote_copy(
                src_ref=v_ref, dst_ref=gath.at[me], send_sem=send_sems.at[k - 1], recv_sem=recv_sems.at[k - 1],
                device_id=peer, device_id_type=MESH))
        for cp in sends:
            cp.start()
        for k in range(1, 8):
            peer = (x ^ (k >> 2), y ^ ((k >> 1) & 1), c ^ (k & 1))
            pltpu.make_async_remote_copy(
                src_ref=v_ref, dst_ref=gath.at[4 * peer[0] + 2 * peer[1] + peer[2]], send_sem=send_sems.at[k - 1],
                recv_sem=recv_sems.at[k - 1], device_id=peer, device_id_type=MESH).wait_recv()
        for cp in sends:
            cp.wait_send()
        acc = gath[0]
        for d in range(1, 8):
            acc = acc + gath[d]
        o_ref[...] = acc

    return pl.pallas_call(
        body, name="allreduce_small",
        in_specs=[pl.BlockSpec(memory_space=pltpu.VMEM)], out_specs=pl.BlockSpec(memory_space=pltpu.VMEM),
        out_shape=jax.ShapeDtypeStruct(vec.shape, F32),
        scratch_shapes=[pltpu.VMEM((8,) + vec.shape, F32), pltpu.SemaphoreType.DMA((7,)),
                        pltpu.SemaphoreType.DMA((7,))],
    )(vec)


def _adamw(w, g, m, v, name, after):
    shape = w.shape
    if w.ndim == 1 or w.size <= 128 * 128:
        two_d = (1, w.size) if w.size % 128 else (w.size // 128, 128)
    else:
        two_d = (w.shape[0], w.size // w.shape[0])
    rows, cols = two_d
    tr = _row_tile(rows, 8) if rows % 8 == 0 and rows > 512 else rows

    def body(w_ref, g_ref, m_ref, v_ref, after_ref, d_ref, nm_ref, nv_ref):
        gr = g_ref[...]
        nm = B1 * m_ref[...] + (1.0 - B1) * gr
        nv = B2 * v_ref[...] + (1.0 - B2) * (gr * gr)
        m_hat = nm / (1.0 - B1 ** STEP)
        v_hat = nv / (1.0 - B2 ** STEP)
        d_ref[...] = -LR * (m_hat / (jnp.sqrt(v_hat) + AEPS) + WD * w_ref[...])
        nm_ref[...] = nm
        nv_ref[...] = nv

    spec = pl.BlockSpec((tr, cols), lambda i: (i, 0))
    outs = pl.pallas_call(
        body, name=name, grid=(rows // tr,),
        in_specs=[spec] * 4 + [pl.BlockSpec(memory_space=pl.ANY)], out_specs=[spec] * 3,
        out_shape=[jax.ShapeDtypeStruct(two_d, F32)] * 3,
        compiler_params=_params(("parallel",)),
    )(*[t.reshape(two_d) for t in (w, g, m, v)], after)
    return [o.reshape(shape) for o in outs]


def _rows_stacked(g, lo, n_rows):
    return g[:, lo:lo + n_rows].reshape(4 * n_rows, D)


def _rows_to_slots(t):
    return t.reshape(4, t.shape[0] // 4, D)


def _pack_first(w_in, w_conv_out, conv_dw_w):
    dw = jnp.pad(conv_dw_w.reshape(CW, 256), ((0, 1), (0, 0)))
    dw_bits = lax.bitcast_convert_type(dw, BF16).reshape(16, D)
    return jnp.concatenate([w_in.T.astype(BF16), w_conv_out.astype(BF16), dw_bits, jnp.zeros((16, D), BF16)],
                           axis=0)


def _unpack_first(g):
    w_in_t = _rows_stacked(g, 0, ROWS_W_IN)
    wc = _rows_stacked(g, ROWS_W_IN, ROWS_SQ)
    o = ROWS_W_IN + ROWS_SQ
    dw = lax.bitcast_convert_type(g[:, o:o + 16].reshape(4, 32, 256, 2), F32)
    return w_in_t, wc, jnp.transpose(dw, (1, 0, 2)).reshape(32, D)


def _pack_late(w_attn_out, w_merge_out, w_ffn_down):
    return jnp.concatenate([w_attn_out.astype(BF16), w_merge_out.astype(BF16), w_ffn_down.astype(BF16)], axis=0)


def _unpack_late(g):
    return (_rows_stacked(g, 0, ROWS_SQ), _rows_stacked(g, ROWS_SQ, ROWS_SQ),
            _rows_stacked(g, 2 * ROWS_SQ, ROWS_DOWN))


class _Exchanges:
    def __init__(self, late_pack, wf_shard):
        self.c_idx = lax.axis_index("c").astype(jnp.int32).reshape(1)
        packs = [late_pack, wf_shard]
        self.late_plan = _plan_gather_direct([p.shape[0] // 2 for p in packs])
        slots = [lax.empty((4,) + p.shape, BF16) for p in packs]
        self.late = _start_copies("gather_late_start", packs + slots, 6 * len(packs), self.late_plan)
        self.first_token = self.late[3][0, 0]

    def late_weights(self, after):
        send_sems, recv_sems, bufs, _ = self.late
        pack, wf_shard, slots, wf_slots = _wait_copies("gather_late_wait", send_sems, recv_sems, bufs, after,
                                                       self.late_plan)
        wa, wm, wd = _unpack_late(_own_slot(slots, pack))
        return wa, wm, _own_slot(wf_slots, wf_shard), wd

    def reduce_start(self, d_wd, d_wf4, d_wm, d_wa, d_wc):
        gs = [_rows_to_slots(d_wd), jnp.concatenate([_rows_to_slots(t) for t in (d_wm, d_wa, d_wc)], axis=1), d_wf4]
        self.halves = [g.shape[1] // 2 for g in gs]
        self.swap_plan = _plan_swap_halves(self.halves)
        lands = [lax.empty((4, h, g.shape[2]), F32) for g, h in zip(gs, self.halves)]
        self.swap = _start_copies("grad_swap_start", gs + lands, len(gs), self.swap_plan)
        return self.swap[3][0, 0]

    def reduce_mid(self, after):
        send_sems, recv_sems, bufs, _ = self.swap
        bufs = _wait_copies("grad_swap_wait", send_sems, recv_sems, bufs, after, self.swap_plan)
        n = len(self.halves)
        ps = [_add_halves(bufs[b], bufs[n + b], self.c_idx, "grad_add_halves_%d" % b) for b in range(n)]
        self.scatter_plan = _plan_scatter_chips(n)
        self.scatter = _start_copies("grad_scatter_start", ps + [lax.empty(p.shape, BF16) for p in ps], 3 * n,
                                     self.scatter_plan)
        return self.scatter[3][0, 0]

    def reduce_late(self, after):
        send_sems, recv_sems, bufs, _ = self.scatter
        bufs = _wait_copies("grad_scatter_wait", send_sems, recv_sems, bufs, after, self.scatter_plan)
        n = len(self.halves)
        fs = [_sum_chips(_own_slot(bufs[n + b], _own_piece(bufs[b])), self.c_idx, "grad_sum_chips_%d" % b)
              for b in range(n)]
        self.join_plan = _plan_join_halves(self.halves)
        self.join = _start_copies("grad_join_start", fs, n, self.join_plan)

    def reduce_end(self, after):
        send_sems, recv_sems, bufs, _ = self.join
        g_wd, sq, g_wf = _wait_copies("grad_join_wait", send_sems, recv_sems, bufs, after, self.join_plan)
        return g_wd, g_wf, sq[0:ROWS_SQ], sq[ROWS_SQ:2 * ROWS_SQ], sq[2 * ROWS_SQ:3 * ROWS_SQ]

    def w_in_start(self, d_w_in_t):
        g = _rows_to_slots(d_w_in_t)
        self.w_half = g.shape[1] // 2
        self.w_swap_plan = _plan_swap_halves([self.w_half])
        self.w_swap = _start_copies("grad_w_in_swap_start", [g, lax.empty((4, self.w_half, D), F32)], 1,
                                    self.w_swap_plan)
        return self.w_swap[3]

    def w_in_mid(self, after):
        send_sems, recv_sems, bufs, _ = self.w_swap
        g, got = _wait_copies("grad_w_in_swap_wait", send_sems, recv_sems, bufs, after, self.w_swap_plan)
        p = _add_halves(g, got, self.c_idx, "grad_add_halves_w_in")
        self.w_scatter_plan = _plan_scatter_chips(1)
        self.w_scatter = _start_copies("grad_w_in_scatter_start", [p, lax.empty(p.shape, BF16)], 3,
                                       self.w_scatter_plan)
        return self.w_scatter[3]

    def w_in_end(self, after):
        send_sems, recv_sems, bufs, _ = self.w_scatter
        p, got = _wait_copies("grad_w_in_scatter_wait", send_sems, recv_sems, bufs, after, self.w_scatter_plan)
        return _join_halves(_sum_chips(_own_slot(got, _own_piece(p)), self.c_idx, "grad_sum_chips_w_in"))


def _local_grads(x, loss_target, norm_mix_g, conv_dw_b, conv_ln_g, conv_ln_b, q_norm_g, k_norm_g, sinks, norm_ffn_g,
                 w_in, wc, dw, exchanges):
    n_seq, S, _ = x.shape
    T = n_seq * S
    x2 = x.reshape(T, D)
    tgt = loss_target.reshape(T, D)
    row = lambda t: t.reshape(1, -1)
    g1, g2 = row(norm_mix_g), row(norm_ffn_g)
    qg, kg = jnp.broadcast_to(q_norm_g.reshape(HD, 1), (HD, GROUP * BLK)), row(k_norm_g)
    lng, lnb, dwb = row(conv_ln_g), row(conv_ln_b), row(conv_dw_b)
    sink_rows = jnp.repeat(sinks.reshape(NKV, GROUP), BLK, axis=1)

    xn, ag, q, kv, gg, h1, h3, yc = _in_proj_conv_fwd(x2, g1 + exchanges.first_token, w_in, dw, dwb, lng, lnb, wc,
                                                      n_seq, S)
    o = _attn_fwd(q, kv, qg, kg, sink_rows, n_seq, S)
    wa, wm, wf, wd = exchanges.late_weights([o, yc])
    mix, h = _merge_fwd(x2, gg, yc, o, wa, wm)
    dh, dhb, hn, act, dout, dgu, ffn_stats = _ffn(h, tgt, g2, wf, wd)
    d_wd = _tn_matmul(act, dout, "dw_ffn_down")
    d_wf = _tn_matmul(hn, dgu, "dw_ffn_in", column_blocks=True)
    d_wm = _tn_matmul(mix, dhb, "dw_merge")
    dgg, dyc, dya, do, dh3 = _merge_bwd(dhb, gg, yc, o, wm, wa, wc)
    d_wa = _tn_matmul(o, dya, "dw_attn_out")
    d_wc = _tn_matmul(h3, dyc, "dw_conv_out")
    token = exchanges.reduce_start(d_wd, d_wf, d_wm, d_wa, d_wc)
    dq, dkc, dkp, dqg, dsk = _attn_bwd(q, kv, do, qg + token, kg, sink_rows, n_seq, S)
    token = exchanges.reduce_mid([dq])
    dkv, dkg = _kv_bwd(kv, dkc, dkp, jnp.broadcast_to(k_norm_g.reshape(HD, 1), (HD, BLK)), n_seq, S)
    dag, ddw, conv_stats = _conv_bwd(h1, dh3, ag, dw, lng + token, lnb, n_seq, S)
    dx, dg1 = _in_proj_bwd(dag, dq, dkv, dgg, dh, x2, g1, w_in)
    exchanges.reduce_late([dx])
    d_w_in = jnp.concatenate([_tn_matmul(dag, xn, "dw_in_conv"), _tn_matmul(dq, xn, "dw_in_q"),
                              _tn_matmul(dkv, xn, "dw_in_kv"), _tn_matmul(dgg, xn, "dw_in_gates")], axis=0)

    heads = jnp.concatenate([jnp.sum(dqg, axis=1), jnp.sum(dkg, axis=1), dsk[0, :NQ],
                             jnp.zeros((D - 2 * HD - NQ,), F32)])
    vec = jnp.concatenate([dg1, conv_stats[0:3], ffn_stats[0:1], heads[None], ffn_stats[1:2], jnp.zeros((1, D), F32),
                           ddw], axis=0)
    return ffn_stats[1], dx.reshape(x.shape), d_w_in, vec


def kernel(x, norm_mix_g, w_in, conv_dw_w, conv_dw_b, conv_ln_g, conv_ln_b, w_conv_out, q_norm_g, k_norm_g, sinks, w_attn_out, w_merge_out, norm_ffn_g, w_ffn_in, w_ffn_down, loss_target, m_norm_mix_g, m_w_in, m_conv_dw_w, m_conv_dw_b, m_conv_ln_g, m_conv_ln_b, m_w_conv_out, m_q_norm_g, m_k_norm_g, m_sinks, m_w_attn_out, m_w_merge_out, m_norm_ffn_g, m_w_ffn_in, m_w_ffn_down, v_norm_mix_g, v_w_in, v_conv_dw_w, v_conv_dw_b, v_conv_ln_g, v_conv_ln_b, v_w_conv_out, v_q_norm_g, v_k_norm_g, v_sinks, v_w_attn_out, v_w_merge_out, v_norm_ffn_g, v_w_ffn_in, v_w_ffn_down):
    chip = 2 * lax.axis_index("x") + lax.axis_index("y")

    first, token = _gather_weights(_pack_first(w_in, w_conv_out, conv_dw_w))
    exchanges = _Exchanges(_pack_late(w_attn_out, w_merge_out, w_ffn_down) + token.astype(BF16),
                           w_ffn_in.astype(BF16) + token.astype(BF16))
    _, grad_x, d_w_in, vec = _local_grads(x, loss_target, norm_mix_g, conv_dw_b, conv_ln_g, conv_ln_b, q_norm_g,
                                          k_norm_g, sinks, norm_ffn_g, *_unpack_first(first), exchanges)

    g_wd, g_wf, g_wm, g_wa, g_wc = exchanges.reduce_end([d_w_in])
    w_in_token = exchanges.w_in_start(d_w_in)
    small = _allreduce_small(vec + w_in_token[0, 0])
    loss = 0.5 / D * jnp.sum(small[6])
    g_dw = lax.dynamic_slice_in_dim(small[8:8 + CW], chip * 256, 256, axis=1).reshape(CW, 1, 256)
    grads = {
        "norm_mix_g": small[0], "conv_dw_w": g_dw, "conv_dw_b": small[1], "conv_ln_g": small[2],
        "conv_ln_b": small[3], "w_conv_out": g_wc, "q_norm_g": small[5, 0:HD], "k_norm_g": small[5, HD:2 * HD],
        "sinks": small[5, 2 * HD:2 * HD + NQ], "w_attn_out": g_wa, "w_merge_out": g_wm, "norm_ffn_g": small[4],
        "w_ffn_in": g_wf, "w_ffn_down": g_wd,
    }
    weights = dict(norm_mix_g=norm_mix_g, w_in=w_in, conv_dw_w=conv_dw_w, conv_dw_b=conv_dw_b, conv_ln_g=conv_ln_g,
                   conv_ln_b=conv_ln_b, w_conv_out=w_conv_out, q_norm_g=q_norm_g, k_norm_g=k_norm_g, sinks=sinks,
                   w_attn_out=w_attn_out, w_merge_out=w_merge_out, norm_ffn_g=norm_ffn_g, w_ffn_in=w_ffn_in,
                   w_ffn_down=w_ffn_down)
    m_in = dict(norm_mix_g=m_norm_mix_g, w_in=m_w_in, conv_dw_w=m_conv_dw_w, conv_dw_b=m_conv_dw_b,
                conv_ln_g=m_conv_ln_g, conv_ln_b=m_conv_ln_b, w_conv_out=m_w_conv_out, q_norm_g=m_q_norm_g,
                k_norm_g=m_k_norm_g, sinks=m_sinks, w_attn_out=m_w_attn_out, w_merge_out=m_w_merge_out,
                norm_ffn_g=m_norm_ffn_g, w_ffn_in=m_w_ffn_in, w_ffn_down=m_w_ffn_down)
    v_in = dict(norm_mix_g=v_norm_mix_g, w_in=v_w_in, conv_dw_w=v_conv_dw_w, conv_dw_b=v_conv_dw_b,
                conv_ln_g=v_conv_ln_g, conv_ln_b=v_conv_ln_b, w_conv_out=v_w_conv_out, q_norm_g=v_q_norm_g,
                k_norm_g=v_k_norm_g, sinks=v_sinks, w_attn_out=v_w_attn_out, w_merge_out=v_w_merge_out,
                norm_ffn_g=v_norm_ffn_g, w_ffn_in=v_w_ffn_in, w_ffn_down=v_w_ffn_down)
    names = list(weights)
    updates = {}
    after = exchanges.w_in_mid([small])
    for n in names:
        if n != "w_in":
            updates[n] = _adamw(weights[n], grads[n], m_in[n], v_in[n], "adamw_" + n, after)
    g_w_in_t = exchanges.w_in_end([updates[n][0] for n in updates])
    grads["w_in"] = g_w_in_t.T
    updates["w_in"] = [t.T for t in _adamw(w_in.T, g_w_in_t, m_w_in.T, v_w_in.T, "adamw_w_in", g_w_in_t)]
    return (loss, grad_x, *[grads[n] for n in names], *[updates[n][0] for n in names],
            *[updates[n][1] for n in names], *[updates[n][2] for n in names])
```

```python
import math

import jax
import jax.numpy as jnp
import numpy as np
from jax import lax
from jax.experimental import pallas as pl
from jax.experimental.pallas import tpu as pltpu

F32 = jnp.float32
BF16 = jnp.bfloat16

D = 1024
CW = 31
HD = 64
NQ = 16
NKV = 2
GROUP = NQ // NKV
BLK = 128
DFF = 2816
EPS = 1e-6
NEG = -1e30
IN_COLS = 5376
U_CONV, U_Q, U_KV = slice(0, 2 * D), slice(2 * D, 3 * D), slice(3 * D, 3 * D + 2 * NKV * HD)
U_GATES = slice(3 * D + 2 * NKV * HD, IN_COLS)
SCALE = 1.0 / math.sqrt(HD)

LR, B1, B2, AEPS, WD, STEP = 0.001, 0.9, 0.999, 1e-08, 0.01, 10

MIB = 1024 * 1024
MESH = pl.DeviceIdType.MESH

ROWS_W_IN = IN_COLS // 4
ROWS_SQ = D // 4
ROWS_FFN_IN = 2 * DFF // 4
ROWS_DOWN = DFF // 4


def _sig(x):
    return 1.0 / (1.0 + jnp.exp(-x))


def _dot(a, b):
    return jnp.dot(a, b, preferred_element_type=F32)


def _dot_nt(a, b):
    return lax.dot_general(a, b, (((1,), (1,)), ((), ())), preferred_element_type=F32)


def _dot_tn(a, b):
    return lax.dot_general(a, b, (((0,), (0,)), ((), ())), preferred_element_type=F32)


def _params(sem, vmem_mib=48):
    return pltpu.CompilerParams(dimension_semantics=sem, vmem_limit_bytes=vmem_mib * MIB)


def _resident(shape):
    return pl.BlockSpec(shape, lambda *_: (0,) * len(shape), pipeline_mode=pl.Buffered(1))


def _whole(shape):
    return pl.BlockSpec(shape, lambda *_: (0,) * len(shape))


def _rows(tm, cols):
    return pl.BlockSpec((tm, cols), lambda i: (i, 0))


def _tap_phases():
    return [(phase, list(range(phase, CW, 8))) for phase in range(8)]


def _shift_copies(dst, src, base):
    for phase, taps in _tap_phases():
        n = dst.shape[1] - 8 * (4 - len(taps))
        dst[phase, 0:n, :] = src[base + phase:base + phase + n, :]


def _in_proj_conv_fwd(x2, g1, w_in, dw, dwb, lng, lnb, wc, n_seq, S):
    T = n_seq * S
    tc = min(256, S)
    nt = S // tc

    def body(x_ref, g_ref, w_ref, dw_ref, dwb_ref, lng_ref, lnb_ref, wc_ref, xn_ref, ag_ref, q_ref, kv_ref, gg_ref,
             h1_ref, h3_ref, yc_ref, ext, sh):
        i = pl.program_id(1)

        @pl.when(i == 0)
        def _():
            ext[0:32, :] = jnp.zeros((32, D), F32)

        x = x_ref[...]
        rstd = lax.rsqrt(jnp.mean(x * x, axis=-1, keepdims=True) + EPS)
        xn = (x * rstd * g_ref[...]).astype(BF16)
        xn_ref[...] = xn
        ag = _dot_nt(xn, w_ref[U_CONV, :])
        ag_ref[...] = ag
        ext[32:32 + tc, :] = ag[:, 0:D] * _sig(ag[:, D:2 * D])
        q_ref[...] = _dot_nt(xn, w_ref[U_Q, :])
        kv_ref[...] = _dot_nt(xn, w_ref[U_KV, :])
        gg_ref[...] = _dot_nt(xn, w_ref[U_GATES, :])
        _shift_copies(sh, ext, 2)
        for cb in range(D // 128):
            cs = slice(cb * 128, (cb + 1) * 128)
            acc = jnp.broadcast_to(dwb_ref[:, cs], (tc, 128))
            for phase, taps in _tap_phases():
                for m, j in enumerate(taps):
                    acc = acc + dw_ref[j:j + 1, cs] * sh[phase, 8 * m:8 * m + tc, cs]
            h1_ref[:, cs] = acc
        ext[0:32, :] = ext[tc:tc + 32, :]
        h1 = h1_ref[...]
        mu = jnp.mean(h1, axis=-1, keepdims=True)
        cen = h1 - mu
        var = jnp.mean(cen * cen, axis=-1, keepdims=True)
        h2 = cen * lax.rsqrt(var + EPS) * lng_ref[...] + lnb_ref[...]
        h3 = (h2 * _sig(h2)).astype(BF16)
        h3_ref[...] = h3
        yc_ref[...] = _dot(h3, wc_ref[...])

    tile = lambda cols: pl.BlockSpec((tc, cols), lambda b, i: (b * nt + i, 0))
    shape = lambda cols, dtype: jax.ShapeDtypeStruct((T, cols), dtype)
    return pl.pallas_call(
        body, name="in_proj_conv_fwd", grid=(n_seq, nt),
        in_specs=[tile(D), _resident((1, D)), _resident((IN_COLS, D)), _resident((32, D)), _resident((1, D)),
                  _resident((1, D)), _resident((1, D)), _resident((D, D))],
        out_specs=[tile(D), tile(2 * D), tile(D), tile(256), tile(2 * D), tile(D), tile(D), tile(D)],
        out_shape=[shape(D, BF16), shape(2 * D, F32), shape(D, F32), shape(256, F32), shape(2 * D, F32),
                   shape(D, F32), shape(D, BF16), shape(D, F32)],
        scratch_shapes=[pltpu.VMEM((32 + tc, D), F32), pltpu.VMEM((8, tc + 24, D), F32)],
        compiler_params=_params(("parallel", "arbitrary"), 56),
    )(x2, g1, w_in, dw, dwb, lng, lnb, wc)


def _attn_consts():
    k = np.arange(BLK)[:, None]
    i = np.arange(GROUP * BLK)[None, :] % BLK
    from_prev = k > i
    dist = np.where(from_prev, i + BLK - k, i - k).astype(np.float32)
    head = np.arange(GROUP * BLK)[None, :] // BLK
    bias = []
    for kh in range(NKV):
        slope = np.exp2(-8.0 * (kh * GROUP + head + 1) / NQ).astype(np.float32)
        bias.append(-slope * dist)
    return jnp.asarray(from_prev.astype(np.float32)), jnp.asarray(np.stack(bias))


def _heads_to_lanes(t, kh):
    return jnp.concatenate([t[(kh * GROUP + g) * HD:(kh * GROUP + g + 1) * HD, :] for g in range(GROUP)], axis=1)


def _lanes_to_heads(t):
    return jnp.concatenate([t[:, g * BLK:(g + 1) * BLK] for g in range(GROUP)], axis=0)


def _rms64(t):
    return lax.rsqrt(jnp.mean(t * t, axis=-1, keepdims=True) + EPS)


def _attn_probs(kh, n, q_t, kvc_ref, kvp_ref, qg_ref, kg_ref, tri_ref, bias_ref, sink_ref):
    ks = slice(kh * HD, (kh + 1) * HD)
    vs = slice(2 * HD + kh * HD, 2 * HD + (kh + 1) * HD)
    kp, kc = kvp_ref[:, ks], kvc_ref[:, ks]
    kpb = (kp * _rms64(kp) * kg_ref[...]).astype(BF16)
    kcb = (kc * _rms64(kc) * kg_ref[...]).astype(BF16)
    qs = _heads_to_lanes(q_t, kh)
    rq = lax.rsqrt(jnp.mean(qs * qs, axis=0, keepdims=True) + EPS)
    qy = qs * rq
    qhb = (qy * (qg_ref[...] * SCALE)).astype(BF16)
    from_prev = tri_ref[...] > 0.5
    no_prev = jnp.where(n > 0, 0.0, NEG)
    s = jnp.where(from_prev, _dot(kpb, qhb) + no_prev, _dot(kcb, qhb)) + bias_ref[kh]
    sink = sink_ref[kh:kh + 1, :]
    m = jnp.maximum(jnp.max(s, axis=0, keepdims=True), sink)
    e = jnp.exp(s - m)
    es = jnp.exp(sink - m)
    rz = 1.0 / (jnp.sum(e, axis=0, keepdims=True) + es)
    prev_mask = tri_ref[...].astype(BF16)
    return e * rz, es * rz, prev_mask, qhb, kpb, kcb, kvp_ref[:, vs].astype(BF16), kvc_ref[:, vs].astype(BF16), qy, rq


def _unfold(t, prev_mask):
    prev = t * prev_mask
    return prev, t - prev


def _attn_specs(n_seq, S):
    nb = S // BLK
    cur = lambda cols: pl.BlockSpec((BLK, cols), lambda b, n: (b * nb + n, 0))
    prev = lambda cols: pl.BlockSpec((BLK, cols), lambda b, n: (b * nb + jnp.maximum(n - 1, 0), 0))
    consts = [_resident((HD, GROUP * BLK)), _resident((1, HD)), _resident((BLK, GROUP * BLK)),
              _resident((NKV, BLK, GROUP * BLK)), _resident((NKV, GROUP * BLK))]
    return nb, cur, prev, consts


def _attn_fwd(q, kv, qg_cols, kg, sink_rows, n_seq, S):
    T = n_seq * S
    nb, cur, prev, consts = _attn_specs(n_seq, S)
    tri, bias = _attn_consts()

    def body(q_ref, kvc_ref, kvp_ref, qg_ref, kg_ref, tri_ref, bias_ref, sink_ref, o_ref):
        n = pl.program_id(1)
        q_t = q_ref[...].T
        o_t = []
        for kh in range(NKV):
            p, _, prev_mask, _, _, _, vpb, vcb, _, _ = _attn_probs(kh, n, q_t, kvc_ref, kvp_ref, qg_ref, kg_ref,
                                                                   tri_ref, bias_ref, sink_ref)
            pp, pc = _unfold(p.astype(BF16), prev_mask)
            o_t.append(_lanes_to_heads(_dot_tn(vpb, pp) + _dot_tn(vcb, pc)))
        o_ref[...] = jnp.concatenate(o_t, axis=0).T.astype(BF16)

    return pl.pallas_call(
        body, name="attn_fwd", grid=(n_seq, nb),
        in_specs=[cur(D), cur(256), prev(256)] + consts,
        out_specs=cur(D),
        out_shape=jax.ShapeDtypeStruct((T, D), BF16),
        compiler_params=_params(("parallel", "parallel")),
    )(q, kv, kv, qg_cols, kg, tri, bias, sink_rows)


def _merge_fwd(x2, gg, yc, o, wa, wm):
    T = x2.shape[0]
    tm = min(512, T)

    def body(x_ref, gg_ref, yc_ref, o_ref, wa_ref, wm_ref, mix_ref, h_ref):
        ya = _dot(o_ref[...], wa_ref[...])
        mix = (_sig(gg_ref[:, 0:D]) * yc_ref[...] + _sig(gg_ref[:, D:2 * D]) * ya).astype(BF16)
        mix_ref[...] = mix
        h_ref[...] = x_ref[...] + _dot(mix, wm_ref[...])

    return pl.pallas_call(
        body, name="merge_fwd", grid=(T // tm,),
        in_specs=[_rows(tm, D), _rows(tm, 2 * D), _rows(tm, D), _rows(tm, D), _resident((D, D)), _resident((D, D))],
        out_specs=[_rows(tm, D), _rows(tm, D)],
        out_shape=[jax.ShapeDtypeStruct((T, D), BF16), jax.ShapeDtypeStruct((T, D), F32)],
        compiler_params=_params(("parallel",)),
    )(x2, gg, yc, o, wa, wm)


FF_CHUNK = DFF // 2


def _ffn(h, tgt, g2, wf, wd):
    T = h.shape[0]
    tm = min(256, T)

    def body(h_ref, t_ref, g_ref, wf_ref, wd_ref, dh_ref, dhb_ref, hn_ref, act_ref, dout_ref, dgu_ref, st_ref,
             gsc, usc):
        @pl.when(pl.program_id(0) == 0)
        def _():
            st_ref[...] = jnp.zeros((8, D), F32)

        hh = h_ref[...]
        rstd = lax.rsqrt(jnp.mean(hh * hh, axis=-1, keepdims=True) + EPS)
        hhat = hh * rstd
        hn = (hhat * g_ref[...]).astype(BF16)
        hn_ref[...] = hn
        out = hh
        for c in range(DFF // FF_CHUNK):
            cs = slice(c * FF_CHUNK, (c + 1) * FF_CHUNK)
            us = slice(DFF + c * FF_CHUNK, DFF + (c + 1) * FF_CHUNK)
            g = _dot(hn, wf_ref[c])
            u = _dot(hn, wf_ref[2 + c])
            gsc[:, cs] = g
            usc[:, cs] = u
            act = (g * _sig(g) * u).astype(BF16)
            act_ref[:, cs] = act
            out = out + _dot(act, wd_ref[cs, :])
        err = out - t_ref[...]
        dout = err * (1.0 / D)
        doutb = dout.astype(BF16)
        dout_ref[...] = doutb
        dhn = jnp.zeros((tm, D), F32)
        for c in range(DFF // FF_CHUNK):
            cs = slice(c * FF_CHUNK, (c + 1) * FF_CHUNK)
            us = slice(DFF + c * FF_CHUNK, DFF + (c + 1) * FF_CHUNK)
            g = gsc[:, cs]
            u = usc[:, cs]
            dact = _dot_nt(doutb, wd_ref[cs, :])
            sg = _sig(g)
            dg = (dact * u * (sg * (1.0 + g * (1.0 - sg)))).astype(BF16)
            du = (dact * (g * sg)).astype(BF16)
            dgu_ref[:, cs] = dg
            dgu_ref[:, us] = du
            dhn = dhn + _dot_nt(dg, wf_ref[c]) + _dot_nt(du, wf_ref[2 + c])
        st_ref[0:1, :] += jnp.sum(dhn * hhat, axis=0, keepdims=True)
        st_ref[1:2, :] += jnp.sum(err * err, axis=0, keepdims=True)
        dhh = dhn * g_ref[...]
        dh = dout + rstd * (dhh - hhat * jnp.mean(dhh * hhat, axis=-1, keepdims=True))
        dh_ref[...] = dh
        dhb_ref[...] = dh.astype(BF16)

    return pl.pallas_call(
        body, name="ffn_fwd_bwd", grid=(T // tm,),
        in_specs=[_rows(tm, D), _rows(tm, D), _resident((1, D)), _resident((4, D, FF_CHUNK)), _resident((DFF, D))],
        out_specs=[_rows(tm, D), _rows(tm, D), _rows(tm, D), _rows(tm, DFF), _rows(tm, D), _rows(tm, 2 * DFF),
                   _whole((8, D))],
        out_shape=[jax.ShapeDtypeStruct((T, D), F32), jax.ShapeDtypeStruct((T, D), BF16),
                   jax.ShapeDtypeStruct((T, D), BF16), jax.ShapeDtypeStruct((T, DFF), BF16),
                   jax.ShapeDtypeStruct((T, D), BF16), jax.ShapeDtypeStruct((T, 2 * DFF), BF16),
                   jax.ShapeDtypeStruct((8, D), F32)],
        scratch_shapes=[pltpu.VMEM((tm, DFF), F32), pltpu.VMEM((tm, DFF), F32)],
        compiler_params=_params(("arbitrary",), 56),
    )(h, tgt, g2, wf, wd)


def _merge_bwd(dhb, gg, yc, o, wm, wa, wc):
    T = dhb.shape[0]
    tm = min(512, T)

    def body(dh_ref, gg_ref, yc_ref, o_ref, wm_ref, wa_ref, wc_ref, dgg_ref, dyc_ref, dya_ref, do_ref, dh3_ref):
        dmix = _dot_nt(dh_ref[...], wm_ref[...])
        gc = _sig(gg_ref[:, 0:D])
        ga = _sig(gg_ref[:, D:2 * D])
        yc = yc_ref[...]
        ya = _dot(o_ref[...], wa_ref[...])
        dgg_ref[:, 0:D] = (dmix * yc * gc * (1.0 - gc)).astype(BF16)
        dgg_ref[:, D:2 * D] = (dmix * ya * ga * (1.0 - ga)).astype(BF16)
        dyc = (dmix * gc).astype(BF16)
        dya = (dmix * ga).astype(BF16)
        dyc_ref[...] = dyc
        dya_ref[...] = dya
        do_ref[...] = _dot_nt(dya, wa_ref[...]).astype(BF16)
        dh3_ref[...] = _dot_nt(dyc, wc_ref[...])

    return pl.pallas_call(
        body, name="merge_bwd", grid=(T // tm,),
        in_specs=[_rows(tm, D), _rows(tm, 2 * D), _rows(tm, D), _rows(tm, D), _resident((D, D)), _resident((D, D)),
                  _resident((D, D))],
        out_specs=[_rows(tm, 2 * D), _rows(tm, D), _rows(tm, D), _rows(tm, D), _rows(tm, D)],
        out_shape=[jax.ShapeDtypeStruct((T, 2 * D), BF16), jax.ShapeDtypeStruct((T, D), BF16),
                   jax.ShapeDtypeStruct((T, D), BF16), jax.ShapeDtypeStruct((T, D), BF16),
                   jax.ShapeDtypeStruct((T, D), F32)],
        compiler_params=_params(("parallel",), 56),
    )(dhb, gg, yc, o, wm, wa, wc)


def _attn_bwd(q, kv, do, qg_cols, kg, sink_rows, n_seq, S):
    T = n_seq * S
    nb, cur, prev, consts = _attn_specs(n_seq, S)
    tri, bias = _attn_consts()

    def body(q_ref, kvc_ref, kvp_ref, do_ref, qg_ref, kg_ref, tri_ref, bias_ref, sink_ref, dq_ref, dkc_ref, dkp_ref,
             dqg_ref, dsk_ref):
        n = pl.program_id(1)

        @pl.when((pl.program_id(0) == 0) & (n == 0))
        def _():
            dqg_ref[...] = jnp.zeros((HD, BLK), F32)
            dsk_ref[...] = jnp.zeros((8, 128), F32)

        lane = lax.broadcasted_iota(jnp.int32, (1, 128), 1)
        q_t = q_ref[...].T
        do_t = do_ref[...].astype(F32).T
        dq_t = []
        for kh in range(NKV):
            p, ps, prev_mask, qhb, kpb, kcb, vpb, vcb, qy, rq = _attn_probs(
                kh, n, q_t, kvc_ref, kvp_ref, qg_ref, kg_ref, tri_ref, bias_ref, sink_ref)
            dob = _heads_to_lanes(do_t, kh).astype(BF16)
            dp = jnp.where(tri_ref[...] > 0.5, _dot(vpb, dob), _dot(vcb, dob))
            delta = jnp.sum(p * dp, axis=0, keepdims=True)
            dsp, dsc = _unfold((p * (dp - delta)).astype(BF16), prev_mask)
            pp, pc = _unfold(p.astype(BF16), prev_mask)
            dsink = -ps * delta
            dqh = (_dot_tn(kpb, dsp) + _dot_tn(kcb, dsc)) * SCALE
            dqg = dqh * qy
            dqg_ref[...] += sum(dqg[:, g * BLK:(g + 1) * BLK] for g in range(GROUP))
            dy = dqh * qg_ref[...]
            dq_t.append(_lanes_to_heads(rq * (dy - qy * jnp.mean(dy * qy, axis=0, keepdims=True))))
            row = jnp.zeros((1, 128), F32)
            for g in range(GROUP):
                h = kh * GROUP + g
                row = row + jnp.where(lane == h, jnp.sum(dsink[:, g * BLK:(g + 1) * BLK], axis=1, keepdims=True), 0.0)
            dsk_ref[0:1, :] += row
            ks = slice(kh * HD, (kh + 1) * HD)
            vs = slice(2 * HD + kh * HD, 2 * HD + (kh + 1) * HD)
            dkp_ref[:, ks] = _dot_nt(dsp, qhb)
            dkc_ref[:, ks] = _dot_nt(dsc, qhb)
            dkp_ref[:, vs] = _dot_nt(pp, dob)
            dkc_ref[:, vs] = _dot_nt(pc, dob)
        dq_ref[...] = jnp.concatenate(dq_t, axis=0).T.astype(BF16)

    return pl.pallas_call(
        body, name="attn_bwd", grid=(n_seq, nb),
        in_specs=[cur(D), cur(256), prev(256), cur(D)] + consts,
        out_specs=[cur(D), cur(256), cur(256), _whole((HD, BLK)), _whole((8, 128))],
        out_shape=[jax.ShapeDtypeStruct((T, D), BF16), jax.ShapeDtypeStruct((T, 256), F32),
                   jax.ShapeDtypeStruct((T, 256), F32), jax.ShapeDtypeStruct((HD, BLK), F32),
                   jax.ShapeDtypeStruct((8, 128), F32)],
        compiler_params=_params(("arbitrary", "arbitrary")),
    )(q, kv, kv, do, qg_cols, kg, tri, bias, sink_rows)


def _kv_bwd(kv, dkc, dkp, kg_cols, n_seq, S):
    T = n_seq * S
    seq = lambda cols: pl.BlockSpec((S, cols), lambda b: (b, 0))

    def body(kv_ref, dkc_ref, dkp_ref, kg_ref, dkv_ref, dkg_ref):
        @pl.when(pl.program_id(0) == 0)
        def _():
            dkg_ref[...] = jnp.zeros((HD, BLK), F32)

        from_next = jnp.concatenate([dkp_ref[BLK:S, :], jnp.zeros((BLK, 256), F32)], axis=0)
        d = dkc_ref[...] + from_next
        d_t = d[:, 0:2 * HD].T
        k_t = kv_ref[:, 0:2 * HD].T
        kg = jnp.concatenate([kg_ref[...]] * (S // BLK), axis=1)
        out = []
        for kh in range(NKV):
            k = k_t[kh * HD:(kh + 1) * HD, :]
            r = lax.rsqrt(jnp.mean(k * k, axis=0, keepdims=True) + EPS)
            y = k * r
            dkh = d_t[kh * HD:(kh + 1) * HD, :]
            dkg = dkh * y
            dkg_ref[...] += sum(dkg[:, j * BLK:(j + 1) * BLK] for j in range(S // BLK))
            dy = dkh * kg
            out.append(r * (dy - y * jnp.mean(dy * y, axis=0, keepdims=True)))
        dkv_ref[:, 0:2 * HD] = jnp.concatenate(out, axis=0).T.astype(BF16)
        dkv_ref[:, 2 * HD:4 * HD] = d[:, 2 * HD:4 * HD].astype(BF16)

    return pl.pallas_call(
        body, name="kv_bwd", grid=(n_seq,),
        in_specs=[seq(256), seq(256), seq(256), _resident((HD, BLK))],
        out_specs=[seq(256), _whole((HD, BLK))],
        out_shape=[jax.ShapeDtypeStruct((T, 256), BF16), jax.ShapeDtypeStruct((HD, BLK), F32)],
        compiler_params=_params(("arbitrary",)),
    )(kv, dkc, dkp, kg_cols)


def _conv_bwd(h1, dh3, ag, dw, lng, lnb, n_seq, S):
    T = n_seq * S
    tc = min(256, S)
    nt = S // tc

    def body(h1_ref, dh3_ref, a_ref, gt_ref, dw_ref, lng_ref, lnb_ref, dag_ref, ddw_ref, st_ref, extd, acc8, shd):
        i = pl.program_id(1)

        @pl.when((pl.program_id(0) == 0) & (i == 0))
        def _():
            acc8[...] = jnp.zeros((CW * 8, D), F32)
            st_ref[...] = jnp.zeros((8, D), F32)

        @pl.when(i == 0)
        def _():
            extd[tc:tc + 32, :] = jnp.zeros((32, D), F32)

        h1 = h1_ref[...]
        mu = jnp.mean(h1, axis=-1, keepdims=True)
        cen = h1 - mu
        rstd = lax.rsqrt(jnp.mean(cen * cen, axis=-1, keepdims=True) + EPS)
        xh = cen * rstd
        h2 = xh * lng_ref[...] + lnb_ref[...]
        sg = _sig(h2)
        dh2 = dh3_ref[...] * (sg * (1.0 + h2 * (1.0 - sg)))
        st_ref[1:2, :] += jnp.sum(dh2 * xh, axis=0, keepdims=True)
        st_ref[2:3, :] += jnp.sum(dh2, axis=0, keepdims=True)
        dxh = dh2 * lng_ref[...]
        dh1 = rstd * (dxh - jnp.mean(dxh, axis=-1, keepdims=True)
                      - xh * jnp.mean(dxh * xh, axis=-1, keepdims=True))
        st_ref[0:1, :] += jnp.sum(dh1, axis=0, keepdims=True)
        extd[0:tc, :] = dh1
        _shift_copies(shd, extd, 0)
        for cb in range(D // 128):
            cs = slice(cb * 128, (cb + 1) * 128)
            for rb in range(tc // 128):
                rs = slice(rb * 128, (rb + 1) * 128)
                a = a_ref[rs, cs]
                sgt = _sig(gt_ref[rs, cs])
                h0 = a * sgt
                acc = jnp.zeros((128, 128), F32)
                for phase, offs in _tap_phases():
                    for m, o in enumerate(offs):
                        j = CW - 1 - o
                        ahead = shd[phase, rb * 128 + 8 * m:rb * 128 + 8 * m + 128, cs]
                        acc = acc + dw_ref[j:j + 1, cs] * ahead
                        acc8[j * 8:(j + 1) * 8, cs] += jnp.sum((h0 * ahead).reshape(16, 8, 128), axis=0)
                dag_ref[rs, cs] = (acc * sgt).astype(BF16)
                dag_ref[rs, cb * 128 + D:(cb + 1) * 128 + D] = (acc * a * sgt * (1.0 - sgt)).astype(BF16)
        extd[tc:tc + 32, :] = extd[0:32, :]

        @pl.when((pl.program_id(0) == n_seq - 1) & (i == nt - 1))
        def _():
            for j in range(CW):
                ddw_ref[j:j + 1, :] = jnp.sum(acc8[j * 8:(j + 1) * 8, :], axis=0, keepdims=True)
            ddw_ref[CW:32, :] = jnp.zeros((32 - CW, D), F32)

    tile = lambda col: pl.BlockSpec((tc, D), lambda b, i: (b * nt + (nt - 1 - i), col))
    return pl.pallas_call(
        body, name="conv_bwd", grid=(n_seq, nt),
        in_specs=[tile(0), tile(0), tile(0), tile(1), _resident((32, D)), _resident((1, D)), _resident((1, D))],
        out_specs=[pl.BlockSpec((tc, 2 * D), lambda b, i: (b * nt + (nt - 1 - i), 0)), _whole((32, D)),
                   _whole((8, D))],
        out_shape=[jax.ShapeDtypeStruct((T, 2 * D), BF16), jax.ShapeDtypeStruct((32, D), F32),
                   jax.ShapeDtypeStruct((8, D), F32)],
        scratch_shapes=[pltpu.VMEM((tc + 32, D), F32), pltpu.VMEM((CW * 8, D), F32),
                        pltpu.VMEM((8, tc + 24, D), F32)],
        compiler_params=_params(("arbitrary", "arbitrary")),
    )(h1, dh3, ag, ag, dw, lng, lnb)


def _in_proj_bwd(dag, dq, dkv, dgg, dh, x2, g1, w_in):
    T = x2.shape[0]
    tm = min(512, T)

    def body(dag_ref, dq_ref, dkv_ref, dgg_ref, dh_ref, x_ref, g_ref, w_ref, dx_ref, dg_ref):
        @pl.when(pl.program_id(0) == 0)
        def _():
            dg_ref[...] = jnp.zeros((1, D), F32)

        dxn = (_dot(dag_ref[...], w_ref[U_CONV, :]) + _dot(dq_ref[...], w_ref[U_Q, :])
               + _dot(dkv_ref[...], w_ref[U_KV, :]) + _dot(dgg_ref[...], w_ref[U_GATES, :]))
        x = x_ref[...]
        rstd = lax.rsqrt(jnp.mean(x * x, axis=-1, keepdims=True) + EPS)
        xh = x * rstd
        dg_ref[...] += jnp.sum(dxn * xh, axis=0, keepdims=True)
        dxh = dxn * g_ref[...]
        dx_ref[...] = dh_ref[...] + rstd * (dxh - xh * jnp.mean(dxh * xh, axis=-1, keepdims=True))

    return pl.pallas_call(
        body, name="in_proj_bwd", grid=(T // tm,),
        in_specs=[_rows(tm, 2 * D), _rows(tm, D), _rows(tm, 256), _rows(tm, 2 * D), _rows(tm, D), _rows(tm, D),
                  _resident((1, D)), _resident((IN_COLS, D))],
        out_specs=[_rows(tm, D), _whole((1, D))],
        out_shape=[jax.ShapeDtypeStruct((T, D), F32), jax.ShapeDtypeStruct((1, D), F32)],
        compiler_params=_params(("arbitrary",)),
    )(dag, dq, dkv, dgg, dh, x2, g1, w_in)


def _tn_matmul(a, b, name, column_blocks=False):
    T, K = a.shape
    N = b.shape[1]
    tk = K if K <= 1024 else K // 2
    tn = N if N <= 1024 else (1024 if N % 1024 == 0 and not column_blocks else N // 4)
    tt = min(2048, T)
    assert K % tk == 0 and N % tn == 0 and T % tt == 0 and tk % 128 == 0 and tn % 128 == 0

    def body(a_ref, b_ref, o_ref):
        @pl.when(pl.program_id(2) == 0)
        def _():
            o_ref[...] = jnp.zeros((tk, tn), F32)

        o_ref[...] += _dot_tn(a_ref[...], b_ref[...])

    return pl.pallas_call(
        body, name=name, grid=(K // tk, N // tn, T // tt),
        in_specs=[pl.BlockSpec((tt, tk), lambda i, j, t: (t, i)), pl.BlockSpec((tt, tn), lambda i, j, t: (t, j))],
        out_specs=(pl.BlockSpec((None, tk, tn), lambda i, j, t: (j, i, 0)) if column_blocks
                   else pl.BlockSpec((tk, tn), lambda i, j, t: (i, j))),
        out_shape=jax.ShapeDtypeStruct((N // tn, K, tn) if column_blocks else (K, N), F32),
        compiler_params=_params(("parallel", "parallel", "arbitrary")),
    )(a, b)


def _place():
    x, y, c = lax.axis_index("x"), lax.axis_index("y"), lax.axis_index("c")
    chips = [(1 - x, y), (x, 1 - y), (1 - x, 1 - y)]
    return x, y, c, chips


def _own_slot(slots, mine):
    chip = 2 * lax.axis_index("x") + lax.axis_index("y")
    return lax.dynamic_update_slice(slots, mine[None], (chip,) + (0,) * mine.ndim)


def _row_tile(rows, unit):
    return max(t for t in range(unit, 513, unit) if rows % t == 0)


def _gather_weights(pack):
    rows = pack.shape[0]
    half = rows // 2

    def body(src, dst, token, send_sems, recv_sems):
        x, y, c, chips = _place()

        def piece(px, py, pc):
            return dst.at[2 * px + py, pl.ds(pc * half, half), :]

        def copy(k, block, to, from_src=False):
            return pltpu.make_async_remote_copy(
                src_ref=src.at[pl.ds(c * half, half), :] if from_src else piece(*block), dst_ref=piece(*block),
                send_sem=send_sems.at[k], recv_sem=recv_sems.at[k], device_id=to, device_id_type=MESH)

        first = [copy(k, (x, y, c), (*chip, c), from_src=True) for k, chip in enumerate(chips)]
        for cp in first:
            cp.start()
        passed = [copy(3 + k, (*chip, c), (x, y, 1 - c)) for k, chip in enumerate(chips)]
        for k, chip in enumerate(chips):
            copy(k, (*chip, c), (x, y, c)).wait_recv()
            passed[k].start()
        for k, chip in enumerate(chips):
            copy(3 + k, (*chip, 1 - c), (x, y, c)).wait_recv()
        for cp in first + passed:
            cp.wait_send()
        token[...] = jnp.zeros((8, 128), F32)

    got, token = pl.pallas_call(
        body, name="gather_weights",
        in_specs=[pl.BlockSpec(memory_space=pl.ANY)],
        out_specs=[pl.BlockSpec(memory_space=pl.ANY), pl.BlockSpec(memory_space=pltpu.VMEM)],
        out_shape=[jax.ShapeDtypeStruct((4, rows, D), pack.dtype), jax.ShapeDtypeStruct((8, 128), F32)],
        scratch_shapes=[pltpu.SemaphoreType.DMA((6,)), pltpu.SemaphoreType.DMA((6,))],
        compiler_params=pltpu.CompilerParams(has_side_effects=True),
    )(pack)
    return _own_slot(got, pack), token[0, 0]


def _add_halves(g, got, c_idx, name="grad_add_halves"):
    rows, w = g.shape[1], g.shape[2]
    half = rows // 2
    tr = _row_tile(half, 16)
    nt = half // tr

    def body(c_ref, g_ref, r_ref, o_ref):
        o_ref[...] = (g_ref[...] + r_ref[...]).astype(BF16)

    return pl.pallas_call(
        body, name=name,
        grid_spec=pltpu.PrefetchScalarGridSpec(
            num_scalar_prefetch=1, grid=(4, nt),
            in_specs=[pl.BlockSpec((1, tr, w), lambda q, i, c_ref: (q, c_ref[0] * nt + i, 0)),
                      pl.BlockSpec((1, tr, w), lambda q, i, c_ref: (q, i, 0))],
            out_specs=pl.BlockSpec((1, tr, w), lambda q, i, c_ref: (q, i, 0))),
        out_shape=jax.ShapeDtypeStruct((4, half, w), BF16),
        compiler_params=_params(("parallel", "parallel")),
    )(c_idx, g, got)


def _own_piece(p):
    chip = 2 * lax.axis_index("x") + lax.axis_index("y")
    return lax.dynamic_index_in_dim(p, chip, axis=0, keepdims=False)


def _sum_chips(r, c_idx, name="grad_sum_chips"):
    half, w = r.shape[1], r.shape[2]
    tr = _row_tile(half, 16)
    nt = half // tr

    def body(c_ref, r_ref, o_ref):
        acc = r_ref[0].astype(F32)
        for q in range(1, 4):
            acc = acc + r_ref[q].astype(F32)
        o_ref[...] = acc

    return pl.pallas_call(
        body, name=name,
        grid_spec=pltpu.PrefetchScalarGridSpec(
            num_scalar_prefetch=1, grid=(nt,),
            in_specs=[pl.BlockSpec((4, tr, w), lambda i, c_ref: (0, i, 0))],
            out_specs=pl.BlockSpec((tr, w), lambda i, c_ref: (c_ref[0] * nt + i, 0))),
        out_shape=jax.ShapeDtypeStruct((2 * half, w), F32),
        compiler_params=_params(("parallel",)),
    )(c_idx, r)


def _join_halves(f):
    half = f.shape[0] // 2

    def body(src, dst, send_sem, recv_sem):
        x, y, c, _ = _place()
        cp = pltpu.make_async_remote_copy(
            src_ref=src.at[pl.ds(c * half, half), :], dst_ref=dst.at[pl.ds(c * half, half), :], send_sem=send_sem,
            recv_sem=recv_sem, device_id=(x, y, 1 - c), device_id_type=MESH)
        cp.start()
        pltpu.make_async_remote_copy(
            src_ref=src.at[pl.ds(c * half, half), :], dst_ref=dst.at[pl.ds((1 - c) * half, half), :],
            send_sem=send_sem, recv_sem=recv_sem, device_id=(x, y, 1 - c), device_id_type=MESH).wait_recv()
        cp.wait_send()

    return pl.pallas_call(
        body, name="grad_join_halves",
        in_specs=[pl.BlockSpec(memory_space=pl.ANY)], out_specs=pl.BlockSpec(memory_space=pl.ANY),
        out_shape=jax.ShapeDtypeStruct(f.shape, f.dtype), input_output_aliases={0: 0},
        scratch_shapes=[pltpu.SemaphoreType.DMA, pltpu.SemaphoreType.DMA],
        compiler_params=pltpu.CompilerParams(has_side_effects=True),
    )(f)


_HBM = pl.BlockSpec(memory_space=pltpu.HBM)
_SEM = pl.BlockSpec(memory_space=pltpu.SEMAPHORE)
_EFFECT = pltpu.SideEffectType.DATAFLOW_SIDE_EFFECTING


def _start_copies(name, bufs, n_sems, plan):
    nb = len(bufs)

    def body(*refs):
        for cp in plan(refs[:nb], refs[nb], refs[nb + 1])[0]:
            cp.start()
        refs[-1][...] = jnp.zeros((8, 128), F32)

    out = pl.pallas_call(
        body, name=name,
        out_shape=(pltpu.SemaphoreType.DMA((n_sems,)), pltpu.SemaphoreType.DMA((n_sems,)),
                   *[pltpu.HBM(b.shape, b.dtype) for b in bufs], jax.ShapeDtypeStruct((8, 128), F32)),
        in_specs=[_HBM] * nb, out_specs=(_SEM, _SEM, *[_HBM] * nb, pl.BlockSpec(memory_space=pltpu.VMEM)),
        input_output_aliases={i: 2 + i for i in range(nb)},
        compiler_params=pltpu.CompilerParams(has_side_effects=_EFFECT),
    )(*[pltpu.with_memory_space_constraint(b, pltpu.HBM) for b in bufs])
    return out[0], out[1], list(out[2:2 + nb]), out[-1]


def _wait_copies(name, send_sems, recv_sems, bufs, after, plan):
    nb = len(bufs)

    def body(*refs):
        _, sends, recvs = plan(refs[:nb], refs[nb], refs[nb + 1])
        for cp in sends:
            cp.wait_send()
        for cp in recvs:
            cp.wait_recv()

    out = pl.pallas_call(
        body, name=name,
        out_shape=tuple(pltpu.HBM(b.shape, b.dtype) for b in bufs),
        in_specs=[_HBM] * nb + [_SEM, _SEM] + [pl.BlockSpec(memory_space=pl.ANY)] * len(after),
        out_specs=tuple([_HBM] * nb),
        input_output_aliases={i: i for i in range(nb)},
        compiler_params=pltpu.CompilerParams(has_side_effects=_EFFECT),
    )(*bufs, send_sems, recv_sems, *after)
    return list(out)


def _plan_gather_direct(halves):
    n = len(halves)

    def plan(refs, send_sems, recv_sems):
        x, y, c, chips = _place()
        starts, recvs = [], []
        for b, half in enumerate(halves):
            src, land = refs[b], refs[n + b]
            for k, (cx, cy) in enumerate(chips):
                for d in range(2):
                    other = c if d == 0 else 1 - c
                    i = 6 * b + 2 * k + d
                    starts.append(pltpu.make_async_remote_copy(
                        src_ref=src.at[pl.ds(c * half, half), :],
                        dst_ref=land.at[2 * x + y, pl.ds(c * half, half), :],
                        send_sem=send_sems.at[i], recv_sem=recv_sems.at[i], device_id=(cx, cy, other),
                        device_id_type=MESH))
                    recvs.append(pltpu.make_async_remote_copy(
                        src_ref=src.at[pl.ds(c * half, half), :],
                        dst_ref=land.at[2 * cx + cy, pl.ds(other * half, half), :],
                        send_sem=send_sems.at[i], recv_sem=recv_sems.at[i], device_id=(cx, cy, other),
                        device_id_type=MESH))
        return starts, starts, recvs
    return plan


def _plan_swap_halves(halves):
    n = len(halves)

    def plan(refs, send_sems, recv_sems):
        x, y, c, _ = _place()
        cps = [pltpu.make_async_remote_copy(
            src_ref=refs[b].at[:, pl.ds((1 - c) * half, half), :], dst_ref=refs[n + b], send_sem=send_sems.at[b],
            recv_sem=recv_sems.at[b], device_id=(x, y, 1 - c), device_id_type=MESH)
            for b, half in enumerate(halves)]
        return cps, cps, cps
    return plan


def _plan_scatter_chips(n):
    def plan(refs, send_sems, recv_sems):
        x, y, c, chips = _place()
        me = 2 * x + y
        starts, recvs = [], []
        for b in range(n):
            src, land = refs[b], refs[n + b]
            for k, (cx, cy) in enumerate(chips):
                i = 3 * b + k
                starts.append(pltpu.make_async_remote_copy(
                    src_ref=src.at[2 * cx + cy], dst_ref=land.at[me], send_sem=send_sems.at[i],
                    recv_sem=recv_sems.at[i], device_id=(cx, cy, c), device_id_type=MESH))
                recvs.append(pltpu.make_async_remote_copy(
                    src_ref=src.at[me], dst_ref=land.at[2 * cx + cy], send_sem=send_sems.at[i],
                    recv_sem=recv_sems.at[i], device_id=(cx, cy, c), device_id_type=MESH))
        return starts, starts, recvs
    return plan


def _plan_join_halves(halves):
    def plan(refs, send_sems, recv_sems):
        x, y, c, _ = _place()
        starts, recvs = [], []
        for b, half in enumerate(halves):
            mine, theirs = refs[b].at[pl.ds(c * half, half), :], refs[b].at[pl.ds((1 - c) * half, half), :]
            starts.append(pltpu.make_async_remote_copy(
                src_ref=mine, dst_ref=mine, send_sem=send_sems.at[b], recv_sem=recv_sems.at[b],
                device_id=(x, y, 1 - c), device_id_type=MESH))
            recvs.append(pltpu.make_async_remote_copy(
                src_ref=mine, dst_ref=theirs, send_sem=send_sems.at[b], recv_sem=recv_sems.at[b],
                device_id=(x, y, 1 - c), device_id_type=MESH))
        return starts, starts, recvs
    return plan


def _allreduce_small(vec):
    def body(v_ref, o_ref, gath, send_sems, recv_sems):
        x, y, c, _ = _place()
        me = 4 * x + 2 * y + c
        gath[me] = v_ref[...]
        sends = []
        for k in range(1, 8):
            peer = (x ^ (k >> 2), y ^ ((k >> 1) & 1), c ^ (k & 1))
            sends.append(pltpu.make_async_remote_copy(
                src_ref=v_ref, dst_ref=gath.at[me], send_sem=send_sems.at[k - 1], recv_sem=recv_sems.at[k - 1],
                device_id=peer, device_id_type=MESH))
        for cp in sends:
            cp.start()
        for k in range(1, 8):
            peer = (x ^ (k >> 2), y ^ ((k >> 1) & 1), c ^ (k & 1))
            pltpu.make_async_remote_copy(
                src_ref=v_ref, dst_ref=gath.at[4 * peer[0] + 2 * peer[1] + peer[2]], send_sem=send_sems.at[k - 1],
                recv_sem=recv_sems.at[k - 1], device_id=peer, device_id_type=MESH).wait_recv()
        for cp in sends:
            cp.wait_send()
        acc = gath[0]
        for d in range(1, 8):
            acc = acc + gath[d]
        o_ref[...] = acc

    return pl.pallas_call(
        body, name="allreduce_small",
        in_specs=[pl.BlockSpec(memory_space=pltpu.VMEM)], out_specs=pl.BlockSpec(memory_space=pltpu.VMEM),
        out_shape=jax.ShapeDtypeStruct(vec.shape, F32),
        scratch_shapes=[pltpu.VMEM((8,) + vec.shape, F32), pltpu.SemaphoreType.DMA((7,)),
                        pltpu.SemaphoreType.DMA((7,))],
    )(vec)


def _adamw(w, g, m, v, name, after):
    shape = w.shape
    if w.ndim == 1 or w.size <= 128 * 128:
        two_d = (1, w.size) if w.size % 128 else (w.size // 128, 128)
    else:
        two_d = (w.shape[0], w.size // w.shape[0])
    rows, cols = two_d
    tr = _row_tile(rows, 8) if rows % 8 == 0 and rows > 512 else rows

    def body(w_ref, g_ref, m_ref, v_ref, after_ref, d_ref, nm_ref, nv_ref):
        gr = g_ref[...]
        nm = B1 * m_ref[...] + (1.0 - B1) * gr
        nv = B2 * v_ref[...] + (1.0 - B2) * (gr * gr)
        m_hat = nm / (1.0 - B1 ** STEP)
        v_hat = nv / (1.0 - B2 ** STEP)
        d_ref[...] = -LR * (m_hat / (jnp.sqrt(v_hat) + AEPS) + WD * w_ref[...])
        nm_ref[...] = nm
        nv_ref[...] = nv

    spec = pl.BlockSpec((tr, cols), lambda i: (i, 0))
    outs = pl.pallas_call(
        body, name=name, grid=(rows // tr,),
        in_specs=[spec] * 4 + [pl.BlockSpec(memory_space=pl.ANY)], out_specs=[spec] * 3,
        out_shape=[jax.ShapeDtypeStruct(two_d, F32)] * 3,
        compiler_params=_params(("parallel",)),
    )(*[t.reshape(two_d) for t in (w, g, m, v)], after)
    return [o.reshape(shape) for o in outs]


def _rows_stacked(g, lo, n_rows):
    return g[:, lo:lo + n_rows].reshape(4 * n_rows, D)


def _rows_to_slots(t):
    return t.reshape(4, t.shape[0] // 4, D)


def _pack_first(w_in, w_conv_out, conv_dw_w):
    dw = jnp.pad(conv_dw_w.reshape(CW, 256), ((0, 1), (0, 0)))
    dw_bits = lax.bitcast_convert_type(dw, BF16).reshape(16, D)
    return jnp.concatenate([w_in.T.astype(BF16), w_conv_out.astype(BF16), dw_bits, jnp.zeros((16, D), BF16)],
                           axis=0)


def _unpack_first(g):
    w_in_t = _rows_stacked(g, 0, ROWS_W_IN)
    wc = _rows_stacked(g, ROWS_W_IN, ROWS_SQ)
    o = ROWS_W_IN + ROWS_SQ
    dw = lax.bitcast_convert_type(g[:, o:o + 16].reshape(4, 32, 256, 2), F32)
    return w_in_t, wc, jnp.transpose(dw, (1, 0, 2)).reshape(32, D)


def _pack_late(w_attn_out, w_merge_out, w_ffn_down):
    return jnp.concatenate([w_attn_out.astype(BF16), w_merge_out.astype(BF16), w_ffn_down.astype(BF16)], axis=0)


def _unpack_late(g):
    return (_rows_stacked(g, 0, ROWS_SQ), _rows_stacked(g, ROWS_SQ, ROWS_SQ),
            _rows_stacked(g, 2 * ROWS_SQ, ROWS_DOWN))


class _Exchanges:
    def __init__(self, late_pack, wf_shard):
        self.c_idx = lax.axis_index("c").astype(jnp.int32).reshape(1)
        packs = [late_pack, wf_shard]
        self.late_plan = _plan_gather_direct([p.shape[0] // 2 for p in packs])
        slots = [lax.empty((4,) + p.shape, BF16) for p in packs]
        self.late = _start_copies("gather_late_start", packs + slots, 6 * len(packs), self.late_plan)
        self.first_token = self.late[3][0, 0]

    def late_weights(self, after):
        send_sems, recv_sems, bufs, _ = self.late
        pack, wf_shard, slots, wf_slots = _wait_copies("gather_late_wait", send_sems, recv_sems, bufs, after,
                                                       self.late_plan)
        wa, wm, wd = _unpack_late(_own_slot(slots, pack))
        return wa, wm, _own_slot(wf_slots, wf_shard), wd

    def reduce_start(self, d_wd, d_wf4, d_wm, d_wa, d_wc):
        gs = [_rows_to_slots(d_wd), jnp.concatenate([_rows_to_slots(t) for t in (d_wm, d_wa, d_wc)], axis=1), d_wf4]
        self.halves = [g.shape[1] // 2 for g in gs]
        self.swap_plan = _plan_swap_halves(self.halves)
        lands = [lax.empty((4, h, g.shape[2]), F32) for g, h in zip(gs, self.halves)]
        self.swap = _start_copies("grad_swap_start", gs + lands, len(gs), self.swap_plan)
        return self.swap[3][0, 0]

    def reduce_mid(self, after):
        send_sems, recv_sems, bufs, _ = self.swap
        bufs = _wait_copies("grad_swap_wait", send_sems, recv_sems, bufs, after, self.swap_plan)
        n = len(self.halves)
        ps = [_add_halves(bufs[b], bufs[n + b], self.c_idx, "grad_add_halves_%d" % b) for b in range(n)]
        self.scatter_plan = _plan_scatter_chips(n)
        self.scatter = _start_copies("grad_scatter_start", ps + [lax.empty(p.shape, BF16) for p in ps], 3 * n,
                                     self.scatter_plan)
        return self.scatter[3][0, 0]

    def reduce_late(self, after):
        send_sems, recv_sems, bufs, _ = self.scatter
        bufs = _wait_copies("grad_scatter_wait", send_sems, recv_sems, bufs, after, self.scatter_plan)
        n = len(self.halves)
        fs = [_sum_chips(_own_slot(bufs[n + b], _own_piece(bufs[b])), self.c_idx, "grad_sum_chips_%d" % b)
              for b in range(n)]
        self.join_plan = _plan_join_halves(self.halves)
        self.join = _start_copies("grad_join_start", fs, n, self.join_plan)

    def reduce_end(self, after):
        send_sems, recv_sems, bufs, _ = self.join
        g_wd, sq, g_wf = _wait_copies("grad_join_wait", send_sems, recv_sems, bufs, after, self.join_plan)
        return g_wd, g_wf, sq[0:ROWS_SQ], sq[ROWS_SQ:2 * ROWS_SQ], sq[2 * ROWS_SQ:3 * ROWS_SQ]

    def w_in_start(self, d_w_in_t):
        g = _rows_to_slots(d_w_in_t)
        self.w_half = g.shape[1] // 2
        self.w_swap_plan = _plan_swap_halves([self.w_half])
        self.w_swap = _start_copies("grad_w_in_swap_start", [g, lax.empty((4, self.w_half, D), F32)], 1,
                                    self.w_swap_plan)
        return self.w_swap[3]

    def w_in_mid(self, after):
        send_sems, recv_sems, bufs, _ = self.w_swap
        g, got = _wait_copies("grad_w_in_swap_wait", send_sems, recv_sems, bufs, after, self.w_swap_plan)
        p = _add_halves(g, got, self.c_idx, "grad_add_halves_w_in")
        self.w_scatter_plan = _plan_scatter_chips(1)
        self.w_scatter = _start_copies("grad_w_in_scatter_start", [p, lax.empty(p.shape, BF16)], 3,
                                       self.w_scatter_plan)
        return self.w_scatter[3]

    def w_in_end(self, after):
        send_sems, recv_sems, bufs, _ = self.w_scatter
        p, got = _wait_copies("grad_w_in_scatter_wait", send_sems, recv_sems, bufs, after, self.w_scatter_plan)
        return _join_halves(_sum_chips(_own_slot(got, _own_piece(p)), self.c_idx, "grad_sum_chips_w_in"))


def _local_grads(x, loss_target, norm_mix_g, conv_dw_b, conv_ln_g, conv_ln_b, q_norm_g, k_norm_g, sinks, norm_ffn_g,
                 w_in, wc, dw, exchanges):
    n_seq, S, _ = x.shape
    T = n_seq * S
    x2 = x.reshape(T, D)
    tgt = loss_target.reshape(T, D)
    row = lambda t: t.reshape(1, -1)
    g1, g2 = row(norm_mix_g), row(norm_ffn_g)
    qg, kg = jnp.broadcast_to(q_norm_g.reshape(HD, 1), (HD, GROUP * BLK)), row(k_norm_g)
    lng, lnb, dwb = row(conv_ln_g), row(conv_ln_b), row(conv_dw_b)
    sink_rows = jnp.repeat(sinks.reshape(NKV, GROUP), BLK, axis=1)

    xn, ag, q, kv, gg, h1, h3, yc = _in_proj_conv_fwd(x2, g1 + exchanges.first_token, w_in, dw, dwb, lng, lnb, wc,
                                                      n_seq, S)
    o = _attn_fwd(q, kv, qg, kg, sink_rows, n_seq, S)
    wa, wm, wf, wd = exchanges.late_weights([o, yc])
    mix, h = _merge_fwd(x2, gg, yc, o, wa, wm)
    dh, dhb, hn, act, dout, dgu, ffn_stats = _ffn(h, tgt, g2, wf, wd)
    d_wd = _tn_matmul(act, dout, "dw_ffn_down")
    d_wf = _tn_matmul(hn, dgu, "dw_ffn_in", column_blocks=True)
    d_wm = _tn_matmul(mix, dhb, "dw_merge")
    dgg, dyc, dya, do, dh3 = _merge_bwd(dhb, gg, yc, o, wm, wa, wc)
    d_wa = _tn_matmul(o, dya, "dw_attn_out")
    d_wc = _tn_matmul(h3, dyc, "dw_conv_out")
    token = exchanges.reduce_start(d_wd, d_wf, d_wm, d_wa, d_wc)
    dq, dkc, dkp, dqg, dsk = _attn_bwd(q, kv, do, qg + token, kg, sink_rows, n_seq, S)
    token = exchanges.reduce_mid([dq])
    dkv, dkg = _kv_bwd(kv, dkc, dkp, jnp.broadcast_to(k_norm_g.reshape(HD, 1), (HD, BLK)), n_seq, S)
    dag, ddw, conv_stats = _conv_bwd(h1, dh3, ag, dw, lng + token, lnb, n_seq, S)
    dx, dg1 = _in_proj_bwd(dag, dq, dkv, dgg, dh, x2, g1, w_in)
    exchanges.reduce_late([dx])
    d_w_in = jnp.concatenate([_tn_matmul(dag, xn, "dw_in_conv"), _tn_matmul(dq, xn, "dw_in_q"),
                              _tn_matmul(dkv, xn, "dw_in_kv"), _tn_matmul(dgg, xn, "dw_in_gates")], axis=0)

    heads = jnp.concatenate([jnp.sum(dqg, axis=1), jnp.sum(dkg, axis=1), dsk[0, :NQ],
                             jnp.zeros((D - 2 * HD - NQ,), F32)])
    vec = jnp.concatenate([dg1, conv_stats[0:3], ffn_stats[0:1], heads[None], ffn_stats[1:2], jnp.zeros((1, D), F32),
                           ddw], axis=0)
    return ffn_stats[1], dx.reshape(x.shape), d_w_in, vec


def kernel(x, norm_mix_g, w_in, conv_dw_w, conv_dw_b, conv_ln_g, conv_ln_b, w_conv_out, q_norm_g, k_norm_g, sinks, w_attn_out, w_merge_out, norm_ffn_g, w_ffn_in, w_ffn_down, loss_target, m_norm_mix_g, m_w_in, m_conv_dw_w, m_conv_dw_b, m_conv_ln_g, m_conv_ln_b, m_w_conv_out, m_q_norm_g, m_k_norm_g, m_sinks, m_w_attn_out, m_w_merge_out, m_norm_ffn_g, m_w_ffn_in, m_w_ffn_down, v_norm_mix_g, v_w_in, v_conv_dw_w, v_conv_dw_b, v_conv_ln_g, v_conv_ln_b, v_w_conv_out, v_q_norm_g, v_k_norm_g, v_sinks, v_w_attn_out, v_w_merge_out, v_norm_ffn_g, v_w_ffn_in, v_w_ffn_down):
    chip = 2 * lax.axis_index("x") + lax.axis_index("y")

    first, token = _gather_weights(_pack_first(w_in, w_conv_out, conv_dw_w))
    exchanges = _Exchanges(_pack_late(w_attn_out, w_merge_out, w_ffn_down) + token.astype(BF16),
                           w_ffn_in.astype(BF16) + token.astype(BF16))
    _, grad_x, d_w_in, vec = _local_grads(x, loss_target, norm_mix_g, conv_dw_b, conv_ln_g, conv_ln_b, q_norm_g,
                                          k_norm_g, sinks, norm_ffn_g, *_unpack_first(first), exchanges)

    g_wd, g_wf, g_wm, g_wa, g_wc = exchanges.reduce_end([d_w_in])
    w_in_token = exchanges.w_in_start(d_w_in)
    small = _allreduce_small(vec + w_in_token[0, 0])
    loss = 0.5 / D * jnp.sum(small[6])
    g_dw = lax.dynamic_slice_in_dim(small[8:8 + CW], chip * 256, 256, axis=1).reshape(CW, 1, 256)
    grads = {
        "norm_mix_g": small[0], "conv_dw_w": g_dw, "conv_dw_b": small[1], "conv_ln_g": small[2],
        "conv_ln_b": small[3], "w_conv_out": g_wc, "q_norm_g": small[5, 0:HD], "k_norm_g": small[5, HD:2 * HD],
        "sinks": small[5, 2 * HD:2 * HD + NQ], "w_attn_out": g_wa, "w_merge_out": g_wm, "norm_ffn_g": small[4],
        "w_ffn_in": g_wf, "w_ffn_down": g_wd,
    }
    weights = dict(norm_mix_g=norm_mix_g, w_in=w_in, conv_dw_w=conv_dw_w, conv_dw_b=conv_dw_b, conv_ln_g=conv_ln_g,
                   conv_ln_b=conv_ln_b, w_conv_out=w_conv_out, q_norm_g=q_norm_g, k_norm_g=k_norm_g, sinks=sinks,
                   w_attn_out=w_attn_out, w_merge_out=w_merge_out, norm_ffn_g=norm_ffn_g, w_ffn_in=w_ffn_in,
                   w_ffn_down=w_ffn_down)
    m_in = dict(norm_mix_g=m_norm_mix_g, w_in=m_w_in, conv_dw_w=m_conv_dw_w, conv_dw_b=m_conv_dw_b,
                conv_ln_g=m_conv_ln_g, conv_ln_b=m_conv_ln_b, w_conv_out=m_w_conv_out, q_norm_g=m_q_norm_g,
                k_norm_g=m_k_norm_g, sinks=m_sinks, w_attn_out=m_w_attn_out, w_merge_out=m_w_merge_out,
                norm_ffn_g=m_norm_ffn_g, w_ffn_in=m_w_ffn_in, w_ffn_down=m_w_ffn_down)
    v_in = dict(norm_mix_g=v_norm_mix_g, w_in=v_w_in, conv_dw_w=v_conv_dw_w, conv_dw_b=v_conv_dw_b,
                conv_ln_g=v_conv_ln_g, conv_ln_b=v_conv_ln_b, w_conv_out=v_w_conv_out, q_norm_g=v_q_norm_g,
                k_norm_g=v_k_norm_g, sinks=v_sinks, w_attn_out=v_w_attn_out, w_merge_out=v_w_merge_out,
                norm_ffn_g=v_norm_ffn_g, w_ffn_in=v_w_ffn_in, w_ffn_down=v_w_ffn_down)
    names = list(weights)
    updates = {}
    after = exchanges.w_in_mid([small])
    for n in names:
        if n != "w_in":
            updates[n] = _adamw(weights[n], grads[n], m_in[n], v_in[n], "adamw_" + n, after)
    g_w_in_t = exchanges.w_in_end([updates[n][0] for n in updates])
    grads["w_in"] = g_w_in_t.T
    updates["w_in"] = [t.T for t in _adamw(w_in.T, g_w_in_t, m_w_in.T, v_w_in.T, "adamw_w_in", g_w_in_t)]
    return (loss, grad_x, *[grads[n] for n in names], *[updates[n][0] for n in names],
            *[updates[n][1] for n in names], *[updates[n][2] for n in names])
```

```python
import math

import jax
import jax.numpy as jnp
import numpy as np
from jax import lax
from jax.experimental import pallas as pl
from jax.experimental.pallas import tpu as pltpu

F32 = jnp.float32
BF16 = jnp.bfloat16

D = 1024
CW = 31
HD = 64
NQ = 16
NKV = 2
GROUP = NQ // NKV
BLK = 128
DFF = 2816
EPS = 1e-6
NEG = -1e30
IN_COLS = 5376
U_CONV, U_Q, U_KV = slice(0, 2 * D), slice(2 * D, 3 * D), slice(3 * D, 3 * D + 2 * NKV * HD)
U_GATES = slice(3 * D + 2 * NKV * HD, IN_COLS)
SCALE = 1.0 / math.sqrt(HD)

LR, B1, B2, AEPS, WD, STEP = 0.001, 0.9, 0.999, 1e-08, 0.01, 10

MIB = 1024 * 1024
MESH = pl.DeviceIdType.MESH

ROWS_W_IN = IN_COLS // 4
ROWS_SQ = D // 4
ROWS_FFN_IN = 2 * DFF // 4
ROWS_DOWN = DFF // 4


def _sig(x):
    return 1.0 / (1.0 + jnp.exp(-x))


def _dot(a, b):
    return jnp.dot(a, b, preferred_element_type=F32)


def _dot_nt(a, b):
    return lax.dot_general(a, b, (((1,), (1,)), ((), ())), preferred_element_type=F32)


def _dot_tn(a, b):
    return lax.dot_general(a, b, (((0,), (0,)), ((), ())), preferred_element_type=F32)


def _params(sem, vmem_mib=48):
    return pltpu.CompilerParams(dimension_semantics=sem, vmem_limit_bytes=vmem_mib * MIB)


def _resident(shape):
    return pl.BlockSpec(shape, lambda *_: (0,) * len(shape), pipeline_mode=pl.Buffered(1))


def _whole(shape):
    return pl.BlockSpec(shape, lambda *_: (0,) * len(shape))


def _rows(tm, cols):
    return pl.BlockSpec((tm, cols), lambda i: (i, 0))


def _tap_phases():
    return [(phase, list(range(phase, CW, 8))) for phase in range(8)]


def _shift_copies(dst, src, base):
    for phase, taps in _tap_phases():
        n = dst.shape[1] - 8 * (4 - len(taps))
        dst[phase, 0:n, :] = src[base + phase:base + phase + n, :]


def _in_proj_conv_fwd(x2, g1, w_in, dw, dwb, lng, lnb, wc, n_seq, S):
    T = n_seq * S
    tc = min(256, S)
    nt = S // tc

    def body(x_ref, g_ref, w_ref, dw_ref, dwb_ref, lng_ref, lnb_ref, wc_ref, xn_ref, ag_ref, q_ref, kv_ref, gg_ref,
             h1_ref, h3_ref, yc_ref, ext, sh):
        i = pl.program_id(1)

        @pl.when(i == 0)
        def _():
            ext[0:32, :] = jnp.zeros((32, D), F32)

        x = x_ref[...]
        rstd = lax.rsqrt(jnp.mean(x * x, axis=-1, keepdims=True) + EPS)
        xn = (x * rstd * g_ref[...]).astype(BF16)
        xn_ref[...] = xn
        ag = _dot_nt(xn, w_ref[U_CONV, :])
        ag_ref[...] = ag
        ext[32:32 + tc, :] = ag[:, 0:D] * _sig(ag[:, D:2 * D])
        q_ref[...] = _dot_nt(xn, w_ref[U_Q, :])
        kv_ref[...] = _dot_nt(xn, w_ref[U_KV, :])
        gg_ref[...] = _dot_nt(xn, w_ref[U_GATES, :])
        _shift_copies(sh, ext, 2)
        for cb in range(D // 128):
            cs = slice(cb * 128, (cb + 1) * 128)
            acc = jnp.broadcast_to(dwb_ref[:, cs], (tc, 128))
            for phase, taps in _tap_phases():
                for m, j in enumerate(taps):
                    acc = acc + dw_ref[j:j + 1, cs] * sh[phase, 8 * m:8 * m + tc, cs]
            h1_ref[:, cs] = acc
        ext[0:32, :] = ext[tc:tc + 32, :]
        h1 = h1_ref[...]
        mu = jnp.mean(h1, axis=-1, keepdims=True)
        cen = h1 - mu
        var = jnp.mean(cen * cen, axis=-1, keepdims=True)
        h2 = cen * lax.rsqrt(var + EPS) * lng_ref[...] + lnb_ref[...]
        h3 = (h2 * _sig(h2)).astype(BF16)
        h3_ref[...] = h3
        yc_ref[...] = _dot(h3, wc_ref[...])

    tile = lambda cols: pl.BlockSpec((tc, cols), lambda b, i: (b * nt + i, 0))
    shape = lambda cols, dtype: jax.ShapeDtypeStruct((T, cols), dtype)
    return pl.pallas_call(
        body, name="in_proj_conv_fwd", grid=(n_seq, nt),
        in_specs=[tile(D), _resident((1, D)), _resident((IN_COLS, D)), _resident((32, D)), _resident((1, D)),
                  _resident((1, D)), _resident((1, D)), _resident((D, D))],
        out_specs=[tile(D), tile(2 * D), tile(D), tile(256), tile(2 * D), tile(D), tile(D), tile(D)],
        out_shape=[shape(D, BF16), shape(2 * D, F32), shape(D, F32), shape(256, F32), shape(2 * D, F32),
                   shape(D, F32), shape(D, BF16), shape(D, F32)],
        scratch_shapes=[pltpu.VMEM((32 + tc, D), F32), pltpu.VMEM((8, tc + 24, D), F32)],
        compiler_params=_params(("parallel", "arbitrary"), 56),
    )(x2, g1, w_in, dw, dwb, lng, lnb, wc)


def _attn_consts():
    k = np.arange(BLK)[:, None]
    i = np.arange(GROUP * BLK)[None, :] % BLK
    from_prev = k > i
    dist = np.where(from_prev, i + BLK - k, i - k).astype(np.float32)
    head = np.arange(GROUP * BLK)[None, :] // BLK
    bias = []
    for kh in range(NKV):
        slope = np.exp2(-8.0 * (kh * GROUP + head + 1) / NQ).astype(np.float32)
        bias.append(-slope * dist)
    return jnp.asarray(from_prev.astype(np.float32)), jnp.asarray(np.stack(bias))


def _heads_to_lanes(t, kh):
    return jnp.concatenate([t[(kh * GROUP + g) * HD:(kh * GROUP + g + 1) * HD, :] for g in range(GROUP)], axis=1)


def _lanes_to_heads(t):
    return jnp.concatenate([t[:, g * BLK:(g + 1) * BLK] for g in range(GROUP)], axis=0)


def _rms64(t):
    return lax.rsqrt(jnp.mean(t * t, axis=-1, keepdims=True) + EPS)


def _attn_probs(kh, n, q_t, kvc_ref, kvp_ref, qg_ref, kg_ref, tri_ref, bias_ref, sink_ref):
    ks = slice(kh * HD, (kh + 1) * HD)
    vs = slice(2 * HD + kh * HD, 2 * HD + (kh + 1) * HD)
    kp, kc = kvp_ref[:, ks], kvc_ref[:, ks]
    kpb = (kp * _rms64(kp) * kg_ref[...]).astype(BF16)
    kcb = (kc * _rms64(kc) * kg_ref[...]).astype(BF16)
    qs = _heads_to_lanes(q_t, kh)
    rq = lax.rsqrt(jnp.mean(qs * qs, axis=0, keepdims=True) + EPS)
    qy = qs * rq
    qhb = (qy * (qg_ref[...] * SCALE)).astype(BF16)
    from_prev = tri_ref[...] > 0.5
    no_prev = jnp.where(n > 0, 0.0, NEG)
    s = jnp.where(from_prev, _dot(kpb, qhb) + no_prev, _dot(kcb, qhb)) + bias_ref[kh]
    sink = sink_ref[kh:kh + 1, :]
    m = jnp.maximum(jnp.max(s, axis=0, keepdims=True), sink)
    e = jnp.exp(s - m)
    es = jnp.exp(sink - m)
    rz = 1.0 / (jnp.sum(e, axis=0, keepdims=True) + es)
    prev_mask = tri_ref[...].astype(BF16)
    return e * rz, es * rz, prev_mask, qhb, kpb, kcb, kvp_ref[:, vs].astype(BF16), kvc_ref[:, vs].astype(BF16), qy, rq


def _unfold(t, prev_mask):
    prev = t * prev_mask
    return prev, t - prev


def _attn_specs(n_seq, S):
    nb = S // BLK
    cur = lambda cols: pl.BlockSpec((BLK, cols), lambda b, n: (b * nb + n, 0))
    prev = lambda cols: pl.BlockSpec((BLK, cols), lambda b, n: (b * nb + jnp.maximum(n - 1, 0), 0))
    consts = [_resident((HD, GROUP * BLK)), _resident((1, HD)), _resident((BLK, GROUP * BLK)),
              _resident((NKV, BLK, GROUP * BLK)), _resident((NKV, GROUP * BLK))]
    return nb, cur, prev, consts


def _attn_fwd(q, kv, qg_cols, kg, sink_rows, n_seq, S):
    T = n_seq * S
    nb, cur, prev, consts = _attn_specs(n_seq, S)
    tri, bias = _attn_consts()

    def body(q_ref, kvc_ref, kvp_ref, qg_ref, kg_ref, tri_ref, bias_ref, sink_ref, o_ref):
        n = pl.program_id(1)
        q_t = q_ref[...].T
        o_t = []
        for kh in range(NKV):
            p, _, prev_mask, _, _, _, vpb, vcb, _, _ = _attn_probs(kh, n, q_t, kvc_ref, kvp_ref, qg_ref, kg_ref,
                                                                   tri_ref, bias_ref, sink_ref)
            pp, pc = _unfold(p.astype(BF16), prev_mask)
            o_t.append(_lanes_to_heads(_dot_tn(vpb, pp) + _dot_tn(vcb, pc)))
        o_ref[...] = jnp.concatenate(o_t, axis=0).T.astype(BF16)

    return pl.pallas_call(
        body, name="attn_fwd", grid=(n_seq, nb),
        in_specs=[cur(D), cur(256), prev(256)] + consts,
        out_specs=cur(D),
        out_shape=jax.ShapeDtypeStruct((T, D), BF16),
        compiler_params=_params(("parallel", "parallel")),
    )(q, kv, kv, qg_cols, kg, tri, bias, sink_rows)


def _merge_fwd(x2, gg, yc, o, wa, wm):
    T = x2.shape[0]
    tm = min(512, T)

    def body(x_ref, gg_ref, yc_ref, o_ref, wa_ref, wm_ref, mix_ref, h_ref):
        ya = _dot(o_ref[...], wa_ref[...])
        mix = (_sig(gg_ref[:, 0:D]) * yc_ref[...] + _sig(gg_ref[:, D:2 * D]) * ya).astype(BF16)
        mix_ref[...] = mix
        h_ref[...] = x_ref[...] + _dot(mix, wm_ref[...])

    return pl.pallas_call(
        body, name="merge_fwd", grid=(T // tm,),
        in_specs=[_rows(tm, D), _rows(tm, 2 * D), _rows(tm, D), _rows(tm, D), _resident((D, D)), _resident((D, D))],
        out_specs=[_rows(tm, D), _rows(tm, D)],
        out_shape=[jax.ShapeDtypeStruct((T, D), BF16), jax.ShapeDtypeStruct((T, D), F32)],
        compiler_params=_params(("parallel",)),
    )(x2, gg, yc, o, wa, wm)


FF_CHUNK = DFF // 2


def _ffn(h, tgt, g2, wf, wd):
    T = h.shape[0]
    tm = min(256, T)

    def body(h_ref, t_ref, g_ref, wf_ref, wd_ref, dh_ref, dhb_ref, hn_ref, act_ref, dout_ref, dgu_ref, st_ref,
             gsc, usc):
        @pl.when(pl.program_id(0) == 0)
        def _():
            st_ref[...] = jnp.zeros((8, D), F32)

        hh = h_ref[...]
        rstd = lax.rsqrt(jnp.mean(hh * hh, axis=-1, keepdims=True) + EPS)
        hhat = hh * rstd
        hn = (hhat * g_ref[...]).astype(BF16)
        hn_ref[...] = hn
        out = hh
        for c in range(DFF // FF_CHUNK):
            cs = slice(c * FF_CHUNK, (c + 1) * FF_CHUNK)
            us = slice(DFF + c * FF_CHUNK, DFF + (c + 1) * FF_CHUNK)
            g = _dot(hn, wf_ref[c])
            u = _dot(hn, wf_ref[2 + c])
            gsc[:, cs] = g
            usc[:, cs] = u
            act = (g * _sig(g) * u).astype(BF16)
            act_ref[:, cs] = act
            out = out + _dot(act, wd_ref[cs, :])
        err = out - t_ref[...]
        dout = err * (1.0 / D)
        doutb = dout.astype(BF16)
        dout_ref[...] = doutb
        dhn = jnp.zeros((tm, D), F32)
        for c in range(DFF // FF_CHUNK):
            cs = slice(c * FF_CHUNK, (c + 1) * FF_CHUNK)
            us = slice(DFF + c * FF_CHUNK, DFF + (c + 1) * FF_CHUNK)
            g = gsc[:, cs]
            u = usc[:, cs]
            dact = _dot_nt(doutb, wd_ref[cs, :])
            sg = _sig(g)
            dg = (dact * u * (sg * (1.0 + g * (1.0 - sg)))).astype(BF16)
            du = (dact * (g * sg)).astype(BF16)
            dgu_ref[:, cs] = dg
            dgu_ref[:, us] = du
            dhn = dhn + _dot_nt(dg, wf_ref[c]) + _dot_nt(du, wf_ref[2 + c])
        st_ref[0:1, :] += jnp.sum(dhn * hhat, axis=0, keepdims=True)
        st_ref[1:2, :] += jnp.sum(err * err, axis=0, keepdims=True)
        dhh = dhn * g_ref[...]
        dh = dout + rstd * (dhh - hhat * jnp.mean(dhh * hhat, axis=-1, keepdims=True))
        dh_ref[...] = dh
        dhb_ref[...] = dh.astype(BF16)

    return pl.pallas_call(
        body, name="ffn_fwd_bwd", grid=(T // tm,),
        in_specs=[_rows(tm, D), _rows(tm, D), _resident((1, D)), _resident((4, D, FF_CHUNK)), _resident((DFF, D))],
        out_specs=[_rows(tm, D), _rows(tm, D), _rows(tm, D), _rows(tm, DFF), _rows(tm, D), _rows(tm, 2 * DFF),
                   _whole((8, D))],
        out_shape=[jax.ShapeDtypeStruct((T, D), F32), jax.ShapeDtypeStruct((T, D), BF16),
                   jax.ShapeDtypeStruct((T, D), BF16), jax.ShapeDtypeStruct((T, DFF), BF16),
                   jax.ShapeDtypeStruct((T, D), BF16), jax.ShapeDtypeStruct((T, 2 * DFF), BF16),
                   jax.ShapeDtypeStruct((8, D), F32)],
        scratch_shapes=[pltpu.VMEM((tm, DFF), F32), pltpu.VMEM((tm, DFF), F32)],
        compiler_params=_params(("arbitrary",), 56),
    )(h, tgt, g2, wf, wd)


def _merge_bwd(dhb, gg, yc, o, wm, wa, wc):
    T = dhb.shape[0]
    tm = min(512, T)

    def body(dh_ref, gg_ref, yc_ref, o_ref, wm_ref, wa_ref, wc_ref, dgg_ref, dyc_ref, dya_ref, do_ref, dh3_ref):
        dmix = _dot_nt(dh_ref[...], wm_ref[...])
        gc = _sig(gg_ref[:, 0:D])
        ga = _sig(gg_ref[:, D:2 * D])
        yc = yc_ref[...]
        ya = _dot(o_ref[...], wa_ref[...])
        dgg_ref[:, 0:D] = (dmix * yc * gc * (1.0 - gc)).astype(BF16)
        dgg_ref[:, D:2 * D] = (dmix * ya * ga * (1.0 - ga)).astype(BF16)
        dyc = (dmix * gc).astype(BF16)
        dya = (dmix * ga).astype(BF16)
        dyc_ref[...] = dyc
        dya_ref[...] = dya
        do_ref[...] = _dot_nt(dya, wa_ref[...]).astype(BF16)
        dh3_ref[...] = _dot_nt(dyc, wc_ref[...])

    return pl.pallas_call(
        body, name="merge_bwd", grid=(T // tm,),
        in_specs=[_rows(tm, D), _rows(tm, 2 * D), _rows(tm, D), _rows(tm, D), _resident((D, D)), _resident((D, D)),
                  _resident((D, D))],
        out_specs=[_rows(tm, 2 * D), _rows(tm, D), _rows(tm, D), _rows(tm, D), _rows(tm, D)],
        out_shape=[jax.ShapeDtypeStruct((T, 2 * D), BF16), jax.ShapeDtypeStruct((T, D), BF16),
                   jax.ShapeDtypeStruct((T, D), BF16), jax.ShapeDtypeStruct((T, D), BF16),
                   jax.ShapeDtypeStruct((T, D), F32)],
        compiler_params=_params(("parallel",), 56),
    )(dhb, gg, yc, o, wm, wa, wc)


def _attn_bwd(q, kv, do, qg_cols, kg, sink_rows, n_seq, S):
    T = n_seq * S
    nb, cur, prev, consts = _attn_specs(n_seq, S)
    tri, bias = _attn_consts()

    def body(q_ref, kvc_ref, kvp_ref, do_ref, qg_ref, kg_ref, tri_ref, bias_ref, sink_ref, dq_ref, dkc_ref, dkp_ref,
             dqg_ref, dsk_ref):
        n = pl.program_id(1)

        @pl.when((pl.program_id(0) == 0) & (n == 0))
        def _():
            dqg_ref[...] = jnp.zeros((HD, BLK), F32)
            dsk_ref[...] = jnp.zeros((8, 128), F32)

        lane = lax.broadcasted_iota(jnp.int32, (1, 128), 1)
        q_t = q_ref[...].T
        do_t = do_ref[...].astype(F32).T
        dq_t = []
        for kh in range(NKV):
            p, ps, prev_mask, qhb, kpb, kcb, vpb, vcb, qy, rq = _attn_probs(
                kh, n, q_t, kvc_ref, kvp_ref, qg_ref, kg_ref, tri_ref, bias_ref, sink_ref)
            dob = _heads_to_lanes(do_t, kh).astype(BF16)
            dp = jnp.where(tri_ref[...] > 0.5, _dot(vpb, dob), _dot(vcb, dob))
            delta = jnp.sum(p * dp, axis=0, keepdims=True)
            dsp, dsc = _unfold((p * (dp - delta)).astype(BF16), prev_mask)
            pp, pc = _unfold(p.astype(BF16), prev_mask)
            dsink = -ps * delta
            dqh = (_dot_tn(kpb, dsp) + _dot_tn(kcb, dsc)) * SCALE
            dqg = dqh * qy
            dqg_ref[...] += sum(dqg[:, g * BLK:(g + 1) * BLK] for g in range(GROUP))
            dy = dqh * qg_ref[...]
            dq_t.append(_lanes_to_heads(rq * (dy - qy * jnp.mean(dy * qy, axis=0, keepdims=True))))
            row = jnp.zeros((1, 128), F32)
            for g in range(GROUP):
                h = kh * GROUP + g
                row = row + jnp.where(lane == h, jnp.sum(dsink[:, g * BLK:(g + 1) * BLK], axis=1, keepdims=True), 0.0)
            dsk_ref[0:1, :] += row
            ks = slice(kh * HD, (kh + 1) * HD)
            vs = slice(2 * HD + kh * HD, 2 * HD + (kh + 1) * HD)
            dkp_ref[:, ks] = _dot_nt(dsp, qhb)
            dkc_ref[:, ks] = _dot_nt(dsc, qhb)
            dkp_ref[:, vs] = _dot_nt(pp, dob)
            dkc_ref[:, vs] = _dot_nt(pc, dob)
        dq_ref[...] = jnp.concatenate(dq_t, axis=0).T.astype(BF16)

    return pl.pallas_call(
        body, name="attn_bwd", grid=(n_seq, nb),
        in_specs=[cur(D), cur(256), prev(256), cur(D)] + consts,
        out_specs=[cur(D), cur(256), cur(256), _whole((HD, BLK)), _whole((8, 128))],
        out_shape=[jax.ShapeDtypeStruct((T, D), BF16), jax.ShapeDtypeStruct((T, 256), F32),
                   jax.ShapeDtypeStruct((T, 256), F32), jax.ShapeDtypeStruct((HD, BLK), F32),
                   jax.ShapeDtypeStruct((8, 128), F32)],
        compiler_params=_params(("arbitrary", "arbitrary")),
    )(q, kv, kv, do, qg_cols, kg, tri, bias, sink_rows)


def _kv_bwd(kv, dkc, dkp, kg_cols, n_seq, S):
    T = n_seq * S
    seq = lambda cols: pl.BlockSpec((S, cols), lambda b: (b, 0))

    def body(kv_ref, dkc_ref, dkp_ref, kg_ref, dkv_ref, dkg_ref):
        @pl.when(pl.program_id(0) == 0)
        def _():
            dkg_ref[...] = jnp.zeros((HD, BLK), F32)

        from_next = jnp.concatenate([dkp_ref[BLK:S, :], jnp.zeros((BLK, 256), F32)], axis=0)
        d = dkc_ref[...] + from_next
        d_t = d[:, 0:2 * HD].T
        k_t = kv_ref[:, 0:2 * HD].T
        kg = jnp.concatenate([kg_ref[...]] * (S // BLK), axis=1)
        out = []
        for kh in range(NKV):
            k = k_t[kh * HD:(kh + 1) * HD, :]
            r = lax.rsqrt(jnp.mean(k * k, axis=0, keepdims=True) + EPS)
            y = k * r
            dkh = d_t[kh * HD:(kh + 1) * HD, :]
            dkg = dkh * y
            dkg_ref[...] += sum(dkg[:, j * BLK:(j + 1) * BLK] for j in range(S // BLK))
            dy = dkh * kg
            out.append(r * (dy - y * jnp.mean(dy * y, axis=0, keepdims=True)))
        dkv_ref[:, 0:2 * HD] = jnp.concatenate(out, axis=0).T.astype(BF16)
        dkv_ref[:, 2 * HD:4 * HD] = d[:, 2 * HD:4 * HD].astype(BF16)

    return pl.pallas_call(
        body, name="kv_bwd", grid=(n_seq,),
        in_specs=[seq(256), seq(256), seq(256), _resident((HD, BLK))],
        out_specs=[seq(256), _whole((HD, BLK))],
        out_shape=[jax.ShapeDtypeStruct((T, 256), BF16), jax.ShapeDtypeStruct((HD, BLK), F32)],
        compiler_params=_params(("arbitrary",)),
    )(kv, dkc, dkp, kg_cols)


def _conv_bwd(h1, dh3, ag, dw, lng, lnb, n_seq, S):
    T = n_seq * S
    tc = min(256, S)
    nt = S // tc

    def body(h1_ref, dh3_ref, a_ref, gt_ref, dw_ref, lng_ref, lnb_ref, dag_ref, ddw_ref, st_ref, extd, acc8, shd):
        i = pl.program_id(1)

        @pl.when((pl.program_id(0) == 0) & (i == 0))
        def _():
            acc8[...] = jnp.zeros((CW * 8, D), F32)
            st_ref[...] = jnp.zeros((8, D), F32)

        @pl.when(i == 0)
        def _():
            extd[tc:tc + 32, :] = jnp.zeros((32, D), F32)

        h1 = h1_ref[...]
        mu = jnp.mean(h1, axis=-1, keepdims=True)
        cen = h1 - mu
        rstd = lax.rsqrt(jnp.mean(cen * cen, axis=-1, keepdims=True) + EPS)
        xh = cen * rstd
        h2 = xh * lng_ref[...] + lnb_ref[...]
        sg = _sig(h2)
        dh2 = dh3_ref[...] * (sg * (1.0 + h2 * (1.0 - sg)))
        st_ref[1:2, :] += jnp.sum(dh2 * xh, axis=0, keepdims=True)
        st_ref[2:3, :] += jnp.sum(dh2, axis=0, keepdims=True)
        dxh = dh2 * lng_ref[...]
        dh1 = rstd * (dxh - jnp.mean(dxh, axis=-1, keepdims=True)
                      - xh * jnp.mean(dxh * xh, axis=-1, keepdims=True))
        st_ref[0:1, :] += jnp.sum(dh1, axis=0, keepdims=True)
        extd[0:tc, :] = dh1
        _shift_copies(shd, extd, 0)
        for cb in range(D // 128):
            cs = slice(cb * 128, (cb + 1) * 128)
            for rb in range(tc // 128):
                rs = slice(rb * 128, (rb + 1) * 128)
                a = a_ref[rs, cs]
                sgt = _sig(gt_ref[rs, cs])
                h0 = a * sgt
                acc = jnp.zeros((128, 128), F32)
                for phase, offs in _tap_phases():
                    for m, o in enumerate(offs):
                        j = CW - 1 - o
                        ahead = shd[phase, rb * 128 + 8 * m:rb * 128 + 8 * m + 128, cs]
                        acc = acc + dw_ref[j:j + 1, cs] * ahead
                        acc8[j * 8:(j + 1) * 8, cs] += jnp.sum((h0 * ahead).reshape(16, 8, 128), axis=0)
                dag_ref[rs, cs] = (acc * sgt).astype(BF16)
                dag_ref[rs, cb * 128 + D:(cb + 1) * 128 + D] = (acc * a * sgt * (1.0 - sgt)).astype(BF16)
        extd[tc:tc + 32, :] = extd[0:32, :]

        @pl.when((pl.program_id(0) == n_seq - 1) & (i == nt - 1))
        def _():
            for j in range(CW):
                ddw_ref[j:j + 1, :] = jnp.sum(acc8[j * 8:(j + 1) * 8, :], axis=0, keepdims=True)
            ddw_ref[CW:32, :] = jnp.zeros((32 - CW, D), F32)

    tile = lambda col: pl.BlockSpec((tc, D), lambda b, i: (b * nt + (nt - 1 - i), col))
    return pl.pallas_call(
        body, name="conv_bwd", grid=(n_seq, nt),
        in_specs=[tile(0), tile(0), tile(0), tile(1), _resident((32, D)), _resident((1, D)), _resident((1, D))],
        out_specs=[pl.BlockSpec((tc, 2 * D), lambda b, i: (b * nt + (nt - 1 - i), 0)), _whole((32, D)),
                   _whole((8, D))],
        out_shape=[jax.ShapeDtypeStruct((T, 2 * D), BF16), jax.ShapeDtypeStruct((32, D), F32),
                   jax.ShapeDtypeStruct((8, D), F32)],
        scratch_shapes=[pltpu.VMEM((tc + 32, D), F32), pltpu.VMEM((CW * 8, D), F32),
                        pltpu.VMEM((8, tc + 24, D), F32)],
        compiler_params=_params(("arbitrary", "arbitrary")),
    )(h1, dh3, ag, ag, dw, lng, lnb)


def _in_proj_bwd(dag, dq, dkv, dgg, dh, x2, g1, w_in):
    T = x2.shape[0]
    tm = min(512, T)

    def body(dag_ref, dq_ref, dkv_ref, dgg_ref, dh_ref, x_ref, g_ref, w_ref, dx_ref, dg_ref):
        @pl.when(pl.program_id(0) == 0)
        def _():
            dg_ref[...] = jnp.zeros((1, D), F32)

        dxn = (_dot(dag_ref[...], w_ref[U_CONV, :]) + _dot(dq_ref[...], w_ref[U_Q, :])
               + _dot(dkv_ref[...], w_ref[U_KV, :]) + _dot(dgg_ref[...], w_ref[U_GATES, :]))
        x = x_ref[...]
        rstd = lax.rsqrt(jnp.mean(x * x, axis=-1, keepdims=True) + EPS)
        xh = x * rstd
        dg_ref[...] += jnp.sum(dxn * xh, axis=0, keepdims=True)
        dxh = dxn * g_ref[...]
        dx_ref[...] = dh_ref[...] + rstd * (dxh - xh * jnp.mean(dxh * xh, axis=-1, keepdims=True))

    return pl.pallas_call(
        body, name="in_proj_bwd", grid=(T // tm,),
        in_specs=[_rows(tm, 2 * D), _rows(tm, D), _rows(tm, 256), _rows(tm, 2 * D), _rows(tm, D), _rows(tm, D),
                  _resident((1, D)), _resident((IN_COLS, D))],
        out_specs=[_rows(tm, D), _whole((1, D))],
        out_shape=[jax.ShapeDtypeStruct((T, D), F32), jax.ShapeDtypeStruct((1, D), F32)],
        compiler_params=_params(("arbitrary",)),
    )(dag, dq, dkv, dgg, dh, x2, g1, w_in)


def _tn_matmul(a, b, name, column_blocks=False):
    T, K = a.shape
    N = b.shape[1]
    tk = K if K <= 1024 else K // 2
    tn = N if N <= 1024 else (1024 if N % 1024 == 0 and not column_blocks else N // 4)
    tt = min(2048, T)
    assert K % tk == 0 and N % tn == 0 and T % tt == 0 and tk % 128 == 0 and tn % 128 == 0

    def body(a_ref, b_ref, o_ref):
        @pl.when(pl.program_id(2) == 0)
        def _():
            o_ref[...] = jnp.zeros((tk, tn), F32)

        o_ref[...] += _dot_tn(a_ref[...], b_ref[...])

    return pl.pallas_call(
        body, name=name, grid=(K // tk, N // tn, T // tt),
        in_specs=[pl.BlockSpec((tt, tk), lambda i, j, t: (t, i)), pl.BlockSpec((tt, tn), lambda i, j, t: (t, j))],
        out_specs=(pl.BlockSpec((None, tk, tn), lambda i, j, t: (j, i, 0)) if column_blocks
                   else pl.BlockSpec((tk, tn), lambda i, j, t: (i, j))),
        out_shape=jax.ShapeDtypeStruct((N // tn, K, tn) if column_blocks else (K, N), F32),
        compiler_params=_params(("parallel", "parallel", "arbitrary")),
    )(a, b)


def _place():
    x, y, c = lax.axis_index("x"), lax.axis_index("y"), lax.axis_index("c")
    chips = [(1 - x, y), (x, 1 - y), (1 - x, 1 - y)]
    return x, y, c, chips


def _own_slot(slots, mine):
    chip = 2 * lax.axis_index("x") + lax.axis_index("y")
    return lax.dynamic_update_slice(slots, mine[None], (chip,) + (0,) * mine.ndim)


def _row_tile(rows, unit):
    return max(t for t in range(unit, 513, unit) if rows % t == 0)


def _gather_weights(packs):
    n = len(packs)
    halves = [p.shape[0] // 2 for p in packs]

    def body(*refs):
        srcs, dsts, token, send_sems, recv_sems = refs[:n], refs[n:2 * n], refs[2 * n], refs[2 * n + 1], refs[2 * n + 2]
        x, y, c, chips = _place()

        def piece(b, px, py, pc):
            return dsts[b].at[2 * px + py, pl.ds(pc * halves[b], halves[b]), :]

        def copy(b, k, block, to, from_src=False):
            mine = srcs[b].at[pl.ds(c * halves[b], halves[b]), :]
            return pltpu.make_async_remote_copy(
                src_ref=mine if from_src else piece(b, *block), dst_ref=piece(b, *block),
                send_sem=send_sems.at[6 * b + k], recv_sem=recv_sems.at[6 * b + k], device_id=to,
                device_id_type=MESH)

        first = [copy(b, k, (x, y, c), (*chip, c), from_src=True) for b in range(n) for k, chip in enumerate(chips)]
        for cp in first:
            cp.start()
        passed = []
        for b in range(n):
            for k, chip in enumerate(chips):
                copy(b, k, (*chip, c), (x, y, c)).wait_recv()
                passed.append(copy(b, 3 + k, (*chip, c), (x, y, 1 - c)))
                passed[-1].start()
        for b in range(n):
            for k, chip in enumerate(chips):
                copy(b, 3 + k, (*chip, 1 - c), (x, y, c)).wait_recv()
        for cp in first + passed:
            cp.wait_send()
        token[...] = jnp.zeros((8, 128), F32)

    outs = pl.pallas_call(
        body, name="gather_weights",
        in_specs=[pl.BlockSpec(memory_space=pl.ANY)] * n,
        out_specs=[pl.BlockSpec(memory_space=pl.ANY)] * n + [pl.BlockSpec(memory_space=pltpu.VMEM)],
        out_shape=[jax.ShapeDtypeStruct((4,) + p.shape, p.dtype) for p in packs]
        + [jax.ShapeDtypeStruct((8, 128), F32)],
        scratch_shapes=[pltpu.SemaphoreType.DMA((6 * n,)), pltpu.SemaphoreType.DMA((6 * n,))],
        compiler_params=pltpu.CompilerParams(has_side_effects=True),
    )(*packs)
    return [_own_slot(got, p) for got, p in zip(outs[:n], packs)], outs[n][0, 0]


def _add_halves(g, got, c_idx, name="grad_add_halves"):
    rows, w = g.shape[1], g.shape[2]
    half = rows // 2
    tr = _row_tile(half, 16)
    nt = half // tr

    def body(c_ref, g_ref, r_ref, o_ref):
        o_ref[...] = (g_ref[...] + r_ref[...]).astype(BF16)

    return pl.pallas_call(
        body, name=name,
        grid_spec=pltpu.PrefetchScalarGridSpec(
            num_scalar_prefetch=1, grid=(4, nt),
            in_specs=[pl.BlockSpec((1, tr, w), lambda q, i, c_ref: (q, c_ref[0] * nt + i, 0)),
                      pl.BlockSpec((1, tr, w), lambda q, i, c_ref: (q, i, 0))],
            out_specs=pl.BlockSpec((1, tr, w), lambda q, i, c_ref: (q, i, 0))),
        out_shape=jax.ShapeDtypeStruct((4, half, w), BF16),
        compiler_params=_params(("parallel", "parallel")),
    )(c_idx, g, got)


def _own_piece(p):
    chip = 2 * lax.axis_index("x") + lax.axis_index("y")
    return lax.dynamic_index_in_dim(p, chip, axis=0, keepdims=False)


def _sum_chips(r, c_idx, name="grad_sum_chips"):
    half, w = r.shape[1], r.shape[2]
    tr = _row_tile(half, 16)
    nt = half // tr

    def body(c_ref, r_ref, o_ref):
        acc = r_ref[0].astype(F32)
        for q in range(1, 4):
            acc = acc + r_ref[q].astype(F32)
        o_ref[...] = acc

    return pl.pallas_call(
        body, name=name,
        grid_spec=pltpu.PrefetchScalarGridSpec(
            num_scalar_prefetch=1, grid=(nt,),
            in_specs=[pl.BlockSpec((4, tr, w), lambda i, c_ref: (0, i, 0))],
            out_specs=pl.BlockSpec((tr, w), lambda i, c_ref: (c_ref[0] * nt + i, 0))),
        out_shape=jax.ShapeDtypeStruct((2 * half, w), F32),
        compiler_params=_params(("parallel",)),
    )(c_idx, r)


def _join_halves(f):
    half = f.shape[0] // 2

    def body(src, dst, send_sem, recv_sem):
        x, y, c, _ = _place()
        cp = pltpu.make_async_remote_copy(
            src_ref=src.at[pl.ds(c * half, half), :], dst_ref=dst.at[pl.ds(c * half, half), :], send_sem=send_sem,
            recv_sem=recv_sem, device_id=(x, y, 1 - c), device_id_type=MESH)
        cp.start()
        pltpu.make_async_remote_copy(
            src_ref=src.at[pl.ds(c * half, half), :], dst_ref=dst.at[pl.ds((1 - c) * half, half), :],
            send_sem=send_sem, recv_sem=recv_sem, device_id=(x, y, 1 - c), device_id_type=MESH).wait_recv()
        cp.wait_send()

    return pl.pallas_call(
        body, name="grad_join_halves",
        in_specs=[pl.BlockSpec(memory_space=pl.ANY)], out_specs=pl.BlockSpec(memory_space=pl.ANY),
        out_shape=jax.ShapeDtypeStruct(f.shape, f.dtype), input_output_aliases={0: 0},
        scratch_shapes=[pltpu.SemaphoreType.DMA, pltpu.SemaphoreType.DMA],
        compiler_params=pltpu.CompilerParams(has_side_effects=True),
    )(f)


_HBM = pl.BlockSpec(memory_space=pltpu.HBM)
_SEM = pl.BlockSpec(memory_space=pltpu.SEMAPHORE)
_EFFECT = pltpu.SideEffectType.DATAFLOW_SIDE_EFFECTING


def _start_copies(name, bufs, n_sems, plan):
    nb = len(bufs)

    def body(*refs):
        for cp in plan(refs[:nb], refs[nb], refs[nb + 1])[0]:
            cp.start()
        refs[-1][...] = jnp.zeros((8, 128), F32)

    out = pl.pallas_call(
        body, name=name,
        out_shape=(pltpu.SemaphoreType.DMA((n_sems,)), pltpu.SemaphoreType.DMA((n_sems,)),
                   *[pltpu.HBM(b.shape, b.dtype) for b in bufs], jax.ShapeDtypeStruct((8, 128), F32)),
        in_specs=[_HBM] * nb, out_specs=(_SEM, _SEM, *[_HBM] * nb, pl.BlockSpec(memory_space=pltpu.VMEM)),
        input_output_aliases={i: 2 + i for i in range(nb)},
        compiler_params=pltpu.CompilerParams(has_side_effects=_EFFECT),
    )(*[pltpu.with_memory_space_constraint(b, pltpu.HBM) for b in bufs])
    return out[0], out[1], list(out[2:2 + nb]), out[-1]


def _wait_copies(name, send_sems, recv_sems, bufs, after, plan):
    nb = len(bufs)

    def body(*refs):
        _, sends, recvs = plan(refs[:nb], refs[nb], refs[nb + 1])
        for cp in sends:
            cp.wait_send()
        for cp in recvs:
            cp.wait_recv()

    out = pl.pallas_call(
        body, name=name,
        out_shape=tuple(pltpu.HBM(b.shape, b.dtype) for b in bufs),
        in_specs=[_HBM] * nb + [_SEM, _SEM] + [pl.BlockSpec(memory_space=pl.ANY)] * len(after),
        out_specs=tuple([_HBM] * nb),
        input_output_aliases={i: i for i in range(nb)},
        compiler_params=pltpu.CompilerParams(has_side_effects=_EFFECT),
    )(*bufs, send_sems, recv_sems, *after)
    return list(out)


def _plan_gather_direct(halves):
    n = len(halves)

    def plan(refs, send_sems, recv_sems):
        x, y, c, chips = _place()
        starts, recvs = [], []
        for b, half in enumerate(halves):
            src, land = refs[b], refs[n + b]
            for k, (cx, cy) in enumerate(chips):
                for d in range(2):
                    other = c if d == 0 else 1 - c
                    i = 6 * b + 2 * k + d
                    starts.append(pltpu.make_async_remote_copy(
                        src_ref=src.at[pl.ds(c * half, half), :],
                        dst_ref=land.at[2 * x + y, pl.ds(c * half, half), :],
                        send_sem=send_sems.at[i], recv_sem=recv_sems.at[i], device_id=(cx, cy, other),
                        device_id_type=MESH))
                    recvs.append(pltpu.make_async_remote_copy(
                        src_ref=src.at[pl.ds(c * half, half), :],
                        dst_ref=land.at[2 * cx + cy, pl.ds(other * half, half), :],
                        send_sem=send_sems.at[i], recv_sem=recv_sems.at[i], device_id=(cx, cy, other),
                        device_id_type=MESH))
        return starts, starts, recvs
    return plan


def _plan_swap_halves(halves):
    n = len(halves)

    def plan(refs, send_sems, recv_sems):
        x, y, c, _ = _place()
        cps = [pltpu.make_async_remote_copy(
            src_ref=refs[b].at[:, pl.ds((1 - c) * half, half), :], dst_ref=refs[n + b], send_sem=send_sems.at[b],
            recv_sem=recv_sems.at[b], device_id=(x, y, 1 - c), device_id_type=MESH)
            for b, half in enumerate(halves)]
        return cps, cps, cps
    return plan


def _plan_scatter_chips(n):
    def plan(refs, send_sems, recv_sems):
        x, y, c, chips = _place()
        me = 2 * x + y
        starts, recvs = [], []
        for b in range(n):
            src, land = refs[b], refs[n + b]
            for k, (cx, cy) in enumerate(chips):
                i = 3 * b + k
                starts.append(pltpu.make_async_remote_copy(
                    src_ref=src.at[2 * cx + cy], dst_ref=land.at[me], send_sem=send_sems.at[i],
                    recv_sem=recv_sems.at[i], device_id=(cx, cy, c), device_id_type=MESH))
                recvs.append(pltpu.make_async_remote_copy(
                    src_ref=src.at[me], dst_ref=land.at[2 * cx + cy], send_sem=send_sems.at[i],
                    recv_sem=recv_sems.at[i], device_id=(cx, cy, c), device_id_type=MESH))
        return starts, starts, recvs
    return plan


def _plan_join_halves(halves):
    def plan(refs, send_sems, recv_sems):
        x, y, c, _ = _place()
        starts, recvs = [], []
        for b, half in enumerate(halves):
            mine, theirs = refs[b].at[pl.ds(c * half, half), :], refs[b].at[pl.ds((1 - c) * half, half), :]
            starts.append(pltpu.make_async_remote_copy(
                src_ref=mine, dst_ref=mine, send_sem=send_sems.at[b], recv_sem=recv_sems.at[b],
                device_id=(x, y, 1 - c), device_id_type=MESH))
            recvs.append(pltpu.make_async_remote_copy(
                src_ref=mine, dst_ref=theirs, send_sem=send_sems.at[b], recv_sem=recv_sems.at[b],
                device_id=(x, y, 1 - c), device_id_type=MESH))
        return starts, starts, recvs
    return plan


def _allreduce_small(vec):
    def body(v_ref, o_ref, gath, send_sems, recv_sems):
        x, y, c, _ = _place()
        me = 4 * x + 2 * y + c
        gath[me] = v_ref[...]
        sends = []
        for k in range(1, 8):
            peer = (x ^ (k >> 2), y ^ ((k >> 1) & 1), c ^ (k & 1))
            sends.append(pltpu.make_async_remote_copy(
                src_ref=v_ref, dst_ref=gath.at[me], send_sem=send_sems.at[k - 1], recv_sem=recv_sems.at[k - 1],
                device_id=peer, device_id_type=MESH))
        for cp in sends:
            cp.start()
        for k in range(1, 8):
            peer = (x ^ (k >> 2), y ^ ((k >> 1) & 1), c ^ (k & 1))
            pltpu.make_async_remote_copy(
                src_ref=v_ref, dst_ref=gath.at[4 * peer[0] + 2 * peer[1] + peer[2]], send_sem=send_sems.at[k - 1],
                recv_sem=recv_sems.at[k - 1], device_id=peer, device_id_type=MESH).wait_recv()
        for cp in sends:
            cp.wait_send()
        acc = gath[0]
        for d in range(1, 8):
            acc = acc + gath[d]
        o_ref[...] = acc

    return pl.pallas_call(
        body, name="allreduce_small",
        in_specs=[pl.BlockSpec(memory_space=pltpu.VMEM)], out_specs=pl.BlockSpec(memory_space=pltpu.VMEM),
        out_shape=jax.ShapeDtypeStruct(vec.shape, F32),
        scratch_shapes=[pltpu.VMEM((8,) + vec.shape, F32), pltpu.SemaphoreType.DMA((7,)),
                        pltpu.SemaphoreType.DMA((7,))],
    )(vec)


def _adamw(w, g, m, v, name, after):
    shape = w.shape
    if w.ndim == 1 or w.size <= 128 * 128:
        two_d = (1, w.size) if w.size % 128 else (w.size // 128, 128)
    else:
        two_d = (w.shape[0], w.size // w.shape[0])
    rows, cols = two_d
    tr = _row_tile(rows, 8) if rows % 8 == 0 and rows > 512 else rows

    def body(w_ref, g_ref, m_ref, v_ref, after_ref, d_ref, nm_ref, nv_ref):
        gr = g_ref[...]
        nm = B1 * m_ref[...] + (1.0 - B1) * gr
        nv = B2 * v_ref[...] + (1.0 - B2) * (gr * gr)
        m_hat = nm / (1.0 - B1 ** STEP)
        v_hat = nv / (1.0 - B2 ** STEP)
        d_ref[...] = -LR * (m_hat / (jnp.sqrt(v_hat) + AEPS) + WD * w_ref[...])
        nm_ref[...] = nm
        nv_ref[...] = nv

    spec = pl.BlockSpec((tr, cols), lambda i: (i, 0))
    outs = pl.pallas_call(
        body, name=name, grid=(rows // tr,),
        in_specs=[spec] * 4 + [pl.BlockSpec(memory_space=pl.ANY)], out_specs=[spec] * 3,
        out_shape=[jax.ShapeDtypeStruct(two_d, F32)] * 3,
        compiler_params=_params(("parallel",)),
    )(*[t.reshape(two_d) for t in (w, g, m, v)], after)
    return [o.reshape(shape) for o in outs]


def _rows_stacked(g):
    return g.reshape(4 * g.shape[1], D)


def _rows_to_slots(t):
    return t.reshape(4, t.shape[0] // 4, D)


def _pack_first(w_in, w_conv_out, conv_dw_w):
    dw = jnp.pad(conv_dw_w.reshape(CW, 256), ((0, 1), (0, 0)))
    dw_bits = lax.bitcast_convert_type(dw, BF16).reshape(16, D)
    return [w_in.T.astype(BF16), w_conv_out.astype(BF16), jnp.pad(dw_bits, ((0, 16), (0, 0)))]


def _unpack_first(w_in_slots, wc_slots, dw_slots):
    dw = lax.bitcast_convert_type(dw_slots[:, 0:16].reshape(4, 32, 256, 2), F32)
    return _rows_stacked(w_in_slots), _rows_stacked(wc_slots), jnp.transpose(dw, (1, 0, 2)).reshape(32, D)


class _Exchanges:
    def __init__(self, late_shards):
        self.c_idx = lax.axis_index("c").astype(jnp.int32).reshape(1)
        packs = list(late_shards)
        self.late_plan = _plan_gather_direct([p.shape[0] // 2 for p in packs])
        slots = [lax.empty((4,) + p.shape, BF16) for p in packs]
        self.late = _start_copies("gather_late_start", packs + slots, 6 * len(packs), self.late_plan)
        self.first_token = self.late[3][0, 0]

    def late_weights(self, after):
        send_sems, recv_sems, bufs, _ = self.late
        bufs = _wait_copies("gather_late_wait", send_sems, recv_sems, bufs, after, self.late_plan)
        wa, wm, wf, wd = [_own_slot(bufs[4 + b], bufs[b]) for b in range(4)]
        return _rows_stacked(wa), _rows_stacked(wm), wf, _rows_stacked(wd)

    def reduce_start(self, d_wd, d_wf4, d_wm, d_wa, d_wc):
        gs = [_rows_to_slots(d_wd), jnp.concatenate([_rows_to_slots(t) for t in (d_wm, d_wa, d_wc)], axis=1), d_wf4]
        self.halves = [g.shape[1] // 2 for g in gs]
        self.swap_plan = _plan_swap_halves(self.halves)
        lands = [lax.empty((4, h, g.shape[2]), F32) for g, h in zip(gs, self.halves)]
        self.swap = _start_copies("grad_swap_start", gs + lands, len(gs), self.swap_plan)
        return self.swap[3][0, 0]

    def reduce_mid(self, after):
        send_sems, recv_sems, bufs, _ = self.swap
        bufs = _wait_copies("grad_swap_wait", send_sems, recv_sems, bufs, after, self.swap_plan)
        n = len(self.halves)
        ps = [_add_halves(bufs[b], bufs[n + b], self.c_idx, "grad_add_halves_%d" % b) for b in range(n)]
        self.scatter_plan = _plan_scatter_chips(n)
        self.scatter = _start_copies("grad_scatter_start", ps + [lax.empty(p.shape, BF16) for p in ps], 3 * n,
                                     self.scatter_plan)
        return self.scatter[3][0, 0]

    def reduce_late(self, after):
        send_sems, recv_sems, bufs, _ = self.scatter
        bufs = _wait_copies("grad_scatter_wait", send_sems, recv_sems, bufs, after, self.scatter_plan)
        n = len(self.halves)
        fs = [_sum_chips(_own_slot(bufs[n + b], _own_piece(bufs[b])), self.c_idx, "grad_sum_chips_%d" % b)
              for b in range(n)]
        self.join_plan = _plan_join_halves(self.halves)
        self.join = _start_copies("grad_join_start", fs, n, self.join_plan)

    def reduce_end(self, after):
        send_sems, recv_sems, bufs, _ = self.join
        g_wd, sq, g_wf = _wait_copies("grad_join_wait", send_sems, recv_sems, bufs, after, self.join_plan)
        return g_wd, g_wf, sq[0:ROWS_SQ], sq[ROWS_SQ:2 * ROWS_SQ], sq[2 * ROWS_SQ:3 * ROWS_SQ]

    def w_in_start(self, d_w_in_t):
        g = _rows_to_slots(d_w_in_t)
        self.w_half = g.shape[1] // 2
        self.w_swap_plan = _plan_swap_halves([self.w_half])
        self.w_swap = _start_copies("grad_w_in_swap_start", [g, lax.empty((4, self.w_half, D), F32)], 1,
                                    self.w_swap_plan)
        return self.w_swap[3]

    def w_in_mid(self, after):
        send_sems, recv_sems, bufs, _ = self.w_swap
        g, got = _wait_copies("grad_w_in_swap_wait", send_sems, recv_sems, bufs, after, self.w_swap_plan)
        p = _add_halves(g, got, self.c_idx, "grad_add_halves_w_in")
        self.w_scatter_plan = _plan_scatter_chips(1)
        self.w_scatter = _start_copies("grad_w_in_scatter_start", [p, lax.empty(p.shape, BF16)], 3,
                                       self.w_scatter_plan)
        return self.w_scatter[3]

    def w_in_end(self, after):
        send_sems, recv_sems, bufs, _ = self.w_scatter
        p, got = _wait_copies("grad_w_in_scatter_wait", send_sems, recv_sems, bufs, after, self.w_scatter_plan)
        return _join_halves(_sum_chips(_own_slot(got, _own_piece(p)), self.c_idx, "grad_sum_chips_w_in"))


def _local_grads(x, loss_target, norm_mix_g, conv_dw_b, conv_ln_g, conv_ln_b, q_norm_g, k_norm_g, sinks, norm_ffn_g,
                 w_in, wc, dw, exchanges):
    n_seq, S, _ = x.shape
    T = n_seq * S
    x2 = x.reshape(T, D)
    tgt = loss_target.reshape(T, D)
    row = lambda t: t.reshape(1, -1)
    g1, g2 = row(norm_mix_g), row(norm_ffn_g)
    qg, kg = jnp.broadcast_to(q_norm_g.reshape(HD, 1), (HD, GROUP * BLK)), row(k_norm_g)
    lng, lnb, dwb = row(conv_ln_g), row(conv_ln_b), row(conv_dw_b)
    sink_rows = jnp.repeat(sinks.reshape(NKV, GROUP), BLK, axis=1)

    xn, ag, q, kv, gg, h1, h3, yc = _in_proj_conv_fwd(x2, g1 + exchanges.first_token, w_in, dw, dwb, lng, lnb, wc,
                                                      n_seq, S)
    o = _attn_fwd(q, kv, qg, kg, sink_rows, n_seq, S)
    wa, wm, wf, wd = exchanges.late_weights([o, yc])
    mix, h = _merge_fwd(x2, gg, yc, o, wa, wm)
    dh, dhb, hn, act, dout, dgu, ffn_stats = _ffn(h, tgt, g2, wf, wd)
    d_wd = _tn_matmul(act, dout, "dw_ffn_down")
    d_wf = _tn_matmul(hn, dgu, "dw_ffn_in", column_blocks=True)
    d_wm = _tn_matmul(mix, dhb, "dw_merge")
    dgg, dyc, dya, do, dh3 = _merge_bwd(dhb, gg, yc, o, wm, wa, wc)
    d_wa = _tn_matmul(o, dya, "dw_attn_out")
    d_wc = _tn_matmul(h3, dyc, "dw_conv_out")
    token = exchanges.reduce_start(d_wd, d_wf, d_wm, d_wa, d_wc)
    dq, dkc, dkp, dqg, dsk = _attn_bwd(q, kv, do, qg + token, kg, sink_rows, n_seq, S)
    token = exchanges.reduce_mid([dq])
    dkv, dkg = _kv_bwd(kv, dkc, dkp, jnp.broadcast_to(k_norm_g.reshape(HD, 1), (HD, BLK)), n_seq, S)
    dag, ddw, conv_stats = _conv_bwd(h1, dh3, ag, dw, lng + token, lnb, n_seq, S)
    dx, dg1 = _in_proj_bwd(dag, dq, dkv, dgg, dh, x2, g1, w_in)
    exchanges.reduce_late([dx])
    d_w_in = jnp.concatenate([_tn_matmul(dag, xn, "dw_in_conv"), _tn_matmul(dq, xn, "dw_in_q"),
                              _tn_matmul(dkv, xn, "dw_in_kv"), _tn_matmul(dgg, xn, "dw_in_gates")], axis=0)

    heads = jnp.concatenate([jnp.sum(dqg, axis=1), jnp.sum(dkg, axis=1), dsk[0, :NQ],
                             jnp.zeros((D - 2 * HD - NQ,), F32)])
    vec = jnp.concatenate([dg1, conv_stats[0:3], ffn_stats[0:1], heads[None], ffn_stats[1:2], jnp.zeros((1, D), F32),
                           ddw], axis=0)
    return ffn_stats[1], dx.reshape(x.shape), d_w_in, vec


def kernel(x, norm_mix_g, w_in, conv_dw_w, conv_dw_b, conv_ln_g, conv_ln_b, w_conv_out, q_norm_g, k_norm_g, sinks, w_attn_out, w_merge_out, norm_ffn_g, w_ffn_in, w_ffn_down, loss_target, m_norm_mix_g, m_w_in, m_conv_dw_w, m_conv_dw_b, m_conv_ln_g, m_conv_ln_b, m_w_conv_out, m_q_norm_g, m_k_norm_g, m_sinks, m_w_attn_out, m_w_merge_out, m_norm_ffn_g, m_w_ffn_in, m_w_ffn_down, v_norm_mix_g, v_w_in, v_conv_dw_w, v_conv_dw_b, v_conv_ln_g, v_conv_ln_b, v_w_conv_out, v_q_norm_g, v_k_norm_g, v_sinks, v_w_attn_out, v_w_merge_out, v_norm_ffn_g, v_w_ffn_in, v_w_ffn_down):
    chip = 2 * lax.axis_index("x") + lax.axis_index("y")

    first, token = _gather_weights(_pack_first(w_in, w_conv_out, conv_dw_w))
    exchanges = _Exchanges([t.astype(BF16) + token.astype(BF16)
                            for t in (w_attn_out, w_merge_out, w_ffn_in, w_ffn_down)])
    _, grad_x, d_w_in, vec = _local_grads(x, loss_target, norm_mix_g, conv_dw_b, conv_ln_g, conv_ln_b, q_norm_g,
                                          k_norm_g, sinks, norm_ffn_g, *_unpack_first(*first), exchanges)

    g_wd, g_wf, g_wm, g_wa, g_wc = exchanges.reduce_end([d_w_in])
    w_in_token = exchanges.w_in_start(d_w_in)
    small = _allreduce_small(vec + w_in_token[0, 0])
    loss = 0.5 / D * jnp.sum(small[6])
    g_dw = lax.dynamic_slice_in_dim(small[8:8 + CW], chip * 256, 256, axis=1).reshape(CW, 1, 256)
    grads = {
        "norm_mix_g": small[0], "conv_dw_w": g_dw, "conv_dw_b": small[1], "conv_ln_g": small[2],
        "conv_ln_b": small[3], "w_conv_out": g_wc, "q_norm_g": small[5, 0:HD], "k_norm_g": small[5, HD:2 * HD],
        "sinks": small[5, 2 * HD:2 * HD + NQ], "w_attn_out": g_wa, "w_merge_out": g_wm, "norm_ffn_g": small[4],
        "w_ffn_in": g_wf, "w_ffn_down": g_wd,
    }
    weights = dict(norm_mix_g=norm_mix_g, w_in=w_in, conv_dw_w=conv_dw_w, conv_dw_b=conv_dw_b, conv_ln_g=conv_ln_g,
                   conv_ln_b=conv_ln_b, w_conv_out=w_conv_out, q_norm_g=q_norm_g, k_norm_g=k_norm_g, sinks=sinks,
                   w_attn_out=w_attn_out, w_merge_out=w_merge_out, norm_ffn_g=norm_ffn_g, w_ffn_in=w_ffn_in,
                   w_ffn_down=w_ffn_down)
    m_in = dict(norm_mix_g=m_norm_mix_g, w_in=m_w_in, conv_dw_w=m_conv_dw_w, conv_dw_b=m_conv_dw_b,
                conv_ln_g=m_conv_ln_g, conv_ln_b=m_conv_ln_b, w_conv_out=m_w_conv_out, q_norm_g=m_q_norm_g,
                k_norm_g=m_k_norm_g, sinks=m_sinks, w_attn_out=m_w_attn_out, w_merge_out=m_w_merge_out,
                norm_ffn_g=m_norm_ffn_g, w_ffn_in=m_w_ffn_in, w_ffn_down=m_w_ffn_down)
    v_in = dict(norm_mix_g=v_norm_mix_g, w_in=v_w_in, conv_dw_w=v_conv_dw_w, conv_dw_b=v_conv_dw_b,
                conv_ln_g=v_conv_ln_g, conv_ln_b=v_conv_ln_b, w_conv_out=v_w_conv_out, q_norm_g=v_q_norm_g,
                k_norm_g=v_k_norm_g, sinks=v_sinks, w_attn_out=v_w_attn_out, w_merge_out=v_w_merge_out,
                norm_ffn_g=v_norm_ffn_g, w_ffn_in=v_w_ffn_in, w_ffn_down=v_w_ffn_down)
    names = list(weights)
    updates = {}
    after = exchanges.w_in_mid([small])
    for n in names:
        if n != "w_in":
            updates[n] = _adamw(weights[n], grads[n], m_in[n], v_in[n], "adamw_" + n, after)
    g_w_in_t = exchanges.w_in_end([updates[n][0] for n in updates])
    grads["w_in"] = g_w_in_t.T
    updates["w_in"] = [t.T for t in _adamw(w_in.T, g_w_in_t, m_w_in.T, v_w_in.T, "adamw_w_in", g_w_in_t)]
    return (loss, grad_x, *[grads[n] for n in names], *[updates[n][0] for n in names],
            *[updates[n][1] for n in names], *[updates[n][2] for n in names])
```

```python
import math

import jax
import jax.numpy as jnp
import numpy as np
from jax import lax
from jax.experimental import pallas as pl
from jax.experimental.pallas import tpu as pltpu

F32 = jnp.float32
BF16 = jnp.bfloat16

D = 1024
CW = 31
HD = 64
NQ = 16
NKV = 2
GROUP = NQ // NKV
BLK = 128
DFF = 2816
EPS = 1e-6
NEG = -1e30
IN_COLS = 5376
U_CONV, U_Q, U_KV = slice(0, 2 * D), slice(2 * D, 3 * D), slice(3 * D, 3 * D + 2 * NKV * HD)
U_GATES = slice(3 * D + 2 * NKV * HD, IN_COLS)
SCALE = 1.0 / math.sqrt(HD)

LR, B1, B2, AEPS, WD, STEP = 0.001, 0.9, 0.999, 1e-08, 0.01, 10

MIB = 1024 * 1024
MESH = pl.DeviceIdType.MESH

ROWS_W_IN = IN_COLS // 4
ROWS_SQ = D // 4
ROWS_FFN_IN = 2 * DFF // 4
ROWS_DOWN = DFF // 4


def _sig(x):
    return 1.0 / (1.0 + jnp.exp(-x))


def _dot(a, b):
    return jnp.dot(a, b, preferred_element_type=F32)


def _dot_nt(a, b):
    return lax.dot_general(a, b, (((1,), (1,)), ((), ())), preferred_element_type=F32)


def _dot_tn(a, b):
    return lax.dot_general(a, b, (((0,), (0,)), ((), ())), preferred_element_type=F32)


def _params(sem, vmem_mib=48):
    return pltpu.CompilerParams(dimension_semantics=sem, vmem_limit_bytes=vmem_mib * MIB)


def _resident(shape):
    return pl.BlockSpec(shape, lambda *_: (0,) * len(shape), pipeline_mode=pl.Buffered(1))


def _whole(shape):
    return pl.BlockSpec(shape, lambda *_: (0,) * len(shape))


def _rows(tm, cols):
    return pl.BlockSpec((tm, cols), lambda i: (i, 0))


def _tap_phases():
    return [(phase, list(range(phase, CW, 8))) for phase in range(8)]


def _shift_copies(dst, src, base):
    for phase, taps in _tap_phases():
        n = dst.shape[1] - 8 * (4 - len(taps))
        dst[phase, 0:n, :] = src[base + phase:base + phase + n, :]


def _in_proj_conv_fwd(x2, g1, w_in, dw, dwb, lng, lnb, wc, n_seq, S):
    T = n_seq * S
    tc = min(256, S)
    nt = S // tc

    def body(x_ref, g_ref, w_ref, dw_ref, dwb_ref, lng_ref, lnb_ref, wc_ref, xn_ref, ag_ref, q_ref, kv_ref, gg_ref,
             h1_ref, h3_ref, yc_ref, ext, sh):
        i = pl.program_id(1)

        @pl.when(i == 0)
        def _():
            ext[0:32, :] = jnp.zeros((32, D), F32)

        x = x_ref[...]
        rstd = lax.rsqrt(jnp.mean(x * x, axis=-1, keepdims=True) + EPS)
        xn = (x * rstd * g_ref[...]).astype(BF16)
        xn_ref[...] = xn
        ag = _dot_nt(xn, w_ref[U_CONV, :])
        ag_ref[...] = ag
        ext[32:32 + tc, :] = ag[:, 0:D] * _sig(ag[:, D:2 * D])
        q_ref[...] = _dot_nt(xn, w_ref[U_Q, :])
        kv_ref[...] = _dot_nt(xn, w_ref[U_KV, :])
        gg_ref[...] = _dot_nt(xn, w_ref[U_GATES, :])
        _shift_copies(sh, ext, 2)
        for cb in range(D // 128):
            cs = slice(cb * 128, (cb + 1) * 128)
            acc = jnp.broadcast_to(dwb_ref[:, cs], (tc, 128))
            for phase, taps in _tap_phases():
                for m, j in enumerate(taps):
                    acc = acc + dw_ref[j:j + 1, cs] * sh[phase, 8 * m:8 * m + tc, cs]
            h1_ref[:, cs] = acc
        ext[0:32, :] = ext[tc:tc + 32, :]
        h1 = h1_ref[...]
        mu = jnp.mean(h1, axis=-1, keepdims=True)
        cen = h1 - mu
        var = jnp.mean(cen * cen, axis=-1, keepdims=True)
        h2 = cen * lax.rsqrt(var + EPS) * lng_ref[...] + lnb_ref[...]
        h3 = (h2 * _sig(h2)).astype(BF16)
        h3_ref[...] = h3
        yc_ref[...] = _dot(h3, wc_ref[...])

    tile = lambda cols: pl.BlockSpec((tc, cols), lambda b, i: (b * nt + i, 0))
    shape = lambda cols, dtype: jax.ShapeDtypeStruct((T, cols), dtype)
    return pl.pallas_call(
        body, name="in_proj_conv_fwd", grid=(n_seq, nt),
        in_specs=[tile(D), _resident((1, D)), _resident((IN_COLS, D)), _resident((32, D)), _resident((1, D)),
                  _resident((1, D)), _resident((1, D)), _resident((D, D))],
        out_specs=[tile(D), tile(2 * D), tile(D), tile(256), tile(2 * D), tile(D), tile(D), tile(D)],
        out_shape=[shape(D, BF16), shape(2 * D, F32), shape(D, F32), shape(256, F32), shape(2 * D, F32),
                   shape(D, F32), shape(D, BF16), shape(D, F32)],
        scratch_shapes=[pltpu.VMEM((32 + tc, D), F32), pltpu.VMEM((8, tc + 24, D), F32)],
        compiler_params=_params(("parallel", "arbitrary"), 56),
    )(x2, g1, w_in, dw, dwb, lng, lnb, wc)


def _attn_consts():
    k = np.arange(BLK)[:, None]
    i = np.arange(GROUP * BLK)[None, :] % BLK
    from_prev = k > i
    dist = np.where(from_prev, i + BLK - k, i - k).astype(np.float32)
    head = np.arange(GROUP * BLK)[None, :] // BLK
    bias = []
    for kh in range(NKV):
        slope = np.exp2(-8.0 * (kh * GROUP + head + 1) / NQ).astype(np.float32)
        bias.append(-slope * dist)
    return jnp.asarray(from_prev.astype(np.float32)), jnp.asarray(np.stack(bias))


def _heads_to_lanes(t, kh):
    return jnp.concatenate([t[(kh * GROUP + g) * HD:(kh * GROUP + g + 1) * HD, :] for g in range(GROUP)], axis=1)


def _lanes_to_heads(t):
    return jnp.concatenate([t[:, g * BLK:(g + 1) * BLK] for g in range(GROUP)], axis=0)


def _rms64(t):
    return lax.rsqrt(jnp.mean(t * t, axis=-1, keepdims=True) + EPS)


def _attn_probs(kh, n, q_t, kvc_ref, kvp_ref, qg_ref, kg_ref, tri_ref, bias_ref, sink_ref):
    ks = slice(kh * HD, (kh + 1) * HD)
    vs = slice(2 * HD + kh * HD, 2 * HD + (kh + 1) * HD)
    kp, kc = kvp_ref[:, ks], kvc_ref[:, ks]
    kpb = (kp * _rms64(kp) * kg_ref[...]).astype(BF16)
    kcb = (kc * _rms64(kc) * kg_ref[...]).astype(BF16)
    qs = _heads_to_lanes(q_t, kh)
    rq = lax.rsqrt(jnp.mean(qs * qs, axis=0, keepdims=True) + EPS)
    qy = qs * rq
    qhb = (qy * (qg_ref[...] * SCALE)).astype(BF16)
    from_prev = tri_ref[...] > 0.5
    no_prev = jnp.where(n > 0, 0.0, NEG)
    s = jnp.where(from_prev, _dot(kpb, qhb) + no_prev, _dot(kcb, qhb)) + bias_ref[kh]
    sink = sink_ref[kh:kh + 1, :]
    m = jnp.maximum(jnp.max(s, axis=0, keepdims=True), sink)
    e = jnp.exp(s - m)
    es = jnp.exp(sink - m)
    rz = 1.0 / (jnp.sum(e, axis=0, keepdims=True) + es)
    prev_mask = tri_ref[...].astype(BF16)
    return e * rz, es * rz, prev_mask, qhb, kpb, kcb, kvp_ref[:, vs].astype(BF16), kvc_ref[:, vs].astype(BF16), qy, rq


def _unfold(t, prev_mask):
    prev = t * prev_mask
    return prev, t - prev


def _attn_specs(n_seq, S):
    nb = S // BLK
    cur = lambda cols: pl.BlockSpec((BLK, cols), lambda b, n: (b * nb + n, 0))
    prev = lambda cols: pl.BlockSpec((BLK, cols), lambda b, n: (b * nb + jnp.maximum(n - 1, 0), 0))
    consts = [_resident((HD, GROUP * BLK)), _resident((1, HD)), _resident((BLK, GROUP * BLK)),
              _resident((NKV, BLK, GROUP * BLK)), _resident((NKV, GROUP * BLK))]
    return nb, cur, prev, consts


def _attn_fwd(q, kv, qg_cols, kg, sink_rows, n_seq, S):
    T = n_seq * S
    nb, cur, prev, consts = _attn_specs(n_seq, S)
    tri, bias = _attn_consts()

    def body(q_ref, kvc_ref, kvp_ref, qg_ref, kg_ref, tri_ref, bias_ref, sink_ref, o_ref):
        n = pl.program_id(1)
        q_t = q_ref[...].T
        o_t = []
        for kh in range(NKV):
            p, _, prev_mask, _, _, _, vpb, vcb, _, _ = _attn_probs(kh, n, q_t, kvc_ref, kvp_ref, qg_ref, kg_ref,
                                                                   tri_ref, bias_ref, sink_ref)
            pp, pc = _unfold(p.astype(BF16), prev_mask)
            o_t.append(_lanes_to_heads(_dot_tn(vpb, pp) + _dot_tn(vcb, pc)))
        o_ref[...] = jnp.concatenate(o_t, axis=0).T.astype(BF16)

    return pl.pallas_call(
        body, name="attn_fwd", grid=(n_seq, nb),
        in_specs=[cur(D), cur(256), prev(256)] + consts,
        out_specs=cur(D),
        out_shape=jax.ShapeDtypeStruct((T, D), BF16),
        compiler_params=_params(("parallel", "parallel")),
    )(q, kv, kv, qg_cols, kg, tri, bias, sink_rows)


def _merge_fwd(x2, gg, yc, o, wa, wm):
    T = x2.shape[0]
    tm = min(512, T)

    def body(x_ref, gg_ref, yc_ref, o_ref, wa_ref, wm_ref, mix_ref, h_ref):
        ya = _dot(o_ref[...], wa_ref[...])
        mix = (_sig(gg_ref[:, 0:D]) * yc_ref[...] + _sig(gg_ref[:, D:2 * D]) * ya).astype(BF16)
        mix_ref[...] = mix
        h_ref[...] = x_ref[...] + _dot(mix, wm_ref[...])

    return pl.pallas_call(
        body, name="merge_fwd", grid=(T // tm,),
        in_specs=[_rows(tm, D), _rows(tm, 2 * D), _rows(tm, D), _rows(tm, D), _resident((D, D)), _resident((D, D))],
        out_specs=[_rows(tm, D), _rows(tm, D)],
        out_shape=[jax.ShapeDtypeStruct((T, D), BF16), jax.ShapeDtypeStruct((T, D), F32)],
        compiler_params=_params(("parallel",)),
    )(x2, gg, yc, o, wa, wm)


FF_CHUNK = DFF // 2


def _ffn(h, tgt, g2, wf, wd):
    T = h.shape[0]
    tm = min(256, T)

    def body(h_ref, t_ref, g_ref, wf_ref, wd_ref, dh_ref, dhb_ref, hn_ref, act_ref, dout_ref, dgu_ref, st_ref,
             gsc, usc):
        @pl.when(pl.program_id(0) == 0)
        def _():
            st_ref[...] = jnp.zeros((8, D), F32)

        hh = h_ref[...]
        rstd = lax.rsqrt(jnp.mean(hh * hh, axis=-1, keepdims=True) + EPS)
        hhat = hh * rstd
        hn = (hhat * g_ref[...]).astype(BF16)
        hn_ref[...] = hn
        out = hh
        for c in range(DFF // FF_CHUNK):
            cs = slice(c * FF_CHUNK, (c + 1) * FF_CHUNK)
            us = slice(DFF + c * FF_CHUNK, DFF + (c + 1) * FF_CHUNK)
            g = _dot(hn, wf_ref[c])
            u = _dot(hn, wf_ref[2 + c])
            gsc[:, cs] = g
            usc[:, cs] = u
            act = (g * _sig(g) * u).astype(BF16)
            act_ref[:, cs] = act
            out = out + _dot(act, wd_ref[cs, :])
        err = out - t_ref[...]
        dout = err * (1.0 / D)
        doutb = dout.astype(BF16)
        dout_ref[...] = doutb
        dhn = jnp.zeros((tm, D), F32)
        for c in range(DFF // FF_CHUNK):
            cs = slice(c * FF_CHUNK, (c + 1) * FF_CHUNK)
            us = slice(DFF + c * FF_CHUNK, DFF + (c + 1) * FF_CHUNK)
            g = gsc[:, cs]
            u = usc[:, cs]
            dact = _dot_nt(doutb, wd_ref[cs, :])
            sg = _sig(g)
            dg = (dact * u * (sg * (1.0 + g * (1.0 - sg)))).astype(BF16)
            du = (dact * (g * sg)).astype(BF16)
            dgu_ref[:, cs] = dg
            dgu_ref[:, us] = du
            dhn = dhn + _dot_nt(dg, wf_ref[c]) + _dot_nt(du, wf_ref[2 + c])
        st_ref[0:1, :] += jnp.sum(dhn * hhat, axis=0, keepdims=True)
        st_ref[1:2, :] += jnp.sum(err * err, axis=0, keepdims=True)
        dhh = dhn * g_ref[...]
        dh = dout + rstd * (dhh - hhat * jnp.mean(dhh * hhat, axis=-1, keepdims=True))
        dh_ref[...] = dh
        dhb_ref[...] = dh.astype(BF16)

    return pl.pallas_call(
        body, name="ffn_fwd_bwd", grid=(T // tm,),
        in_specs=[_rows(tm, D), _rows(tm, D), _resident((1, D)), _resident((4, D, FF_CHUNK)), _resident((DFF, D))],
        out_specs=[_rows(tm, D), _rows(tm, D), _rows(tm, D), _rows(tm, DFF), _rows(tm, D), _rows(tm, 2 * DFF),
                   _whole((8, D))],
        out_shape=[jax.ShapeDtypeStruct((T, D), F32), jax.ShapeDtypeStruct((T, D), BF16),
                   jax.ShapeDtypeStruct((T, D), BF16), jax.ShapeDtypeStruct((T, DFF), BF16),
                   jax.ShapeDtypeStruct((T, D), BF16), jax.ShapeDtypeStruct((T, 2 * DFF), BF16),
                   jax.ShapeDtypeStruct((8, D), F32)],
        scratch_shapes=[pltpu.VMEM((tm, DFF), F32), pltpu.VMEM((tm, DFF), F32)],
        compiler_params=_params(("arbitrary",), 56),
    )(h, tgt, g2, wf, wd)


def _merge_bwd(dhb, gg, yc, o, wm, wa, wc):
    T = dhb.shape[0]
    tm = min(512, T)

    def body(dh_ref, gg_ref, yc_ref, o_ref, wm_ref, wa_ref, wc_ref, dgg_ref, dyc_ref, dya_ref, do_ref, dh3_ref):
        dmix = _dot_nt(dh_ref[...], wm_ref[...])
        gc = _sig(gg_ref[:, 0:D])
        ga = _sig(gg_ref[:, D:2 * D])
        yc = yc_ref[...]
        ya = _dot(o_ref[...], wa_ref[...])
        dgg_ref[:, 0:D] = (dmix * yc * gc * (1.0 - gc)).astype(BF16)
        dgg_ref[:, D:2 * D] = (dmix * ya * ga * (1.0 - ga)).astype(BF16)
        dyc = (dmix * gc).astype(BF16)
        dya = (dmix * ga).astype(BF16)
        dyc_ref[...] = dyc
        dya_ref[...] = dya
        do_ref[...] = _dot_nt(dya, wa_ref[...]).astype(BF16)
        dh3_ref[...] = _dot_nt(dyc, wc_ref[...])

    return pl.pallas_call(
        body, name="merge_bwd", grid=(T // tm,),
        in_specs=[_rows(tm, D), _rows(tm, 2 * D), _rows(tm, D), _rows(tm, D), _resident((D, D)), _resident((D, D)),
                  _resident((D, D))],
        out_specs=[_rows(tm, 2 * D), _rows(tm, D), _rows(tm, D), _rows(tm, D), _rows(tm, D)],
        out_shape=[jax.ShapeDtypeStruct((T, 2 * D), BF16), jax.ShapeDtypeStruct((T, D), BF16),
                   jax.ShapeDtypeStruct((T, D), BF16), jax.ShapeDtypeStruct((T, D), BF16),
                   jax.ShapeDtypeStruct((T, D), F32)],
        compiler_params=_params(("parallel",), 56),
    )(dhb, gg, yc, o, wm, wa, wc)


def _attn_bwd(q, kv, do, qg_cols, kg, sink_rows, n_seq, S):
    T = n_seq * S
    nb, cur, prev, consts = _attn_specs(n_seq, S)
    tri, bias = _attn_consts()

    def body(q_ref, kvc_ref, kvp_ref, do_ref, qg_ref, kg_ref, tri_ref, bias_ref, sink_ref, dq_ref, dkc_ref, dkp_ref,
             dqg_ref, dsk_ref):
        n = pl.program_id(1)

        @pl.when((pl.program_id(0) == 0) & (n == 0))
        def _():
            dqg_ref[...] = jnp.zeros((HD, BLK), F32)
            dsk_ref[...] = jnp.zeros((8, 128), F32)

        lane = lax.broadcasted_iota(jnp.int32, (1, 128), 1)
        q_t = q_ref[...].T
        do_t = do_ref[...].astype(F32).T
        dq_t = []
        for kh in range(NKV):
            p, ps, prev_mask, qhb, kpb, kcb, vpb, vcb, qy, rq = _attn_probs(
                kh, n, q_t, kvc_ref, kvp_ref, qg_ref, kg_ref, tri_ref, bias_ref, sink_ref)
            dob = _heads_to_lanes(do_t, kh).astype(BF16)
            dp = jnp.where(tri_ref[...] > 0.5, _dot(vpb, dob), _dot(vcb, dob))
            delta = jnp.sum(p * dp, axis=0, keepdims=True)
            dsp, dsc = _unfold((p * (dp - delta)).astype(BF16), prev_mask)
            pp, pc = _unfold(p.astype(BF16), prev_mask)
            dsink = -ps * delta
            dqh = (_dot_tn(kpb, dsp) + _dot_tn(kcb, dsc)) * SCALE
            dqg = dqh * qy
            dqg_ref[...] += sum(dqg[:, g * BLK:(g + 1) * BLK] for g in range(GROUP))
            dy = dqh * qg_ref[...]
            dq_t.append(_lanes_to_heads(rq * (dy - qy * jnp.mean(dy * qy, axis=0, keepdims=True))))
            row = jnp.zeros((1, 128), F32)
            for g in range(GROUP):
                h = kh * GROUP + g
                row = row + jnp.where(lane == h, jnp.sum(dsink[:, g * BLK:(g + 1) * BLK], axis=1, keepdims=True), 0.0)
            dsk_ref[0:1, :] += row
            ks = slice(kh * HD, (kh + 1) * HD)
            vs = slice(2 * HD + kh * HD, 2 * HD + (kh + 1) * HD)
            dkp_ref[:, ks] = _dot_nt(dsp, qhb)
            dkc_ref[:, ks] = _dot_nt(dsc, qhb)
            dkp_ref[:, vs] = _dot_nt(pp, dob)
            dkc_ref[:, vs] = _dot_nt(pc, dob)
        dq_ref[...] = jnp.concatenate(dq_t, axis=0).T.astype(BF16)

    return pl.pallas_call(
        body, name="attn_bwd", grid=(n_seq, nb),
        in_specs=[cur(D), cur(256), prev(256), cur(D)] + consts,
        out_specs=[cur(D), cur(256), cur(256), _whole((HD, BLK)), _whole((8, 128))],
        out_shape=[jax.ShapeDtypeStruct((T, D), BF16), jax.ShapeDtypeStruct((T, 256), F32),
                   jax.ShapeDtypeStruct((T, 256), F32), jax.ShapeDtypeStruct((HD, BLK), F32),
                   jax.ShapeDtypeStruct((8, 128), F32)],
        compiler_params=_params(("arbitrary", "arbitrary")),
    )(q, kv, kv, do, qg_cols, kg, tri, bias, sink_rows)


def _kv_bwd(kv, dkc, dkp, kg_cols, n_seq, S):
    T = n_seq * S
    seq = lambda cols: pl.BlockSpec((S, cols), lambda b: (b, 0))

    def body(kv_ref, dkc_ref, dkp_ref, kg_ref, dkv_ref, dkg_ref):
        @pl.when(pl.program_id(0) == 0)
        def _():
            dkg_ref[...] = jnp.zeros((HD, BLK), F32)

        from_next = jnp.concatenate([dkp_ref[BLK:S, :], jnp.zeros((BLK, 256), F32)], axis=0)
        d = dkc_ref[...] + from_next
        d_t = d[:, 0:2 * HD].T
        k_t = kv_ref[:, 0:2 * HD].T
        kg = jnp.concatenate([kg_ref[...]] * (S // BLK), axis=1)
        out = []
        for kh in range(NKV):
            k = k_t[kh * HD:(kh + 1) * HD, :]
            r = lax.rsqrt(jnp.mean(k * k, axis=0, keepdims=True) + EPS)
            y = k * r
            dkh = d_t[kh * HD:(kh + 1) * HD, :]
            dkg = dkh * y
            dkg_ref[...] += sum(dkg[:, j * BLK:(j + 1) * BLK] for j in range(S // BLK))
            dy = dkh * kg
            out.append(r * (dy - y * jnp.mean(dy * y, axis=0, keepdims=True)))
        dkv_ref[:, 0:2 * HD] = jnp.concatenate(out, axis=0).T.astype(BF16)
        dkv_ref[:, 2 * HD:4 * HD] = d[:, 2 * HD:4 * HD].astype(BF16)

    return pl.pallas_call(
        body, name="kv_bwd", grid=(n_seq,),
        in_specs=[seq(256), seq(256), seq(256), _resident((HD, BLK))],
        out_specs=[seq(256), _whole((HD, BLK))],
        out_shape=[jax.ShapeDtypeStruct((T, 256), BF16), jax.ShapeDtypeStruct((HD, BLK), F32)],
        compiler_params=_params(("arbitrary",)),
    )(kv, dkc, dkp, kg_cols)


def _conv_bwd(h1, dh3, ag, dw, lng, lnb, n_seq, S):
    T = n_seq * S
    tc = min(512, S)
    nt = S // tc

    def body(h1_ref, dh3_ref, a_ref, gt_ref, dw_ref, lng_ref, lnb_ref, dag_ref, ddw_ref, st_ref, extd, acc8, shd):
        i = pl.program_id(1)

        @pl.when((pl.program_id(0) == 0) & (i == 0))
        def _():
            acc8[...] = jnp.zeros((CW * 8, D), F32)
            st_ref[...] = jnp.zeros((8, D), F32)

        @pl.when(i == 0)
        def _():
            extd[tc:tc + 32, :] = jnp.zeros((32, D), F32)

        h1 = h1_ref[...]
        mu = jnp.mean(h1, axis=-1, keepdims=True)
        cen = h1 - mu
        rstd = lax.rsqrt(jnp.mean(cen * cen, axis=-1, keepdims=True) + EPS)
        xh = cen * rstd
        h2 = xh * lng_ref[...] + lnb_ref[...]
        sg = _sig(h2)
        dh2 = dh3_ref[...] * (sg * (1.0 + h2 * (1.0 - sg)))
        st_ref[1:2, :] += jnp.sum(dh2 * xh, axis=0, keepdims=True)
        st_ref[2:3, :] += jnp.sum(dh2, axis=0, keepdims=True)
        dxh = dh2 * lng_ref[...]
        dh1 = rstd * (dxh - jnp.mean(dxh, axis=-1, keepdims=True)
                      - xh * jnp.mean(dxh * xh, axis=-1, keepdims=True))
        st_ref[0:1, :] += jnp.sum(dh1, axis=0, keepdims=True)
        extd[0:tc, :] = dh1
        _shift_copies(shd, extd, 0)
        for cb in range(D // 128):
            cs = slice(cb * 128, (cb + 1) * 128)
            for rb in range(tc // 128):
                rs = slice(rb * 128, (rb + 1) * 128)
                a = a_ref[rs, cs]
                sgt = _sig(gt_ref[rs, cs])
                h0 = a * sgt
                acc = jnp.zeros((128, 128), F32)
                for phase, offs in _tap_phases():
                    for m, o in enumerate(offs):
                        j = CW - 1 - o
                        ahead = shd[phase, rb * 128 + 8 * m:rb * 128 + 8 * m + 128, cs]
                        acc = acc + dw_ref[j:j + 1, cs] * ahead
                        acc8[j * 8:(j + 1) * 8, cs] += jnp.sum((h0 * ahead).reshape(16, 8, 128), axis=0)
                dag_ref[rs, cs] = (acc * sgt).astype(BF16)
                dag_ref[rs, cb * 128 + D:(cb + 1) * 128 + D] = (acc * a * sgt * (1.0 - sgt)).astype(BF16)
        extd[tc:tc + 32, :] = extd[0:32, :]

        @pl.when((pl.program_id(0) == n_seq - 1) & (i == nt - 1))
        def _():
            for j in range(CW):
                ddw_ref[j:j + 1, :] = jnp.sum(acc8[j * 8:(j + 1) * 8, :], axis=0, keepdims=True)
            ddw_ref[CW:32, :] = jnp.zeros((32 - CW, D), F32)

    tile = lambda col: pl.BlockSpec((tc, D), lambda b, i: (b * nt + (nt - 1 - i), col))
    return pl.pallas_call(
        body, name="conv_bwd", grid=(n_seq, nt),
        in_specs=[tile(0), tile(0), tile(0), tile(1), _resident((32, D)), _resident((1, D)), _resident((1, D))],
        out_specs=[pl.BlockSpec((tc, 2 * D), lambda b, i: (b * nt + (nt - 1 - i), 0)), _whole((32, D)),
                   _whole((8, D))],
        out_shape=[jax.ShapeDtypeStruct((T, 2 * D), BF16), jax.ShapeDtypeStruct((32, D), F32),
                   jax.ShapeDtypeStruct((8, D), F32)],
        scratch_shapes=[pltpu.VMEM((tc + 32, D), F32), pltpu.VMEM((CW * 8, D), F32),
                        pltpu.VMEM((8, tc + 24, D), F32)],
        compiler_params=_params(("arbitrary", "arbitrary"), 56),
    )(h1, dh3, ag, ag, dw, lng, lnb)


def _in_proj_bwd(dag, dq, dkv, dgg, dh, x2, g1, w_in):
    T = x2.shape[0]
    tm = min(512, T)

    def body(dag_ref, dq_ref, dkv_ref, dgg_ref, dh_ref, x_ref, g_ref, w_ref, dx_ref, dg_ref):
        @pl.when(pl.program_id(0) == 0)
        def _():
            dg_ref[...] = jnp.zeros((1, D), F32)

        dxn = (_dot(dag_ref[...], w_ref[U_CONV, :]) + _dot(dq_ref[...], w_ref[U_Q, :])
               + _dot(dkv_ref[...], w_ref[U_KV, :]) + _dot(dgg_ref[...], w_ref[U_GATES, :]))
        x = x_ref[...]
        rstd = lax.rsqrt(jnp.mean(x * x, axis=-1, keepdims=True) + EPS)
        xh = x * rstd
        dg_ref[...] += jnp.sum(dxn * xh, axis=0, keepdims=True)
        dxh = dxn * g_ref[...]
        dx_ref[...] = dh_ref[...] + rstd * (dxh - xh * jnp.mean(dxh * xh, axis=-1, keepdims=True))

    return pl.pallas_call(
        body, name="in_proj_bwd", grid=(T // tm,),
        in_specs=[_rows(tm, 2 * D), _rows(tm, D), _rows(tm, 256), _rows(tm, 2 * D), _rows(tm, D), _rows(tm, D),
                  _resident((1, D)), _resident((IN_COLS, D))],
        out_specs=[_rows(tm, D), _whole((1, D))],
        out_shape=[jax.ShapeDtypeStruct((T, D), F32), jax.ShapeDtypeStruct((1, D), F32)],
        compiler_params=_params(("arbitrary",)),
    )(dag, dq, dkv, dgg, dh, x2, g1, w_in)


def _tn_matmul(a, b, name, column_blocks=False):
    T, K = a.shape
    N = b.shape[1]
    tk = K if K <= 1024 else K // 2
    tn = N if N <= 1024 else (1024 if N % 1024 == 0 and not column_blocks else N // 4)
    tt = min(2048, T)
    assert K % tk == 0 and N % tn == 0 and T % tt == 0 and tk % 128 == 0 and tn % 128 == 0

    def body(a_ref, b_ref, o_ref):
        @pl.when(pl.program_id(2) == 0)
        def _():
            o_ref[...] = jnp.zeros((tk, tn), F32)

        o_ref[...] += _dot_tn(a_ref[...], b_ref[...])

    return pl.pallas_call(
        body, name=name, grid=(K // tk, N // tn, T // tt),
        in_specs=[pl.BlockSpec((tt, tk), lambda i, j, t: (t, i)), pl.BlockSpec((tt, tn), lambda i, j, t: (t, j))],
        out_specs=(pl.BlockSpec((None, tk, tn), lambda i, j, t: (j, i, 0)) if column_blocks
                   else pl.BlockSpec((tk, tn), lambda i, j, t: (i, j))),
        out_shape=jax.ShapeDtypeStruct((N // tn, K, tn) if column_blocks else (K, N), F32),
        compiler_params=_params(("parallel", "parallel", "arbitrary")),
    )(a, b)


def _place():
    x, y, c = lax.axis_index("x"), lax.axis_index("y"), lax.axis_index("c")
    chips = [(1 - x, y), (x, 1 - y), (1 - x, 1 - y)]
    return x, y, c, chips


def _own_slot(slots, mine):
    chip = 2 * lax.axis_index("x") + lax.axis_index("y")
    return lax.dynamic_update_slice(slots, mine[None], (chip,) + (0,) * mine.ndim)


def _row_tile(rows, unit):
    return max(t for t in range(unit, 513, unit) if rows % t == 0)


def _gather_weights(packs):
    n = len(packs)
    halves = [p.shape[0] // 2 for p in packs]

    def body(*refs):
        srcs, dsts, token, send_sems, recv_sems = refs[:n], refs[n:2 * n], refs[2 * n], refs[2 * n + 1], refs[2 * n + 2]
        x, y, c, chips = _place()

        def piece(b, px, py, pc):
            return dsts[b].at[2 * px + py, pl.ds(pc * halves[b], halves[b]), :]

        def copy(b, k, block, to, from_src=False):
            mine = srcs[b].at[pl.ds(c * halves[b], halves[b]), :]
            return pltpu.make_async_remote_copy(
                src_ref=mine if from_src else piece(b, *block), dst_ref=piece(b, *block),
                send_sem=send_sems.at[6 * b + k], recv_sem=recv_sems.at[6 * b + k], device_id=to,
                device_id_type=MESH)

        first = [copy(b, k, (x, y, c), (*chip, c), from_src=True) for b in range(n) for k, chip in enumerate(chips)]
        for cp in first:
            cp.start()
        passed = []
        for b in range(n):
            for k, chip in enumerate(chips):
                copy(b, k, (*chip, c), (x, y, c)).wait_recv()
                passed.append(copy(b, 3 + k, (*chip, c), (x, y, 1 - c)))
                passed[-1].start()
        for b in range(n):
            for k, chip in enumerate(chips):
                copy(b, 3 + k, (*chip, 1 - c), (x, y, c)).wait_recv()
        for cp in first + passed:
            cp.wait_send()
        token[...] = jnp.zeros((8, 128), F32)

    outs = pl.pallas_call(
        body, name="gather_weights",
        in_specs=[pl.BlockSpec(memory_space=pl.ANY)] * n,
        out_specs=[pl.BlockSpec(memory_space=pl.ANY)] * n + [pl.BlockSpec(memory_space=pltpu.VMEM)],
        out_shape=[jax.ShapeDtypeStruct((4,) + p.shape, p.dtype) for p in packs]
        + [jax.ShapeDtypeStruct((8, 128), F32)],
        scratch_shapes=[pltpu.SemaphoreType.DMA((6 * n,)), pltpu.SemaphoreType.DMA((6 * n,))],
        compiler_params=pltpu.CompilerParams(has_side_effects=True),
    )(*packs)
    return [_own_slot(got, p) for got, p in zip(outs[:n], packs)], outs[n][0, 0]


def _add_halves(g, got, c_idx, name="grad_add_halves"):
    rows, w = g.shape[1], g.shape[2]
    half = rows // 2
    tr = _row_tile(half, 16)
    nt = half // tr

    def body(c_ref, g_ref, r_ref, o_ref):
        o_ref[...] = (g_ref[...] + r_ref[...]).astype(BF16)

    return pl.pallas_call(
        body, name=name,
        grid_spec=pltpu.PrefetchScalarGridSpec(
            num_scalar_prefetch=1, grid=(4, nt),
            in_specs=[pl.BlockSpec((1, tr, w), lambda q, i, c_ref: (q, c_ref[0] * nt + i, 0)),
                      pl.BlockSpec((1, tr, w), lambda q, i, c_ref: (q, i, 0))],
            out_specs=pl.BlockSpec((1, tr, w), lambda q, i, c_ref: (q, i, 0))),
        out_shape=jax.ShapeDtypeStruct((4, half, w), BF16),
        compiler_params=_params(("parallel", "parallel")),
    )(c_idx, g, got)


def _own_piece(p):
    chip = 2 * lax.axis_index("x") + lax.axis_index("y")
    return lax.dynamic_index_in_dim(p, chip, axis=0, keepdims=False)


def _sum_chips(r, c_idx, name="grad_sum_chips"):
    half, w = r.shape[1], r.shape[2]
    tr = _row_tile(half, 16)
    nt = half // tr

    def body(c_ref, r_ref, o_ref):
        acc = r_ref[0].astype(F32)
        for q in range(1, 4):
            acc = acc + r_ref[q].astype(F32)
        o_ref[...] = acc

    return pl.pallas_call(
        body, name=name,
        grid_spec=pltpu.PrefetchScalarGridSpec(
            num_scalar_prefetch=1, grid=(nt,),
            in_specs=[pl.BlockSpec((4, tr, w), lambda i, c_ref: (0, i, 0))],
            out_specs=pl.BlockSpec((tr, w), lambda i, c_ref: (c_ref[0] * nt + i, 0))),
        out_shape=jax.ShapeDtypeStruct((2 * half, w), F32),
        compiler_params=_params(("parallel",)),
    )(c_idx, r)


def _join_halves(f):
    half = f.shape[0] // 2

    def body(src, dst, send_sem, recv_sem):
        x, y, c, _ = _place()
        cp = pltpu.make_async_remote_copy(
            src_ref=src.at[pl.ds(c * half, half), :], dst_ref=dst.at[pl.ds(c * half, half), :], send_sem=send_sem,
            recv_sem=recv_sem, device_id=(x, y, 1 - c), device_id_type=MESH)
        cp.start()
        pltpu.make_async_remote_copy(
            src_ref=src.at[pl.ds(c * half, half), :], dst_ref=dst.at[pl.ds((1 - c) * half, half), :],
            send_sem=send_sem, recv_sem=recv_sem, device_id=(x, y, 1 - c), device_id_type=MESH).wait_recv()
        cp.wait_send()

    return pl.pallas_call(
        body, name="grad_join_halves",
        in_specs=[pl.BlockSpec(memory_space=pl.ANY)], out_specs=pl.BlockSpec(memory_space=pl.ANY),
        out_shape=jax.ShapeDtypeStruct(f.shape, f.dtype), input_output_aliases={0: 0},
        scratch_shapes=[pltpu.SemaphoreType.DMA, pltpu.SemaphoreType.DMA],
        compiler_params=pltpu.CompilerParams(has_side_effects=True),
    )(f)


_HBM = pl.BlockSpec(memory_space=pltpu.HBM)
_SEM = pl.BlockSpec(memory_space=pltpu.SEMAPHORE)
_EFFECT = pltpu.SideEffectType.DATAFLOW_SIDE_EFFECTING


def _start_copies(name, bufs, n_sems, plan):
    nb = len(bufs)

    def body(*refs):
        for cp in plan(refs[:nb], refs[nb], refs[nb + 1])[0]:
            cp.start()
        refs[-1][...] = jnp.zeros((8, 128), F32)

    out = pl.pallas_call(
        body, name=name,
        out_shape=(pltpu.SemaphoreType.DMA((n_sems,)), pltpu.SemaphoreType.DMA((n_sems,)),
                   *[pltpu.HBM(b.shape, b.dtype) for b in bufs], jax.ShapeDtypeStruct((8, 128), F32)),
        in_specs=[_HBM] * nb, out_specs=(_SEM, _SEM, *[_HBM] * nb, pl.BlockSpec(memory_space=pltpu.VMEM)),
        input_output_aliases={i: 2 + i for i in range(nb)},
        compiler_params=pltpu.CompilerParams(has_side_effects=_EFFECT),
    )(*[pltpu.with_memory_space_constraint(b, pltpu.HBM) for b in bufs])
    return out[0], out[1], list(out[2:2 + nb]), out[-1]


def _wait_copies(name, send_sems, recv_sems, bufs, after, plan):
    nb = len(bufs)

    def body(*refs):
        _, sends, recvs = plan(refs[:nb], refs[nb], refs[nb + 1])
        for cp in sends:
            cp.wait_send()
        for cp in recvs:
            cp.wait_recv()

    out = pl.pallas_call(
        body, name=name,
        out_shape=tuple(pltpu.HBM(b.shape, b.dtype) for b in bufs),
        in_specs=[_HBM] * nb + [_SEM, _SEM] + [pl.BlockSpec(memory_space=pl.ANY)] * len(after),
        out_specs=tuple([_HBM] * nb),
        input_output_aliases={i: i for i in range(nb)},
        compiler_params=pltpu.CompilerParams(has_side_effects=_EFFECT),
    )(*bufs, send_sems, recv_sems, *after)
    return list(out)


def _plan_gather_direct(halves):
    n = len(halves)

    def plan(refs, send_sems, recv_sems):
        x, y, c, chips = _place()
        starts, recvs = [], []
        for b, half in enumerate(halves):
            src, land = refs[b], refs[n + b]
            for k, (cx, cy) in enumerate(chips):
                for d in range(2):
                    other = c if d == 0 else 1 - c
                    i = 6 * b + 2 * k + d
                    starts.append(pltpu.make_async_remote_copy(
                        src_ref=src.at[pl.ds(c * half, half), :],
                        dst_ref=land.at[2 * x + y, pl.ds(c * half, half), :],
                        send_sem=send_sems.at[i], recv_sem=recv_sems.at[i], device_id=(cx, cy, other),
                        device_id_type=MESH))
                    recvs.append(pltpu.make_async_remote_copy(
                        src_ref=src.at[pl.ds(c * half, half), :],
                        dst_ref=land.at[2 * cx + cy, pl.ds(other * half, half), :],
                        send_sem=send_sems.at[i], recv_sem=recv_sems.at[i], device_id=(cx, cy, other),
                        device_id_type=MESH))
        return starts, starts, recvs
    return plan


def _plan_swap_halves(halves):
    n = len(halves)

    def plan(refs, send_sems, recv_sems):
        x, y, c, _ = _place()
        cps = [pltpu.make_async_remote_copy(
            src_ref=refs[b].at[:, pl.ds((1 - c) * half, half), :], dst_ref=refs[n + b], send_sem=send_sems.at[b],
            recv_sem=recv_sems.at[b], device_id=(x, y, 1 - c), device_id_type=MESH)
            for b, half in enumerate(halves)]
        return cps, cps, cps
    return plan


def _plan_scatter_chips(n):
    def plan(refs, send_sems, recv_sems):
        x, y, c, chips = _place()
        me = 2 * x + y
        starts, recvs = [], []
        for b in range(n):
            src, land = refs[b], refs[n + b]
            for k, (cx, cy) in enumerate(chips):
                i = 3 * b + k
                starts.append(pltpu.make_async_remote_copy(
                    src_ref=src.at[2 * cx + cy], dst_ref=land.at[me], send_sem=send_sems.at[i],
                    recv_sem=recv_sems.at[i], device_id=(cx, cy, c), device_id_type=MESH))
                recvs.append(pltpu.make_async_remote_copy(
                    src_ref=src.at[me], dst_ref=land.at[2 * cx + cy], send_sem=send_sems.at[i],
                    recv_sem=recv_sems.at[i], device_id=(cx, cy, c), device_id_type=MESH))
        return starts, starts, recvs
    return plan


def _plan_join_halves(halves):
    def plan(refs, send_sems, recv_sems):
        x, y, c, _ = _place()
        starts, recvs = [], []
        for b, half in enumerate(halves):
            mine, theirs = refs[b].at[pl.ds(c * half, half), :], refs[b].at[pl.ds((1 - c) * half, half), :]
            starts.append(pltpu.make_async_remote_copy(
                src_ref=mine, dst_ref=mine, send_sem=send_sems.at[b], recv_sem=recv_sems.at[b],
                device_id=(x, y, 1 - c), device_id_type=MESH))
            recvs.append(pltpu.make_async_remote_copy(
                src_ref=mine, dst_ref=theirs, send_sem=send_sems.at[b], recv_sem=recv_sems.at[b],
                device_id=(x, y, 1 - c), device_id_type=MESH))
        return starts, starts, recvs
    return plan


def _allreduce_small(vec):
    def body(v_ref, o_ref, gath, send_sems, recv_sems):
        x, y, c, _ = _place()
        me = 4 * x + 2 * y + c
        gath[me] = v_ref[...]
        sends = []
        for k in range(1, 8):
            peer = (x ^ (k >> 2), y ^ ((k >> 1) & 1), c ^ (k & 1))
            sends.append(pltpu.make_async_remote_copy(
                src_ref=v_ref, dst_ref=gath.at[me], send_sem=send_sems.at[k - 1], recv_sem=recv_sems.at[k - 1],
                device_id=peer, device_id_type=MESH))
        for cp in sends:
            cp.start()
        for k in range(1, 8):
            peer = (x ^ (k >> 2), y ^ ((k >> 1) & 1), c ^ (k & 1))
            pltpu.make_async_remote_copy(
                src_ref=v_ref, dst_ref=gath.at[4 * peer[0] + 2 * peer[1] + peer[2]], send_sem=send_sems.at[k - 1],
                recv_sem=recv_sems.at[k - 1], device_id=peer, device_id_type=MESH).wait_recv()
        for cp in sends:
            cp.wait_send()
        acc = gath[0]
        for d in range(1, 8):
            acc = acc + gath[d]
        o_ref[...] = acc

    return pl.pallas_call(
        body, name="allreduce_small",
        in_specs=[pl.BlockSpec(memory_space=pltpu.VMEM)], out_specs=pl.BlockSpec(memory_space=pltpu.VMEM),
        out_shape=jax.ShapeDtypeStruct(vec.shape, F32),
        scratch_shapes=[pltpu.VMEM((8,) + vec.shape, F32), pltpu.SemaphoreType.DMA((7,)),
                        pltpu.SemaphoreType.DMA((7,))],
    )(vec)


def _adamw(w, g, m, v, name, after):
    shape = w.shape
    if w.ndim == 1 or w.size <= 128 * 128:
        two_d = (1, w.size) if w.size % 128 else (w.size // 128, 128)
    else:
        two_d = (w.shape[0], w.size // w.shape[0])
    rows, cols = two_d
    tr = _row_tile(rows, 8) if rows % 8 == 0 and rows > 512 else rows

    def body(w_ref, g_ref, m_ref, v_ref, after_ref, d_ref, nm_ref, nv_ref):
        gr = g_ref[...]
        nm = B1 * m_ref[...] + (1.0 - B1) * gr
        nv = B2 * v_ref[...] + (1.0 - B2) * (gr * gr)
        m_hat = nm / (1.0 - B1 ** STEP)
        v_hat = nv / (1.0 - B2 ** STEP)
        d_ref[...] = -LR * (m_hat / (jnp.sqrt(v_hat) + AEPS) + WD * w_ref[...])
        nm_ref[...] = nm
        nv_ref[...] = nv

    spec = pl.BlockSpec((tr, cols), lambda i: (i, 0))
    outs = pl.pallas_call(
        body, name=name, grid=(rows // tr,),
        in_specs=[spec] * 4 + [pl.BlockSpec(memory_space=pl.ANY)], out_specs=[spec] * 3,
        out_shape=[jax.ShapeDtypeStruct(two_d, F32)] * 3,
        compiler_params=_params(("parallel",)),
    )(*[t.reshape(two_d) for t in (w, g, m, v)], after)
    return [o.reshape(shape) for o in outs]


def _rows_stacked(g):
    return g.reshape(4 * g.shape[1], D)


def _rows_to_slots(t):
    return t.reshape(4, t.shape[0] // 4, D)


def _pack_first(w_in, w_conv_out, conv_dw_w):
    dw = jnp.pad(conv_dw_w.reshape(CW, 256), ((0, 1), (0, 0)))
    dw_bits = lax.bitcast_convert_type(dw, BF16).reshape(16, D)
    return [w_in.T.astype(BF16), w_conv_out.astype(BF16), jnp.pad(dw_bits, ((0, 16), (0, 0)))]


def _unpack_first(w_in_slots, wc_slots, dw_slots):
    dw = lax.bitcast_convert_type(dw_slots[:, 0:16].reshape(4, 32, 256, 2), F32)
    return _rows_stacked(w_in_slots), _rows_stacked(wc_slots), jnp.transpose(dw, (1, 0, 2)).reshape(32, D)


class _Exchanges:
    def __init__(self, late_shards):
        self.c_idx = lax.axis_index("c").astype(jnp.int32).reshape(1)
        packs = list(late_shards)
        self.late_plan = _plan_gather_direct([p.shape[0] // 2 for p in packs])
        slots = [lax.empty((4,) + p.shape, BF16) for p in packs]
        self.late = _start_copies("gather_late_start", packs + slots, 6 * len(packs), self.late_plan)
        self.first_token = self.late[3][0, 0]

    def late_weights(self, after):
        send_sems, recv_sems, bufs, _ = self.late
        bufs = _wait_copies("gather_late_wait", send_sems, recv_sems, bufs, after, self.late_plan)
        wa, wm, wf, wd = [_own_slot(bufs[4 + b], bufs[b]) for b in range(4)]
        return _rows_stacked(wa), _rows_stacked(wm), wf, _rows_stacked(wd)

    def reduce_start(self, d_wd, d_wf4, d_wm, d_wa, d_wc):
        gs = [_rows_to_slots(d_wd), jnp.concatenate([_rows_to_slots(t) for t in (d_wm, d_wa, d_wc)], axis=1), d_wf4]
        self.halves = [g.shape[1] // 2 for g in gs]
        self.swap_plan = _plan_swap_halves(self.halves)
        lands = [lax.empty((4, h, g.shape[2]), F32) for g, h in zip(gs, self.halves)]
        self.swap = _start_copies("grad_swap_start", gs + lands, len(gs), self.swap_plan)
        return self.swap[3][0, 0]

    def reduce_mid(self, after):
        send_sems, recv_sems, bufs, _ = self.swap
        bufs = _wait_copies("grad_swap_wait", send_sems, recv_sems, bufs, after, self.swap_plan)
        n = len(self.halves)
        ps = [_add_halves(bufs[b], bufs[n + b], self.c_idx, "grad_add_halves_%d" % b) for b in range(n)]
        self.scatter_plan = _plan_scatter_chips(n)
        self.scatter = _start_copies("grad_scatter_start", ps + [lax.empty(p.shape, BF16) for p in ps], 3 * n,
                                     self.scatter_plan)
        return self.scatter[3][0, 0]

    def reduce_late(self, after):
        send_sems, recv_sems, bufs, _ = self.scatter
        bufs = _wait_copies("grad_scatter_wait", send_sems, recv_sems, bufs, after, self.scatter_plan)
        n = len(self.halves)
        fs = [_sum_chips(_own_slot(bufs[n + b], _own_piece(bufs[b])), self.c_idx, "grad_sum_chips_%d" % b)
              for b in range(n)]
        self.join_plan = _plan_join_halves(self.halves)
        self.join = _start_copies("grad_join_start", fs, n, self.join_plan)

    def reduce_end(self, after):
        send_sems, recv_sems, bufs, _ = self.join
        g_wd, sq, g_wf = _wait_copies("grad_join_wait", send_sems, recv_sems, bufs, after, self.join_plan)
        return g_wd, g_wf, sq[0:ROWS_SQ], sq[ROWS_SQ:2 * ROWS_SQ], sq[2 * ROWS_SQ:3 * ROWS_SQ]

    def w_in_start(self, d_w_in_t):
        g = _rows_to_slots(d_w_in_t)
        self.w_half = g.shape[1] // 2
        self.w_swap_plan = _plan_swap_halves([self.w_half])
        self.w_swap = _start_copies("grad_w_in_swap_start", [g, lax.empty((4, self.w_half, D), F32)], 1,
                                    self.w_swap_plan)
        return self.w_swap[3]

    def w_in_mid(self, after):
        send_sems, recv_sems, bufs, _ = self.w_swap
        g, got = _wait_copies("grad_w_in_swap_wait", send_sems, recv_sems, bufs, after, self.w_swap_plan)
        p = _add_halves(g, got, self.c_idx, "grad_add_halves_w_in")
        self.w_scatter_plan = _plan_scatter_chips(1)
        self.w_scatter = _start_copies("grad_w_in_scatter_start", [p, lax.empty(p.shape, BF16)], 3,
                                       self.w_scatter_plan)
        return self.w_scatter[3]

    def w_in_end(self, after):
        send_sems, recv_sems, bufs, _ = self.w_scatter
        p, got = _wait_copies("grad_w_in_scatter_wait", send_sems, recv_sems, bufs, after, self.w_scatter_plan)
        return _join_halves(_sum_chips(_own_slot(got, _own_piece(p)), self.c_idx, "grad_sum_chips_w_in"))


def _local_grads(x, loss_target, norm_mix_g, conv_dw_b, conv_ln_g, conv_ln_b, q_norm_g, k_norm_g, sinks, norm_ffn_g,
                 w_in, wc, dw, exchanges):
    n_seq, S, _ = x.shape
    T = n_seq * S
    x2 = x.reshape(T, D)
    tgt = loss_target.reshape(T, D)
    row = lambda t: t.reshape(1, -1)
    g1, g2 = row(norm_mix_g), row(norm_ffn_g)
    qg, kg = jnp.broadcast_to(q_norm_g.reshape(HD, 1), (HD, GROUP * BLK)), row(k_norm_g)
    lng, lnb, dwb = row(conv_ln_g), row(conv_ln_b), row(conv_dw_b)
    sink_rows = jnp.repeat(sinks.reshape(NKV, GROUP), BLK, axis=1)

    xn, ag, q, kv, gg, h1, h3, yc = _in_proj_conv_fwd(x2, g1 + exchanges.first_token, w_in, dw, dwb, lng, lnb, wc,
                                                      n_seq, S)
    o = _attn_fwd(q, kv, qg, kg, sink_rows, n_seq, S)
    wa, wm, wf, wd = exchanges.late_weights([o, yc])
    mix, h = _merge_fwd(x2, gg, yc, o, wa, wm)
    dh, dhb, hn, act, dout, dgu, ffn_stats = _ffn(h, tgt, g2, wf, wd)
    d_wd = _tn_matmul(act, dout, "dw_ffn_down")
    d_wf = _tn_matmul(hn, dgu, "dw_ffn_in", column_blocks=True)
    d_wm = _tn_matmul(mix, dhb, "dw_merge")
    dgg, dyc, dya, do, dh3 = _merge_bwd(dhb, gg, yc, o, wm, wa, wc)
    d_wa = _tn_matmul(o, dya, "dw_attn_out")
    d_wc = _tn_matmul(h3, dyc, "dw_conv_out")
    token = exchanges.reduce_start(d_wd, d_wf, d_wm, d_wa, d_wc)
    dq, dkc, dkp, dqg, dsk = _attn_bwd(q, kv, do, qg + token, kg, sink_rows, n_seq, S)
    token = exchanges.reduce_mid([dq])
    dkv, dkg = _kv_bwd(kv, dkc, dkp, jnp.broadcast_to(k_norm_g.reshape(HD, 1), (HD, BLK)), n_seq, S)
    dag, ddw, conv_stats = _conv_bwd(h1, dh3, ag, dw, lng + token, lnb, n_seq, S)
    dx, dg1 = _in_proj_bwd(dag, dq, dkv, dgg, dh, x2, g1, w_in)
    exchanges.reduce_late([dx])
    d_w_in = jnp.concatenate([_tn_matmul(dag, xn, "dw_in_conv"), _tn_matmul(dq, xn, "dw_in_q"),
                              _tn_matmul(dkv, xn, "dw_in_kv"), _tn_matmul(dgg, xn, "dw_in_gates")], axis=0)

    heads = jnp.concatenate([jnp.sum(dqg, axis=1), jnp.sum(dkg, axis=1), dsk[0, :NQ],
                             jnp.zeros((D - 2 * HD - NQ,), F32)])
    vec = jnp.concatenate([dg1, conv_stats[0:3], ffn_stats[0:1], heads[None], ffn_stats[1:2], jnp.zeros((1, D), F32),
                           ddw], axis=0)
    return ffn_stats[1], dx.reshape(x.shape), d_w_in, vec


def kernel(x, norm_mix_g, w_in, conv_dw_w, conv_dw_b, conv_ln_g, conv_ln_b, w_conv_out, q_norm_g, k_norm_g, sinks, w_attn_out, w_merge_out, norm_ffn_g, w_ffn_in, w_ffn_down, loss_target, m_norm_mix_g, m_w_in, m_conv_dw_w, m_conv_dw_b, m_conv_ln_g, m_conv_ln_b, m_w_conv_out, m_q_norm_g, m_k_norm_g, m_sinks, m_w_attn_out, m_w_merge_out, m_norm_ffn_g, m_w_ffn_in, m_w_ffn_down, v_norm_mix_g, v_w_in, v_conv_dw_w, v_conv_dw_b, v_conv_ln_g, v_conv_ln_b, v_w_conv_out, v_q_norm_g, v_k_norm_g, v_sinks, v_w_attn_out, v_w_merge_out, v_norm_ffn_g, v_w_ffn_in, v_w_ffn_down):
    chip = 2 * lax.axis_index("x") + lax.axis_index("y")

    first, token = _gather_weights(_pack_first(w_in, w_conv_out, conv_dw_w))
    exchanges = _Exchanges([t.astype(BF16) + token.astype(BF16)
                            for t in (w_attn_out, w_merge_out, w_ffn_in, w_ffn_down)])
    _, grad_x, d_w_in, vec = _local_grads(x, loss_target, norm_mix_g, conv_dw_b, conv_ln_g, conv_ln_b, q_norm_g,
                                          k_norm_g, sinks, norm_ffn_g, *_unpack_first(*first), exchanges)

    g_wd, g_wf, g_wm, g_wa, g_wc = exchanges.reduce_end([d_w_in])
    w_in_token = exchanges.w_in_start(d_w_in)
    small = _allreduce_small(vec + w_in_token[0, 0])
    loss = 0.5 / D * jnp.sum(small[6])
    g_dw = lax.dynamic_slice_in_dim(small[8:8 + CW], chip * 256, 256, axis=1).reshape(CW, 1, 256)
    grads = {
        "norm_mix_g": small[0], "conv_dw_w": g_dw, "conv_dw_b": small[1], "conv_ln_g": small[2],
        "conv_ln_b": small[3], "w_conv_out": g_wc, "q_norm_g": small[5, 0:HD], "k_norm_g": small[5, HD:2 * HD],
        "sinks": small[5, 2 * HD:2 * HD + NQ], "w_attn_out": g_wa, "w_merge_out": g_wm, "norm_ffn_g": small[4],
        "w_ffn_in": g_wf, "w_ffn_down": g_wd,
    }
    weights = dict(norm_mix_g=norm_mix_g, w_in=w_in, conv_dw_w=conv_dw_w, conv_dw_b=conv_dw_b, conv_ln_g=conv_ln_g,
                   conv_ln_b=conv_ln_b, w_conv_out=w_conv_out, q_norm_g=q_norm_g, k_norm_g=k_norm_g, sinks=sinks,
                   w_attn_out=w_attn_out, w_merge_out=w_merge_out, norm_ffn_g=norm_ffn_g, w_ffn_in=w_ffn_in,
                   w_ffn_down=w_ffn_down)
    m_in = dict(norm_mix_g=m_norm_mix_g, w_in=m_w_in, conv_dw_w=m_conv_dw_w, conv_dw_b=m_conv_dw_b,
                conv_ln_g=m_conv_ln_g, conv_ln_b=m_conv_ln_b, w_conv_out=m_w_conv_out, q_norm_g=m_q_norm_g,
                k_norm_g=m_k_norm_g, sinks=m_sinks, w_attn_out=m_w_attn_out, w_merge_out=m_w_merge_out,
                norm_ffn_g=m_norm_ffn_g, w_ffn_in=m_w_ffn_in, w_ffn_down=m_w_ffn_down)
    v_in = dict(norm_mix_g=v_norm_mix_g, w_in=v_w_in, conv_dw_w=v_conv_dw_w, conv_dw_b=v_conv_dw_b,
                conv_ln_g=v_conv_ln_g, conv_ln_b=v_conv_ln_b, w_conv_out=v_w_conv_out, q_norm_g=v_q_norm_g,
                k_norm_g=v_k_norm_g, sinks=v_sinks, w_attn_out=v_w_attn_out, w_merge_out=v_w_merge_out,
                norm_ffn_g=v_norm_ffn_g, w_ffn_in=v_w_ffn_in, w_ffn_down=v_w_ffn_down)
    names = list(weights)
    updates = {}
    after = exchanges.w_in_mid([small])
    for n in names:
        if n != "w_in":
            updates[n] = _adamw(weights[n], grads[n], m_in[n], v_in[n], "adamw_" + n, after)
    g_w_in_t = exchanges.w_in_end([updates[n][0] for n in updates])
    grads["w_in"] = g_w_in_t.T
    updates["w_in"] = [t.T for t in _adamw(w_in.T, g_w_in_t, m_w_in.T, v_w_in.T, "adamw_w_in", g_w_in_t)]
    return (loss, grad_x, *[grads[n] for n in names], *[updates[n][0] for n in names],
            *[updates[n][1] for n in names], *[updates[n][2] for n in names])
```

```python
import math

import jax
import jax.numpy as jnp
import numpy as np
from jax import lax
from jax.experimental import pallas as pl
from jax.experimental.pallas import tpu as pltpu

F32 = jnp.float32
BF16 = jnp.bfloat16

D = 1024
CW = 31
HD = 64
NQ = 16
NKV = 2
GROUP = NQ // NKV
BLK = 128
DFF = 2816
EPS = 1e-6
NEG = -1e30
IN_COLS = 5376
U_CONV, U_Q, U_KV = slice(0, 2 * D), slice(2 * D, 3 * D), slice(3 * D, 3 * D + 2 * NKV * HD)
U_GATES = slice(3 * D + 2 * NKV * HD, IN_COLS)
SCALE = 1.0 / math.sqrt(HD)

LR, B1, B2, AEPS, WD, STEP = 0.001, 0.9, 0.999, 1e-08, 0.01, 10

MIB = 1024 * 1024
MESH = pl.DeviceIdType.MESH

ROWS_W_IN = IN_COLS // 4
ROWS_SQ = D // 4
ROWS_FFN_IN = 2 * DFF // 4
ROWS_DOWN = DFF // 4


def _sig(x):
    return 1.0 / (1.0 + jnp.exp(-x))


def _dot(a, b):
    return jnp.dot(a, b, preferred_element_type=F32)


def _dot_nt(a, b):
    return lax.dot_general(a, b, (((1,), (1,)), ((), ())), preferred_element_type=F32)


def _dot_tn(a, b):
    return lax.dot_general(a, b, (((0,), (0,)), ((), ())), preferred_element_type=F32)


def _params(sem, vmem_mib=48):
    return pltpu.CompilerParams(dimension_semantics=sem, vmem_limit_bytes=vmem_mib * MIB)


def _resident(shape):
    return pl.BlockSpec(shape, lambda *_: (0,) * len(shape), pipeline_mode=pl.Buffered(1))


def _whole(shape):
    return pl.BlockSpec(shape, lambda *_: (0,) * len(shape))


def _rows(tm, cols):
    return pl.BlockSpec((tm, cols), lambda i: (i, 0))


def _tap_phases():
    return [(phase, list(range(phase, CW, 8))) for phase in range(8)]


def _shift_copies(dst, src, base):
    for phase, taps in _tap_phases():
        n = dst.shape[1] - 8 * (4 - len(taps))
        dst[phase, 0:n, :] = src[base + phase:base + phase + n, :]


def _in_proj_conv_fwd(x2, g1, w_in, dw, dwb, lng, lnb, n_seq, S):
    T = n_seq * S
    tc = min(256, S)
    nt = S // tc

    def body(x_ref, g_ref, w_ref, dw_ref, dwb_ref, lng_ref, lnb_ref, xn_ref, ag_ref, q_ref, kv_ref, gg_ref,
             h1_ref, h3_ref, ext, sh):
        i = pl.program_id(1)

        @pl.when(i == 0)
        def _():
            ext[0:32, :] = jnp.zeros((32, D), F32)

        x = x_ref[...]
        rstd = lax.rsqrt(jnp.mean(x * x, axis=-1, keepdims=True) + EPS)
        xn = (x * rstd * g_ref[...]).astype(BF16)
        xn_ref[...] = xn
        ag = _dot_nt(xn, w_ref[U_CONV, :])
        ag_ref[...] = ag
        ext[32:32 + tc, :] = ag[:, 0:D] * _sig(ag[:, D:2 * D])
        q_ref[...] = _dot_nt(xn, w_ref[U_Q, :])
        kv_ref[...] = _dot_nt(xn, w_ref[U_KV, :])
        gg_ref[...] = _dot_nt(xn, w_ref[U_GATES, :])
        _shift_copies(sh, ext, 2)
        for cb in range(D // 128):
            cs = slice(cb * 128, (cb + 1) * 128)
            acc = jnp.broadcast_to(dwb_ref[:, cs], (tc, 128))
            for phase, taps in _tap_phases():
                for m, j in enumerate(taps):
                    acc = acc + dw_ref[j:j + 1, cs] * sh[phase, 8 * m:8 * m + tc, cs]
            h1_ref[:, cs] = acc
        ext[0:32, :] = ext[tc:tc + 32, :]
        h1 = h1_ref[...]
        mu = jnp.mean(h1, axis=-1, keepdims=True)
        cen = h1 - mu
        var = jnp.mean(cen * cen, axis=-1, keepdims=True)
        h2 = cen * lax.rsqrt(var + EPS) * lng_ref[...] + lnb_ref[...]
        h3 = (h2 * _sig(h2)).astype(BF16)
        h3_ref[...] = h3

    tile = lambda cols: pl.BlockSpec((tc, cols), lambda b, i: (b * nt + i, 0))
    shape = lambda cols, dtype: jax.ShapeDtypeStruct((T, cols), dtype)
    return pl.pallas_call(
        body, name="in_proj_conv_fwd", grid=(n_seq, nt),
        in_specs=[tile(D), _resident((1, D)), _resident((IN_COLS, D)), _resident((32, D)), _resident((1, D)),
                  _resident((1, D)), _resident((1, D))],
        out_specs=[tile(D), tile(2 * D), tile(D), tile(256), tile(2 * D), tile(D), tile(D)],
        out_shape=[shape(D, BF16), shape(2 * D, F32), shape(D, F32), shape(256, F32), shape(2 * D, F32),
                   shape(D, F32), shape(D, BF16)],
        scratch_shapes=[pltpu.VMEM((32 + tc, D), F32), pltpu.VMEM((8, tc + 24, D), F32)],
        compiler_params=_params(("parallel", "arbitrary"), 56),
    )(x2, g1, w_in, dw, dwb, lng, lnb)


def _attn_consts():
    k = np.arange(BLK)[:, None]
    i = np.arange(GROUP * BLK)[None, :] % BLK
    from_prev = k > i
    dist = np.where(from_prev, i + BLK - k, i - k).astype(np.float32)
    head = np.arange(GROUP * BLK)[None, :] // BLK
    bias = []
    for kh in range(NKV):
        slope = np.exp2(-8.0 * (kh * GROUP + head + 1) / NQ).astype(np.float32)
        bias.append(-slope * dist)
    return jnp.asarray(from_prev.astype(np.float32)), jnp.asarray(np.stack(bias))


def _heads_to_lanes(t, kh):
    return jnp.concatenate([t[(kh * GROUP + g) * HD:(kh * GROUP + g + 1) * HD, :] for g in range(GROUP)], axis=1)


def _lanes_to_heads(t):
    return jnp.concatenate([t[:, g * BLK:(g + 1) * BLK] for g in range(GROUP)], axis=0)


def _rms64(t):
    return lax.rsqrt(jnp.mean(t * t, axis=-1, keepdims=True) + EPS)


def _attn_probs(kh, n, q_t, kvc_ref, kvp_ref, qg_ref, kg_ref, tri_ref, bias_ref, sink_ref):
    ks = slice(kh * HD, (kh + 1) * HD)
    vs = slice(2 * HD + kh * HD, 2 * HD + (kh + 1) * HD)
    kp, kc = kvp_ref[:, ks], kvc_ref[:, ks]
    kpb = (kp * _rms64(kp) * kg_ref[...]).astype(BF16)
    kcb = (kc * _rms64(kc) * kg_ref[...]).astype(BF16)
    qs = _heads_to_lanes(q_t, kh)
    rq = lax.rsqrt(jnp.mean(qs * qs, axis=0, keepdims=True) + EPS)
    qy = qs * rq
    qhb = (qy * (qg_ref[...] * SCALE)).astype(BF16)
    from_prev = tri_ref[...] > 0.5
    no_prev = jnp.where(n > 0, 0.0, NEG)
    s = jnp.where(from_prev, _dot(kpb, qhb) + no_prev, _dot(kcb, qhb)) + bias_ref[kh]
    sink = sink_ref[kh:kh + 1, :]
    m = jnp.maximum(jnp.max(s, axis=0, keepdims=True), sink)
    e = jnp.exp(s - m)
    es = jnp.exp(sink - m)
    rz = 1.0 / (jnp.sum(e, axis=0, keepdims=True) + es)
    prev_mask = tri_ref[...].astype(BF16)
    return e * rz, es * rz, prev_mask, qhb, kpb, kcb, kvp_ref[:, vs].astype(BF16), kvc_ref[:, vs].astype(BF16), qy, rq


def _unfold(t, prev_mask):
    prev = t * prev_mask
    return prev, t - prev


def _attn_specs(n_seq, S):
    nb = S // BLK
    cur = lambda cols: pl.BlockSpec((BLK, cols), lambda b, n: (b * nb + n, 0))
    prev = lambda cols: pl.BlockSpec((BLK, cols), lambda b, n: (b * nb + jnp.maximum(n - 1, 0), 0))
    consts = [_resident((HD, GROUP * BLK)), _resident((1, HD)), _resident((BLK, GROUP * BLK)),
              _resident((NKV, BLK, GROUP * BLK)), _resident((NKV, GROUP * BLK))]
    return nb, cur, prev, consts


def _attn_fwd(q, kv, qg_cols, kg, sink_rows, n_seq, S):
    T = n_seq * S
    nb, cur, prev, consts = _attn_specs(n_seq, S)
    tri, bias = _attn_consts()

    def body(q_ref, kvc_ref, kvp_ref, qg_ref, kg_ref, tri_ref, bias_ref, sink_ref, o_ref):
        n = pl.program_id(1)
        q_t = q_ref[...].T
        o_t = []
        for kh in range(NKV):
            p, _, prev_mask, _, _, _, vpb, vcb, _, _ = _attn_probs(kh, n, q_t, kvc_ref, kvp_ref, qg_ref, kg_ref,
                                                                   tri_ref, bias_ref, sink_ref)
            pp, pc = _unfold(p.astype(BF16), prev_mask)
            o_t.append(_lanes_to_heads(_dot_tn(vpb, pp) + _dot_tn(vcb, pc)))
        o_ref[...] = jnp.concatenate(o_t, axis=0).T.astype(BF16)

    return pl.pallas_call(
        body, name="attn_fwd", grid=(n_seq, nb),
        in_specs=[cur(D), cur(256), prev(256)] + consts,
        out_specs=cur(D),
        out_shape=jax.ShapeDtypeStruct((T, D), BF16),
        compiler_params=_params(("parallel", "parallel")),
    )(q, kv, kv, qg_cols, kg, tri, bias, sink_rows)


def _merge_fwd(x2, gg, h3, o, wa, wm, wc):
    T = x2.shape[0]
    tm = min(512, T)

    def body(x_ref, gg_ref, h3_ref, o_ref, wa_ref, wm_ref, wc_ref, mix_ref, h_ref):
        yc = _dot(h3_ref[...], wc_ref[...])
        ya = _dot(o_ref[...], wa_ref[...])
        mix = (_sig(gg_ref[:, 0:D]) * yc + _sig(gg_ref[:, D:2 * D]) * ya).astype(BF16)
        mix_ref[...] = mix
        h_ref[...] = x_ref[...] + _dot(mix, wm_ref[...])

    return pl.pallas_call(
        body, name="merge_fwd", grid=(T // tm,),
        in_specs=[_rows(tm, D), _rows(tm, 2 * D), _rows(tm, D), _rows(tm, D), _resident((D, D)), _resident((D, D)),
                  _resident((D, D))],
        out_specs=[_rows(tm, D), _rows(tm, D)],
        out_shape=[jax.ShapeDtypeStruct((T, D), BF16), jax.ShapeDtypeStruct((T, D), F32)],
        compiler_params=_params(("parallel",)),
    )(x2, gg, h3, o, wa, wm, wc)


FF_CHUNK = DFF // 2


def _ffn(h, tgt, g2, wf, wd):
    T = h.shape[0]
    tm = min(256, T)

    def body(h_ref, t_ref, g_ref, wf_ref, wd_ref, dh_ref, dhb_ref, hn_ref, act_ref, dout_ref, dgu_ref, st_ref,
             gsc, usc):
        @pl.when(pl.program_id(0) == 0)
        def _():
            st_ref[...] = jnp.zeros((8, D), F32)

        hh = h_ref[...]
        rstd = lax.rsqrt(jnp.mean(hh * hh, axis=-1, keepdims=True) + EPS)
        hhat = hh * rstd
        hn = (hhat * g_ref[...]).astype(BF16)
        hn_ref[...] = hn
        out = hh
        for c in range(DFF // FF_CHUNK):
            cs = slice(c * FF_CHUNK, (c + 1) * FF_CHUNK)
            us = slice(DFF + c * FF_CHUNK, DFF + (c + 1) * FF_CHUNK)
            g = _dot(hn, wf_ref[c])
            u = _dot(hn, wf_ref[2 + c])
            gsc[:, cs] = g
            usc[:, cs] = u
            act = (g * _sig(g) * u).astype(BF16)
            act_ref[:, cs] = act
            out = out + _dot(act, wd_ref[cs, :])
        err = out - t_ref[...]
        dout = err * (1.0 / D)
        doutb = dout.astype(BF16)
        dout_ref[...] = doutb
        dhn = jnp.zeros((tm, D), F32)
        for c in range(DFF // FF_CHUNK):
            cs = slice(c * FF_CHUNK, (c + 1) * FF_CHUNK)
            us = slice(DFF + c * FF_CHUNK, DFF + (c + 1) * FF_CHUNK)
            g = gsc[:, cs]
            u = usc[:, cs]
            dact = _dot_nt(doutb, wd_ref[cs, :])
            sg = _sig(g)
            dg = (dact * u * (sg * (1.0 + g * (1.0 - sg)))).astype(BF16)
            du = (dact * (g * sg)).astype(BF16)
            dgu_ref[:, cs] = dg
            dgu_ref[:, us] = du
            dhn = dhn + _dot_nt(dg, wf_ref[c]) + _dot_nt(du, wf_ref[2 + c])
        st_ref[0:1, :] += jnp.sum(dhn * hhat, axis=0, keepdims=True)
        st_ref[1:2, :] += jnp.sum(err * err, axis=0, keepdims=True)
        dhh = dhn * g_ref[...]
        dh = dout + rstd * (dhh - hhat * jnp.mean(dhh * hhat, axis=-1, keepdims=True))
        dh_ref[...] = dh
        dhb_ref[...] = dh.astype(BF16)

    return pl.pallas_call(
        body, name="ffn_fwd_bwd", grid=(T // tm,),
        in_specs=[_rows(tm, D), _rows(tm, D), _resident((1, D)), _resident((4, D, FF_CHUNK)), _resident((DFF, D))],
        out_specs=[_rows(tm, D), _rows(tm, D), _rows(tm, D), _rows(tm, DFF), _rows(tm, D), _rows(tm, 2 * DFF),
                   _whole((8, D))],
        out_shape=[jax.ShapeDtypeStruct((T, D), F32), jax.ShapeDtypeStruct((T, D), BF16),
                   jax.ShapeDtypeStruct((T, D), BF16), jax.ShapeDtypeStruct((T, DFF), BF16),
                   jax.ShapeDtypeStruct((T, D), BF16), jax.ShapeDtypeStruct((T, 2 * DFF), BF16),
                   jax.ShapeDtypeStruct((8, D), F32)],
        scratch_shapes=[pltpu.VMEM((tm, DFF), F32), pltpu.VMEM((tm, DFF), F32)],
        compiler_params=_params(("arbitrary",), 56),
    )(h, tgt, g2, wf, wd)


def _merge_bwd(dhb, gg, h3, o, wm, wa, wc):
    T = dhb.shape[0]
    tm = min(512, T)

    def body(dh_ref, gg_ref, h3_ref, o_ref, wm_ref, wa_ref, wc_ref, dgg_ref, dyc_ref, dya_ref, do_ref, dh3_ref):
        dmix = _dot_nt(dh_ref[...], wm_ref[...])
        gc = _sig(gg_ref[:, 0:D])
        ga = _sig(gg_ref[:, D:2 * D])
        yc = _dot(h3_ref[...], wc_ref[...])
        ya = _dot(o_ref[...], wa_ref[...])
        dgg_ref[:, 0:D] = (dmix * yc * gc * (1.0 - gc)).astype(BF16)
        dgg_ref[:, D:2 * D] = (dmix * ya * ga * (1.0 - ga)).astype(BF16)
        dyc = (dmix * gc).astype(BF16)
        dya = (dmix * ga).astype(BF16)
        dyc_ref[...] = dyc
        dya_ref[...] = dya
        do_ref[...] = _dot_nt(dya, wa_ref[...]).astype(BF16)
        dh3_ref[...] = _dot_nt(dyc, wc_ref[...])

    return pl.pallas_call(
        body, name="merge_bwd", grid=(T // tm,),
        in_specs=[_rows(tm, D), _rows(tm, 2 * D), _rows(tm, D), _rows(tm, D), _resident((D, D)), _resident((D, D)),
                  _resident((D, D))],
        out_specs=[_rows(tm, 2 * D), _rows(tm, D), _rows(tm, D), _rows(tm, D), _rows(tm, D)],
        out_shape=[jax.ShapeDtypeStruct((T, 2 * D), BF16), jax.ShapeDtypeStruct((T, D), BF16),
                   jax.ShapeDtypeStruct((T, D), BF16), jax.ShapeDtypeStruct((T, D), BF16),
                   jax.ShapeDtypeStruct((T, D), F32)],
        compiler_params=_params(("parallel",), 56),
    )(dhb, gg, h3, o, wm, wa, wc)


def _attn_bwd(q, kv, do, qg_cols, kg, sink_rows, n_seq, S):
    T = n_seq * S
    nb, cur, prev, consts = _attn_specs(n_seq, S)
    tri, bias = _attn_consts()

    def body(q_ref, kvc_ref, kvp_ref, do_ref, qg_ref, kg_ref, tri_ref, bias_ref, sink_ref, dq_ref, dkc_ref, dkp_ref,
             dqg_ref, dsk_ref):
        n = pl.program_id(1)

        @pl.when((pl.program_id(0) == 0) & (n == 0))
        def _():
            dqg_ref[...] = jnp.zeros((HD, BLK), F32)
            dsk_ref[...] = jnp.zeros((8, 128), F32)

        lane = lax.broadcasted_iota(jnp.int32, (1, 128), 1)
        q_t = q_ref[...].T
        do_t = do_ref[...].astype(F32).T
        dq_t = []
        for kh in range(NKV):
            p, ps, prev_mask, qhb, kpb, kcb, vpb, vcb, qy, rq = _attn_probs(
                kh, n, q_t, kvc_ref, kvp_ref, qg_ref, kg_ref, tri_ref, bias_ref, sink_ref)
            dob = _heads_to_lanes(do_t, kh).astype(BF16)
            dp = jnp.where(tri_ref[...] > 0.5, _dot(vpb, dob), _dot(vcb, dob))
            delta = jnp.sum(p * dp, axis=0, keepdims=True)
            dsp, dsc = _unfold((p * (dp - delta)).astype(BF16), prev_mask)
            pp, pc = _unfold(p.astype(BF16), prev_mask)
            dsink = -ps * delta
            dqh = (_dot_tn(kpb, dsp) + _dot_tn(kcb, dsc)) * SCALE
            dqg = dqh * qy
            dqg_ref[...] += sum(dqg[:, g * BLK:(g + 1) * BLK] for g in range(GROUP))
            dy = dqh * qg_ref[...]
            dq_t.append(_lanes_to_heads(rq * (dy - qy * jnp.mean(dy * qy, axis=0, keepdims=True))))
            row = jnp.zeros((1, 128), F32)
            for g in range(GROUP):
                h = kh * GROUP + g
                row = row + jnp.where(lane == h, jnp.sum(dsink[:, g * BLK:(g + 1) * BLK], axis=1, keepdims=True), 0.0)
            dsk_ref[0:1, :] += row
            ks = slice(kh * HD, (kh + 1) * HD)
            vs = slice(2 * HD + kh * HD, 2 * HD + (kh + 1) * HD)
            dkp_ref[:, ks] = _dot_nt(dsp, qhb)
            dkc_ref[:, ks] = _dot_nt(dsc, qhb)
            dkp_ref[:, vs] = _dot_nt(pp, dob)
            dkc_ref[:, vs] = _dot_nt(pc, dob)
        dq_ref[...] = jnp.concatenate(dq_t, axis=0).T.astype(BF16)

    return pl.pallas_call(
        body, name="attn_bwd", grid=(n_seq, nb),
        in_specs=[cur(D), cur(256), prev(256), cur(D)] + consts,
        out_specs=[cur(D), cur(256), cur(256), _whole((HD, BLK)), _whole((8, 128))],
        out_shape=[jax.ShapeDtypeStruct((T, D), BF16), jax.ShapeDtypeStruct((T, 256), F32),
                   jax.ShapeDtypeStruct((T, 256), F32), jax.ShapeDtypeStruct((HD, BLK), F32),
                   jax.ShapeDtypeStruct((8, 128), F32)],
        compiler_params=_params(("arbitrary", "arbitrary")),
    )(q, kv, kv, do, qg_cols, kg, tri, bias, sink_rows)


def _kv_bwd(kv, dkc, dkp, kg_cols, n_seq, S):
    T = n_seq * S
    seq = lambda cols: pl.BlockSpec((S, cols), lambda b: (b, 0))

    def body(kv_ref, dkc_ref, dkp_ref, kg_ref, dkv_ref, dkg_ref):
        @pl.when(pl.program_id(0) == 0)
        def _():
            dkg_ref[...] = jnp.zeros((HD, BLK), F32)

        from_next = jnp.concatenate([dkp_ref[BLK:S, :], jnp.zeros((BLK, 256), F32)], axis=0)
        d = dkc_ref[...] + from_next
        d_t = d[:, 0:2 * HD].T
        k_t = kv_ref[:, 0:2 * HD].T
        kg = jnp.concatenate([kg_ref[...]] * (S // BLK), axis=1)
        out = []
        for kh in range(NKV):
            k = k_t[kh * HD:(kh + 1) * HD, :]
            r = lax.rsqrt(jnp.mean(k * k, axis=0, keepdims=True) + EPS)
            y = k * r
            dkh = d_t[kh * HD:(kh + 1) * HD, :]
            dkg = dkh * y
            dkg_ref[...] += sum(dkg[:, j * BLK:(j + 1) * BLK] for j in range(S // BLK))
            dy = dkh * kg
            out.append(r * (dy - y * jnp.mean(dy * y, axis=0, keepdims=True)))
        dkv_ref[:, 0:2 * HD] = jnp.concatenate(out, axis=0).T.astype(BF16)
        dkv_ref[:, 2 * HD:4 * HD] = d[:, 2 * HD:4 * HD].astype(BF16)

    return pl.pallas_call(
        body, name="kv_bwd", grid=(n_seq,),
        in_specs=[seq(256), seq(256), seq(256), _resident((HD, BLK))],
        out_specs=[seq(256), _whole((HD, BLK))],
        out_shape=[jax.ShapeDtypeStruct((T, 256), BF16), jax.ShapeDtypeStruct((HD, BLK), F32)],
        compiler_params=_params(("arbitrary",)),
    )(kv, dkc, dkp, kg_cols)


def _conv_bwd(h1, dh3, ag, dw, lng, lnb, n_seq, S):
    T = n_seq * S
    tc = min(256, S)
    nt = S // tc

    def body(h1_ref, dh3_ref, a_ref, gt_ref, dw_ref, lng_ref, lnb_ref, dag_ref, ddw_ref, st_ref, extd, acc8, shd):
        i = pl.program_id(1)

        @pl.when((pl.program_id(0) == 0) & (i == 0))
        def _():
            acc8[...] = jnp.zeros((CW * 8, D), F32)
            st_ref[...] = jnp.zeros((8, D), F32)

        @pl.when(i == 0)
        def _():
            extd[tc:tc + 32, :] = jnp.zeros((32, D), F32)

        h1 = h1_ref[...]
        mu = jnp.mean(h1, axis=-1, keepdims=True)
        cen = h1 - mu
        rstd = lax.rsqrt(jnp.mean(cen * cen, axis=-1, keepdims=True) + EPS)
        xh = cen * rstd
        h2 = xh * lng_ref[...] + lnb_ref[...]
        sg = _sig(h2)
        dh2 = dh3_ref[...] * (sg * (1.0 + h2 * (1.0 - sg)))
        st_ref[1:2, :] += jnp.sum(dh2 * xh, axis=0, keepdims=True)
        st_ref[2:3, :] += jnp.sum(dh2, axis=0, keepdims=True)
        dxh = dh2 * lng_ref[...]
        dh1 = rstd * (dxh - jnp.mean(dxh, axis=-1, keepdims=True)
                      - xh * jnp.mean(dxh * xh, axis=-1, keepdims=True))
        st_ref[0:1, :] += jnp.sum(dh1, axis=0, keepdims=True)
        extd[0:tc, :] = dh1
        _shift_copies(shd, extd, 0)
        for cb in range(D // 128):
            cs = slice(cb * 128, (cb + 1) * 128)
            for rb in range(tc // 128):
                rs = slice(rb * 128, (rb + 1) * 128)
                a = a_ref[rs, cs]
                sgt = _sig(gt_ref[rs, cs])
                h0 = a * sgt
                acc = jnp.zeros((128, 128), F32)
                for phase, offs in _tap_phases():
                    for m, o in enumerate(offs):
                        j = CW - 1 - o
                        ahead = shd[phase, rb * 128 + 8 * m:rb * 128 + 8 * m + 128, cs]
                        acc = acc + dw_ref[j:j + 1, cs] * ahead
                        acc8[j * 8:(j + 1) * 8, cs] += jnp.sum((h0 * ahead).reshape(16, 8, 128), axis=0)
                dag_ref[rs, cs] = (acc * sgt).astype(BF16)
                dag_ref[rs, cb * 128 + D:(cb + 1) * 128 + D] = (acc * a * sgt * (1.0 - sgt)).astype(BF16)
        extd[tc:tc + 32, :] = extd[0:32, :]

        @pl.when((pl.program_id(0) == n_seq - 1) & (i == nt - 1))
        def _():
            for j in range(CW):
                ddw_ref[j:j + 1, :] = jnp.sum(acc8[j * 8:(j + 1) * 8, :], axis=0, keepdims=True)
            ddw_ref[CW:32, :] = jnp.zeros((32 - CW, D), F32)

    tile = lambda col: pl.BlockSpec((tc, D), lambda b, i: (b * nt + (nt - 1 - i), col))
    return pl.pallas_call(
        body, name="conv_bwd", grid=(n_seq, nt),
        in_specs=[tile(0), tile(0), tile(0), tile(1), _resident((32, D)), _resident((1, D)), _resident((1, D))],
        out_specs=[pl.BlockSpec((tc, 2 * D), lambda b, i: (b * nt + (nt - 1 - i), 0)), _whole((32, D)),
                   _whole((8, D))],
        out_shape=[jax.ShapeDtypeStruct((T, 2 * D), BF16), jax.ShapeDtypeStruct((32, D), F32),
                   jax.ShapeDtypeStruct((8, D), F32)],
        scratch_shapes=[pltpu.VMEM((tc + 32, D), F32), pltpu.VMEM((CW * 8, D), F32),
                        pltpu.VMEM((8, tc + 24, D), F32)],
        compiler_params=_params(("arbitrary", "arbitrary")),
    )(h1, dh3, ag, ag, dw, lng, lnb)


def _in_proj_bwd(dag, dq, dkv, dgg, dh, x2, g1, w_in):
    T = x2.shape[0]
    tm = min(512, T)

    def body(dag_ref, dq_ref, dkv_ref, dgg_ref, dh_ref, x_ref, g_ref, w_ref, dx_ref, dg_ref):
        @pl.when(pl.program_id(0) == 0)
        def _():
            dg_ref[...] = jnp.zeros((1, D), F32)

        dxn = (_dot(dag_ref[...], w_ref[U_CONV, :]) + _dot(dq_ref[...], w_ref[U_Q, :])
               + _dot(dkv_ref[...], w_ref[U_KV, :]) + _dot(dgg_ref[...], w_ref[U_GATES, :]))
        x = x_ref[...]
        rstd = lax.rsqrt(jnp.mean(x * x, axis=-1, keepdims=True) + EPS)
        xh = x * rstd
        dg_ref[...] += jnp.sum(dxn * xh, axis=0, keepdims=True)
        dxh = dxn * g_ref[...]
        dx_ref[...] = dh_ref[...] + rstd * (dxh - xh * jnp.mean(dxh * xh, axis=-1, keepdims=True))

    return pl.pallas_call(
        body, name="in_proj_bwd", grid=(T // tm,),
        in_specs=[_rows(tm, 2 * D), _rows(tm, D), _rows(tm, 256), _rows(tm, 2 * D), _rows(tm, D), _rows(tm, D),
                  _resident((1, D)), _resident((IN_COLS, D))],
        out_specs=[_rows(tm, D), _whole((1, D))],
        out_shape=[jax.ShapeDtypeStruct((T, D), F32), jax.ShapeDtypeStruct((1, D), F32)],
        compiler_params=_params(("arbitrary",)),
    )(dag, dq, dkv, dgg, dh, x2, g1, w_in)


def _tn_matmul(a, b, name, column_blocks=False):
    T, K = a.shape
    N = b.shape[1]
    tk = K if K <= 1024 else K // 2
    tn = N if N <= 1024 else (1024 if N % 1024 == 0 and not column_blocks else N // 4)
    tt = min(2048, T)
    assert K % tk == 0 and N % tn == 0 and T % tt == 0 and tk % 128 == 0 and tn % 128 == 0

    def body(a_ref, b_ref, o_ref):
        @pl.when(pl.program_id(2) == 0)
        def _():
            o_ref[...] = jnp.zeros((tk, tn), F32)

        o_ref[...] += _dot_tn(a_ref[...], b_ref[...])

    return pl.pallas_call(
        body, name=name, grid=(K // tk, N // tn, T // tt),
        in_specs=[pl.BlockSpec((tt, tk), lambda i, j, t: (t, i)), pl.BlockSpec((tt, tn), lambda i, j, t: (t, j))],
        out_specs=(pl.BlockSpec((None, tk, tn), lambda i, j, t: (j, i, 0)) if column_blocks
                   else pl.BlockSpec((tk, tn), lambda i, j, t: (i, j))),
        out_shape=jax.ShapeDtypeStruct((N // tn, K, tn) if column_blocks else (K, N), F32),
        compiler_params=_params(("parallel", "parallel", "arbitrary")),
    )(a, b)


def _place():
    x, y, c = lax.axis_index("x"), lax.axis_index("y"), lax.axis_index("c")
    chips = [(1 - x, y), (x, 1 - y), (1 - x, 1 - y)]
    return x, y, c, chips


def _own_slot(slots, mine):
    chip = 2 * lax.axis_index("x") + lax.axis_index("y")
    return lax.dynamic_update_slice(slots, mine[None], (chip,) + (0,) * mine.ndim)


def _row_tile(rows, unit):
    return max(t for t in range(unit, 513, unit) if rows % t == 0)


def _gather_weights(packs):
    n = len(packs)
    halves = [p.shape[0] // 2 for p in packs]

    def body(*refs):
        srcs, dsts, token, send_sems, recv_sems = refs[:n], refs[n:2 * n], refs[2 * n], refs[2 * n + 1], refs[2 * n + 2]
        x, y, c, chips = _place()

        def piece(b, px, py, pc):
            return dsts[b].at[2 * px + py, pl.ds(pc * halves[b], halves[b]), :]

        def copy(b, k, block, to, from_src=False):
            mine = srcs[b].at[pl.ds(c * halves[b], halves[b]), :]
            return pltpu.make_async_remote_copy(
                src_ref=mine if from_src else piece(b, *block), dst_ref=piece(b, *block),
                send_sem=send_sems.at[6 * b + k], recv_sem=recv_sems.at[6 * b + k], device_id=to,
                device_id_type=MESH)

        first = [copy(b, k, (x, y, c), (*chip, c), from_src=True) for b in range(n) for k, chip in enumerate(chips)]
        for cp in first:
            cp.start()
        passed = []
        for b in range(n):
            for k, chip in enumerate(chips):
                copy(b, k, (*chip, c), (x, y, c)).wait_recv()
                passed.append(copy(b, 3 + k, (*chip, c), (x, y, 1 - c)))
                passed[-1].start()
        for b in range(n):
            for k, chip in enumerate(chips):
                copy(b, 3 + k, (*chip, 1 - c), (x, y, c)).wait_recv()
        for cp in first + passed:
            cp.wait_send()
        token[...] = jnp.zeros((8, 128), F32)

    outs = pl.pallas_call(
        body, name="gather_weights",
        in_specs=[pl.BlockSpec(memory_space=pl.ANY)] * n,
        out_specs=[pl.BlockSpec(memory_space=pl.ANY)] * n + [pl.BlockSpec(memory_space=pltpu.VMEM)],
        out_shape=[jax.ShapeDtypeStruct((4,) + p.shape, p.dtype) for p in packs]
        + [jax.ShapeDtypeStruct((8, 128), F32)],
        scratch_shapes=[pltpu.SemaphoreType.DMA((6 * n,)), pltpu.SemaphoreType.DMA((6 * n,))],
        compiler_params=pltpu.CompilerParams(has_side_effects=True),
    )(*packs)
    return [_own_slot(got, p) for got, p in zip(outs[:n], packs)], outs[n][0, 0]


def _add_halves(g, got, c_idx, name="grad_add_halves"):
    rows, w = g.shape[1], g.shape[2]
    half = rows // 2
    tr = _row_tile(half, 16)
    nt = half // tr

    def body(c_ref, g_ref, r_ref, o_ref):
        o_ref[...] = (g_ref[...] + r_ref[...]).astype(BF16)

    return pl.pallas_call(
        body, name=name,
        grid_spec=pltpu.PrefetchScalarGridSpec(
            num_scalar_prefetch=1, grid=(4, nt),
            in_specs=[pl.BlockSpec((1, tr, w), lambda q, i, c_ref: (q, c_ref[0] * nt + i, 0)),
                      pl.BlockSpec((1, tr, w), lambda q, i, c_ref: (q, i, 0))],
            out_specs=pl.BlockSpec((1, tr, w), lambda q, i, c_ref: (q, i, 0))),
        out_shape=jax.ShapeDtypeStruct((4, half, w), BF16),
        compiler_params=_params(("parallel", "parallel")),
    )(c_idx, g, got)


def _own_piece(p):
    chip = 2 * lax.axis_index("x") + lax.axis_index("y")
    return lax.dynamic_index_in_dim(p, chip, axis=0, keepdims=False)


def _sum_chips(r, c_idx, name="grad_sum_chips"):
    half, w = r.shape[1], r.shape[2]
    tr = _row_tile(half, 16)
    nt = half // tr

    def body(c_ref, r_ref, o_ref):
        acc = r_ref[0].astype(F32)
        for q in range(1, 4):
            acc = acc + r_ref[q].astype(F32)
        o_ref[...] = acc

    return pl.pallas_call(
        body, name=name,
        grid_spec=pltpu.PrefetchScalarGridSpec(
            num_scalar_prefetch=1, grid=(nt,),
            in_specs=[pl.BlockSpec((4, tr, w), lambda i, c_ref: (0, i, 0))],
            out_specs=pl.BlockSpec((tr, w), lambda i, c_ref: (c_ref[0] * nt + i, 0))),
        out_shape=jax.ShapeDtypeStruct((2 * half, w), F32),
        compiler_params=_params(("parallel",)),
    )(c_idx, r)


def _join_halves(f):
    half = f.shape[0] // 2

    def body(src, dst, send_sem, recv_sem):
        x, y, c, _ = _place()
        cp = pltpu.make_async_remote_copy(
            src_ref=src.at[pl.ds(c * half, half), :], dst_ref=dst.at[pl.ds(c * half, half), :], send_sem=send_sem,
            recv_sem=recv_sem, device_id=(x, y, 1 - c), device_id_type=MESH)
        cp.start()
        pltpu.make_async_remote_copy(
            src_ref=src.at[pl.ds(c * half, half), :], dst_ref=dst.at[pl.ds((1 - c) * half, half), :],
            send_sem=send_sem, recv_sem=recv_sem, device_id=(x, y, 1 - c), device_id_type=MESH).wait_recv()
        cp.wait_send()

    return pl.pallas_call(
        body, name="grad_join_halves",
        in_specs=[pl.BlockSpec(memory_space=pl.ANY)], out_specs=pl.BlockSpec(memory_space=pl.ANY),
        out_shape=jax.ShapeDtypeStruct(f.shape, f.dtype), input_output_aliases={0: 0},
        scratch_shapes=[pltpu.SemaphoreType.DMA, pltpu.SemaphoreType.DMA],
        compiler_params=pltpu.CompilerParams(has_side_effects=True),
    )(f)


_HBM = pl.BlockSpec(memory_space=pltpu.HBM)
_SEM = pl.BlockSpec(memory_space=pltpu.SEMAPHORE)
_EFFECT = pltpu.SideEffectType.DATAFLOW_SIDE_EFFECTING


def _start_copies(name, bufs, n_sems, plan):
    nb = len(bufs)

    def body(*refs):
        for cp in plan(refs[:nb], refs[nb], refs[nb + 1])[0]:
            cp.start()
        refs[-1][...] = jnp.zeros((8, 128), F32)

    out = pl.pallas_call(
        body, name=name,
        out_shape=(pltpu.SemaphoreType.DMA((n_sems,)), pltpu.SemaphoreType.DMA((n_sems,)),
                   *[pltpu.HBM(b.shape, b.dtype) for b in bufs], jax.ShapeDtypeStruct((8, 128), F32)),
        in_specs=[_HBM] * nb, out_specs=(_SEM, _SEM, *[_HBM] * nb, pl.BlockSpec(memory_space=pltpu.VMEM)),
        input_output_aliases={i: 2 + i for i in range(nb)},
        compiler_params=pltpu.CompilerParams(has_side_effects=_EFFECT),
    )(*[pltpu.with_memory_space_constraint(b, pltpu.HBM) for b in bufs])
    return out[0], out[1], list(out[2:2 + nb]), out[-1]


def _wait_copies(name, send_sems, recv_sems, bufs, after, plan):
    nb = len(bufs)

    def body(*refs):
        _, sends, recvs = plan(refs[:nb], refs[nb], refs[nb + 1])
        for cp in sends:
            cp.wait_send()
        for cp in recvs:
            cp.wait_recv()

    out = pl.pallas_call(
        body, name=name,
        out_shape=tuple(pltpu.HBM(b.shape, b.dtype) for b in bufs),
        in_specs=[_HBM] * nb + [_SEM, _SEM] + [pl.BlockSpec(memory_space=pl.ANY)] * len(after),
        out_specs=tuple([_HBM] * nb),
        input_output_aliases={i: i for i in range(nb)},
        compiler_params=pltpu.CompilerParams(has_side_effects=_EFFECT),
    )(*bufs, send_sems, recv_sems, *after)
    return list(out)


def _plan_gather_direct(halves):
    n = len(halves)

    def plan(refs, send_sems, recv_sems):
        x, y, c, chips = _place()
        starts, recvs = [], []
        for b, half in enumerate(halves):
            src, land = refs[b], refs[n + b]
            for k, (cx, cy) in enumerate(chips):
                for d in range(2):
                    other = c if d == 0 else 1 - c
                    i = 6 * b + 2 * k + d
                    starts.append(pltpu.make_async_remote_copy(
                        src_ref=src.at[pl.ds(c * half, half), :],
                        dst_ref=land.at[2 * x + y, pl.ds(c * half, half), :],
                        send_sem=send_sems.at[i], recv_sem=recv_sems.at[i], device_id=(cx, cy, other),
                        device_id_type=MESH))
                    recvs.append(pltpu.make_async_remote_copy(
                        src_ref=src.at[pl.ds(c * half, half), :],
                        dst_ref=land.at[2 * cx + cy, pl.ds(other * half, half), :],
                        send_sem=send_sems.at[i], recv_sem=recv_sems.at[i], device_id=(cx, cy, other),
                        device_id_type=MESH))
        return starts, starts, recvs
    return plan


def _plan_swap_halves(halves):
    n = len(halves)

    def plan(refs, send_sems, recv_sems):
        x, y, c, _ = _place()
        cps = [pltpu.make_async_remote_copy(
            src_ref=refs[b].at[:, pl.ds((1 - c) * half, half), :], dst_ref=refs[n + b], send_sem=send_sems.at[b],
            recv_sem=recv_sems.at[b], device_id=(x, y, 1 - c), device_id_type=MESH)
            for b, half in enumerate(halves)]
        return cps, cps, cps
    return plan


def _plan_scatter_chips(n):
    def plan(refs, send_sems, recv_sems):
        x, y, c, chips = _place()
        me = 2 * x + y
        starts, recvs = [], []
        for b in range(n):
            src, land = refs[b], refs[n + b]
            for k, (cx, cy) in enumerate(chips):
                i = 3 * b + k
                starts.append(pltpu.make_async_remote_copy(
                    src_ref=src.at[2 * cx + cy], dst_ref=land.at[me], send_sem=send_sems.at[i],
                    recv_sem=recv_sems.at[i], device_id=(cx, cy, c), device_id_type=MESH))
                recvs.append(pltpu.make_async_remote_copy(
                    src_ref=src.at[me], dst_ref=land.at[2 * cx + cy], send_sem=send_sems.at[i],
                    recv_sem=recv_sems.at[i], device_id=(cx, cy, c), device_id_type=MESH))
        return starts, starts, recvs
    return plan


def _plan_join_halves(halves):
    def plan(refs, send_sems, recv_sems):
        x, y, c, _ = _place()
        starts, recvs = [], []
        for b, half in enumerate(halves):
            mine, theirs = refs[b].at[pl.ds(c * half, half), :], refs[b].at[pl.ds((1 - c) * half, half), :]
            starts.append(pltpu.make_async_remote_copy(
                src_ref=mine, dst_ref=mine, send_sem=send_sems.at[b], recv_sem=recv_sems.at[b],
                device_id=(x, y, 1 - c), device_id_type=MESH))
            recvs.append(pltpu.make_async_remote_copy(
                src_ref=mine, dst_ref=theirs, send_sem=send_sems.at[b], recv_sem=recv_sems.at[b],
                device_id=(x, y, 1 - c), device_id_type=MESH))
        return starts, starts, recvs
    return plan


def _allreduce_small(vec):
    def body(v_ref, o_ref, gath, send_sems, recv_sems):
        x, y, c, _ = _place()
        me = 4 * x + 2 * y + c
        gath[me] = v_ref[...]
        sends = []
        for k in range(1, 8):
            peer = (x ^ (k >> 2), y ^ ((k >> 1) & 1), c ^ (k & 1))
            sends.append(pltpu.make_async_remote_copy(
                src_ref=v_ref, dst_ref=gath.at[me], send_sem=send_sems.at[k - 1], recv_sem=recv_sems.at[k - 1],
                device_id=peer, device_id_type=MESH))
        for cp in sends:
            cp.start()
        for k in range(1, 8):
            peer = (x ^ (k >> 2), y ^ ((k >> 1) & 1), c ^ (k & 1))
            pltpu.make_async_remote_copy(
                src_ref=v_ref, dst_ref=gath.at[4 * peer[0] + 2 * peer[1] + peer[2]], send_sem=send_sems.at[k - 1],
                recv_sem=recv_sems.at[k - 1], device_id=peer, device_id_type=MESH).wait_recv()
        for cp in sends:
            cp.wait_send()
        acc = gath[0]
        for d in range(1, 8):
            acc = acc + gath[d]
        o_ref[...] = acc

    return pl.pallas_call(
        body, name="allreduce_small",
        in_specs=[pl.BlockSpec(memory_space=pltpu.VMEM)], out_specs=pl.BlockSpec(memory_space=pltpu.VMEM),
        out_shape=jax.ShapeDtypeStruct(vec.shape, F32),
        scratch_shapes=[pltpu.VMEM((8,) + vec.shape, F32), pltpu.SemaphoreType.DMA((7,)),
                        pltpu.SemaphoreType.DMA((7,))],
    )(vec)


def _adamw(w, g, m, v, name, after):
    shape = w.shape
    if w.ndim == 1 or w.size <= 128 * 128:
        two_d = (1, w.size) if w.size % 128 else (w.size // 128, 128)
    else:
        two_d = (w.shape[0], w.size // w.shape[0])
    rows, cols = two_d
    tr = _row_tile(rows, 8) if rows % 8 == 0 and rows > 512 else rows

    def body(w_ref, g_ref, m_ref, v_ref, after_ref, d_ref, nm_ref, nv_ref):
        gr = g_ref[...]
        nm = B1 * m_ref[...] + (1.0 - B1) * gr
        nv = B2 * v_ref[...] + (1.0 - B2) * (gr * gr)
        m_hat = nm / (1.0 - B1 ** STEP)
        v_hat = nv / (1.0 - B2 ** STEP)
        d_ref[...] = -LR * (m_hat / (jnp.sqrt(v_hat) + AEPS) + WD * w_ref[...])
        nm_ref[...] = nm
        nv_ref[...] = nv

    spec = pl.BlockSpec((tr, cols), lambda i: (i, 0))
    outs = pl.pallas_call(
        body, name=name, grid=(rows // tr,),
        in_specs=[spec] * 4 + [pl.BlockSpec(memory_space=pl.ANY)], out_specs=[spec] * 3,
        out_shape=[jax.ShapeDtypeStruct(two_d, F32)] * 3,
        compiler_params=_params(("parallel",)),
    )(*[t.reshape(two_d) for t in (w, g, m, v)], after)
    return [o.reshape(shape) for o in outs]


def _rows_stacked(g):
    return g.reshape(4 * g.shape[1], D)


def _rows_to_slots(t):
    return t.reshape(4, t.shape[0] // 4, D)


def _pack_first(w_in, conv_dw_w):
    dw = jnp.pad(conv_dw_w.reshape(CW, 256), ((0, 1), (0, 0)))
    dw_bits = lax.bitcast_convert_type(dw, BF16).reshape(16, D)
    return [w_in.T.astype(BF16), jnp.pad(dw_bits, ((0, 16), (0, 0)))]


def _unpack_first(w_in_slots, dw_slots):
    dw = lax.bitcast_convert_type(dw_slots[:, 0:16].reshape(4, 32, 256, 2), F32)
    return _rows_stacked(w_in_slots), jnp.transpose(dw, (1, 0, 2)).reshape(32, D)


class _Exchanges:
    def __init__(self, late_shards):
        self.c_idx = lax.axis_index("c").astype(jnp.int32).reshape(1)
        packs = list(late_shards)
        self.late_plan = _plan_gather_direct([p.shape[0] // 2 for p in packs])
        slots = [lax.empty((4,) + p.shape, BF16) for p in packs]
        self.late = _start_copies("gather_late_start", packs + slots, 6 * len(packs), self.late_plan)
        self.first_token = self.late[3][0, 0]

    def late_weights(self, after):
        send_sems, recv_sems, bufs, _ = self.late
        bufs = _wait_copies("gather_late_wait", send_sems, recv_sems, bufs, after, self.late_plan)
        wa, wm, wf, wd, wc = [_own_slot(bufs[5 + b], bufs[b]) for b in range(5)]
        return _rows_stacked(wa), _rows_stacked(wm), wf, _rows_stacked(wd), _rows_stacked(wc)

    def reduce_start(self, d_wd, d_wf4, d_wm, d_wa, d_wc):
        gs = [_rows_to_slots(d_wd), jnp.concatenate([_rows_to_slots(t) for t in (d_wm, d_wa, d_wc)], axis=1), d_wf4]
        self.halves = [g.shape[1] // 2 for g in gs]
        self.swap_plan = _plan_swap_halves(self.halves)
        lands = [lax.empty((4, h, g.shape[2]), F32) for g, h in zip(gs, self.halves)]
        self.swap = _start_copies("grad_swap_start", gs + lands, len(gs), self.swap_plan)
        return self.swap[3][0, 0]

    def reduce_mid(self, after):
        send_sems, recv_sems, bufs, _ = self.swap
        bufs = _wait_copies("grad_swap_wait", send_sems, recv_sems, bufs, after, self.swap_plan)
        n = len(self.halves)
        ps = [_add_halves(bufs[b], bufs[n + b], self.c_idx, "grad_add_halves_%d" % b) for b in range(n)]
        self.scatter_plan = _plan_scatter_chips(n)
        self.scatter = _start_copies("grad_scatter_start", ps + [lax.empty(p.shape, BF16) for p in ps], 3 * n,
                                     self.scatter_plan)
        return self.scatter[3][0, 0]

    def reduce_late(self, after):
        send_sems, recv_sems, bufs, _ = self.scatter
        bufs = _wait_copies("grad_scatter_wait", send_sems, recv_sems, bufs, after, self.scatter_plan)
        n = len(self.halves)
        fs = [_sum_chips(_own_slot(bufs[n + b], _own_piece(bufs[b])), self.c_idx, "grad_sum_chips_%d" % b)
              for b in range(n)]
        self.join_plan = _plan_join_halves(self.halves)
        self.join = _start_copies("grad_join_start", fs, n, self.join_plan)

    def reduce_end(self, after):
        send_sems, recv_sems, bufs, _ = self.join
        g_wd, sq, g_wf = _wait_copies("grad_join_wait", send_sems, recv_sems, bufs, after, self.join_plan)
        return g_wd, g_wf, sq[0:ROWS_SQ], sq[ROWS_SQ:2 * ROWS_SQ], sq[2 * ROWS_SQ:3 * ROWS_SQ]

    def w_in_start(self, d_w_in_t):
        g = _rows_to_slots(d_w_in_t)
        self.w_half = g.shape[1] // 2
        self.w_swap_plan = _plan_swap_halves([self.w_half])
        self.w_swap = _start_copies("grad_w_in_swap_start", [g, lax.empty((4, self.w_half, D), F32)], 1,
                                    self.w_swap_plan)
        return self.w_swap[3]

    def w_in_mid(self, after):
        send_sems, recv_sems, bufs, _ = self.w_swap
        g, got = _wait_copies("grad_w_in_swap_wait", send_sems, recv_sems, bufs, after, self.w_swap_plan)
        p = _add_halves(g, got, self.c_idx, "grad_add_halves_w_in")
        self.w_scatter_plan = _plan_scatter_chips(1)
        self.w_scatter = _start_copies("grad_w_in_scatter_start", [p, lax.empty(p.shape, BF16)], 3,
                                       self.w_scatter_plan)
        return self.w_scatter[3]

    def w_in_end(self, after):
        send_sems, recv_sems, bufs, _ = self.w_scatter
        p, got = _wait_copies("grad_w_in_scatter_wait", send_sems, recv_sems, bufs, after, self.w_scatter_plan)
        return _join_halves(_sum_chips(_own_slot(got, _own_piece(p)), self.c_idx, "grad_sum_chips_w_in"))


def _local_grads(x, loss_target, norm_mix_g, conv_dw_b, conv_ln_g, conv_ln_b, q_norm_g, k_norm_g, sinks, norm_ffn_g,
                 w_in, dw, exchanges):
    n_seq, S, _ = x.shape
    T = n_seq * S
    x2 = x.reshape(T, D)
    tgt = loss_target.reshape(T, D)
    row = lambda t: t.reshape(1, -1)
    g1, g2 = row(norm_mix_g), row(norm_ffn_g)
    qg, kg = jnp.broadcast_to(q_norm_g.reshape(HD, 1), (HD, GROUP * BLK)), row(k_norm_g)
    lng, lnb, dwb = row(conv_ln_g), row(conv_ln_b), row(conv_dw_b)
    sink_rows = jnp.repeat(sinks.reshape(NKV, GROUP), BLK, axis=1)

    xn, ag, q, kv, gg, h1, h3 = _in_proj_conv_fwd(x2, g1 + exchanges.first_token, w_in, dw, dwb, lng, lnb, n_seq, S)
    o = _attn_fwd(q, kv, qg, kg, sink_rows, n_seq, S)
    wa, wm, wf, wd, wc = exchanges.late_weights([o, h3])
    mix, h = _merge_fwd(x2, gg, h3, o, wa, wm, wc)
    dh, dhb, hn, act, dout, dgu, ffn_stats = _ffn(h, tgt, g2, wf, wd)
    d_wd = _tn_matmul(act, dout, "dw_ffn_down")
    d_wf = _tn_matmul(hn, dgu, "dw_ffn_in", column_blocks=True)
    d_wm = _tn_matmul(mix, dhb, "dw_merge")
    dgg, dyc, dya, do, dh3 = _merge_bwd(dhb, gg, h3, o, wm, wa, wc)
    d_wa = _tn_matmul(o, dya, "dw_attn_out")
    d_wc = _tn_matmul(h3, dyc, "dw_conv_out")
    token = exchanges.reduce_start(d_wd, d_wf, d_wm, d_wa, d_wc)
    dq, dkc, dkp, dqg, dsk = _attn_bwd(q, kv, do, qg + token, kg, sink_rows, n_seq, S)
    token = exchanges.reduce_mid([dq])
    dkv, dkg = _kv_bwd(kv, dkc, dkp, jnp.broadcast_to(k_norm_g.reshape(HD, 1), (HD, BLK)), n_seq, S)
    dag, ddw, conv_stats = _conv_bwd(h1, dh3, ag, dw, lng + token, lnb, n_seq, S)
    dx, dg1 = _in_proj_bwd(dag, dq, dkv, dgg, dh, x2, g1, w_in)
    exchanges.reduce_late([dx])
    d_w_in = jnp.concatenate([_tn_matmul(dag, xn, "dw_in_conv"), _tn_matmul(dq, xn, "dw_in_q"),
                              _tn_matmul(dkv, xn, "dw_in_kv"), _tn_matmul(dgg, xn, "dw_in_gates")], axis=0)

    heads = jnp.concatenate([jnp.sum(dqg, axis=1), jnp.sum(dkg, axis=1), dsk[0, :NQ],
                             jnp.zeros((D - 2 * HD - NQ,), F32)])
    vec = jnp.concatenate([dg1, conv_stats[0:3], ffn_stats[0:1], heads[None], ffn_stats[1:2], jnp.zeros((1, D), F32),
                           ddw], axis=0)
    return ffn_stats[1], dx.reshape(x.shape), d_w_in, vec


def kernel(x, norm_mix_g, w_in, conv_dw_w, conv_dw_b, conv_ln_g, conv_ln_b, w_conv_out, q_norm_g, k_norm_g, sinks, w_attn_out, w_merge_out, norm_ffn_g, w_ffn_in, w_ffn_down, loss_target, m_norm_mix_g, m_w_in, m_conv_dw_w, m_conv_dw_b, m_conv_ln_g, m_conv_ln_b, m_w_conv_out, m_q_norm_g, m_k_norm_g, m_sinks, m_w_attn_out, m_w_merge_out, m_norm_ffn_g, m_w_ffn_in, m_w_ffn_down, v_norm_mix_g, v_w_in, v_conv_dw_w, v_conv_dw_b, v_conv_ln_g, v_conv_ln_b, v_w_conv_out, v_q_norm_g, v_k_norm_g, v_sinks, v_w_attn_out, v_w_merge_out, v_norm_ffn_g, v_w_ffn_in, v_w_ffn_down):
    chip = 2 * lax.axis_index("x") + lax.axis_index("y")

    first, token = _gather_weights(_pack_first(w_in, conv_dw_w))
    exchanges = _Exchanges([t.astype(BF16) + token.astype(BF16)
                            for t in (w_attn_out, w_merge_out, w_ffn_in, w_ffn_down, w_conv_out)])
    _, grad_x, d_w_in, vec = _local_grads(x, loss_target, norm_mix_g, conv_dw_b, conv_ln_g, conv_ln_b, q_norm_g,
                                          k_norm_g, sinks, norm_ffn_g, *_unpack_first(*first), exchanges)

    g_wd, g_wf, g_wm, g_wa, g_wc = exchanges.reduce_end([d_w_in])
    w_in_token = exchanges.w_in_start(d_w_in)
    small = _allreduce_small(vec + w_in_token[0, 0])
    loss = 0.5 / D * jnp.sum(small[6])
    g_dw = lax.dynamic_slice_in_dim(small[8:8 + CW], chip * 256, 256, axis=1).reshape(CW, 1, 256)
    grads = {
        "norm_mix_g": small[0], "conv_dw_w": g_dw, "conv_dw_b": small[1], "conv_ln_g": small[2],
        "conv_ln_b": small[3], "w_conv_out": g_wc, "q_norm_g": small[5, 0:HD], "k_norm_g": small[5, HD:2 * HD],
        "sinks": small[5, 2 * HD:2 * HD + NQ], "w_attn_out": g_wa, "w_merge_out": g_wm, "norm_ffn_g": small[4],
        "w_ffn_in": g_wf, "w_ffn_down": g_wd,
    }
    weights = dict(norm_mix_g=norm_mix_g, w_in=w_in, conv_dw_w=conv_dw_w, conv_dw_b=conv_dw_b, conv_ln_g=conv_ln_g,
                   conv_ln_b=conv_ln_b, w_conv_out=w_conv_out, q_norm_g=q_norm_g, k_norm_g=k_norm_g, sinks=sinks,
                   w_attn_out=w_attn_out, w_merge_out=w_merge_out, norm_ffn_g=norm_ffn_g, w_ffn_in=w_ffn_in,
                   w_ffn_down=w_ffn_down)
    m_in = dict(norm_mix_g=m_norm_mix_g, w_in=m_w_in, conv_dw_w=m_conv_dw_w, conv_dw_b=m_conv_dw_b,
                conv_ln_g=m_conv_ln_g, conv_ln_b=m_conv_ln_b, w_conv_out=m_w_conv_out, q_norm_g=m_q_norm_g,
                k_norm_g=m_k_norm_g, sinks=m_sinks, w_attn_out=m_w_attn_out, w_merge_out=m_w_merge_out,
                norm_ffn_g=m_norm_ffn_g, w_ffn_in=m_w_ffn_in, w_ffn_down=m_w_ffn_down)
    v_in = dict(norm_mix_g=v_norm_mix_g, w_in=v_w_in, conv_dw_w=v_conv_dw_w, conv_dw_b=v_conv_dw_b,
                conv_ln_g=v_conv_ln_g, conv_ln_b=v_conv_ln_b, w_conv_out=v_w_conv_out, q_norm_g=v_q_norm_g,
                k_norm_g=v_k_norm_g, sinks=v_sinks, w_attn_out=v_w_attn_out, w_merge_out=v_w_merge_out,
                norm_ffn_g=v_norm_ffn_g, w_ffn_in=v_w_ffn_in, w_ffn_down=v_w_ffn_down)
    names = list(weights)
    updates = {}
    after = exchanges.w_in_mid([small])
    for n in names:
        if n != "w_in":
            updates[n] = _adamw(weights[n], grads[n], m_in[n], v_in[n], "adamw_" + n, after)
    g_w_in_t = exchanges.w_in_end([updates[n][0] for n in updates])
    grads["w_in"] = g_w_in_t.T
    updates["w_in"] = [t.T for t in _adamw(w_in.T, g_w_in_t, m_w_in.T, v_w_in.T, "adamw_w_in", g_w_in_t)]
    return (loss, grad_x, *[grads[n] for n in names], *[updates[n][0] for n in names],
            *[updates[n][1] for n in names], *[updates[n][2] for n in names])
```

```python
import math

import jax
import jax.numpy as jnp
import numpy as np
from jax import lax
from jax.experimental import pallas as pl
from jax.experimental.pallas import tpu as pltpu

F32 = jnp.float32
BF16 = jnp.bfloat16

D = 1024
CW = 31
HD = 64
NQ = 16
NKV = 2
GROUP = NQ // NKV
BLK = 128
DFF = 2816
EPS = 1e-6
NEG = -1e30
IN_COLS = 5376
U_CONV, U_Q, U_KV = slice(0, 2 * D), slice(2 * D, 3 * D), slice(3 * D, 3 * D + 2 * NKV * HD)
U_GATES = slice(3 * D + 2 * NKV * HD, IN_COLS)
SCALE = 1.0 / math.sqrt(HD)

LR, B1, B2, AEPS, WD, STEP = 0.001, 0.9, 0.999, 1e-08, 0.01, 10

MIB = 1024 * 1024
MESH = pl.DeviceIdType.MESH

ROWS_W_IN = IN_COLS // 4
ROWS_SQ = D // 4
ROWS_FFN_IN = 2 * DFF // 4
ROWS_DOWN = DFF // 4


def _sig(x):
    return 1.0 / (1.0 + jnp.exp(-x))


def _dot(a, b):
    return jnp.dot(a, b, preferred_element_type=F32)


def _dot_nt(a, b):
    return lax.dot_general(a, b, (((1,), (1,)), ((), ())), preferred_element_type=F32)


def _dot_tn(a, b):
    return lax.dot_general(a, b, (((0,), (0,)), ((), ())), preferred_element_type=F32)


def _params(sem, vmem_mib=48):
    return pltpu.CompilerParams(dimension_semantics=sem, vmem_limit_bytes=vmem_mib * MIB)


def _resident(shape):
    return pl.BlockSpec(shape, lambda *_: (0,) * len(shape), pipeline_mode=pl.Buffered(1))


def _whole(shape):
    return pl.BlockSpec(shape, lambda *_: (0,) * len(shape))


def _rows(tm, cols):
    return pl.BlockSpec((tm, cols), lambda i: (i, 0))


def _tap_phases():
    return [(phase, list(range(phase, CW, 8))) for phase in range(8)]


def _shift_copies(dst, src, base):
    for phase, taps in _tap_phases():
        n = dst.shape[1] - 8 * (4 - len(taps))
        dst[phase, 0:n, :] = src[base + phase:base + phase + n, :]


def _in_proj_conv_fwd(x2, g1, w_in, dw, dwb, lng, lnb, n_seq, S):
    T = n_seq * S
    tc = min(256, S)
    nt = S // tc

    def body(x_ref, g_ref, w_ref, dw_ref, dwb_ref, lng_ref, lnb_ref, xn_ref, ag_ref, q_ref, kv_ref, gg_ref,
             h1_ref, h3_ref, ext, sh):
        i = pl.program_id(1)

        @pl.when(i == 0)
        def _():
            ext[0:32, :] = jnp.zeros((32, D), F32)

        x = x_ref[...]
        rstd = lax.rsqrt(jnp.mean(x * x, axis=-1, keepdims=True) + EPS)
        xn = (x * rstd * g_ref[...]).astype(BF16)
        xn_ref[...] = xn
        ag = _dot_nt(xn, w_ref[U_CONV, :])
        ag_ref[...] = ag
        ext[32:32 + tc, :] = ag[:, 0:D] * _sig(ag[:, D:2 * D])
        q_ref[...] = _dot_nt(xn, w_ref[U_Q, :])
        kv_ref[...] = _dot_nt(xn, w_ref[U_KV, :])
        gg_ref[...] = _dot_nt(xn, w_ref[U_GATES, :])
        _shift_copies(sh, ext, 2)
        for cb in range(D // 128):
            cs = slice(cb * 128, (cb + 1) * 128)
            acc = jnp.broadcast_to(dwb_ref[:, cs], (tc, 128))
            for phase, taps in _tap_phases():
                for m, j in enumerate(taps):
                    acc = acc + dw_ref[j:j + 1, cs] * sh[phase, 8 * m:8 * m + tc, cs]
            h1_ref[:, cs] = acc
        ext[0:32, :] = ext[tc:tc + 32, :]
        h1 = h1_ref[...]
        mu = jnp.mean(h1, axis=-1, keepdims=True)
        cen = h1 - mu
        var = jnp.mean(cen * cen, axis=-1, keepdims=True)
        h2 = cen * lax.rsqrt(var + EPS) * lng_ref[...] + lnb_ref[...]
        h3 = (h2 * _sig(h2)).astype(BF16)
        h3_ref[...] = h3

    tile = lambda cols: pl.BlockSpec((tc, cols), lambda b, i: (b * nt + i, 0))
    shape = lambda cols, dtype: jax.ShapeDtypeStruct((T, cols), dtype)
    return pl.pallas_call(
        body, name="in_proj_conv_fwd", grid=(n_seq, nt),
        in_specs=[tile(D), _resident((1, D)), _resident((IN_COLS, D)), _resident((32, D)), _resident((1, D)),
                  _resident((1, D)), _resident((1, D))],
        out_specs=[tile(D), tile(2 * D), tile(D), tile(256), tile(2 * D), tile(D), tile(D)],
        out_shape=[shape(D, BF16), shape(2 * D, F32), shape(D, F32), shape(256, F32), shape(2 * D, F32),
                   shape(D, F32), shape(D, BF16)],
        scratch_shapes=[pltpu.VMEM((32 + tc, D), F32), pltpu.VMEM((8, tc + 24, D), F32)],
        compiler_params=_params(("parallel", "arbitrary"), 56),
    )(x2, g1, w_in, dw, dwb, lng, lnb)


def _attn_consts():
    k = np.arange(BLK)[:, None]
    i = np.arange(GROUP * BLK)[None, :] % BLK
    from_prev = k > i
    dist = np.where(from_prev, i + BLK - k, i - k).astype(np.float32)
    head = np.arange(GROUP * BLK)[None, :] // BLK
    bias = []
    for kh in range(NKV):
        slope = np.exp2(-8.0 * (kh * GROUP + head + 1) / NQ).astype(np.float32)
        bias.append(-slope * dist)
    return jnp.asarray(from_prev.astype(np.float32)), jnp.asarray(np.stack(bias))


def _heads_to_lanes(t, kh):
    return jnp.concatenate([t[(kh * GROUP + g) * HD:(kh * GROUP + g + 1) * HD, :] for g in range(GROUP)], axis=1)


def _lanes_to_heads(t):
    return jnp.concatenate([t[:, g * BLK:(g + 1) * BLK] for g in range(GROUP)], axis=0)


def _rms64(t):
    return lax.rsqrt(jnp.mean(t * t, axis=-1, keepdims=True) + EPS)


def _attn_probs(kh, n, q_t, kvc_ref, kvp_ref, qg_ref, kg_ref, tri_ref, bias_ref, sink_ref):
    ks = slice(kh * HD, (kh + 1) * HD)
    vs = slice(2 * HD + kh * HD, 2 * HD + (kh + 1) * HD)
    kp, kc = kvp_ref[:, ks], kvc_ref[:, ks]
    kpb = (kp * _rms64(kp) * kg_ref[...]).astype(BF16)
    kcb = (kc * _rms64(kc) * kg_ref[...]).astype(BF16)
    qs = _heads_to_lanes(q_t, kh)
    rq = lax.rsqrt(jnp.mean(qs * qs, axis=0, keepdims=True) + EPS)
    qy = qs * rq
    qhb = (qy * (qg_ref[...] * SCALE)).astype(BF16)
    from_prev = tri_ref[...] > 0.5
    no_prev = jnp.where(n > 0, 0.0, NEG)
    s = jnp.where(from_prev, _dot(kpb, qhb) + no_prev, _dot(kcb, qhb)) + bias_ref[kh]
    sink = sink_ref[kh:kh + 1, :]
    m = jnp.maximum(jnp.max(s, axis=0, keepdims=True), sink)
    e = jnp.exp(s - m)
    es = jnp.exp(sink - m)
    rz = 1.0 / (jnp.sum(e, axis=0, keepdims=True) + es)
    prev_mask = tri_ref[...].astype(BF16)
    return e * rz, es * rz, prev_mask, qhb, kpb, kcb, kvp_ref[:, vs].astype(BF16), kvc_ref[:, vs].astype(BF16), qy, rq


def _unfold(t, prev_mask):
    prev = t * prev_mask
    return prev, t - prev


def _attn_specs(n_seq, S):
    nb = S // BLK
    cur = lambda cols: pl.BlockSpec((BLK, cols), lambda b, n: (b * nb + n, 0))
    prev = lambda cols: pl.BlockSpec((BLK, cols), lambda b, n: (b * nb + jnp.maximum(n - 1, 0), 0))
    consts = [_resident((HD, GROUP * BLK)), _resident((1, HD)), _resident((BLK, GROUP * BLK)),
              _resident((NKV, BLK, GROUP * BLK)), _resident((NKV, GROUP * BLK))]
    return nb, cur, prev, consts


def _attn_fwd(q, kv, qg_cols, kg, sink_rows, n_seq, S):
    T = n_seq * S
    nb, cur, prev, consts = _attn_specs(n_seq, S)
    tri, bias = _attn_consts()

    def body(q_ref, kvc_ref, kvp_ref, qg_ref, kg_ref, tri_ref, bias_ref, sink_ref, o_ref):
        n = pl.program_id(1)
        q_t = q_ref[...].T
        o_t = []
        for kh in range(NKV):
            p, _, prev_mask, _, _, _, vpb, vcb, _, _ = _attn_probs(kh, n, q_t, kvc_ref, kvp_ref, qg_ref, kg_ref,
                                                                   tri_ref, bias_ref, sink_ref)
            pp, pc = _unfold(p.astype(BF16), prev_mask)
            o_t.append(_lanes_to_heads(_dot_tn(vpb, pp) + _dot_tn(vcb, pc)))
        o_ref[...] = jnp.concatenate(o_t, axis=0).T.astype(BF16)

    return pl.pallas_call(
        body, name="attn_fwd", grid=(n_seq, nb),
        in_specs=[cur(D), cur(256), prev(256)] + consts,
        out_specs=cur(D),
        out_shape=jax.ShapeDtypeStruct((T, D), BF16),
        compiler_params=_params(("parallel", "parallel")),
    )(q, kv, kv, qg_cols, kg, tri, bias, sink_rows)


def _merge_fwd(x2, gg, h3, o, wa, wm, wc):
    T = x2.shape[0]
    tm = min(512, T)

    def body(x_ref, gg_ref, h3_ref, o_ref, wa_ref, wm_ref, wc_ref, mix_ref, h_ref, yc_ref):
        yc = _dot(h3_ref[...], wc_ref[...])
        yc_ref[...] = yc
        ya = _dot(o_ref[...], wa_ref[...])
        mix = (_sig(gg_ref[:, 0:D]) * yc + _sig(gg_ref[:, D:2 * D]) * ya).astype(BF16)
        mix_ref[...] = mix
        h_ref[...] = x_ref[...] + _dot(mix, wm_ref[...])

    return pl.pallas_call(
        body, name="merge_fwd", grid=(T // tm,),
        in_specs=[_rows(tm, D), _rows(tm, 2 * D), _rows(tm, D), _rows(tm, D), _resident((D, D)), _resident((D, D)),
                  _resident((D, D))],
        out_specs=[_rows(tm, D), _rows(tm, D), _rows(tm, D)],
        out_shape=[jax.ShapeDtypeStruct((T, D), BF16), jax.ShapeDtypeStruct((T, D), F32),
                   jax.ShapeDtypeStruct((T, D), F32)],
        compiler_params=_params(("parallel",)),
    )(x2, gg, h3, o, wa, wm, wc)


FF_CHUNK = DFF // 2


def _ffn(h, tgt, g2, wf, wd):
    T = h.shape[0]
    tm = min(256, T)

    def body(h_ref, t_ref, g_ref, wf_ref, wd_ref, dh_ref, dhb_ref, hn_ref, act_ref, dout_ref, dgu_ref, st_ref,
             gsc, usc):
        @pl.when(pl.program_id(0) == 0)
        def _():
            st_ref[...] = jnp.zeros((8, D), F32)

        hh = h_ref[...]
        rstd = lax.rsqrt(jnp.mean(hh * hh, axis=-1, keepdims=True) + EPS)
        hhat = hh * rstd
        hn = (hhat * g_ref[...]).astype(BF16)
        hn_ref[...] = hn
        out = hh
        for c in range(DFF // FF_CHUNK):
            cs = slice(c * FF_CHUNK, (c + 1) * FF_CHUNK)
            us = slice(DFF + c * FF_CHUNK, DFF + (c + 1) * FF_CHUNK)
            g = _dot(hn, wf_ref[c])
            u = _dot(hn, wf_ref[2 + c])
            gsc[:, cs] = g
            usc[:, cs] = u
            act = (g * _sig(g) * u).astype(BF16)
            act_ref[:, cs] = act
            out = out + _dot(act, wd_ref[cs, :])
        err = out - t_ref[...]
        dout = err * (1.0 / D)
        doutb = dout.astype(BF16)
        dout_ref[...] = doutb
        dhn = jnp.zeros((tm, D), F32)
        for c in range(DFF // FF_CHUNK):
            cs = slice(c * FF_CHUNK, (c + 1) * FF_CHUNK)
            us = slice(DFF + c * FF_CHUNK, DFF + (c + 1) * FF_CHUNK)
            g = gsc[:, cs]
            u = usc[:, cs]
            dact = _dot_nt(doutb, wd_ref[cs, :])
            sg = _sig(g)
            dg = (dact * u * (sg * (1.0 + g * (1.0 - sg)))).astype(BF16)
            du = (dact * (g * sg)).astype(BF16)
            dgu_ref[:, cs] = dg
            dgu_ref[:, us] = du
            dhn = dhn + _dot_nt(dg, wf_ref[c]) + _dot_nt(du, wf_ref[2 + c])
        st_ref[0:1, :] += jnp.sum(dhn * hhat, axis=0, keepdims=True)
        st_ref[1:2, :] += jnp.sum(err * err, axis=0, keepdims=True)
        dhh = dhn * g_ref[...]
        dh = dout + rstd * (dhh - hhat * jnp.mean(dhh * hhat, axis=-1, keepdims=True))
        dh_ref[...] = dh
        dhb_ref[...] = dh.astype(BF16)

    return pl.pallas_call(
        body, name="ffn_fwd_bwd", grid=(T // tm,),
        in_specs=[_rows(tm, D), _rows(tm, D), _resident((1, D)), _resident((4, D, FF_CHUNK)), _resident((DFF, D))],
        out_specs=[_rows(tm, D), _rows(tm, D), _rows(tm, D), _rows(tm, DFF), _rows(tm, D), _rows(tm, 2 * DFF),
                   _whole((8, D))],
        out_shape=[jax.ShapeDtypeStruct((T, D), F32), jax.ShapeDtypeStruct((T, D), BF16),
                   jax.ShapeDtypeStruct((T, D), BF16), jax.ShapeDtypeStruct((T, DFF), BF16),
                   jax.ShapeDtypeStruct((T, D), BF16), jax.ShapeDtypeStruct((T, 2 * DFF), BF16),
                   jax.ShapeDtypeStruct((8, D), F32)],
        scratch_shapes=[pltpu.VMEM((tm, DFF), F32), pltpu.VMEM((tm, DFF), F32)],
        compiler_params=_params(("arbitrary",), 56),
    )(h, tgt, g2, wf, wd)


def _merge_bwd(dhb, gg, yc, o, wm, wa, wc):
    T = dhb.shape[0]
    tm = min(512, T)

    def body(dh_ref, gg_ref, yc_ref, o_ref, wm_ref, wa_ref, wc_ref, dgg_ref, dyc_ref, dya_ref, do_ref, dh3_ref):
        dmix = _dot_nt(dh_ref[...], wm_ref[...])
        gc = _sig(gg_ref[:, 0:D])
        ga = _sig(gg_ref[:, D:2 * D])
        yc = yc_ref[...]
        ya = _dot(o_ref[...], wa_ref[...])
        dgg_ref[:, 0:D] = (dmix * yc * gc * (1.0 - gc)).astype(BF16)
        dgg_ref[:, D:2 * D] = (dmix * ya * ga * (1.0 - ga)).astype(BF16)
        dyc = (dmix * gc).astype(BF16)
        dya = (dmix * ga).astype(BF16)
        dyc_ref[...] = dyc
        dya_ref[...] = dya
        do_ref[...] = _dot_nt(dya, wa_ref[...]).astype(BF16)
        dh3_ref[...] = _dot_nt(dyc, wc_ref[...])

    return pl.pallas_call(
        body, name="merge_bwd", grid=(T // tm,),
        in_specs=[_rows(tm, D), _rows(tm, 2 * D), _rows(tm, D), _rows(tm, D), _resident((D, D)), _resident((D, D)),
                  _resident((D, D))],
        out_specs=[_rows(tm, 2 * D), _rows(tm, D), _rows(tm, D), _rows(tm, D), _rows(tm, D)],
        out_shape=[jax.ShapeDtypeStruct((T, 2 * D), BF16), jax.ShapeDtypeStruct((T, D), BF16),
                   jax.ShapeDtypeStruct((T, D), BF16), jax.ShapeDtypeStruct((T, D), BF16),
                   jax.ShapeDtypeStruct((T, D), F32)],
        compiler_params=_params(("parallel",), 56),
    )(dhb, gg, yc, o, wm, wa, wc)


def _attn_bwd(q, kv, do, qg_cols, kg, sink_rows, n_seq, S):
    T = n_seq * S
    nb, cur, prev, consts = _attn_specs(n_seq, S)
    tri, bias = _attn_consts()

    def body(q_ref, kvc_ref, kvp_ref, do_ref, qg_ref, kg_ref, tri_ref, bias_ref, sink_ref, dq_ref, dkc_ref, dkp_ref,
             dqg_ref, dsk_ref):
        n = pl.program_id(1)

        @pl.when((pl.program_id(0) == 0) & (n == 0))
        def _():
            dqg_ref[...] = jnp.zeros((HD, BLK), F32)
            dsk_ref[...] = jnp.zeros((8, 128), F32)

        lane = lax.broadcasted_iota(jnp.int32, (1, 128), 1)
        q_t = q_ref[...].T
        do_t = do_ref[...].astype(F32).T
        dq_t = []
        for kh in range(NKV):
            p, ps, prev_mask, qhb, kpb, kcb, vpb, vcb, qy, rq = _attn_probs(
                kh, n, q_t, kvc_ref, kvp_ref, qg_ref, kg_ref, tri_ref, bias_ref, sink_ref)
            dob = _heads_to_lanes(do_t, kh).astype(BF16)
            dp = jnp.where(tri_ref[...] > 0.5, _dot(vpb, dob), _dot(vcb, dob))
            delta = jnp.sum(p * dp, axis=0, keepdims=True)
            dsp, dsc = _unfold((p * (dp - delta)).astype(BF16), prev_mask)
            pp, pc = _unfold(p.astype(BF16), prev_mask)
            dsink = -ps * delta
            dqh = (_dot_tn(kpb, dsp) + _dot_tn(kcb, dsc)) * SCALE
            dqg = dqh * qy
            dqg_ref[...] += sum(dqg[:, g * BLK:(g + 1) * BLK] for g in range(GROUP))
            dy = dqh * qg_ref[...]
            dq_t.append(_lanes_to_heads(rq * (dy - qy * jnp.mean(dy * qy, axis=0, keepdims=True))))
            row = jnp.zeros((1, 128), F32)
            for g in range(GROUP):
                h = kh * GROUP + g
                row = row + jnp.where(lane == h, jnp.sum(dsink[:, g * BLK:(g + 1) * BLK], axis=1, keepdims=True), 0.0)
            dsk_ref[0:1, :] += row
            ks = slice(kh * HD, (kh + 1) * HD)
            vs = slice(2 * HD + kh * HD, 2 * HD + (kh + 1) * HD)
            dkp_ref[:, ks] = _dot_nt(dsp, qhb)
            dkc_ref[:, ks] = _dot_nt(dsc, qhb)
            dkp_ref[:, vs] = _dot_nt(pp, dob)
            dkc_ref[:, vs] = _dot_nt(pc, dob)
        dq_ref[...] = jnp.concatenate(dq_t, axis=0).T.astype(BF16)

    return pl.pallas_call(
        body, name="attn_bwd", grid=(n_seq, nb),
        in_specs=[cur(D), cur(256), prev(256), cur(D)] + consts,
        out_specs=[cur(D), cur(256), cur(256), _whole((HD, BLK)), _whole((8, 128))],
        out_shape=[jax.ShapeDtypeStruct((T, D), BF16), jax.ShapeDtypeStruct((T, 256), F32),
                   jax.ShapeDtypeStruct((T, 256), F32), jax.ShapeDtypeStruct((HD, BLK), F32),
                   jax.ShapeDtypeStruct((8, 128), F32)],
        compiler_params=_params(("arbitrary", "arbitrary")),
    )(q, kv, kv, do, qg_cols, kg, tri, bias, sink_rows)


def _kv_bwd(kv, dkc, dkp, kg_cols, n_seq, S):
    T = n_seq * S
    seq = lambda cols: pl.BlockSpec((S, cols), lambda b: (b, 0))

    def body(kv_ref, dkc_ref, dkp_ref, kg_ref, dkv_ref, dkg_ref):
        @pl.when(pl.program_id(0) == 0)
        def _():
            dkg_ref[...] = jnp.zeros((HD, BLK), F32)

        from_next = jnp.concatenate([dkp_ref[BLK:S, :], jnp.zeros((BLK, 256), F32)], axis=0)
        d = dkc_ref[...] + from_next
        d_t = d[:, 0:2 * HD].T
        k_t = kv_ref[:, 0:2 * HD].T
        kg = jnp.concatenate([kg_ref[...]] * (S // BLK), axis=1)
        out = []
        for kh in range(NKV):
            k = k_t[kh * HD:(kh + 1) * HD, :]
            r = lax.rsqrt(jnp.mean(k * k, axis=0, keepdims=True) + EPS)
            y = k * r
            dkh = d_t[kh * HD:(kh + 1) * HD, :]
            dkg = dkh * y
            dkg_ref[...] += sum(dkg[:, j * BLK:(j + 1) * BLK] for j in range(S // BLK))
            dy = dkh * kg
            out.append(r * (dy - y * jnp.mean(dy * y, axis=0, keepdims=True)))
        dkv_ref[:, 0:2 * HD] = jnp.concatenate(out, axis=0).T.astype(BF16)
        dkv_ref[:, 2 * HD:4 * HD] = d[:, 2 * HD:4 * HD].astype(BF16)

    return pl.pallas_call(
        body, name="kv_bwd", grid=(n_seq,),
        in_specs=[seq(256), seq(256), seq(256), _resident((HD, BLK))],
        out_specs=[seq(256), _whole((HD, BLK))],
        out_shape=[jax.ShapeDtypeStruct((T, 256), BF16), jax.ShapeDtypeStruct((HD, BLK), F32)],
        compiler_params=_params(("arbitrary",)),
    )(kv, dkc, dkp, kg_cols)


def _conv_bwd(h1, dh3, ag, dw, lng, lnb, n_seq, S):
    T = n_seq * S
    tc = min(256, S)
    nt = S // tc

    def body(h1_ref, dh3_ref, a_ref, gt_ref, dw_ref, lng_ref, lnb_ref, dag_ref, ddw_ref, st_ref, extd, acc8, shd):
        i = pl.program_id(1)

        @pl.when((pl.program_id(0) == 0) & (i == 0))
        def _():
            acc8[...] = jnp.zeros((CW * 8, D), F32)
            st_ref[...] = jnp.zeros((8, D), F32)

        @pl.when(i == 0)
        def _():
            extd[tc:tc + 32, :] = jnp.zeros((32, D), F32)

        h1 = h1_ref[...]
        mu = jnp.mean(h1, axis=-1, keepdims=True)
        cen = h1 - mu
        rstd = lax.rsqrt(jnp.mean(cen * cen, axis=-1, keepdims=True) + EPS)
        xh = cen * rstd
        h2 = xh * lng_ref[...] + lnb_ref[...]
        sg = _sig(h2)
        dh2 = dh3_ref[...] * (sg * (1.0 + h2 * (1.0 - sg)))
        st_ref[1:2, :] += jnp.sum(dh2 * xh, axis=0, keepdims=True)
        st_ref[2:3, :] += jnp.sum(dh2, axis=0, keepdims=True)
        dxh = dh2 * lng_ref[...]
        dh1 = rstd * (dxh - jnp.mean(dxh, axis=-1, keepdims=True)
                      - xh * jnp.mean(dxh * xh, axis=-1, keepdims=True))
        st_ref[0:1, :] += jnp.sum(dh1, axis=0, keepdims=True)
        extd[0:tc, :] = dh1
        _shift_copies(shd, extd, 0)
        for cb in range(D // 128):
            cs = slice(cb * 128, (cb + 1) * 128)
            for rb in range(tc // 128):
                rs = slice(rb * 128, (rb + 1) * 128)
                a = a_ref[rs, cs]
                sgt = _sig(gt_ref[rs, cs])
                h0 = a * sgt
                acc = jnp.zeros((128, 128), F32)
                for phase, offs in _tap_phases():
                    for m, o in enumerate(offs):
                        j = CW - 1 - o
                        ahead = shd[phase, rb * 128 + 8 * m:rb * 128 + 8 * m + 128, cs]
                        acc = acc + dw_ref[j:j + 1, cs] * ahead
                        acc8[j * 8:(j + 1) * 8, cs] += jnp.sum((h0 * ahead).reshape(16, 8, 128), axis=0)
                dag_ref[rs, cs] = (acc * sgt).astype(BF16)
                dag_ref[rs, cb * 128 + D:(cb + 1) * 128 + D] = (acc * a * sgt * (1.0 - sgt)).astype(BF16)
        extd[tc:tc + 32, :] = extd[0:32, :]

        @pl.when((pl.program_id(0) == n_seq - 1) & (i == nt - 1))
        def _():
            for j in range(CW):
                ddw_ref[j:j + 1, :] = jnp.sum(acc8[j * 8:(j + 1) * 8, :], axis=0, keepdims=True)
            ddw_ref[CW:32, :] = jnp.zeros((32 - CW, D), F32)

    tile = lambda col: pl.BlockSpec((tc, D), lambda b, i: (b * nt + (nt - 1 - i), col))
    return pl.pallas_call(
        body, name="conv_bwd", grid=(n_seq, nt),
        in_specs=[tile(0), tile(0), tile(0), tile(1), _resident((32, D)), _resident((1, D)), _resident((1, D))],
        out_specs=[pl.BlockSpec((tc, 2 * D), lambda b, i: (b * nt + (nt - 1 - i), 0)), _whole((32, D)),
                   _whole((8, D))],
        out_shape=[jax.ShapeDtypeStruct((T, 2 * D), BF16), jax.ShapeDtypeStruct((32, D), F32),
                   jax.ShapeDtypeStruct((8, D), F32)],
        scratch_shapes=[pltpu.VMEM((tc + 32, D), F32), pltpu.VMEM((CW * 8, D), F32),
                        pltpu.VMEM((8, tc + 24, D), F32)],
        compiler_params=_params(("arbitrary", "arbitrary")),
    )(h1, dh3, ag, ag, dw, lng, lnb)


def _in_proj_bwd(dag, dq, dkv, dgg, dh, x2, g1, w_in):
    T = x2.shape[0]
    tm = min(512, T)

    def body(dag_ref, dq_ref, dkv_ref, dgg_ref, dh_ref, x_ref, g_ref, w_ref, dx_ref, dg_ref):
        @pl.when(pl.program_id(0) == 0)
        def _():
            dg_ref[...] = jnp.zeros((1, D), F32)

        dxn = (_dot(dag_ref[...], w_ref[U_CONV, :]) + _dot(dq_ref[...], w_ref[U_Q, :])
               + _dot(dkv_ref[...], w_ref[U_KV, :]) + _dot(dgg_ref[...], w_ref[U_GATES, :]))
        x = x_ref[...]
        rstd = lax.rsqrt(jnp.mean(x * x, axis=-1, keepdims=True) + EPS)
        xh = x * rstd
        dg_ref[...] += jnp.sum(dxn * xh, axis=0, keepdims=True)
        dxh = dxn * g_ref[...]
        dx_ref[...] = dh_ref[...] + rstd * (dxh - xh * jnp.mean(dxh * xh, axis=-1, keepdims=True))

    return pl.pallas_call(
        body, name="in_proj_bwd", grid=(T // tm,),
        in_specs=[_rows(tm, 2 * D), _rows(tm, D), _rows(tm, 256), _rows(tm, 2 * D), _rows(tm, D), _rows(tm, D),
                  _resident((1, D)), _resident((IN_COLS, D))],
        out_specs=[_rows(tm, D), _whole((1, D))],
        out_shape=[jax.ShapeDtypeStruct((T, D), F32), jax.ShapeDtypeStruct((1, D), F32)],
        compiler_params=_params(("arbitrary",)),
    )(dag, dq, dkv, dgg, dh, x2, g1, w_in)


def _tn_matmul(a, b, name, column_blocks=False):
    T, K = a.shape
    N = b.shape[1]
    tk = K if K <= 1024 else K // 2
    tn = N if N <= 1024 else (1024 if N % 1024 == 0 and not column_blocks else N // 4)
    tt = min(2048, T)
    assert K % tk == 0 and N % tn == 0 and T % tt == 0 and tk % 128 == 0 and tn % 128 == 0

    def body(a_ref, b_ref, o_ref):
        @pl.when(pl.program_id(2) == 0)
        def _():
            o_ref[...] = jnp.zeros((tk, tn), F32)

        o_ref[...] += _dot_tn(a_ref[...], b_ref[...])

    return pl.pallas_call(
        body, name=name, grid=(K // tk, N // tn, T // tt),
        in_specs=[pl.BlockSpec((tt, tk), lambda i, j, t: (t, i)), pl.BlockSpec((tt, tn), lambda i, j, t: (t, j))],
        out_specs=(pl.BlockSpec((None, tk, tn), lambda i, j, t: (j, i, 0)) if column_blocks
                   else pl.BlockSpec((tk, tn), lambda i, j, t: (i, j))),
        out_shape=jax.ShapeDtypeStruct((N // tn, K, tn) if column_blocks else (K, N), F32),
        compiler_params=_params(("parallel", "parallel", "arbitrary")),
    )(a, b)


def _place():
    x, y, c = lax.axis_index("x"), lax.axis_index("y"), lax.axis_index("c")
    chips = [(1 - x, y), (x, 1 - y), (1 - x, 1 - y)]
    return x, y, c, chips


def _own_slot(slots, mine):
    chip = 2 * lax.axis_index("x") + lax.axis_index("y")
    return lax.dynamic_update_slice(slots, mine[None], (chip,) + (0,) * mine.ndim)


def _row_tile(rows, unit):
    return max(t for t in range(unit, 513, unit) if rows % t == 0)


def _gather_weights(packs):
    n = len(packs)
    halves = [p.shape[0] // 2 for p in packs]

    def body(*refs):
        srcs, dsts, token, send_sems, recv_sems = refs[:n], refs[n:2 * n], refs[2 * n], refs[2 * n + 1], refs[2 * n + 2]
        x, y, c, chips = _place()

        def piece(b, px, py, pc):
            return dsts[b].at[2 * px + py, pl.ds(pc * halves[b], halves[b]), :]

        def copy(b, k, block, to, from_src=False):
            mine = srcs[b].at[pl.ds(c * halves[b], halves[b]), :]
            return pltpu.make_async_remote_copy(
                src_ref=mine if from_src else piece(b, *block), dst_ref=piece(b, *block),
                send_sem=send_sems.at[6 * b + k], recv_sem=recv_sems.at[6 * b + k], device_id=to,
                device_id_type=MESH)

        first = [copy(b, k, (x, y, c), (*chip, c), from_src=True) for b in range(n) for k, chip in enumerate(chips)]
        for cp in first:
            cp.start()
        passed = []
        for b in range(n):
            for k, chip in enumerate(chips):
                copy(b, k, (*chip, c), (x, y, c)).wait_recv()
                passed.append(copy(b, 3 + k, (*chip, c), (x, y, 1 - c)))
                passed[-1].start()
        for b in range(n):
            for k, chip in enumerate(chips):
                copy(b, 3 + k, (*chip, 1 - c), (x, y, c)).wait_recv()
        for cp in first + passed:
            cp.wait_send()
        token[...] = jnp.zeros((8, 128), F32)

    outs = pl.pallas_call(
        body, name="gather_weights",
        in_specs=[pl.BlockSpec(memory_space=pl.ANY)] * n,
        out_specs=[pl.BlockSpec(memory_space=pl.ANY)] * n + [pl.BlockSpec(memory_space=pltpu.VMEM)],
        out_shape=[jax.ShapeDtypeStruct((4,) + p.shape, p.dtype) for p in packs]
        + [jax.ShapeDtypeStruct((8, 128), F32)],
        scratch_shapes=[pltpu.SemaphoreType.DMA((6 * n,)), pltpu.SemaphoreType.DMA((6 * n,))],
        compiler_params=pltpu.CompilerParams(has_side_effects=True),
    )(*packs)
    return [_own_slot(got, p) for got, p in zip(outs[:n], packs)], outs[n][0, 0]


def _add_halves(g, got, c_idx, name="grad_add_halves"):
    rows, w = g.shape[1], g.shape[2]
    half = rows // 2
    tr = _row_tile(half, 16)
    nt = half // tr

    def body(c_ref, g_ref, r_ref, o_ref):
        o_ref[...] = (g_ref[...] + r_ref[...]).astype(BF16)

    return pl.pallas_call(
        body, name=name,
        grid_spec=pltpu.PrefetchScalarGridSpec(
            num_scalar_prefetch=1, grid=(4, nt),
            in_specs=[pl.BlockSpec((1, tr, w), lambda q, i, c_ref: (q, c_ref[0] * nt + i, 0)),
                      pl.BlockSpec((1, tr, w), lambda q, i, c_ref: (q, i, 0))],
            out_specs=pl.BlockSpec((1, tr, w), lambda q, i, c_ref: (q, i, 0))),
        out_shape=jax.ShapeDtypeStruct((4, half, w), BF16),
        compiler_params=_params(("parallel", "parallel")),
    )(c_idx, g, got)


def _own_piece(p):
    chip = 2 * lax.axis_index("x") + lax.axis_index("y")
    return lax.dynamic_index_in_dim(p, chip, axis=0, keepdims=False)


def _sum_chips(r, c_idx, name="grad_sum_chips"):
    half, w = r.shape[1], r.shape[2]
    tr = _row_tile(half, 16)
    nt = half // tr

    def body(c_ref, r_ref, o_ref):
        acc = r_ref[0].astype(F32)
        for q in range(1, 4):
            acc = acc + r_ref[q].astype(F32)
        o_ref[...] = acc

    return pl.pallas_call(
        body, name=name,
        grid_spec=pltpu.PrefetchScalarGridSpec(
            num_scalar_prefetch=1, grid=(nt,),
            in_specs=[pl.BlockSpec((4, tr, w), lambda i, c_ref: (0, i, 0))],
            out_specs=pl.BlockSpec((tr, w), lambda i, c_ref: (c_ref[0] * nt + i, 0))),
        out_shape=jax.ShapeDtypeStruct((2 * half, w), F32),
        compiler_params=_params(("parallel",)),
    )(c_idx, r)


def _join_halves(f):
    half = f.shape[0] // 2

    def body(src, dst, send_sem, recv_sem):
        x, y, c, _ = _place()
        cp = pltpu.make_async_remote_copy(
            src_ref=src.at[pl.ds(c * half, half), :], dst_ref=dst.at[pl.ds(c * half, half), :], send_sem=send_sem,
            recv_sem=recv_sem, device_id=(x, y, 1 - c), device_id_type=MESH)
        cp.start()
        pltpu.make_async_remote_copy(
            src_ref=src.at[pl.ds(c * half, half), :], dst_ref=dst.at[pl.ds((1 - c) * half, half), :],
            send_sem=send_sem, recv_sem=recv_sem, device_id=(x, y, 1 - c), device_id_type=MESH).wait_recv()
        cp.wait_send()

    return pl.pallas_call(
        body, name="grad_join_halves",
        in_specs=[pl.BlockSpec(memory_space=pl.ANY)], out_specs=pl.BlockSpec(memory_space=pl.ANY),
        out_shape=jax.ShapeDtypeStruct(f.shape, f.dtype), input_output_aliases={0: 0},
        scratch_shapes=[pltpu.SemaphoreType.DMA, pltpu.SemaphoreType.DMA],
        compiler_params=pltpu.CompilerParams(has_side_effects=True),
    )(f)


_HBM = pl.BlockSpec(memory_space=pltpu.HBM)
_SEM = pl.BlockSpec(memory_space=pltpu.SEMAPHORE)
_EFFECT = pltpu.SideEffectType.DATAFLOW_SIDE_EFFECTING


def _start_copies(name, bufs, n_sems, plan):
    nb = len(bufs)

    def body(*refs):
        for cp in plan(refs[:nb], refs[nb], refs[nb + 1])[0]:
            cp.start()
        refs[-1][...] = jnp.zeros((8, 128), F32)

    out = pl.pallas_call(
        body, name=name,
        out_shape=(pltpu.SemaphoreType.DMA((n_sems,)), pltpu.SemaphoreType.DMA((n_sems,)),
                   *[pltpu.HBM(b.shape, b.dtype) for b in bufs], jax.ShapeDtypeStruct((8, 128), F32)),
        in_specs=[_HBM] * nb, out_specs=(_SEM, _SEM, *[_HBM] * nb, pl.BlockSpec(memory_space=pltpu.VMEM)),
        input_output_aliases={i: 2 + i for i in range(nb)},
        compiler_params=pltpu.CompilerParams(has_side_effects=_EFFECT),
    )(*[pltpu.with_memory_space_constraint(b, pltpu.HBM) for b in bufs])
    return out[0], out[1], list(out[2:2 + nb]), out[-1]


def _wait_copies(name, send_sems, recv_sems, bufs, after, plan):
    nb = len(bufs)

    def body(*refs):
        _, sends, recvs = plan(refs[:nb], refs[nb], refs[nb + 1])
        for cp in sends:
            cp.wait_send()
        for cp in recvs:
            cp.wait_recv()

    out = pl.pallas_call(
        body, name=name,
        out_shape=tuple(pltpu.HBM(b.shape, b.dtype) for b in bufs),
        in_specs=[_HBM] * nb + [_SEM, _SEM] + [pl.BlockSpec(memory_space=pl.ANY)] * len(after),
        out_specs=tuple([_HBM] * nb),
        input_output_aliases={i: i for i in range(nb)},
        compiler_params=pltpu.CompilerParams(has_side_effects=_EFFECT),
    )(*bufs, send_sems, recv_sems, *after)
    return list(out)


def _plan_gather_direct(halves):
    n = len(halves)

    def plan(refs, send_sems, recv_sems):
        x, y, c, chips = _place()
        starts, recvs = [], []
        for b, half in enumerate(halves):
            src, land = refs[b], refs[n + b]
            for k, (cx, cy) in enumerate(chips):
                for d in range(2):
                    other = c if d == 0 else 1 - c
                    i = 6 * b + 2 * k + d
                    starts.append(pltpu.make_async_remote_copy(
                        src_ref=src.at[pl.ds(c * half, half), :],
                        dst_ref=land.at[2 * x + y, pl.ds(c * half, half), :],
                        send_sem=send_sems.at[i], recv_sem=recv_sems.at[i], device_id=(cx, cy, other),
                        device_id_type=MESH))
                    recvs.append(pltpu.make_async_remote_copy(
                        src_ref=src.at[pl.ds(c * half, half), :],
                        dst_ref=land.at[2 * cx + cy, pl.ds(other * half, half), :],
                        send_sem=send_sems.at[i], recv_sem=recv_sems.at[i], device_id=(cx, cy, other),
                        device_id_type=MESH))
        return starts, starts, recvs
    return plan


def _plan_swap_halves(halves):
    n = len(halves)

    def plan(refs, send_sems, recv_sems):
        x, y, c, _ = _place()
        cps = [pltpu.make_async_remote_copy(
            src_ref=refs[b].at[:, pl.ds((1 - c) * half, half), :], dst_ref=refs[n + b], send_sem=send_sems.at[b],
            recv_sem=recv_sems.at[b], device_id=(x, y, 1 - c), device_id_type=MESH)
            for b, half in enumerate(halves)]
        return cps, cps, cps
    return plan


def _plan_scatter_chips(n):
    def plan(refs, send_sems, recv_sems):
        x, y, c, chips = _place()
        me = 2 * x + y
        starts, recvs = [], []
        for b in range(n):
            src, land = refs[b], refs[n + b]
            for k, (cx, cy) in enumerate(chips):
                i = 3 * b + k
                starts.append(pltpu.make_async_remote_copy(
                    src_ref=src.at[2 * cx + cy], dst_ref=land.at[me], send_sem=send_sems.at[i],
                    recv_sem=recv_sems.at[i], device_id=(cx, cy, c), device_id_type=MESH))
                recvs.append(pltpu.make_async_remote_copy(
                    src_ref=src.at[me], dst_ref=land.at[2 * cx + cy], send_sem=send_sems.at[i],
                    recv_sem=recv_sems.at[i], device_id=(cx, cy, c), device_id_type=MESH))
        return starts, starts, recvs
    return plan


def _plan_join_halves(halves):
    def plan(refs, send_sems, recv_sems):
        x, y, c, _ = _place()
        starts, recvs = [], []
        for b, half in enumerate(halves):
            mine, theirs = refs[b].at[pl.ds(c * half, half), :], refs[b].at[pl.ds((1 - c) * half, half), :]
            starts.append(pltpu.make_async_remote_copy(
                src_ref=mine, dst_ref=mine, send_sem=send_sems.at[b], recv_sem=recv_sems.at[b],
                device_id=(x, y, 1 - c), device_id_type=MESH))
            recvs.append(pltpu.make_async_remote_copy(
                src_ref=mine, dst_ref=theirs, send_sem=send_sems.at[b], recv_sem=recv_sems.at[b],
                device_id=(x, y, 1 - c), device_id_type=MESH))
        return starts, starts, recvs
    return plan


def _allreduce_small(vec):
    def body(v_ref, o_ref, gath, send_sems, recv_sems):
        x, y, c, _ = _place()
        me = 4 * x + 2 * y + c
        gath[me] = v_ref[...]
        sends = []
        for k in range(1, 8):
            peer = (x ^ (k >> 2), y ^ ((k >> 1) & 1), c ^ (k & 1))
            sends.append(pltpu.make_async_remote_copy(
                src_ref=v_ref, dst_ref=gath.at[me], send_sem=send_sems.at[k - 1], recv_sem=recv_sems.at[k - 1],
                device_id=peer, device_id_type=MESH))
        for cp in sends:
            cp.start()
        for k in range(1, 8):
            peer = (x ^ (k >> 2), y ^ ((k >> 1) & 1), c ^ (k & 1))
            pltpu.make_async_remote_copy(
                src_ref=v_ref, dst_ref=gath.at[4 * peer[0] + 2 * peer[1] + peer[2]], send_sem=send_sems.at[k - 1],
                recv_sem=recv_sems.at[k - 1], device_id=peer, device_id_type=MESH).wait_recv()
        for cp in sends:
            cp.wait_send()
        acc = gath[0]
        for d in range(1, 8):
            acc = acc + gath[d]
        o_ref[...] = acc

    return pl.pallas_call(
        body, name="allreduce_small",
        in_specs=[pl.BlockSpec(memory_space=pltpu.VMEM)], out_specs=pl.BlockSpec(memory_space=pltpu.VMEM),
        out_shape=jax.ShapeDtypeStruct(vec.shape, F32),
        scratch_shapes=[pltpu.VMEM((8,) + vec.shape, F32), pltpu.SemaphoreType.DMA((7,)),
                        pltpu.SemaphoreType.DMA((7,))],
    )(vec)


def _adamw(w, g, m, v, name, after):
    shape = w.shape
    if w.ndim == 1 or w.size <= 128 * 128:
        two_d = (1, w.size) if w.size % 128 else (w.size // 128, 128)
    else:
        two_d = (w.shape[0], w.size // w.shape[0])
    rows, cols = two_d
    tr = _row_tile(rows, 8) if rows % 8 == 0 and rows > 512 else rows

    def body(w_ref, g_ref, m_ref, v_ref, after_ref, d_ref, nm_ref, nv_ref):
        gr = g_ref[...]
        nm = B1 * m_ref[...] + (1.0 - B1) * gr
        nv = B2 * v_ref[...] + (1.0 - B2) * (gr * gr)
        m_hat = nm / (1.0 - B1 ** STEP)
        v_hat = nv / (1.0 - B2 ** STEP)
        d_ref[...] = -LR * (m_hat / (jnp.sqrt(v_hat) + AEPS) + WD * w_ref[...])
        nm_ref[...] = nm
        nv_ref[...] = nv

    spec = pl.BlockSpec((tr, cols), lambda i: (i, 0))
    outs = pl.pallas_call(
        body, name=name, grid=(rows // tr,),
        in_specs=[spec] * 4 + [pl.BlockSpec(memory_space=pl.ANY)], out_specs=[spec] * 3,
        out_shape=[jax.ShapeDtypeStruct(two_d, F32)] * 3,
        compiler_params=_params(("parallel",)),
    )(*[t.reshape(two_d) for t in (w, g, m, v)], after)
    return [o.reshape(shape) for o in outs]


def _rows_stacked(g):
    return g.reshape(4 * g.shape[1], D)


def _rows_to_slots(t):
    return t.reshape(4, t.shape[0] // 4, D)


def _pack_first(w_in, conv_dw_w):
    dw = jnp.pad(conv_dw_w.reshape(CW, 256), ((0, 1), (0, 0)))
    dw_bits = lax.bitcast_convert_type(dw, BF16).reshape(16, D)
    return [w_in.T.astype(BF16), jnp.pad(dw_bits, ((0, 16), (0, 0)))]


def _unpack_first(w_in_slots, dw_slots):
    dw = lax.bitcast_convert_type(dw_slots[:, 0:16].reshape(4, 32, 256, 2), F32)
    return _rows_stacked(w_in_slots), jnp.transpose(dw, (1, 0, 2)).reshape(32, D)


class _Exchanges:
    def __init__(self, late_shards):
        self.c_idx = lax.axis_index("c").astype(jnp.int32).reshape(1)
        packs = list(late_shards)
        self.late_plan = _plan_gather_direct([p.shape[0] // 2 for p in packs])
        slots = [lax.empty((4,) + p.shape, BF16) for p in packs]
        self.late = _start_copies("gather_late_start", packs + slots, 6 * len(packs), self.late_plan)
        self.first_token = self.late[3][0, 0]

    def late_weights(self, after):
        send_sems, recv_sems, bufs, _ = self.late
        bufs = _wait_copies("gather_late_wait", send_sems, recv_sems, bufs, after, self.late_plan)
        wa, wm, wf, wd, wc = [_own_slot(bufs[5 + b], bufs[b]) for b in range(5)]
        return _rows_stacked(wa), _rows_stacked(wm), wf, _rows_stacked(wd), _rows_stacked(wc)

    def reduce_start(self, d_wd, d_wf4, d_wm, d_wa, d_wc):
        gs = [_rows_to_slots(d_wd), jnp.concatenate([_rows_to_slots(t) for t in (d_wm, d_wa, d_wc)], axis=1), d_wf4]
        self.halves = [g.shape[1] // 2 for g in gs]
        self.swap_plan = _plan_swap_halves(self.halves)
        lands = [lax.empty((4, h, g.shape[2]), F32) for g, h in zip(gs, self.halves)]
        self.swap = _start_copies("grad_swap_start", gs + lands, len(gs), self.swap_plan)
        return self.swap[3][0, 0]

    def reduce_mid(self, after):
        send_sems, recv_sems, bufs, _ = self.swap
        bufs = _wait_copies("grad_swap_wait", send_sems, recv_sems, bufs, after, self.swap_plan)
        n = len(self.halves)
        ps = [_add_halves(bufs[b], bufs[n + b], self.c_idx, "grad_add_halves_%d" % b) for b in range(n)]
        self.scatter_plan = _plan_scatter_chips(n)
        self.scatter = _start_copies("grad_scatter_start", ps + [lax.empty(p.shape, BF16) for p in ps], 3 * n,
                                     self.scatter_plan)
        return self.scatter[3][0, 0]

    def reduce_late(self, after):
        send_sems, recv_sems, bufs, _ = self.scatter
        bufs = _wait_copies("grad_scatter_wait", send_sems, recv_sems, bufs, after, self.scatter_plan)
        n = len(self.halves)
        fs = [_sum_chips(_own_slot(bufs[n + b], _own_piece(bufs[b])), self.c_idx, "grad_sum_chips_%d" % b)
              for b in range(n)]
        self.join_plan = _plan_join_halves(self.halves)
        self.join = _start_copies("grad_join_start", fs, n, self.join_plan)

    def reduce_end(self, after):
        send_sems, recv_sems, bufs, _ = self.join
        g_wd, sq, g_wf = _wait_copies("grad_join_wait", send_sems, recv_sems, bufs, after, self.join_plan)
        return g_wd, g_wf, sq[0:ROWS_SQ], sq[ROWS_SQ:2 * ROWS_SQ], sq[2 * ROWS_SQ:3 * ROWS_SQ]

    def w_in_start(self, d_w_in_t):
        g = _rows_to_slots(d_w_in_t)
        self.w_half = g.shape[1] // 2
        self.w_swap_plan = _plan_swap_halves([self.w_half])
        self.w_swap = _start_copies("grad_w_in_swap_start", [g, lax.empty((4, self.w_half, D), F32)], 1,
                                    self.w_swap_plan)
        return self.w_swap[3]

    def w_in_mid(self, after):
        send_sems, recv_sems, bufs, _ = self.w_swap
        g, got = _wait_copies("grad_w_in_swap_wait", send_sems, recv_sems, bufs, after, self.w_swap_plan)
        p = _add_halves(g, got, self.c_idx, "grad_add_halves_w_in")
        self.w_scatter_plan = _plan_scatter_chips(1)
        self.w_scatter = _start_copies("grad_w_in_scatter_start", [p, lax.empty(p.shape, BF16)], 3,
                                       self.w_scatter_plan)
        return self.w_scatter[3]

    def w_in_end(self, after):
        send_sems, recv_sems, bufs, _ = self.w_scatter
        p, got = _wait_copies("grad_w_in_scatter_wait", send_sems, recv_sems, bufs, after, self.w_scatter_plan)
        return _join_halves(_sum_chips(_own_slot(got, _own_piece(p)), self.c_idx, "grad_sum_chips_w_in"))


def _local_grads(x, loss_target, norm_mix_g, conv_dw_b, conv_ln_g, conv_ln_b, q_norm_g, k_norm_g, sinks, norm_ffn_g,
                 w_in, dw, exchanges):
    n_seq, S, _ = x.shape
    T = n_seq * S
    x2 = x.reshape(T, D)
    tgt = loss_target.reshape(T, D)
    row = lambda t: t.reshape(1, -1)
    g1, g2 = row(norm_mix_g), row(norm_ffn_g)
    qg, kg = jnp.broadcast_to(q_norm_g.reshape(HD, 1), (HD, GROUP * BLK)), row(k_norm_g)
    lng, lnb, dwb = row(conv_ln_g), row(conv_ln_b), row(conv_dw_b)
    sink_rows = jnp.repeat(sinks.reshape(NKV, GROUP), BLK, axis=1)

    xn, ag, q, kv, gg, h1, h3 = _in_proj_conv_fwd(x2, g1 + exchanges.first_token, w_in, dw, dwb, lng, lnb, n_seq, S)
    o = _attn_fwd(q, kv, qg, kg, sink_rows, n_seq, S)
    wa, wm, wf, wd, wc = exchanges.late_weights([o, h3])
    mix, h, yc = _merge_fwd(x2, gg, h3, o, wa, wm, wc)
    dh, dhb, hn, act, dout, dgu, ffn_stats = _ffn(h, tgt, g2, wf, wd)
    d_wd = _tn_matmul(act, dout, "dw_ffn_down")
    d_wf = _tn_matmul(hn, dgu, "dw_ffn_in", column_blocks=True)
    d_wm = _tn_matmul(mix, dhb, "dw_merge")
    dgg, dyc, dya, do, dh3 = _merge_bwd(dhb, gg, yc, o, wm, wa, wc)
    d_wa = _tn_matmul(o, dya, "dw_attn_out")
    d_wc = _tn_matmul(h3, dyc, "dw_conv_out")
    token = exchanges.reduce_start(d_wd, d_wf, d_wm, d_wa, d_wc)
    dq, dkc, dkp, dqg, dsk = _attn_bwd(q, kv, do, qg + token, kg, sink_rows, n_seq, S)
    token = exchanges.reduce_mid([dq])
    dkv, dkg = _kv_bwd(kv, dkc, dkp, jnp.broadcast_to(k_norm_g.reshape(HD, 1), (HD, BLK)), n_seq, S)
    dag, ddw, conv_stats = _conv_bwd(h1, dh3, ag, dw, lng + token, lnb, n_seq, S)
    dx, dg1 = _in_proj_bwd(dag, dq, dkv, dgg, dh, x2, g1, w_in)
    exchanges.reduce_late([dx])
    d_w_in = jnp.concatenate([_tn_matmul(dag, xn, "dw_in_conv"), _tn_matmul(dq, xn, "dw_in_q"),
                              _tn_matmul(dkv, xn, "dw_in_kv"), _tn_matmul(dgg, xn, "dw_in_gates")], axis=0)

    heads = jnp.concatenate([jnp.sum(dqg, axis=1), jnp.sum(dkg, axis=1), dsk[0, :NQ],
                             jnp.zeros((D - 2 * HD - NQ,), F32)])
    vec = jnp.concatenate([dg1, conv_stats[0:3], ffn_stats[0:1], heads[None], ffn_stats[1:2], jnp.zeros((1, D), F32),
                           ddw], axis=0)
    return ffn_stats[1], dx.reshape(x.shape), d_w_in, vec


def kernel(x, norm_mix_g, w_in, conv_dw_w, conv_dw_b, conv_ln_g, conv_ln_b, w_conv_out, q_norm_g, k_norm_g, sinks, w_attn_out, w_merge_out, norm_ffn_g, w_ffn_in, w_ffn_down, loss_target, m_norm_mix_g, m_w_in, m_conv_dw_w, m_conv_dw_b, m_conv_ln_g, m_conv_ln_b, m_w_conv_out, m_q_norm_g, m_k_norm_g, m_sinks, m_w_attn_out, m_w_merge_out, m_norm_ffn_g, m_w_ffn_in, m_w_ffn_down, v_norm_mix_g, v_w_in, v_conv_dw_w, v_conv_dw_b, v_conv_ln_g, v_conv_ln_b, v_w_conv_out, v_q_norm_g, v_k_norm_g, v_sinks, v_w_attn_out, v_w_merge_out, v_norm_ffn_g, v_w_ffn_in, v_w_ffn_down):
    chip = 2 * lax.axis_index("x") + lax.axis_index("y")

    first, token = _gather_weights(_pack_first(w_in, conv_dw_w))
    exchanges = _Exchanges([t.astype(BF16) + token.astype(BF16)
                            for t in (w_attn_out, w_merge_out, w_ffn_in, w_ffn_down, w_conv_out)])
    _, grad_x, d_w_in, vec = _local_grads(x, loss_target, norm_mix_g, conv_dw_b, conv_ln_g, conv_ln_b, q_norm_g,
                                          k_norm_g, sinks, norm_ffn_g, *_unpack_first(*first), exchanges)

    g_wd, g_wf, g_wm, g_wa, g_wc = exchanges.reduce_end([d_w_in])
    w_in_token = exchanges.w_in_start(d_w_in)
    small = _allreduce_small(vec + w_in_token[0, 0])
    loss = 0.5 / D * jnp.sum(small[6])
    g_dw = lax.dynamic_slice_in_dim(small[8:8 + CW], chip * 256, 256, axis=1).reshape(CW, 1, 256)
    grads = {
        "norm_mix_g": small[0], "conv_dw_w": g_dw, "conv_dw_b": small[1], "conv_ln_g": small[2],
        "conv_ln_b": small[3], "w_conv_out": g_wc, "q_norm_g": small[5, 0:HD], "k_norm_g": small[5, HD:2 * HD],
        "sinks": small[5, 2 * HD:2 * HD + NQ], "w_attn_out": g_wa, "w_merge_out": g_wm, "norm_ffn_g": small[4],
        "w_ffn_in": g_wf, "w_ffn_down": g_wd,
    }
    weights = dict(norm_mix_g=norm_mix_g, w_in=w_in, conv_dw_w=conv_dw_w, conv_dw_b=conv_dw_b, conv_ln_g=conv_ln_g,
                   conv_ln_b=conv_ln_b, w_conv_out=w_conv_out, q_norm_g=q_norm_g, k_norm_g=k_norm_g, sinks=sinks,
                   w_attn_out=w_attn_out, w_merge_out=w_merge_out, norm_ffn_g=norm_ffn_g, w_ffn_in=w_ffn_in,
                   w_ffn_down=w_ffn_down)
    m_in = dict(norm_mix_g=m_norm_mix_g, w_in=m_w_in, conv_dw_w=m_conv_dw_w, conv_dw_b=m_conv_dw_b,
                conv_ln_g=m_conv_ln_g, conv_ln_b=m_conv_ln_b, w_conv_out=m_w_conv_out, q_norm_g=m_q_norm_g,
                k_norm_g=m_k_norm_g, sinks=m_sinks, w_attn_out=m_w_attn_out, w_merge_out=m_w_merge_out,
                norm_ffn_g=m_norm_ffn_g, w_ffn_in=m_w_ffn_in, w_ffn_down=m_w_ffn_down)
    v_in = dict(norm_mix_g=v_norm_mix_g, w_in=v_w_in, conv_dw_w=v_conv_dw_w, conv_dw_b=v_conv_dw_b,
                conv_ln_g=v_conv_ln_g, conv_ln_b=v_conv_ln_b, w_conv_out=v_w_conv_out, q_norm_g=v_q_norm_g,
                k_norm_g=v_k_norm_g, sinks=v_sinks, w_attn_out=v_w_attn_out, w_merge_out=v_w_merge_out,
                norm_ffn_g=v_norm_ffn_g, w_ffn_in=v_w_ffn_in, w_ffn_down=v_w_ffn_down)
    names = list(weights)
    updates = {}
    after = exchanges.w_in_mid([small])
    for n in names:
        if n != "w_in":
            updates[n] = _adamw(weights[n], grads[n], m_in[n], v_in[n], "adamw_" + n, after)
    g_w_in_t = exchanges.w_in_end([updates[n][0] for n in updates])
    grads["w_in"] = g_w_in_t.T
    updates["w_in"] = [t.T for t in _adamw(w_in.T, g_w_in_t, m_w_in.T, v_w_in.T, "adamw_w_in", g_w_in_t)]
    return (loss, grad_x, *[grads[n] for n in names], *[updates[n][0] for n in names],
            *[updates[n][1] for n in names], *[updates[n][2] for n in names])
```
